```python
import math
import jax
import jax.numpy as jnp
from jax import lax
import numpy as np

D_MODEL = 1024
BATCH = 16
SEQ = 256
DEPTH = 4
DEC_BATCH = 2
DEC_SEQ = 2048
PAST_LEN = 512

GRID_W = 64
HEAD_DIM = 64
A_HEADS = 6
A_WIDTH = A_HEADS * HEAD_DIM
LORA_W = 64
LORA_A = 64
LORA_G = 128
N_DIR = 2
DECAY_SCALE = 0.606531
B_Q_HEADS = 6
B_KV_HEADS = 2
B_WIDTH = B_Q_HEADS * HEAD_DIM
KV_WIDTH = B_KV_HEADS * HEAD_DIM
ROPE_THETA = 10000.0
Q_BLOCK = 128
POOL_WINDOWS = (2, 4, 8, 16)
C_GROUPS = 4
C_GROUP_W = 64
C_WIDTH = C_GROUPS * C_GROUP_W
MIX_WIDTH = A_WIDTH + B_WIDTH + C_WIDTH
IN_SPLITS = (A_WIDTH, A_WIDTH, A_WIDTH, N_DIR * LORA_W, N_DIR * LORA_A, LORA_G,
             B_WIDTH, KV_WIDTH, KV_WIDTH, C_WIDTH)
IN_COLS = sum(IN_SPLITS)
N_GROUPS = 4
EXP_PER_GROUP = 4
N_EXPERTS = N_GROUPS * EXP_PER_GROUP
TOP_K_INNER = 2
D_EXPERT = 256
ALPHA = (2 * DEPTH) ** 0.25
BETA = (8 * DEPTH) ** -0.25
LN_EPS = 1e-5
GN_EPS = 64e-5
QK_EPS = 1e-6

kernel_name = "hybrid_rwkv7_gqa_pool_hmoe_diffusion_step"

F32 = jnp.float32


def layer_norm(x, g, b):
    xf = x.astype(F32)
    mu = jnp.mean(xf, -1, keepdims=True)
    var = jnp.mean(jnp.square(xf - mu), -1, keepdims=True)
    return ((xf - mu) * lax.rsqrt(var + LN_EPS) * g + b).astype(x.dtype)


def rms_norm_heads(x, g):
    xf = x.astype(F32)
    return (xf * lax.rsqrt(jnp.mean(xf * xf, -1, keepdims=True) + QK_EPS) * g).astype(x.dtype)


def rope_tables(length):
    rows = length // GRID_W
    t = jnp.arange(rows * GRID_W)
    row = (t // GRID_W).astype(F32)
    col = (t % GRID_W).astype(F32)
    n = HEAD_DIM // 4
    inv = 1.0 / (ROPE_THETA ** (jnp.arange(n, dtype=F32) / n))
    ang = jnp.concatenate([row[:, None] * inv, col[:, None] * inv], -1)
    return jnp.cos(ang), jnp.sin(ang)


def apply_rope(x, cos, sin):
    bsz, length, nh, _ = x.shape
    n = HEAD_DIM // 4
    xs = x.astype(F32).reshape(bsz, length, nh, 2, 2, n)
    x1, x2 = xs[..., 0, :], xs[..., 1, :]
    c = cos.reshape(length, 2, n)[None, :, None]
    s = sin.reshape(length, 2, n)[None, :, None]
    out = jnp.stack([x1 * c - x2 * s, x2 * c + x1 * s], axis=-2)
    return out.reshape(x.shape).astype(x.dtype)


def block_attention(q, k, v):
    bsz, lq, hq, hd = q.shape
    hkv = k.shape[2]
    grp = hq // hkv
    nb = lq // Q_BLOCK
    qb = jnp.moveaxis(q.reshape(bsz, nb, Q_BLOCK, hkv, grp, hd), 1, 0)
    kf = k.astype(F32)
    vf = v.astype(F32)
    scale = 1.0 / math.sqrt(hd)

    def one_block(qblk):
        s = jnp.einsum('bqhgd,bkhd->bhgqk', qblk.astype(F32), kf) * scale
        p = jax.nn.softmax(s, axis=-1)
        return jnp.einsum('bhgqk,bkhd->bqhgd', p, vf).astype(q.dtype)

    o = lax.map(one_block, qb)
    return jnp.moveaxis(o, 0, 1).reshape(bsz, lq, hq * hd)


def _both_dirs(t):
    return jnp.stack([t, jnp.flip(t, axis=1)], axis=2)


def _per_dir(t):
    return jnp.stack([t[:, :, 0], jnp.flip(t[:, :, 1], axis=1)], axis=2)


def _time_major(t):
    return jnp.moveaxis(t, 1, 0)


def _rwkv7_step(S, inp):
    r_t, w_t, k_t, v_t, kk_t, a_t = inp
    sa = jnp.einsum('bshvk,bshk->bshv', S, -kk_t)
    S = S * w_t[..., None, :] + sa[..., None] * (kk_t * a_t)[..., None, :] + v_t[..., None] * k_t[..., None, :]
    y = jnp.einsum('bshvk,bshk->bshv', S, r_t)
    return S, y


def rwkv7_mixer(r, k, v, lo_w, lo_a, lo_g, s0, lp):
    bsz, seqlen, _ = r.shape
    hshape = (bsz, seqlen, A_HEADS, HEAD_DIM)
    dshape = (bsz, seqlen, N_DIR, A_HEADS, HEAD_DIM)
    lo_w = lo_w.reshape(bsz, seqlen, N_DIR, LORA_W)
    lo_a = lo_a.reshape(bsz, seqlen, N_DIR, LORA_A)
    w_pre = lp['rw_w0'] + jnp.einsum('blsr,src->blsc', jnp.tanh(lo_w), lp['rw_w2'])
    decay = jnp.exp(-DECAY_SCALE * jax.nn.sigmoid(w_pre.astype(F32)))
    a = jax.nn.sigmoid((lp['rw_a0'] + jnp.einsum('blsr,src->blsc', lo_a, lp['rw_a2'])).astype(F32))
    g = jnp.einsum('blr,rc->blc', jax.nn.sigmoid(lo_g), lp['rw_g2']).astype(F32)
    kf = k.astype(F32)
    kk = (kf * lp['rw_kk']).reshape(hshape)
    kk = kk * lax.rsqrt(jnp.maximum(jnp.sum(kk * kk, -1, keepdims=True), 1e-24))
    kd = kf[:, :, None, :] * (1.0 + (a - 1.0) * lp['rw_ka'])
    rh = r.astype(F32).reshape(hshape)
    vh = v.astype(F32).reshape(hshape)
    kd_h = kd.reshape(dshape)
    xs = (_time_major(_both_dirs(rh)),
          _time_major(_per_dir(decay.reshape(dshape))),
          _time_major(_per_dir(kd_h)),
          _time_major(_both_dirs(vh)),
          _time_major(_both_dirs(kk)),
          _time_major(_per_dir(a.reshape(dshape))))
    s_final, ys = lax.scan(_rwkv7_step, s0.astype(F32), xs)
    ys = jnp.moveaxis(ys, 0, 1)
    y = ys[:, :, 0] + jnp.flip(ys[:, :, 1], axis=1)
    mu = jnp.mean(y, -1, keepdims=True)
    var = jnp.mean(jnp.square(y - mu), -1, keepdims=True)
    yn = ((y - mu) * lax.rsqrt(var + GN_EPS) * lp['rw_gn_g'].reshape(A_HEADS, HEAD_DIM)
          + lp['rw_gn_b'].reshape(A_HEADS, HEAD_DIM))
    bonus = jnp.sum(rh * (kd_h[:, :, 0] + kd_h[:, :, 1]) * lp['rw_rk'], -1, keepdims=True) * vh
    out = (yn + bonus).reshape(bsz, seqlen, A_WIDTH) * g
    return out.astype(r.dtype), s_final


def pool_mixer(p, pool_w, pool_scale):
    bsz, seqlen, _ = p.shape
    pf = p.astype(F32)
    cs = jnp.concatenate([jnp.zeros((bsz, 1, C_WIDTH), F32), jnp.cumsum(pf, axis=1)], axis=1)
    t = jnp.arange(seqlen)
    outs = []
    for gi, win in enumerate(POOL_WINDOWS):
        lo = jnp.clip(t - win // 2, 0, seqlen)
        hi = jnp.clip(t + win // 2, 0, seqlen)
        csg = cs[..., gi * C_GROUP_W:(gi + 1) * C_GROUP_W]
        s = jnp.take(csg, hi, axis=1) - jnp.take(csg, lo, axis=1)
        mean = s / (hi - lo).astype(F32)[None, :, None]
        outs.append(mean - pf[..., gi * C_GROUP_W:(gi + 1) * C_GROUP_W])
    d = jnp.stack(outs, axis=2)
    y = jnp.einsum('blgc,gcd->blgd', d, pool_w.astype(F32)).reshape(bsz, seqlen, C_WIDTH)
    return (y * pool_scale).astype(p.dtype)


def mixing_sublayer(u, lp, rope, ctx_k, ctx_v, s0):
    bsz, seqlen, _ = u.shape
    proj = u @ lp['w_in']
    cuts = [int(i) for i in np.cumsum(IN_SPLITS)[:-1]]
    r, k, v, lo_w, lo_a, lo_g, q, kb, vb, p = jnp.split(proj, cuts, axis=-1)
    a_out, s_final = rwkv7_mixer(r, k, v, lo_w, lo_a, lo_g, s0, lp)
    qh = rms_norm_heads(q.reshape(bsz, seqlen, B_Q_HEADS, HEAD_DIM), lp['q_norm'])
    kh = rms_norm_heads(kb.reshape(bsz, seqlen, B_KV_HEADS, HEAD_DIM), lp['k_norm'])
    vh = vb.reshape(bsz, seqlen, B_KV_HEADS, HEAD_DIM)
    if rope is None:
        q_use, k_all, v_all = qh, kh, vh
    else:
        cos, sin = rope
        q_use = apply_rope(qh, cos, sin)
        k_all = jnp.concatenate([apply_rope(kh, cos, sin), ctx_k.astype(kh.dtype)], axis=1)
        v_all = jnp.concatenate([vh, ctx_v.astype(vh.dtype)], axis=1)
    b_out = block_attention(q_use, k_all, v_all)
    c_out = pool_mixer(p, lp['pool_w'], lp['pool_scale'])
    mixed = jnp.concatenate([a_out, b_out.astype(u.dtype), c_out], axis=-1) @ lp['w_out']
    return mixed, kh, vh, s_final


def hier_moe(u, lp):
    bsz, seqlen, dm = u.shape
    x = u.reshape(-1, dm)
    lg = (x @ lp['rt_grp_w'] + lp['rt_grp_b']).astype(F32)
    pg = jax.nn.softmax(lg, axis=-1)
    sel = jnp.argmax(lg, axis=-1)
    gsel = jnp.take_along_axis(pg, sel[:, None], axis=1)
    le = (x @ lp['rt_exp_w'] + lp['rt_exp_b']).astype(F32).reshape(-1, N_GROUPS, EXP_PER_GROUP)
    le_sel = jnp.take_along_axis(le, sel[:, None, None], axis=1)[:, 0]
    tv, ti = lax.top_k(le_sel, TOP_K_INNER)
    tw = jax.nn.softmax(tv, axis=-1) * gsel
    inner = jnp.einsum('nk,nke->ne', tw, jax.nn.one_hot(ti, EXP_PER_GROUP, dtype=F32))
    gate = (jax.nn.one_hot(sel, N_GROUPS, dtype=F32)[:, :, None] * inner[:, None, :]).reshape(-1, N_EXPERTS)
    hid = jax.nn.silu(jnp.einsum('nd,edf->nef', x, lp['moe_w1'])) * jnp.einsum('nd,edf->nef', x, lp['moe_w3'])
    y = jnp.einsum('nef,efd->nd', hid * gate[:, :, None].astype(hid.dtype), lp['moe_w2'])
    return y.reshape(bsz, seqlen, dm)


def trunk_layer(h, cond, lp, rope, ctx_k, ctx_v, s0):
    mod = (jax.nn.silu(cond) @ lp['w_mod'] + lp['b_mod'])[:, None, :]
    sh1, sc1, g1, sh2, sc2, g2 = jnp.split(mod, 6, axis=-1)
    u = h * (1.0 + sc1) + sh1
    m, k, v, s_final = mixing_sublayer(u, lp, rope, ctx_k, ctx_v, s0)
    h = layer_norm(ALPHA * h + g1 * m, lp['ln1_g'], lp['ln1_b'])
    u = h * (1.0 + sc2) + sh2
    f = hier_moe(u, lp)
    h = layer_norm(ALPHA * h + g2 * f, lp['ln2_g'], lp['ln2_b'])
    return h, k, v, s_final


def setup_inputs(seed: int = 0) -> dict:
    key = jax.random.key(seed)
    ks = iter(jax.random.split(key, 48))

    def nrm(shape, scale):
        return jax.random.normal(next(ks), shape, F32) * scale

    d = D_MODEL
    return {
        'x_prompt': nrm((BATCH, SEQ, d), 1.0),
        'x_sample': nrm((DEC_BATCH, DEC_SEQ, d), 1.0),
        'state_rwkv': nrm((DEC_BATCH, DEPTH, N_DIR, A_HEADS, HEAD_DIM, HEAD_DIM), 0.5),
        'cache_k': nrm((DEC_BATCH, DEPTH, PAST_LEN, B_KV_HEADS, HEAD_DIM), 1.0),
        'cache_v': nrm((DEC_BATCH, DEPTH, PAST_LEN, B_KV_HEADS, HEAD_DIM), 1.0),
        'c': nrm((DEC_BATCH, d), 1.0),
        'c_ctx': nrm((d,), 1.0),
        'w_in': nrm((DEPTH, d, IN_COLS), d ** -0.5),
        'w_out': nrm((DEPTH, MIX_WIDTH, d), MIX_WIDTH ** -0.5 * BETA),
        'w_mod': nrm((DEPTH, d, 6 * d), 0.5 * d ** -0.5),
        'b_mod': nrm((DEPTH, 6 * d), 0.02),
        'ln1_g': 1.0 + nrm((DEPTH, d), 0.05),
        'ln1_b': nrm((DEPTH, d), 0.02),
        'ln2_g': 1.0 + nrm((DEPTH, d), 0.05),
        'ln2_b': nrm((DEPTH, d), 0.02),
        'rw_w0': nrm((DEPTH, N_DIR, A_WIDTH), 0.5),
        'rw_w2': nrm((DEPTH, N_DIR, LORA_W, A_WIDTH), 0.1 * LORA_W ** -0.5),
        'rw_a0': nrm((DEPTH, N_DIR, A_WIDTH), 0.5),
        'rw_a2': nrm((DEPTH, N_DIR, LORA_A, A_WIDTH), LORA_A ** -0.5),
        'rw_g2': nrm((DEPTH, LORA_G, A_WIDTH), LORA_G ** -0.5),
        'rw_kk': 0.85 + nrm((DEPTH, A_WIDTH), 0.05),
        'rw_ka': 1.0 + nrm((DEPTH, A_WIDTH), 0.05),
        'rw_rk': nrm((DEPTH, A_HEADS, HEAD_DIM), 0.1),
        'rw_gn_g': 1.0 + nrm((DEPTH, A_WIDTH), 0.05),
        'rw_gn_b': nrm((DEPTH, A_WIDTH), 0.02),
        'q_norm': 1.0 + nrm((DEPTH, HEAD_DIM), 0.05),
        'k_norm': 1.0 + nrm((DEPTH, HEAD_DIM), 0.05),
        'pool_w': nrm((DEPTH, C_GROUPS, C_GROUP_W, C_GROUP_W), C_GROUP_W ** -0.5),
        'pool_scale': 1.0 + nrm((DEPTH, C_WIDTH), 0.05),
        'rt_grp_w': nrm((DEPTH, d, N_GROUPS), d ** -0.5),
        'rt_grp_b': nrm((DEPTH, N_GROUPS), 0.01),
        'rt_exp_w': nrm((DEPTH, d, N_EXPERTS), d ** -0.5),
        'rt_exp_b': nrm((DEPTH, N_EXPERTS), 0.01),
        'moe_w1': nrm((DEPTH, N_EXPERTS, d, D_EXPERT), d ** -0.5),
        'moe_w3': nrm((DEPTH, N_EXPERTS, d, D_EXPERT), d ** -0.5),
        'moe_w2': nrm((DEPTH, N_EXPERTS, D_EXPERT, d), D_EXPERT ** -0.5 * BETA),
    }


def reference(x_prompt, x_sample, state_rwkv, cache_k, cache_v, c, c_ctx,
              w_in, w_out, w_mod, b_mod, ln1_g, ln1_b, ln2_g, ln2_b,
              rw_w0, rw_w2, rw_a0, rw_a2, rw_g2, rw_kk, rw_ka, rw_rk, rw_gn_g, rw_gn_b,
              q_norm, k_norm, pool_w, pool_scale,
              rt_grp_w, rt_grp_b, rt_exp_w, rt_exp_b, moe_w1, moe_w3, moe_w2):
    rope = rope_tables(x_sample.shape[1])
    s0_ctx = jnp.zeros((x_prompt.shape[0], N_DIR, A_HEADS, HEAD_DIM, HEAD_DIM), F32)
    cond_ctx = c_ctx[None, :]
    hp = x_prompt
    hs = x_sample
    new_s, new_k, new_v = [], [], []
    for l in range(DEPTH):
        lp = {
            'w_in': w_in[l], 'w_out': w_out[l], 'w_mod': w_mod[l], 'b_mod': b_mod[l],
            'ln1_g': ln1_g[l], 'ln1_b': ln1_b[l], 'ln2_g': ln2_g[l], 'ln2_b': ln2_b[l],
            'rw_w0': rw_w0[l], 'rw_w2': rw_w2[l], 'rw_a0': rw_a0[l], 'rw_a2': rw_a2[l],
            'rw_g2': rw_g2[l], 'rw_kk': rw_kk[l], 'rw_ka': rw_ka[l], 'rw_rk': rw_rk[l],
            'rw_gn_g': rw_gn_g[l], 'rw_gn_b': rw_gn_b[l],
            'q_norm': q_norm[l], 'k_norm': k_norm[l],
            'pool_w': pool_w[l], 'pool_scale': pool_scale[l],
            'rt_grp_w': rt_grp_w[l], 'rt_grp_b': rt_grp_b[l],
            'rt_exp_w': rt_exp_w[l], 'rt_exp_b': rt_exp_b[l],
            'moe_w1': moe_w1[l], 'moe_w3': moe_w3[l], 'moe_w2': moe_w2[l],
        }
        hp, k_l, v_l, s_l = trunk_layer(hp, cond_ctx, lp, None, None, None, s0_ctx)
        new_s.append(s_l)
        new_k.append(k_l)
        new_v.append(v_l)
        hs, _, _, _ = trunk_layer(hs, c, lp, rope, cache_k[:, l], cache_v[:, l], state_rwkv[:, l])
    return (hp, hs, jnp.stack(new_s, axis=1), jnp.stack(new_k, axis=1), jnp.stack(new_v, axis=1))
```

```python
import functools
import math

import jax
import jax.numpy as jnp
from jax import lax
from jax.experimental import pallas as pl
from jax.experimental.pallas import tpu as pltpu

F32 = jnp.float32
BF16 = jnp.bfloat16
I32 = jnp.int32
HIGHEST = lax.Precision.HIGHEST

D_MODEL = 1024
GRID_W = 64
HEAD_DIM = 64
A_HEADS = 6
A_WIDTH = A_HEADS * HEAD_DIM
LORA_W = 64
LORA_A = 64
LORA_G = 128
N_DIR = 2
DECAY_SCALE = 0.606531
B_Q_HEADS = 6
B_KV_HEADS = 2
B_WIDTH = B_Q_HEADS * HEAD_DIM
KV_WIDTH = B_KV_HEADS * HEAD_DIM
ROPE_THETA = 10000.0
POOL_WINDOWS = (2, 4, 8, 16)
C_WIDTH = 256
IN_COLS = 2432
N_GROUPS = 4
EXP_PER_GROUP = 4
N_EXPERTS = 16
D_EXPERT = 256
LN_EPS = 1e-5
GN_EPS = 64e-5
QK_EPS = 1e-6

_O_R, _O_K, _O_V = 0, 384, 768
_O_LW, _O_LA, _O_LG = 1152, 1280, 1408
_O_Q, _O_KB, _O_VB, _O_P = 1536, 1920, 2048, 2176

CHUNK = 64
ROUTER_LANES = 128
VMEM_LIMIT = 56 * 1024 * 1024


def _cparams(sem):
    return pltpu.CompilerParams(dimension_semantics=sem, vmem_limit_bytes=VMEM_LIMIT)


def _sigmoid(x):
    return 1.0 / (1.0 + jnp.exp(-x))


def _dot(a, b):
    return jnp.dot(a, b, preferred_element_type=F32)


def _dot_nt(a, b):
    return lax.dot_general(a, b, (((1,), (1,)), ((), ())), preferred_element_type=F32)


def _segsum(x, g_ones):
    hi = x.astype(BF16)
    lo = (x - hi.astype(F32)).astype(BF16)
    return _dot(hi, g_ones) + _dot(lo, g_ones)


def _rope128(x, cos, sin):
    lane = lax.broadcasted_iota(I32, x.shape, 1)
    first = (lane % 32) < 16
    up = pltpu.roll(x, 128 - 16, 1)
    dn = pltpu.roll(x, 16, 1)
    return x * cos + jnp.where(first, up, dn) * sin


def _layer_norm(z, g, b):
    mu = jnp.mean(z, axis=-1, keepdims=True)
    zc = z - mu
    var = jnp.mean(zc * zc, axis=-1, keepdims=True)
    return zc * lax.rsqrt(var + LN_EPS) * g + b


def _mod_kernel(c_ref, w_ref, b_ref, o_ref):
    x = c_ref[...]
    x = x * _sigmoid(x)
    o_ref[...] = jnp.dot(x, w_ref[...], precision=HIGHEST, preferred_element_type=F32) + b_ref[...]


def _modulation(cond8, w_mod, b_mod):
    depth = w_mod.shape[0]
    tn = 1536
    return pl.pallas_call(
        _mod_kernel,
        grid=(depth, 6 * D_MODEL // tn),
        in_specs=[
            pl.BlockSpec((8, D_MODEL), lambda l, j: (0, 0)),
            pl.BlockSpec((None, D_MODEL, tn), lambda l, j: (l, 0, j)),
            pl.BlockSpec((None, 1, tn), lambda l, j: (l, 0, j)),
        ],
        out_specs=pl.BlockSpec((None, 8, tn), lambda l, j: (l, 0, j)),
        out_shape=jax.ShapeDtypeStruct((depth, 8, 6 * D_MODEL), F32),
        compiler_params=_cparams(("arbitrary", "arbitrary")),
    )(cond8, w_mod, b_mod.reshape(depth, 1, 6 * D_MODEL))


def _group_of_row(row0, bounds):
    g = jnp.zeros((), I32)
    for b in bounds:
        g = g + jnp.where(row0 >= b, 1, 0).astype(I32)
    return g


def _in_kernel(h_ref, mod_ref, w_ref, w2_ref, w0_ref, a2_ref, a0_ref, g2_ref, kkw_ref, ka_ref,
               rk_ref, qn_ref, knw_ref, cos_ref, sin_ref, ones_ref,
               r_o, v_o, kk_o, lw_o, kd_o, b_o, g_o, bon_o, q_o, k_o, vb_o, p_o, *, tm, bounds):
    grp = _group_of_row(pl.program_id(0) * tm, bounds)
    mod = mod_ref[pl.ds(grp, 1), :]
    sh1 = mod[:, 0:D_MODEL]
    sc1 = mod[:, D_MODEL:2 * D_MODEL]
    u = h_ref[...] * (1.0 + sc1) + sh1
    proj = _dot(u.astype(BF16), w_ref[...])
    ones = ones_ref[...]

    r = proj[:, _O_R:_O_R + A_WIDTH]
    k = proj[:, _O_K:_O_K + A_WIDTH]
    v = proj[:, _O_V:_O_V + A_WIDTH]
    lo_w = proj[:, _O_LW:_O_LW + 128]
    lo_a = proj[:, _O_LA:_O_LA + 128]
    lo_g = proj[:, _O_LG:_O_LG + 128]

    w_pre = w0_ref[...] + jnp.dot(jnp.tanh(lo_w), w2_ref[...], precision=HIGHEST, preferred_element_type=F32)
    logw = -DECAY_SCALE * _sigmoid(w_pre)
    a = _sigmoid(a0_ref[...] + jnp.dot(lo_a, a2_ref[...], precision=HIGHEST, preferred_element_type=F32))
    g = jnp.dot(_sigmoid(lo_g), g2_ref[...], precision=HIGHEST, preferred_element_type=F32)

    kk = k * kkw_ref[...]
    kk = kk * lax.rsqrt(jnp.maximum(_segsum(kk * kk, ones), 1e-24))
    k2 = jnp.concatenate([k, k], axis=1)
    kd = k2 * (1.0 + (a - 1.0) * ka_ref[...])
    bvec = jnp.concatenate([kk, kk], axis=1) * a
    kd_sum = kd[:, 0:A_WIDTH] + kd[:, A_WIDTH:2 * A_WIDTH]
    bonus = _segsum(r * kd_sum * rk_ref[...], ones) * v

    r_o[...] = r
    v_o[...] = v
    kk_o[...] = kk
    lw_o[...] = logw
    kd_o[...] = kd
    b_o[...] = bvec
    g_o[...] = g
    bon_o[...] = bonus

    cos = cos_ref[...]
    sin = sin_ref[...]
    q = proj[:, _O_Q:_O_Q + B_WIDTH]
    qn = q * lax.rsqrt(_segsum(q * q, ones) * (1.0 / HEAD_DIM) + QK_EPS) * qn_ref[...]
    scale = 1.0 / math.sqrt(HEAD_DIM)
    q_o[...] = jnp.concatenate(
        [_rope128(qn[:, j * 128:(j + 1) * 128], cos, sin) for j in range(B_WIDTH // 128)], axis=1) * scale
    kb = proj[:, _O_KB:_O_KB + KV_WIDTH]
    kn = kb * lax.rsqrt(_segsum(kb * kb, ones[0:KV_WIDTH, 0:KV_WIDTH]) * (1.0 / HEAD_DIM) + QK_EPS) * knw_ref[...]
    k_o[...] = _rope128(kn, cos, sin)
    vb_o[...] = proj[:, _O_VB:_O_VB + KV_WIDTH]
    p_o[...] = proj[:, _O_P:_O_P + C_WIDTH]


def _in_projection(h, mods_l, lw, cos, sin, ones, *, tm, bounds):
    t = h.shape[0]
    full = lambda shape: pl.BlockSpec(shape, lambda i: (0,) * len(shape))
    rows = lambda w: pl.BlockSpec((tm, w), lambda i: (i, 0))
    widths = (A_WIDTH, A_WIDTH, A_WIDTH, 2 * A_WIDTH, 2 * A_WIDTH, 2 * A_WIDTH, A_WIDTH, A_WIDTH,
              B_WIDTH, KV_WIDTH, KV_WIDTH, C_WIDTH)
    return pl.pallas_call(
        functools.partial(_in_kernel, tm=tm, bounds=bounds),
        grid=(t // tm,),
        in_specs=[
            rows(D_MODEL), full((8, 6 * D_MODEL)), full((D_MODEL, IN_COLS)),
            full((128, 2 * A_WIDTH)), full((1, 2 * A_WIDTH)), full((128, 2 * A_WIDTH)), full((1, 2 * A_WIDTH)),
            full((LORA_G, A_WIDTH)), full((1, A_WIDTH)), full((1, 2 * A_WIDTH)), full((1, A_WIDTH)),
            full((1, B_WIDTH)), full((1, KV_WIDTH)), rows(128), rows(128), full((A_WIDTH, A_WIDTH)),
        ],
        out_specs=[rows(w) for w in widths],
        out_shape=[jax.ShapeDtypeStruct((t, w), F32) for w in widths],
        compiler_params=_cparams(("arbitrary",)),
    )(h, mods_l, lw['w_in'], lw['w2blk'], lw['w0'], lw['a2blk'], lw['a0'], lw['g2'], lw['kkw'], lw['ka'],
      lw['rk'], lw['qn'], lw['knw'], cos, sin, ones)


def _scan_kernel(*refs, zero_init, want_state):
    fwd = refs[0:6]
    bwd = refs[6:12]
    pos = 12
    if not zero_init:
        s0_ref = refs[pos]
        pos += 1
    yf_o, yb_o = refs[pos], refs[pos + 1]
    pos += 2
    if want_state:
        sfin_o = refs[pos]
        pos += 1
    s_ref = refs[pos]

    c = pl.program_id(1)
    nc = pl.num_programs(1)

    @pl.when(c == 0)
    def _():
        if zero_init:
            s_ref[...] = jnp.zeros_like(s_ref)
        else:
            s_ref[...] = s0_ref[...]

    ti = lax.broadcasted_iota(I32, (CHUNK, CHUNK), 0)
    si = lax.broadcasted_iota(I32, (CHUNK, CHUNK), 1)
    eye = (ti == si).astype(F32)

    for d, (ins, y_o) in enumerate(((fwd, yf_o), (bwd, yb_o))):
        r_ref, v_ref, kk_ref, lw_ref, kd_ref, b_ref = ins
        if d == 0:
            incl, strict = si <= ti, si < ti
        else:
            incl, strict = si >= ti, si > ti
        lw = lw_ref[...]
        cs = jnp.dot(incl.astype(F32), lw, precision=HIGHEST, preferred_element_type=F32)
        gam = jnp.exp(cs)
        gam_prev = jnp.exp(cs - lw)
        gam_inv = jnp.exp(-cs)
        gam_end = jnp.exp(cs[CHUNK - 1:CHUNK, :] if d == 0 else cs[0:1, :])
        a_hat = -kk_ref[...] * gam_prev
        r_hat = r_ref[...] * gam
        b_hat = b_ref[...] * gam_inv
        k_hat = kd_ref[...] * gam_inv
        v_all = v_ref[...]
        ys = []
        for h in range(A_HEADS):
            sl = slice(h * HEAD_DIM, (h + 1) * HEAD_DIM)
            ah = a_hat[:, sl].astype(BF16)
            rh = r_hat[:, sl].astype(BF16)
            bh = b_hat[:, sl]
            kh = k_hat[:, sl]
            bh16 = bh.astype(BF16)
            kh16 = kh.astype(BF16)
            vh = v_all[:, sl]
            vh16 = vh.astype(BF16)
            a_ab = jnp.where(strict, _dot_nt(ah, bh16), 0.0)
            a_ak = jnp.where(strict, _dot_nt(ah, kh16), 0.0)
            a_rb = jnp.where(incl, _dot_nt(rh, bh16), 0.0)
            a_rk = jnp.where(incl, _dot_nt(rh, kh16), 0.0)
            t_inv = eye + a_ab
            npow = a_ab
            for _ in range(int(math.log2(CHUNK)) - 1):
                np16 = npow.astype(BF16)
                npow = _dot(np16, np16)
                t_inv = t_inv + _dot(npow.astype(BF16), t_inv.astype(BF16))
            s_prev = s_ref[d, h]
            s16 = s_prev.astype(BF16)
            x = _dot_nt(ah, s16) + _dot(a_ak.astype(BF16), vh16)
            u = _dot(t_inv.astype(BF16), x.astype(BF16))
            y = _dot_nt(rh, s16) + _dot(a_rb.astype(BF16), u.astype(BF16)) + _dot(a_rk.astype(BF16), vh16)
            ys.append(y)
            uv_t = jnp.concatenate([u, vh], axis=0).T.astype(BF16)
            bk = jnp.concatenate([bh16, kh16], axis=0)
            s_ref[d, h] = (s_prev + _dot(uv_t, bk)) * gam_end[:, sl]
        y_o[...] = jnp.concatenate(ys, axis=1)

    if want_state:
        @pl.when(c == nc - 1)
        def _():
            sfin_o[...] = s_ref[...]


def _rwkv_scan(ops, s0, *, row0, nseq, seqlen, want_state):
    r, v, kk, logw, kd, bvec = ops
    nc = seqlen // CHUNK
    base = row0 // CHUNK
    zero_init = s0 is None
    fidx = lambda col: (lambda s, c: (base + s * nc + c, col))
    bidx = lambda col: (lambda s, c: (base + s * nc + (nc - 1 - c), col))
    blk = lambda im: pl.BlockSpec((CHUNK, A_WIDTH), im)
    in_specs = [blk(fidx(0))] * 6 + [blk(bidx(0))] * 3 + [blk(bidx(1))] * 3
    args = [r, v, kk, logw, kd, bvec, r, v, kk, logw, kd, bvec]
    state_spec = pl.BlockSpec((None, N_DIR, A_HEADS, HEAD_DIM, HEAD_DIM), lambda s, c: (s, 0, 0, 0, 0))
    if not zero_init:
        in_specs.append(state_spec)
        args.append(s0)
    out_specs = [blk(lambda s, c: (s * nc + c, 0)), blk(lambda s, c: (s * nc + (nc - 1 - c), 0))]
    out_shape = [jax.ShapeDtypeStruct((nseq * seqlen, A_WIDTH), F32)] * 2
    if want_state:
        out_specs.append(state_spec)
        out_shape.append(jax.ShapeDtypeStruct((nseq, N_DIR, A_HEADS, HEAD_DIM, HEAD_DIM), F32))
    return pl.pallas_call(
        functools.partial(_scan_kernel, zero_init=zero_init, want_state=want_state),
        grid=(nseq, nc),
        in_specs=in_specs,
        out_specs=out_specs,
        out_shape=out_shape,
        scratch_shapes=[pltpu.VMEM((N_DIR, A_HEADS, HEAD_DIM, HEAD_DIM), F32)],
        compiler_params=_cparams(("arbitrary", "arbitrary")),
    )(*args)


def _attn_kernel(q_ref, k_ref, v_ref, o_ref, *, tq):
    grp = B_Q_HEADS // B_KV_HEADS
    q = q_ref[...]
    kall = k_ref[...]
    vall = v_ref[...]
    outs = []
    for hk in range(B_KV_HEADS):
        ks = kall[:, hk * HEAD_DIM:(hk + 1) * HEAD_DIM].astype(BF16)
        vs = vall[:, hk * HEAD_DIM:(hk + 1) * HEAD_DIM].astype(BF16)
        qs = jnp.concatenate(
            [q[:, (hk * grp + g) * HEAD_DIM:(hk * grp + g + 1) * HEAD_DIM] for g in range(grp)], axis=0)
        s = _dot_nt(qs.astype(BF16), ks)
        m = jnp.max(s, axis=-1, keepdims=True)
        p = jnp.exp(s - m)
        den = jnp.sum(p, axis=-1, keepdims=True)
        o = _dot(p.astype(BF16), vs) / den
        outs.extend(o[g * tq:(g + 1) * tq] for g in range(grp))
    o_ref[...] = jnp.concatenate(outs, axis=1)


def _attention(q, k_arr, v_arr, out_rows, *, row0, nseq, seqlen, tq, keys_in_token_array):
    nq = seqlen // tq
    base = row0 // tq
    if keys_in_token_array:
        lk = seqlen
        kbase = row0 // seqlen
        kspec = pl.BlockSpec((lk, KV_WIDTH), lambda s, i: (kbase + s, 0))
    else:
        lk = k_arr.shape[1]
        kspec = pl.BlockSpec((None, lk, KV_WIDTH), lambda s, i: (s, 0, 0))
    return pl.pallas_call(
        functools.partial(_attn_kernel, tq=tq),
        grid=(nseq, nq),
        in_specs=[pl.BlockSpec((tq, B_WIDTH), lambda s, i: (base + s * nq + i, 0)), kspec, kspec],
        out_specs=pl.BlockSpec((tq, B_WIDTH), lambda s, i: (s * nq + i, 0)),
        out_shape=jax.ShapeDtypeStruct((out_rows, B_WIDTH), F32),
        compiler_params=_cparams(("arbitrary", "arbitrary")),
    )(q, k_arr, v_arr)


def _pool_kernel(p_ref, w_ref, sc_ref, o_ref, *, seqlen):
    x = p_ref[...]
    t = lax.broadcasted_iota(I32, x.shape, 0)
    lane = lax.broadcasted_iota(I32, x.shape, 1)

    def down(a, k):
        return jnp.where(t >= k, pltpu.roll(a, k, 0), 0.0)

    def up(a, k):
        return jnp.where(t < seqlen - k, pltpu.roll(a, seqlen - k, 0), 0.0)

    past = [x]
    futr = [x]
    for j in range(len(POOL_WINDOWS) - 1):
        past.append(past[j] + down(past[j], 2 ** j))
        futr.append(futr[j] + up(futr[j], 2 ** j))
    win_sum = jnp.zeros_like(x)
    cnt = jnp.zeros_like(x)
    for gi, win in enumerate(POOL_WINDOWS):
        half = win // 2
        in_group = (lane // HEAD_DIM) == gi
        win_sum = jnp.where(in_group, down(past[gi], 1) + futr[gi], win_sum)
        n = (jnp.minimum(t + half, seqlen) - jnp.maximum(t - half, 0)).astype(F32)
        cnt = jnp.where(in_group, n, cnt)
    d = win_sum / cnt - x
    o_ref[...] = _dot(d.astype(BF16), w_ref[...]) * sc_ref[...]


def _pool(p, w_blk, scale, *, row0, nseq, seqlen):
    base = row0 // seqlen
    return pl.pallas_call(
        functools.partial(_pool_kernel, seqlen=seqlen),
        grid=(nseq,),
        in_specs=[pl.BlockSpec((seqlen, C_WIDTH), lambda s: (base + s, 0)),
                  pl.BlockSpec((C_WIDTH, C_WIDTH), lambda s: (0, 0)),
                  pl.BlockSpec((1, C_WIDTH), lambda s: (0, 0))],
        out_specs=pl.BlockSpec((seqlen, C_WIDTH), lambda s: (s, 0)),
        out_shape=jax.ShapeDtypeStruct((nseq * seqlen, C_WIDTH), F32),
        compiler_params=_cparams(("arbitrary",)),
    )(p, w_blk, scale)


def _out_kernel(yf_ref, yb_ref, g_ref, bon_ref, att_ref, pool_ref, h_ref, mod_ref, w_ref, lng_ref, lnb_ref,
                gng_ref, gnb_ref, ones_ref, wr_ref, br_ref, h1_o, u2_o, lg_o, *, tm, bounds, alpha):
    grp = _group_of_row(pl.program_id(0) * tm, bounds)
    mod = mod_ref[pl.ds(grp, 1), :]
    g1 = mod[:, 2 * D_MODEL:3 * D_MODEL]
    sh2 = mod[:, 3 * D_MODEL:4 * D_MODEL]
    sc2 = mod[:, 4 * D_MODEL:5 * D_MODEL]
    ones = ones_ref[...]
    y = yf_ref[...] + yb_ref[...]
    mu = _segsum(y, ones) * (1.0 / HEAD_DIM)
    yc = y - mu
    var = _segsum(yc * yc, ones) * (1.0 / HEAD_DIM)
    yn = yc * lax.rsqrt(var + GN_EPS) * gng_ref[...] + gnb_ref[...]
    a_out = (yn + bon_ref[...]) * g_ref[...]
    m = (_dot(a_out.astype(BF16), w_ref[0:A_WIDTH, :])
         + _dot(att_ref[...].astype(BF16), w_ref[A_WIDTH:A_WIDTH + B_WIDTH, :])
         + _dot(pool_ref[...].astype(BF16), w_ref[A_WIDTH + B_WIDTH:D_MODEL, :]))
    h1 = _layer_norm(alpha * h_ref[...] + g1 * m, lng_ref[...], lnb_ref[...])
    h1_o[...] = h1
    u2 = h1 * (1.0 + sc2) + sh2
    u2_o[...] = u2.astype(BF16)
    lg_o[...] = jnp.dot(u2, wr_ref[...], precision=HIGHEST, preferred_element_type=F32) + br_ref[...]


def _out_projection(yf, yb, g, bonus, att, pool, h, mods_l, lw, ones, *, tm, bounds, alpha):
    t = h.shape[0]
    full = lambda shape: pl.BlockSpec(shape, lambda i: (0,) * len(shape))
    rows = lambda w: pl.BlockSpec((tm, w), lambda i: (i, 0))
    return pl.pallas_call(
        functools.partial(_out_kernel, tm=tm, bounds=bounds, alpha=alpha),
        grid=(t // tm,),
        in_specs=[rows(A_WIDTH), rows(A_WIDTH), rows(A_WIDTH), rows(A_WIDTH), rows(B_WIDTH), rows(C_WIDTH),
                  rows(D_MODEL), full((8, 6 * D_MODEL)), full((D_MODEL, D_MODEL)), full((1, D_MODEL)),
                  full((1, D_MODEL)), full((1, A_WIDTH)), full((1, A_WIDTH)), full((A_WIDTH, A_WIDTH)),
                  full((D_MODEL, ROUTER_LANES)), full((1, ROUTER_LANES))],
        out_specs=[rows(D_MODEL), rows(D_MODEL), rows(ROUTER_LANES)],
        out_shape=[jax.ShapeDtypeStruct((t, D_MODEL), F32), jax.ShapeDtypeStruct((t, D_MODEL), BF16),
                   jax.ShapeDtypeStruct((t, ROUTER_LANES), F32)],
        compiler_params=_cparams(("arbitrary",)),
    )(yf, yb, g, bonus, att, pool, h, mods_l, lw['w_out'], lw['ln1_g'], lw['ln1_b'], lw['gn_g'], lw['gn_b'],
      ones, lw['w_router'], lw['b_router'])


def _router_gates(logits):
    lane = lax.broadcasted_iota(I32, logits.shape, 1)
    lane_f = lane.astype(F32)
    neg = -jnp.inf
    big = 1e9

    def first_lane(mask):
        return jnp.min(jnp.where(mask, lane_f, big), axis=-1, keepdims=True).astype(I32)

    is_grp = lane < N_GROUPS
    is_exp = (lane >= N_GROUPS) & (lane < N_GROUPS + N_EXPERTS)
    gmax = jnp.max(jnp.where(is_grp, logits, neg), axis=-1, keepdims=True)
    sel = first_lane(is_grp & (logits == gmax))
    gsum = jnp.sum(jnp.where(is_grp, jnp.exp(logits - gmax), 0.0), axis=-1, keepdims=True)
    gsel = 1.0 / gsum
    first_exp = N_GROUPS + sel * EXP_PER_GROUP
    in_sel = (lane >= first_exp) & (lane < first_exp + EXP_PER_GROUP)
    v1 = jnp.max(jnp.where(in_sel, logits, neg), axis=-1, keepdims=True)
    i1 = first_lane(in_sel & (logits == v1))
    rest = in_sel & (lane != i1)
    v2 = jnp.max(jnp.where(rest, logits, neg), axis=-1, keepdims=True)
    i2 = first_lane(rest & (logits == v2))
    e2 = jnp.exp(v2 - v1)
    t1 = (1.0 / (1.0 + e2)) * gsel
    t2 = (e2 / (1.0 + e2)) * gsel
    return jnp.where(lane == i1, t1, 0.0) + jnp.where(lane == i2, t2, 0.0)


def _moe_kernel(x_ref, lg_ref, w1_ref, w3_ref, w2_ref, h1_ref, mod_ref, lng_ref, lnb_ref, o_ref,
                acc_ref, gate_ref, *, tm, bounds, alpha):
    e = pl.program_id(1)

    @pl.when(e == 0)
    def _():
        gate_ref[...] = _router_gates(lg_ref[...])
        acc_ref[...] = jnp.zeros_like(acc_ref)

    gates = gate_ref[...]
    lane = lax.broadcasted_iota(I32, gates.shape, 1)
    ge = jnp.sum(jnp.where(lane == e + N_GROUPS, gates, 0.0), axis=-1, keepdims=True)
    x = x_ref[...]
    h1 = _dot(x, w1_ref[...])
    h3 = _dot(x, w3_ref[...])
    hid = (h1 * _sigmoid(h1)) * h3 * ge
    acc_ref[...] += _dot(hid.astype(BF16), w2_ref[...])

    @pl.when(e == pl.num_programs(1) - 1)
    def _():
        grp = _group_of_row(pl.program_id(0) * tm, bounds)
        g2 = mod_ref[pl.ds(grp, 1), 5 * D_MODEL:6 * D_MODEL]
        o_ref[...] = _layer_norm(alpha * h1_ref[...] + g2 * acc_ref[...], lng_ref[...], lnb_ref[...])


def _moe(u2, logits, h1, mods_l, lw, *, tm, bounds, alpha):
    t = h1.shape[0]
    full = lambda shape: pl.BlockSpec(shape, lambda i, e: (0,) * len(shape))
    rows = lambda w: pl.BlockSpec((tm, w), lambda i, e: (i, 0))
    return pl.pallas_call(
        functools.partial(_moe_kernel, tm=tm, bounds=bounds, alpha=alpha),
        grid=(t // tm, N_EXPERTS),
        in_specs=[rows(D_MODEL), rows(ROUTER_LANES),
                  pl.BlockSpec((None, D_MODEL, D_EXPERT), lambda i, e: (e, 0, 0)),
                  pl.BlockSpec((None, D_MODEL, D_EXPERT), lambda i, e: (e, 0, 0)),
                  pl.BlockSpec((None, D_EXPERT, D_MODEL), lambda i, e: (e, 0, 0)),
                  rows(D_MODEL), full((8, 6 * D_MODEL)), full((1, D_MODEL)), full((1, D_MODEL))],
        out_specs=rows(D_MODEL),
        out_shape=jax.ShapeDtypeStruct((t, D_MODEL), F32),
        scratch_shapes=[pltpu.VMEM((tm, D_MODEL), F32), pltpu.VMEM((tm, ROUTER_LANES), F32)],
        compiler_params=_cparams(("arbitrary", "arbitrary")),
    )(u2, logits, lw['moe_w1'], lw['moe_w3'], lw['moe_w2'], h1, mods_l, lw['ln2_g'], lw['ln2_b'])


def _rope_tables(n_ctx, dec_batch, dec_seq):
    n = HEAD_DIM // 4
    pos = jnp.arange(dec_seq)
    row = (pos // GRID_W).astype(F32)
    col = (pos % GRID_W).astype(F32)
    inv = 1.0 / (ROPE_THETA ** (jnp.arange(n, dtype=F32) / n))
    ra, ca = row[:, None] * inv, col[:, None] * inv
    cos64 = jnp.concatenate([jnp.cos(ra), jnp.cos(ra), jnp.cos(ca), jnp.cos(ca)], axis=-1)
    sin64 = jnp.concatenate([-jnp.sin(ra), jnp.sin(ra), -jnp.sin(ca), jnp.sin(ca)], axis=-1)
    cos_s = jnp.tile(jnp.concatenate([cos64, cos64], axis=-1), (dec_batch, 1))
    sin_s = jnp.tile(jnp.concatenate([sin64, sin64], axis=-1), (dec_batch, 1))
    cos = jnp.concatenate([jnp.ones((n_ctx, 128), F32), cos_s], axis=0)
    sin = jnp.concatenate([jnp.zeros((n_ctx, 128), F32), sin_s], axis=0)
    return cos, sin


def _block_diag2(w):
    z = jnp.zeros_like(w[0])
    return jnp.concatenate([jnp.concatenate([w[0], z], axis=1), jnp.concatenate([z, w[1]], axis=1)], axis=0)


def _layer_weights(l, p):
    d = D_MODEL
    pool_blk = jnp.zeros((C_WIDTH, C_WIDTH), F32)
    for gi in range(len(POOL_WINDOWS)):
        pool_blk = pool_blk.at[gi * 64:(gi + 1) * 64, gi * 64:(gi + 1) * 64].set(p['pool_w'][l, gi])
    w_router = jnp.zeros((d, ROUTER_LANES), F32)
    w_router = w_router.at[:, 0:N_GROUPS].set(p['rt_grp_w'][l])
    w_router = w_router.at[:, N_GROUPS:N_GROUPS + N_EXPERTS].set(p['rt_exp_w'][l])
    b_router = jnp.zeros((1, ROUTER_LANES), F32)
    b_router = b_router.at[0, 0:N_GROUPS].set(p['rt_grp_b'][l])
    b_router = b_router.at[0, N_GROUPS:N_GROUPS + N_EXPERTS].set(p['rt_exp_b'][l])
    return {
        'w_in': p['w_in'][l].astype(BF16),
        'w_out': p['w_out'][l].astype(BF16),
        'w2blk': _block_diag2(p['rw_w2'][l]),
        'w0': p['rw_w0'][l].reshape(1, 2 * A_WIDTH),
        'a2blk': _block_diag2(p['rw_a2'][l]),
        'a0': p['rw_a0'][l].reshape(1, 2 * A_WIDTH),
        'g2': p['rw_g2'][l],
        'kkw': p['rw_kk'][l].reshape(1, A_WIDTH),
        'ka': jnp.tile(p['rw_ka'][l].reshape(1, A_WIDTH), (1, 2)),
        'rk': p['rw_rk'][l].reshape(1, A_WIDTH),
        'gn_g': p['rw_gn_g'][l].reshape(1, A_WIDTH),
        'gn_b': p['rw_gn_b'][l].reshape(1, A_WIDTH),
        'qn': jnp.tile(p['q_norm'][l].reshape(1, HEAD_DIM), (1, B_Q_HEADS)),
        'knw': jnp.tile(p['k_norm'][l].reshape(1, HEAD_DIM), (1, B_KV_HEADS)),
        'pool_blk': pool_blk.astype(BF16),
        'pool_scale': p['pool_scale'][l].reshape(1, C_WIDTH),
        'ln1_g': p['ln1_g'][l].reshape(1, d), 'ln1_b': p['ln1_b'][l].reshape(1, d),
        'ln2_g': p['ln2_g'][l].reshape(1, d), 'ln2_b': p['ln2_b'][l].reshape(1, d),
        'w_router': w_router, 'b_router': b_router,
        'moe_w1': p['moe_w1'][l].astype(BF16),
        'moe_w3': p['moe_w3'][l].astype(BF16),
        'moe_w2': p['moe_w2'][l].astype(BF16),
    }


def _forward(x_prompt, x_sample, state_rwkv, cache_k, cache_v, c, c_ctx, p):
    batch, seq, d = x_prompt.shape
    dec_batch, dec_seq, _ = x_sample.shape
    depth = p['w_in'].shape[0]
    past = cache_k.shape[2]
    n_ctx = batch * seq
    n_tok = n_ctx + dec_batch * dec_seq
    alpha = (2 * depth) ** 0.25
    bounds = tuple(n_ctx + j * dec_seq for j in range(dec_batch))
    tm = min(256, seq)
    tm_moe = math.gcd(n_ctx, dec_seq, 1024)
    tq_ctx = min(256, seq)
    tq_dec = min(256, dec_seq)

    cond8 = jnp.zeros((8, d), F32).at[0].set(c_ctx).at[1:1 + dec_batch].set(c)
    mods = _modulation(cond8, p['w_mod'], p['b_mod'])
    cos, sin = _rope_tables(n_ctx, dec_batch, dec_seq)
    head_id = jnp.arange(A_WIDTH) // HEAD_DIM
    ones = (head_id[:, None] == head_id[None, :]).astype(BF16)

    h = jnp.concatenate([x_prompt.reshape(n_ctx, d), x_sample.reshape(dec_batch * dec_seq, d)], axis=0)
    new_s, new_k, new_v = [], [], []
    for l in range(depth):
        lw = _layer_weights(l, p)
        (r, v, kk, logw, kd, bvec, g, bonus, q, k_att, v_att, pp) = _in_projection(
            h, mods[l], lw, cos, sin, ones, tm=tm, bounds=bounds)
        scan_ops = (r, v, kk, logw, kd, bvec)
        yf_c, yb_c, s_fin = _rwkv_scan(scan_ops, None, row0=0, nseq=batch, seqlen=seq, want_state=True)
        yf_d, yb_d = _rwkv_scan(scan_ops, state_rwkv[:, l], row0=n_ctx, nseq=dec_batch, seqlen=dec_seq,
                                want_state=False)
        yf = jnp.concatenate([yf_c, yf_d], axis=0)
        yb = jnp.concatenate([yb_c, yb_d], axis=0)
        att_c = _attention(q, k_att, v_att, n_ctx, row0=0, nseq=batch, seqlen=seq, tq=tq_ctx,
                           keys_in_token_array=True)
        k_dec = jnp.concatenate([k_att[n_ctx:].reshape(dec_batch, dec_seq, KV_WIDTH),
                                 cache_k[:, l].reshape(dec_batch, past, KV_WIDTH)], axis=1)
        v_dec = jnp.concatenate([v_att[n_ctx:].reshape(dec_batch, dec_seq, KV_WIDTH),
                                 cache_v[:, l].reshape(dec_batch, past, KV_WIDTH)], axis=1)
        att_d = _attention(q, k_dec, v_dec, dec_batch * dec_seq, row0=n_ctx, nseq=dec_batch, seqlen=dec_seq,
                           tq=tq_dec, keys_in_token_array=False)
        att = jnp.concatenate([att_c, att_d], axis=0)
        pool_c = _pool(pp, lw['pool_blk'], lw['pool_scale'], row0=0, nseq=batch, seqlen=seq)
        pool_d = _pool(pp, lw['pool_blk'], lw['pool_scale'], row0=n_ctx, nseq=dec_batch, seqlen=dec_seq)
        pool = jnp.concatenate([pool_c, pool_d], axis=0)
        h1, u2, logits = _out_projection(yf, yb, g, bonus, att, pool, h, mods[l], lw, ones,
                                         tm=tm, bounds=bounds, alpha=alpha)
        h = _moe(u2, logits, h1, mods[l], lw, tm=tm_moe, bounds=bounds, alpha=alpha)
        new_s.append(s_fin)
        new_k.append(k_att[:n_ctx].reshape(batch, seq, B_KV_HEADS, HEAD_DIM))
        new_v.append(v_att[:n_ctx].reshape(batch, seq, B_KV_HEADS, HEAD_DIM))
    return (h[:n_ctx].reshape(batch, seq, d), h[n_ctx:].reshape(dec_batch, dec_seq, d),
            jnp.stack(new_s, axis=1), jnp.stack(new_k, axis=1), jnp.stack(new_v, axis=1))


def kernel(x_prompt, x_sample, state_rwkv, cache_k, cache_v, c, c_ctx, w_in, w_out, w_mod, b_mod, ln1_g, ln1_b, ln2_g, ln2_b, rw_w0, rw_w2, rw_a0, rw_a2, rw_g2, rw_kk, rw_ka, rw_rk, rw_gn_g, rw_gn_b, q_norm, k_norm, pool_w, pool_scale, rt_grp_w, rt_grp_b, rt_exp_w, rt_exp_b, moe_w1, moe_w3, moe_w2):
    p = dict(w_in=w_in, w_out=w_out, w_mod=w_mod, b_mod=b_mod, ln1_g=ln1_g, ln1_b=ln1_b, ln2_g=ln2_g,
             ln2_b=ln2_b, rw_w0=rw_w0, rw_w2=rw_w2, rw_a0=rw_a0, rw_a2=rw_a2, rw_g2=rw_g2, rw_kk=rw_kk,
             rw_ka=rw_ka, rw_rk=rw_rk, rw_gn_g=rw_gn_g, rw_gn_b=rw_gn_b, q_norm=q_norm, k_norm=k_norm,
             pool_w=pool_w, pool_scale=pool_scale, rt_grp_w=rt_grp_w, rt_grp_b=rt_grp_b, rt_exp_w=rt_exp_w,
             rt_exp_b=rt_exp_b, moe_w1=moe_w1, moe_w3=moe_w3, moe_w2=moe_w2)
    return _forward(x_prompt, x_sample, state_rwkv, cache_k, cache_v, c, c_ctx, p)
```

```python
import functools
import math

import jax
import jax.numpy as jnp
from jax import lax
from jax.experimental import pallas as pl
from jax.experimental.pallas import tpu as pltpu

F32 = jnp.float32
BF16 = jnp.bfloat16
I32 = jnp.int32
HIGHEST = lax.Precision.HIGHEST

D_MODEL = 1024
GRID_W = 64
HEAD_DIM = 64
A_HEADS = 6
A_WIDTH = A_HEADS * HEAD_DIM
LORA_W = 64
LORA_A = 64
LORA_G = 128
N_DIR = 2
DECAY_SCALE = 0.606531
B_Q_HEADS = 6
B_KV_HEADS = 2
B_WIDTH = B_Q_HEADS * HEAD_DIM
KV_WIDTH = B_KV_HEADS * HEAD_DIM
ROPE_THETA = 10000.0
POOL_WINDOWS = (2, 4, 8, 16)
C_WIDTH = 256
IN_COLS = 2432
N_GROUPS = 4
EXP_PER_GROUP = 4
N_EXPERTS = 16
D_EXPERT = 256
LN_EPS = 1e-5
GN_EPS = 64e-5
QK_EPS = 1e-6

_O_R, _O_K, _O_V = 0, 384, 768
_O_LW, _O_LA, _O_LG = 1152, 1280, 1408
_O_Q, _O_KB, _O_VB, _O_P = 1536, 1920, 2048, 2176

CHUNK = 64
ROUTER_LANES = 128
VMEM_LIMIT = 56 * 1024 * 1024


def _cparams(sem):
    return pltpu.CompilerParams(dimension_semantics=sem, vmem_limit_bytes=VMEM_LIMIT)


def _sigmoid(x):
    return 1.0 / (1.0 + jnp.exp(-x))


def _dot(a, b):
    return jnp.dot(a, b, preferred_element_type=F32)


def _dot_nt(a, b):
    return lax.dot_general(a, b, (((1,), (1,)), ((), ())), preferred_element_type=F32)


def _segsum(x, g_ones):
    hi = x.astype(BF16)
    lo = (x - hi.astype(F32)).astype(BF16)
    return _dot(hi, g_ones) + _dot(lo, g_ones)


def _rope128(x, cos, sin):
    lane = lax.broadcasted_iota(I32, x.shape, 1)
    first = (lane % 32) < 16
    up = pltpu.roll(x, 128 - 16, 1)
    dn = pltpu.roll(x, 16, 1)
    return x * cos + jnp.where(first, up, dn) * sin


def _layer_norm(z, g, b):
    mu = jnp.mean(z, axis=-1, keepdims=True)
    zc = z - mu
    var = jnp.mean(zc * zc, axis=-1, keepdims=True)
    return zc * lax.rsqrt(var + LN_EPS) * g + b


def _mod_kernel(c_ref, w_ref, b_ref, o_ref):
    x = c_ref[...]
    x = x * _sigmoid(x)
    o_ref[...] = jnp.dot(x, w_ref[...], precision=HIGHEST, preferred_element_type=F32) + b_ref[...]


def _modulation(cond8, w_mod, b_mod):
    depth = w_mod.shape[0]
    tn = 1536
    return pl.pallas_call(
        _mod_kernel,
        grid=(depth, 6 * D_MODEL // tn),
        in_specs=[
            pl.BlockSpec((8, D_MODEL), lambda l, j: (0, 0)),
            pl.BlockSpec((None, D_MODEL, tn), lambda l, j: (l, 0, j)),
            pl.BlockSpec((None, 1, tn), lambda l, j: (l, 0, j)),
        ],
        out_specs=pl.BlockSpec((None, 8, tn), lambda l, j: (l, 0, j)),
        out_shape=jax.ShapeDtypeStruct((depth, 8, 6 * D_MODEL), F32),
        compiler_params=_cparams(("arbitrary", "arbitrary")),
        name="modulation",
    )(cond8, w_mod, b_mod.reshape(depth, 1, 6 * D_MODEL))


def _group_of_row(row0, bounds):
    g = jnp.zeros((), I32)
    for b in bounds:
        g = g + jnp.where(row0 >= b, 1, 0).astype(I32)
    return g


def _in_kernel(h_ref, mod_ref, w_ref, w2_ref, w0_ref, a2_ref, a0_ref, g2_ref, kkw_ref, ka_ref,
               rk_ref, qn_ref, knw_ref, cos_ref, sin_ref, ones_ref,
               r_o, v_o, kk_o, lw_o, kd_o, b_o, g_o, bon_o, q_o, k_o, vb_o, p_o, *, tm, bounds):
    grp = _group_of_row(pl.program_id(0) * tm, bounds)
    mod = mod_ref[pl.ds(grp, 1), :]
    sh1 = mod[:, 0:D_MODEL]
    sc1 = mod[:, D_MODEL:2 * D_MODEL]
    u = h_ref[...] * (1.0 + sc1) + sh1
    proj = _dot(u.astype(BF16), w_ref[...])
    ones = ones_ref[...]

    r = proj[:, _O_R:_O_R + A_WIDTH]
    k = proj[:, _O_K:_O_K + A_WIDTH]
    v = proj[:, _O_V:_O_V + A_WIDTH]
    lo_w = proj[:, _O_LW:_O_LW + 128]
    lo_a = proj[:, _O_LA:_O_LA + 128]
    lo_g = proj[:, _O_LG:_O_LG + 128]

    w_pre = w0_ref[...] + jnp.dot(jnp.tanh(lo_w), w2_ref[...], precision=HIGHEST, preferred_element_type=F32)
    logw = -DECAY_SCALE * _sigmoid(w_pre)
    a = _sigmoid(a0_ref[...] + jnp.dot(lo_a, a2_ref[...], precision=HIGHEST, preferred_element_type=F32))
    g = jnp.dot(_sigmoid(lo_g), g2_ref[...], precision=HIGHEST, preferred_element_type=F32)

    kk = k * kkw_ref[...]
    kk = kk * lax.rsqrt(jnp.maximum(_segsum(kk * kk, ones), 1e-24))
    k2 = jnp.concatenate([k, k], axis=1)
    kd = k2 * (1.0 + (a - 1.0) * ka_ref[...])
    bvec = jnp.concatenate([kk, kk], axis=1) * a
    kd_sum = kd[:, 0:A_WIDTH] + kd[:, A_WIDTH:2 * A_WIDTH]
    bonus = _segsum(r * kd_sum * rk_ref[...], ones) * v

    r_o[...] = r
    v_o[...] = v
    kk_o[...] = kk
    lw_o[...] = logw
    kd_o[...] = kd
    b_o[...] = bvec
    g_o[...] = g
    bon_o[...] = bonus

    cos = cos_ref[...]
    sin = sin_ref[...]
    q = proj[:, _O_Q:_O_Q + B_WIDTH]
    qn = q * lax.rsqrt(_segsum(q * q, ones) * (1.0 / HEAD_DIM) + QK_EPS) * qn_ref[...]
    scale = 1.0 / math.sqrt(HEAD_DIM)
    q_o[...] = jnp.concatenate(
        [_rope128(qn[:, j * 128:(j + 1) * 128], cos, sin) for j in range(B_WIDTH // 128)], axis=1) * scale
    kb = proj[:, _O_KB:_O_KB + KV_WIDTH]
    kn = kb * lax.rsqrt(_segsum(kb * kb, ones[0:KV_WIDTH, 0:KV_WIDTH]) * (1.0 / HEAD_DIM) + QK_EPS) * knw_ref[...]
    k_o[...] = _rope128(kn, cos, sin)
    vb_o[...] = proj[:, _O_VB:_O_VB + KV_WIDTH]
    p_o[...] = proj[:, _O_P:_O_P + C_WIDTH]


def _in_projection(h, mods_l, lw, cos, sin, ones, *, tm, bounds):
    t = h.shape[0]
    full = lambda shape: pl.BlockSpec(shape, lambda i: (0,) * len(shape))
    rows = lambda w: pl.BlockSpec((tm, w), lambda i: (i, 0))
    widths = (A_WIDTH, A_WIDTH, A_WIDTH, 2 * A_WIDTH, 2 * A_WIDTH, 2 * A_WIDTH, A_WIDTH, A_WIDTH,
              B_WIDTH, KV_WIDTH, KV_WIDTH, C_WIDTH)
    return pl.pallas_call(
        functools.partial(_in_kernel, tm=tm, bounds=bounds),
        grid=(t // tm,),
        in_specs=[
            rows(D_MODEL), full((8, 6 * D_MODEL)), full((D_MODEL, IN_COLS)),
            full((128, 2 * A_WIDTH)), full((1, 2 * A_WIDTH)), full((128, 2 * A_WIDTH)), full((1, 2 * A_WIDTH)),
            full((LORA_G, A_WIDTH)), full((1, A_WIDTH)), full((1, 2 * A_WIDTH)), full((1, A_WIDTH)),
            full((1, B_WIDTH)), full((1, KV_WIDTH)), rows(128), rows(128), full((A_WIDTH, A_WIDTH)),
        ],
        out_specs=[rows(w) for w in widths],
        out_shape=[jax.ShapeDtypeStruct((t, w), F32) for w in widths],
        compiler_params=_cparams(("arbitrary",)),
        name="in_projection",
    )(h, mods_l, lw['w_in'], lw['w2blk'], lw['w0'], lw['a2blk'], lw['a0'], lw['g2'], lw['kkw'], lw['ka'],
      lw['rk'], lw['qn'], lw['knw'], cos, sin, ones)


def _split3(x):
    hi = x.astype(BF16)
    r1 = x - hi.astype(F32)
    mid = r1.astype(BF16)
    lo = (r1 - mid.astype(F32)).astype(BF16)
    return hi, mid, lo


def _scan_kernel(*refs, sb, nb, zero_init, want_state):
    fwd = refs[0:6]
    bwd = refs[6:12]
    pos = 12
    if not zero_init:
        s0_ref = refs[pos]
        pos += 1
    y_outs = (refs[pos], refs[pos + 1])
    pos += 2
    if want_state:
        sfin_o = refs[pos]
        pos += 1
    h_ref = refs[pos]

    step = pl.program_id(1)
    nstep = pl.num_programs(1)
    chains = [(q, d, h) for q in range(sb) for d in range(N_DIR) for h in range(A_HEADS)]

    @pl.when(step == 0)
    def _():
        if zero_init:
            h_ref[...] = jnp.zeros_like(h_ref)
        else:
            for (q, d, h) in chains:
                h_ref[q, d, h] = s0_ref[q, d, h].T

    c2 = 2 * CHUNK
    ti = lax.broadcasted_iota(I32, (CHUNK, CHUNK), 0)
    si = lax.broadcasted_iota(I32, (CHUNK, CHUNK), 1)
    eye = (ti == si).astype(F32)
    t2 = lax.broadcasted_iota(I32, (c2, c2), 0)
    s2 = lax.broadcasted_iota(I32, (c2, c2), 1) % CHUNK
    incl_off = jnp.where(t2 < CHUNK, 0, 1)
    t2 = t2 % CHUNK
    mask2 = (s2 < t2 + incl_off, s2 > t2 - incl_off)
    incl16 = ((si <= ti).astype(BF16), (si >= ti).astype(BF16))

    lhs, bkt, gcol, t_inv, a_rbk, v16 = {}, {}, {}, {}, {}, {}
    amat = {}
    for q in range(sb):
        for d, ins in enumerate((fwd, bwd)):
            r_ref, v_ref, kk_ref, lw_ref, kd_ref, b_ref = ins
            for j in range(nb):
                rows = pl.ds(j * CHUNK, CHUNK)
                lw = lw_ref[q, rows, :]
                hi, mid, lo = _split3(lw)
                cs = _dot(incl16[d], hi) + _dot(incl16[d], mid) + _dot(incl16[d], lo)
                gam = jnp.exp(cs)
                gam_prev = jnp.exp(cs - lw)
                gam_inv = jnp.exp(-cs)
                gam_end = jnp.exp(cs[CHUNK - 1:CHUNK, :] if d == 0 else cs[0:1, :])
                a_hat = -kk_ref[q, rows, :] * gam_prev
                r_hat = r_ref[q, rows, :] * gam
                b_hat = b_ref[q, rows, :] * gam_inv
                k_hat = kd_ref[q, rows, :] * gam_inv
                v_all = v_ref[q, rows, :]
                for h in range(A_HEADS):
                    key = (q, d, h, j)
                    sl = slice(h * HEAD_DIM, (h + 1) * HEAD_DIM)
                    lhs[key] = jnp.concatenate([a_hat[:, sl], r_hat[:, sl]], axis=0).astype(BF16)
                    bk = jnp.concatenate([b_hat[:, sl], k_hat[:, sl]], axis=0)
                    amat[key] = _dot_nt(lhs[key], bk.astype(BF16))
                    g_col = jnp.sum(eye * gam_end[:, sl], axis=1, keepdims=True)
                    gcol[key] = g_col
                    bkt[key] = (bk.T * g_col).astype(BF16)
                    v16[key] = v_all[:, sl].astype(BF16)
    keys = list(amat.keys())
    npow, tcur = {}, {}
    for key in keys:
        d = key[1]
        am = jnp.where(mask2[d], amat[key], 0.0)
        npow[key] = am[0:CHUNK, 0:CHUNK]
        tcur[key] = eye + npow[key]
        a_rbk[key] = am[CHUNK:c2, :].astype(BF16)
        amat[key] = am[0:CHUNK, CHUNK:c2].astype(BF16)
    nsq = int(math.log2(CHUNK)) - 1
    for key in keys:
        p16 = npow[key].astype(BF16)
        npow[key] = _dot(p16, p16)
    for _ in range(nsq - 1):
        for key in keys:
            p16 = npow[key].astype(BF16)
            res = _dot(p16, jnp.concatenate([p16, tcur[key].astype(BF16)], axis=1))
            npow[key] = res[:, 0:CHUNK]
            tcur[key] = tcur[key] + res[:, CHUNK:c2]
    for key in keys:
        t16 = tcur[key].astype(BF16)
        t_inv[key] = (tcur[key] + _dot(npow[key].astype(BF16), t16)).astype(BF16)

    hcur = {ch: h_ref[ch[0], ch[1], ch[2]] for ch in chains}
    ys = {}
    for jj in range(nb):
        z, u = {}, {}
        kof = lambda ch: (ch[0], ch[1], ch[2], jj if ch[1] == 0 else nb - 1 - jj)
        for ch in chains:
            z[ch] = _dot(lhs[kof(ch)], hcur[ch].astype(BF16))
        for ch in chains:
            key = kof(ch)
            x = z[ch][0:CHUNK] + _dot(amat[key], v16[key])
            u[ch] = _dot(t_inv[key], x.astype(BF16))
        for ch in chains:
            key = kof(ch)
            uv = jnp.concatenate([u[ch].astype(BF16), v16[key]], axis=0)
            res = _dot(jnp.concatenate([a_rbk[key], bkt[key]], axis=0), uv)
            ys[key] = z[ch][CHUNK:c2] + res[0:CHUNK]
            hcur[ch] = hcur[ch] * gcol[key] + res[CHUNK:CHUNK + HEAD_DIM]
    for ch in chains:
        h_ref[ch[0], ch[1], ch[2]] = hcur[ch]
    for q in range(sb):
        for d in range(N_DIR):
            for j in range(nb):
                y_outs[d][q, pl.ds(j * CHUNK, CHUNK), :] = jnp.concatenate(
                    [ys[(q, d, h, j)] for h in range(A_HEADS)], axis=1)

    if want_state:
        @pl.when(step == nstep - 1)
        def _():
            for (q, d, h) in chains:
                sfin_o[q, d, h] = hcur[(q, d, h)].T


def _rwkv_scan(ops, s0, *, row0, nseq, seqlen, sb, nb, want_state):
    g = nb * CHUNK
    nstep = seqlen // g
    zero_init = s0 is None
    ops3 = [a.reshape(a.shape[0] // seqlen, seqlen, a.shape[1]) for a in ops]
    base = row0 // (seqlen * sb)
    fidx = lambda col: (lambda s, c: (base + s, c, col))
    bidx = lambda col: (lambda s, c: (base + s, nstep - 1 - c, col))
    blk = lambda im: pl.BlockSpec((sb, g, A_WIDTH), im)
    in_specs = [blk(fidx(0))] * 6 + [blk(bidx(0))] * 3 + [blk(bidx(1))] * 3
    args = ops3 + ops3
    state_spec = pl.BlockSpec((sb, N_DIR, A_HEADS, HEAD_DIM, HEAD_DIM), lambda s, c: (s, 0, 0, 0, 0))
    if not zero_init:
        in_specs.append(state_spec)
        args.append(s0)
    out_specs = [blk(lambda s, c: (s, c, 0)), blk(lambda s, c: (s, nstep - 1 - c, 0))]
    out_shape = [jax.ShapeDtypeStruct((nseq, seqlen, A_WIDTH), F32)] * 2
    if want_state:
        out_specs.append(state_spec)
        out_shape.append(jax.ShapeDtypeStruct((nseq, N_DIR, A_HEADS, HEAD_DIM, HEAD_DIM), F32))
    return pl.pallas_call(
        functools.partial(_scan_kernel, sb=sb, nb=nb, zero_init=zero_init, want_state=want_state),
        grid=(nseq // sb, nstep),
        in_specs=in_specs,
        out_specs=out_specs,
        out_shape=out_shape,
        scratch_shapes=[pltpu.VMEM((sb, N_DIR, A_HEADS, HEAD_DIM, HEAD_DIM), F32)],
        compiler_params=_cparams(("arbitrary", "arbitrary")),
        name="rwkv_scan_ctx" if zero_init else "rwkv_scan_dec",
    )(*args)


def _attn_kernel(q_ref, k_ref, v_ref, o_ref, *, tq):
    grp = B_Q_HEADS // B_KV_HEADS
    q = q_ref[...]
    kall = k_ref[...]
    vall = v_ref[...]
    outs = []
    for hk in range(B_KV_HEADS):
        ks = kall[:, hk * HEAD_DIM:(hk + 1) * HEAD_DIM].astype(BF16)
        vs = vall[:, hk * HEAD_DIM:(hk + 1) * HEAD_DIM].astype(BF16)
        qs = jnp.concatenate(
            [q[:, (hk * grp + g) * HEAD_DIM:(hk * grp + g + 1) * HEAD_DIM] for g in range(grp)], axis=0)
        s = _dot_nt(qs.astype(BF16), ks)
        m = jnp.max(s, axis=-1, keepdims=True)
        p = jnp.exp(s - m)
        den = jnp.sum(p, axis=-1, keepdims=True)
        o = _dot(p.astype(BF16), vs) / den
        outs.extend(o[g * tq:(g + 1) * tq] for g in range(grp))
    o_ref[...] = jnp.concatenate(outs, axis=1)


def _attention(q, k_arr, v_arr, out_rows, *, row0, nseq, seqlen, tq, keys_in_token_array):
    nq = seqlen // tq
    base = row0 // tq
    if keys_in_token_array:
        lk = seqlen
        kbase = row0 // seqlen
        kspec = pl.BlockSpec((lk, KV_WIDTH), lambda s, i: (kbase + s, 0))
    else:
        lk = k_arr.shape[1]
        kspec = pl.BlockSpec((None, lk, KV_WIDTH), lambda s, i: (s, 0, 0))
    return pl.pallas_call(
        functools.partial(_attn_kernel, tq=tq),
        grid=(nseq, nq),
        in_specs=[pl.BlockSpec((tq, B_WIDTH), lambda s, i: (base + s * nq + i, 0)), kspec, kspec],
        out_specs=pl.BlockSpec((tq, B_WIDTH), lambda s, i: (s * nq + i, 0)),
        out_shape=jax.ShapeDtypeStruct((out_rows, B_WIDTH), F32),
        compiler_params=_cparams(("arbitrary", "arbitrary")),
        name="attention_ctx" if keys_in_token_array else "attention_dec",
    )(q, k_arr, v_arr)


def _pool_kernel(p_ref, w_ref, sc_ref, o_ref, *, seqlen):
    x = p_ref[...]
    t = lax.broadcasted_iota(I32, x.shape, 0)
    lane = lax.broadcasted_iota(I32, x.shape, 1)

    def down(a, k):
        return jnp.where(t >= k, pltpu.roll(a, k, 0), 0.0)

    def up(a, k):
        return jnp.where(t < seqlen - k, pltpu.roll(a, seqlen - k, 0), 0.0)

    past = [x]
    futr = [x]
    for j in range(len(POOL_WINDOWS) - 1):
        past.append(past[j] + down(past[j], 2 ** j))
        futr.append(futr[j] + up(futr[j], 2 ** j))
    win_sum = jnp.zeros_like(x)
    cnt = jnp.zeros_like(x)
    for gi, win in enumerate(POOL_WINDOWS):
        half = win // 2
        in_group = (lane // HEAD_DIM) == gi
        win_sum = jnp.where(in_group, down(past[gi], 1) + futr[gi], win_sum)
        n = (jnp.minimum(t + half, seqlen) - jnp.maximum(t - half, 0)).astype(F32)
        cnt = jnp.where(in_group, n, cnt)
    d = win_sum / cnt - x
    o_ref[...] = _dot(d.astype(BF16), w_ref[...]) * sc_ref[...]


def _pool(p, w_blk, scale, *, row0, nseq, seqlen):
    base = row0 // seqlen
    return pl.pallas_call(
        functools.partial(_pool_kernel, seqlen=seqlen),
        grid=(nseq,),
        in_specs=[pl.BlockSpec((seqlen, C_WIDTH), lambda s: (base + s, 0)),
                  pl.BlockSpec((C_WIDTH, C_WIDTH), lambda s: (0, 0)),
                  pl.BlockSpec((1, C_WIDTH), lambda s: (0, 0))],
        out_specs=pl.BlockSpec((seqlen, C_WIDTH), lambda s: (s, 0)),
        out_shape=jax.ShapeDtypeStruct((nseq * seqlen, C_WIDTH), F32),
        compiler_params=_cparams(("arbitrary",)),
        name="pool_mixer",
    )(p, w_blk, scale)


def _out_kernel(yf_ref, yb_ref, g_ref, bon_ref, att_ref, pool_ref, h_ref, mod_ref, w_ref, lng_ref, lnb_ref,
                gng_ref, gnb_ref, ones_ref, wr_ref, br_ref, h1_o, u2_o, lg_o, *, tm, bounds, alpha):
    grp = _group_of_row(pl.program_id(0) * tm, bounds)
    mod = mod_ref[pl.ds(grp, 1), :]
    g1 = mod[:, 2 * D_MODEL:3 * D_MODEL]
    sh2 = mod[:, 3 * D_MODEL:4 * D_MODEL]
    sc2 = mod[:, 4 * D_MODEL:5 * D_MODEL]
    ones = ones_ref[...]
    y = yf_ref[...] + yb_ref[...]
    mu = _segsum(y, ones) * (1.0 / HEAD_DIM)
    yc = y - mu
    var = _segsum(yc * yc, ones) * (1.0 / HEAD_DIM)
    yn = yc * lax.rsqrt(var + GN_EPS) * gng_ref[...] + gnb_ref[...]
    a_out = (yn + bon_ref[...]) * g_ref[...]
    m = (_dot(a_out.astype(BF16), w_ref[0:A_WIDTH, :])
         + _dot(att_ref[...].astype(BF16), w_ref[A_WIDTH:A_WIDTH + B_WIDTH, :])
         + _dot(pool_ref[...].astype(BF16), w_ref[A_WIDTH + B_WIDTH:D_MODEL, :]))
    h1 = _layer_norm(alpha * h_ref[...] + g1 * m, lng_ref[...], lnb_ref[...])
    h1_o[...] = h1
    u2 = h1 * (1.0 + sc2) + sh2
    u2_o[...] = u2.astype(BF16)
    lg_o[...] = jnp.dot(u2, wr_ref[...], precision=HIGHEST, preferred_element_type=F32) + br_ref[...]


def _out_projection(yf, yb, g, bonus, att, pool, h, mods_l, lw, ones, *, tm, bounds, alpha):
    t = h.shape[0]
    full = lambda shape: pl.BlockSpec(shape, lambda i: (0,) * len(shape))
    rows = lambda w: pl.BlockSpec((tm, w), lambda i: (i, 0))
    return pl.pallas_call(
        functools.partial(_out_kernel, tm=tm, bounds=bounds, alpha=alpha),
        grid=(t // tm,),
        in_specs=[rows(A_WIDTH), rows(A_WIDTH), rows(A_WIDTH), rows(A_WIDTH), rows(B_WIDTH), rows(C_WIDTH),
                  rows(D_MODEL), full((8, 6 * D_MODEL)), full((D_MODEL, D_MODEL)), full((1, D_MODEL)),
                  full((1, D_MODEL)), full((1, A_WIDTH)), full((1, A_WIDTH)), full((A_WIDTH, A_WIDTH)),
                  full((D_MODEL, ROUTER_LANES)), full((1, ROUTER_LANES))],
        out_specs=[rows(D_MODEL), rows(D_MODEL), rows(ROUTER_LANES)],
        out_shape=[jax.ShapeDtypeStruct((t, D_MODEL), F32), jax.ShapeDtypeStruct((t, D_MODEL), BF16),
                   jax.ShapeDtypeStruct((t, ROUTER_LANES), F32)],
        compiler_params=_cparams(("arbitrary",)),
        name="out_projection",
    )(yf, yb, g, bonus, att, pool, h, mods_l, lw['w_out'], lw['ln1_g'], lw['ln1_b'], lw['gn_g'], lw['gn_b'],
      ones, lw['w_router'], lw['b_router'])


def _router_gates(logits):
    lane = lax.broadcasted_iota(I32, logits.shape, 1)
    lane_f = lane.astype(F32)
    neg = -jnp.inf
    big = 1e9

    def first_lane(mask):
        return jnp.min(jnp.where(mask, lane_f, big), axis=-1, keepdims=True).astype(I32)

    is_grp = lane < N_GROUPS
    gmax = jnp.max(jnp.where(is_grp, logits, neg), axis=-1, keepdims=True)
    sel = first_lane(is_grp & (logits == gmax))
    gsum = jnp.sum(jnp.where(is_grp, jnp.exp(logits - gmax), 0.0), axis=-1, keepdims=True)
    gsel = 1.0 / gsum
    first_exp = N_GROUPS + sel * EXP_PER_GROUP
    in_sel = (lane >= first_exp) & (lane < first_exp + EXP_PER_GROUP)
    v1 = jnp.max(jnp.where(in_sel, logits, neg), axis=-1, keepdims=True)
    i1 = first_lane(in_sel & (logits == v1))
    rest = in_sel & (lane != i1)
    v2 = jnp.max(jnp.where(rest, logits, neg), axis=-1, keepdims=True)
    i2 = first_lane(rest & (logits == v2))
    e2 = jnp.exp(v2 - v1)
    t1 = (1.0 / (1.0 + e2)) * gsel
    t2 = (e2 / (1.0 + e2)) * gsel
    return jnp.where(lane == i1, t1, 0.0) + jnp.where(lane == i2, t2, 0.0)


def _moe_kernel(x_ref, lg_ref, w1_ref, w3_ref, w2_ref, h1_ref, mod_ref, lng_ref, lnb_ref, o_ref,
                acc_ref, gate_ref, *, tm, bounds, alpha):
    e = pl.program_id(1)

    @pl.when(e == 0)
    def _():
        gate_ref[...] = _router_gates(lg_ref[...])
        acc_ref[...] = jnp.zeros_like(acc_ref)

    gates = gate_ref[...]
    lane = lax.broadcasted_iota(I32, gates.shape, 1)
    ge = jnp.sum(jnp.where(lane == e + N_GROUPS, gates, 0.0), axis=-1, keepdims=True)
    x = x_ref[...]
    h1 = _dot(x, w1_ref[...])
    h3 = _dot(x, w3_ref[...])
    hid = (h1 * _sigmoid(h1)) * h3 * ge
    acc_ref[...] += _dot(hid.astype(BF16), w2_ref[...])

    @pl.when(e == pl.num_programs(1) - 1)
    def _():
        grp = _group_of_row(pl.program_id(0) * tm, bounds)
        g2 = mod_ref[pl.ds(grp, 1), 5 * D_MODEL:6 * D_MODEL]
        o_ref[...] = _layer_norm(alpha * h1_ref[...] + g2 * acc_ref[...], lng_ref[...], lnb_ref[...])


def _moe(u2, logits, h1, mods_l, lw, *, tm, bounds, alpha):
    t = h1.shape[0]
    full = lambda shape: pl.BlockSpec(shape, lambda i, e: (0,) * len(shape))
    rows = lambda w: pl.BlockSpec((tm, w), lambda i, e: (i, 0))
    return pl.pallas_call(
        functools.partial(_moe_kernel, tm=tm, bounds=bounds, alpha=alpha),
        grid=(t // tm, N_EXPERTS),
        in_specs=[rows(D_MODEL), rows(ROUTER_LANES),
                  pl.BlockSpec((None, D_MODEL, D_EXPERT), lambda i, e: (e, 0, 0)),
                  pl.BlockSpec((None, D_MODEL, D_EXPERT), lambda i, e: (e, 0, 0)),
                  pl.BlockSpec((None, D_EXPERT, D_MODEL), lambda i, e: (e, 0, 0)),
                  rows(D_MODEL), full((8, 6 * D_MODEL)), full((1, D_MODEL)), full((1, D_MODEL))],
        out_specs=rows(D_MODEL),
        out_shape=jax.ShapeDtypeStruct((t, D_MODEL), F32),
        scratch_shapes=[pltpu.VMEM((tm, D_MODEL), F32), pltpu.VMEM((tm, ROUTER_LANES), F32)],
        compiler_params=_cparams(("arbitrary", "arbitrary")),
        name="moe",
    )(u2, logits, lw['moe_w1'], lw['moe_w3'], lw['moe_w2'], h1, mods_l, lw['ln2_g'], lw['ln2_b'])


def _rope_tables(n_ctx, dec_batch, dec_seq):
    n = HEAD_DIM // 4
    pos = jnp.arange(dec_seq)
    row = (pos // GRID_W).astype(F32)
    col = (pos % GRID_W).astype(F32)
    inv = 1.0 / (ROPE_THETA ** (jnp.arange(n, dtype=F32) / n))
    ra, ca = row[:, None] * inv, col[:, None] * inv
    cos64 = jnp.concatenate([jnp.cos(ra), jnp.cos(ra), jnp.cos(ca), jnp.cos(ca)], axis=-1)
    sin64 = jnp.concatenate([-jnp.sin(ra), jnp.sin(ra), -jnp.sin(ca), jnp.sin(ca)], axis=-1)
    cos_s = jnp.tile(jnp.concatenate([cos64, cos64], axis=-1), (dec_batch, 1))
    sin_s = jnp.tile(jnp.concatenate([sin64, sin64], axis=-1), (dec_batch, 1))
    cos = jnp.concatenate([jnp.ones((n_ctx, 128), F32), cos_s], axis=0)
    sin = jnp.concatenate([jnp.zeros((n_ctx, 128), F32), sin_s], axis=0)
    return cos, sin


def _block_diag2(w):
    z = jnp.zeros_like(w[0])
    return jnp.concatenate([jnp.concatenate([w[0], z], axis=1), jnp.concatenate([z, w[1]], axis=1)], axis=0)


def _layer_weights(l, p):
    d = D_MODEL
    pool_blk = jnp.zeros((C_WIDTH, C_WIDTH), F32)
    for gi in range(len(POOL_WINDOWS)):
        pool_blk = pool_blk.at[gi * 64:(gi + 1) * 64, gi * 64:(gi + 1) * 64].set(p['pool_w'][l, gi])
    w_router = jnp.zeros((d, ROUTER_LANES), F32)
    w_router = w_router.at[:, 0:N_GROUPS].set(p['rt_grp_w'][l])
    w_router = w_router.at[:, N_GROUPS:N_GROUPS + N_EXPERTS].set(p['rt_exp_w'][l])
    b_router = jnp.zeros((1, ROUTER_LANES), F32)
    b_router = b_router.at[0, 0:N_GROUPS].set(p['rt_grp_b'][l])
    b_router = b_router.at[0, N_GROUPS:N_GROUPS + N_EXPERTS].set(p['rt_exp_b'][l])
    return {
        'w_in': p['w_in'][l].astype(BF16),
        'w_out': p['w_out'][l].astype(BF16),
        'w2blk': _block_diag2(p['rw_w2'][l]),
        'w0': p['rw_w0'][l].reshape(1, 2 * A_WIDTH),
        'a2blk': _block_diag2(p['rw_a2'][l]),
        'a0': p['rw_a0'][l].reshape(1, 2 * A_WIDTH),
        'g2': p['rw_g2'][l],
        'kkw': p['rw_kk'][l].reshape(1, A_WIDTH),
        'ka': jnp.tile(p['rw_ka'][l].reshape(1, A_WIDTH), (1, 2)),
        'rk': p['rw_rk'][l].reshape(1, A_WIDTH),
        'gn_g': p['rw_gn_g'][l].reshape(1, A_WIDTH),
        'gn_b': p['rw_gn_b'][l].reshape(1, A_WIDTH),
        'qn': jnp.tile(p['q_norm'][l].reshape(1, HEAD_DIM), (1, B_Q_HEADS)),
        'knw': jnp.tile(p['k_norm'][l].reshape(1, HEAD_DIM), (1, B_KV_HEADS)),
        'pool_blk': pool_blk.astype(BF16),
        'pool_scale': p['pool_scale'][l].reshape(1, C_WIDTH),
        'ln1_g': p['ln1_g'][l].reshape(1, d), 'ln1_b': p['ln1_b'][l].reshape(1, d),
        'ln2_g': p['ln2_g'][l].reshape(1, d), 'ln2_b': p['ln2_b'][l].reshape(1, d),
        'w_router': w_router, 'b_router': b_router,
        'moe_w1': p['moe_w1'][l].astype(BF16),
        'moe_w3': p['moe_w3'][l].astype(BF16),
        'moe_w2': p['moe_w2'][l].astype(BF16),
    }


def _forward(x_prompt, x_sample, state_rwkv, cache_k, cache_v, c, c_ctx, p):
    batch, seq, d = x_prompt.shape
    dec_batch, dec_seq, _ = x_sample.shape
    depth = p['w_in'].shape[0]
    past = cache_k.shape[2]
    n_ctx = batch * seq
    n_tok = n_ctx + dec_batch * dec_seq
    alpha = (2 * depth) ** 0.25
    bounds = tuple(n_ctx + j * dec_seq for j in range(dec_batch))
    tm = min(256, seq)
    tm_moe = math.gcd(n_ctx, dec_seq, 1024)
    tq_ctx = min(256, seq)
    tq_dec = min(256, dec_seq)
    sb_ctx = 2

    cond8 = jnp.zeros((8, d), F32).at[0].set(c_ctx).at[1:1 + dec_batch].set(c)
    mods = _modulation(cond8, p['w_mod'], p['b_mod'])
    cos, sin = _rope_tables(n_ctx, dec_batch, dec_seq)
    head_id = jnp.arange(A_WIDTH) // HEAD_DIM
    ones = (head_id[:, None] == head_id[None, :]).astype(BF16)

    h = jnp.concatenate([x_prompt.reshape(n_ctx, d), x_sample.reshape(dec_batch * dec_seq, d)], axis=0)
    new_s, new_k, new_v = [], [], []
    for l in range(depth):
        lw = _layer_weights(l, p)
        (r, v, kk, logw, kd, bvec, g, bonus, q, k_att, v_att, pp) = _in_projection(
            h, mods[l], lw, cos, sin, ones, tm=tm, bounds=bounds)
        scan_ops = (r, v, kk, logw, kd, bvec)
        yf_c, yb_c, s_fin = _rwkv_scan(scan_ops, None, row0=0, nseq=batch, seqlen=seq, sb=sb_ctx,
                                       nb=seq // CHUNK, want_state=True)
        yf_d, yb_d = _rwkv_scan(scan_ops, state_rwkv[:, l], row0=n_ctx, nseq=dec_batch, seqlen=dec_seq,
                                sb=dec_batch, nb=2, want_state=False)
        yf = jnp.concatenate([yf_c.reshape(n_ctx, A_WIDTH), yf_d.reshape(n_tok - n_ctx, A_WIDTH)], axis=0)
        yb = jnp.concatenate([yb_c.reshape(n_ctx, A_WIDTH), yb_d.reshape(n_tok - n_ctx, A_WIDTH)], axis=0)
        att_c = _attention(q, k_att, v_att, n_ctx, row0=0, nseq=batch, seqlen=seq, tq=tq_ctx,
                           keys_in_token_array=True)
        k_dec = jnp.concatenate([k_att[n_ctx:].reshape(dec_batch, dec_seq, KV_WIDTH),
                                 cache_k[:, l].reshape(dec_batch, past, KV_WIDTH)], axis=1)
        v_dec = jnp.concatenate([v_att[n_ctx:].reshape(dec_batch, dec_seq, KV_WIDTH),
                                 cache_v[:, l].reshape(dec_batch, past, KV_WIDTH)], axis=1)
        att_d = _attention(q, k_dec, v_dec, dec_batch * dec_seq, row0=n_ctx, nseq=dec_batch, seqlen=dec_seq,
                           tq=tq_dec, keys_in_token_array=False)
        att = jnp.concatenate([att_c, att_d], axis=0)
        pool_c = _pool(pp, lw['pool_blk'], lw['pool_scale'], row0=0, nseq=batch, seqlen=seq)
        pool_d = _pool(pp, lw['pool_blk'], lw['pool_scale'], row0=n_ctx, nseq=dec_batch, seqlen=dec_seq)
        pool = jnp.concatenate([pool_c, pool_d], axis=0)
        h1, u2, logits = _out_projection(yf, yb, g, bonus, att, pool, h, mods[l], lw, ones,
                                         tm=tm, bounds=bounds, alpha=alpha)
        h = _moe(u2, logits, h1, mods[l], lw, tm=tm_moe, bounds=bounds, alpha=alpha)
        new_s.append(s_fin)
        new_k.append(k_att[:n_ctx].reshape(batch, seq, B_KV_HEADS, HEAD_DIM))
        new_v.append(v_att[:n_ctx].reshape(batch, seq, B_KV_HEADS, HEAD_DIM))
    return (h[:n_ctx].reshape(batch, seq, d), h[n_ctx:].reshape(dec_batch, dec_seq, d),
            jnp.stack(new_s, axis=1), jnp.stack(new_k, axis=1), jnp.stack(new_v, axis=1))


def kernel(x_prompt, x_sample, state_rwkv, cache_k, cache_v, c, c_ctx, w_in, w_out, w_mod, b_mod, ln1_g, ln1_b, ln2_g, ln2_b, rw_w0, rw_w2, rw_a0, rw_a2, rw_g2, rw_kk, rw_ka, rw_rk, rw_gn_g, rw_gn_b, q_norm, k_norm, pool_w, pool_scale, rt_grp_w, rt_grp_b, rt_exp_w, rt_exp_b, moe_w1, moe_w3, moe_w2):
    p = dict(w_in=w_in, w_out=w_out, w_mod=w_mod, b_mod=b_mod, ln1_g=ln1_g, ln1_b=ln1_b, ln2_g=ln2_g,
             ln2_b=ln2_b, rw_w0=rw_w0, rw_w2=rw_w2, rw_a0=rw_a0, rw_a2=rw_a2, rw_g2=rw_g2, rw_kk=rw_kk,
             rw_ka=rw_ka, rw_rk=rw_rk, rw_gn_g=rw_gn_g, rw_gn_b=rw_gn_b, q_norm=q_norm, k_norm=k_norm,
             pool_w=pool_w, pool_scale=pool_scale, rt_grp_w=rt_grp_w, rt_grp_b=rt_grp_b, rt_exp_w=rt_exp_w,
             rt_exp_b=rt_exp_b, moe_w1=moe_w1, moe_w3=moe_w3, moe_w2=moe_w2)
    return _forward(x_prompt, x_sample, state_rwkv, cache_k, cache_v, c, c_ctx, p)
```

```python
import functools
import math

import jax
import jax.numpy as jnp
from jax import lax
from jax.experimental import pallas as pl
from jax.experimental.pallas import tpu as pltpu

F32 = jnp.float32
BF16 = jnp.bfloat16
I32 = jnp.int32
HIGHEST = lax.Precision.HIGHEST

D_MODEL = 1024
GRID_W = 64
HEAD_DIM = 64
A_HEADS = 6
A_WIDTH = A_HEADS * HEAD_DIM
LORA_W = 64
LORA_A = 64
LORA_G = 128
N_DIR = 2
DECAY_SCALE = 0.606531
B_Q_HEADS = 6
B_KV_HEADS = 2
B_WIDTH = B_Q_HEADS * HEAD_DIM
KV_WIDTH = B_KV_HEADS * HEAD_DIM
ROPE_THETA = 10000.0
POOL_WINDOWS = (2, 4, 8, 16)
C_WIDTH = 256
IN_COLS = 2432
N_GROUPS = 4
EXP_PER_GROUP = 4
N_EXPERTS = 16
D_EXPERT = 256
LN_EPS = 1e-5
GN_EPS = 64e-5
QK_EPS = 1e-6

_O_R, _O_K, _O_V = 0, 384, 768
_O_LW, _O_LA, _O_LG = 1152, 1280, 1408
_O_Q, _O_KB, _O_VB, _O_P = 1536, 1920, 2048, 2176

CHUNK = 64
ROUTER_LANES = 128
VMEM_LIMIT = 56 * 1024 * 1024


def _cparams(sem):
    return pltpu.CompilerParams(dimension_semantics=sem, vmem_limit_bytes=VMEM_LIMIT)


def _sigmoid(x):
    return 1.0 / (1.0 + jnp.exp(-x))


def _dot(a, b):
    return jnp.dot(a, b, preferred_element_type=F32)


def _dot_nt(a, b):
    return lax.dot_general(a, b, (((1,), (1,)), ((), ())), preferred_element_type=F32)


def _segsum(x, g_ones):
    hi = x.astype(BF16)
    lo = (x - hi.astype(F32)).astype(BF16)
    return _dot(hi, g_ones) + _dot(lo, g_ones)


def _rope128(x, cos, sin):
    lane = lax.broadcasted_iota(I32, x.shape, 1)
    first = (lane % 32) < 16
    up = pltpu.roll(x, 128 - 16, 1)
    dn = pltpu.roll(x, 16, 1)
    return x * cos + jnp.where(first, up, dn) * sin


def _layer_norm(z, g, b):
    mu = jnp.mean(z, axis=-1, keepdims=True)
    zc = z - mu
    var = jnp.mean(zc * zc, axis=-1, keepdims=True)
    return zc * lax.rsqrt(var + LN_EPS) * g + b


def _mod_kernel(c_ref, w_ref, b_ref, o_ref):
    x = c_ref[...]
    x = x * _sigmoid(x)
    o_ref[...] = jnp.dot(x, w_ref[...], precision=HIGHEST, preferred_element_type=F32) + b_ref[...]


def _modulation(cond8, w_mod, b_mod):
    depth = w_mod.shape[0]
    tn = 1536
    return pl.pallas_call(
        _mod_kernel,
        grid=(depth, 6 * D_MODEL // tn),
        in_specs=[
            pl.BlockSpec((8, D_MODEL), lambda l, j: (0, 0)),
            pl.BlockSpec((None, D_MODEL, tn), lambda l, j: (l, 0, j)),
            pl.BlockSpec((None, 1, tn), lambda l, j: (l, 0, j)),
        ],
        out_specs=pl.BlockSpec((None, 8, tn), lambda l, j: (l, 0, j)),
        out_shape=jax.ShapeDtypeStruct((depth, 8, 6 * D_MODEL), F32),
        compiler_params=_cparams(("arbitrary", "arbitrary")),
        name="modulation",
    )(cond8, w_mod, b_mod.reshape(depth, 1, 6 * D_MODEL))


def _group_of_row(row0, bounds):
    g = jnp.zeros((), I32)
    for b in bounds:
        g = g + jnp.where(row0 >= b, 1, 0).astype(I32)
    return g


def _in_kernel(h_ref, mod_ref, w_ref, w2_ref, w0_ref, a2_ref, a0_ref, g2_ref, kkw_ref, ka_ref,
               rk_ref, qn_ref, knw_ref, cos_ref, sin_ref, ones_ref,
               r_o, v_o, kk_o, lw_o, kd_o, b_o, g_o, bon_o, q_o, k_o, vb_o, p_o, *, tm, bounds):
    grp = _group_of_row(pl.program_id(0) * tm, bounds)
    mod = mod_ref[pl.ds(grp, 1), :]
    sh1 = mod[:, 0:D_MODEL]
    sc1 = mod[:, D_MODEL:2 * D_MODEL]
    u = h_ref[...] * (1.0 + sc1) + sh1
    proj = _dot(u.astype(BF16), w_ref[...])
    ones = ones_ref[...]

    r = proj[:, _O_R:_O_R + A_WIDTH]
    k = proj[:, _O_K:_O_K + A_WIDTH]
    v = proj[:, _O_V:_O_V + A_WIDTH]
    lo_w = proj[:, _O_LW:_O_LW + 128]
    lo_a = proj[:, _O_LA:_O_LA + 128]
    lo_g = proj[:, _O_LG:_O_LG + 128]

    w_pre = w0_ref[...] + _dot(jnp.tanh(lo_w).astype(BF16), w2_ref[...])
    logw = -DECAY_SCALE * _sigmoid(w_pre)
    a = _sigmoid(a0_ref[...] + _dot(lo_a.astype(BF16), a2_ref[...]))
    g = _dot(_sigmoid(lo_g).astype(BF16), g2_ref[...])

    kk = k * kkw_ref[...]
    kk = kk * lax.rsqrt(jnp.maximum(_segsum(kk * kk, ones), 1e-24))
    k2 = jnp.concatenate([k, k], axis=1)
    kd = k2 * (1.0 + (a - 1.0) * ka_ref[...])
    bvec = jnp.concatenate([kk, kk], axis=1) * a
    kd_sum = kd[:, 0:A_WIDTH] + kd[:, A_WIDTH:2 * A_WIDTH]
    bonus = _segsum(r * kd_sum * rk_ref[...], ones) * v

    r_o[...] = r
    v_o[...] = v
    kk_o[...] = kk
    lw_o[...] = logw
    kd_o[...] = kd
    b_o[...] = bvec
    g_o[...] = g
    bon_o[...] = bonus

    cos = cos_ref[...]
    sin = sin_ref[...]
    q = proj[:, _O_Q:_O_Q + B_WIDTH]
    qn = q * lax.rsqrt(_segsum(q * q, ones) * (1.0 / HEAD_DIM) + QK_EPS) * qn_ref[...]
    scale = 1.0 / math.sqrt(HEAD_DIM)
    q_o[...] = jnp.concatenate(
        [_rope128(qn[:, j * 128:(j + 1) * 128], cos, sin) for j in range(B_WIDTH // 128)], axis=1) * scale
    kb = proj[:, _O_KB:_O_KB + KV_WIDTH]
    kn = kb * lax.rsqrt(_segsum(kb * kb, ones[0:KV_WIDTH, 0:KV_WIDTH]) * (1.0 / HEAD_DIM) + QK_EPS) * knw_ref[...]
    k_o[...] = _rope128(kn, cos, sin)
    vb_o[...] = proj[:, _O_VB:_O_VB + KV_WIDTH]
    p_o[...] = proj[:, _O_P:_O_P + C_WIDTH]


def _in_projection(h, mods_l, lw, cos, sin, ones, *, tm, bounds):
    t = h.shape[0]
    full = lambda shape: pl.BlockSpec(shape, lambda i: (0,) * len(shape))
    rows = lambda w: pl.BlockSpec((tm, w), lambda i: (i, 0))
    widths = (A_WIDTH, A_WIDTH, A_WIDTH, 2 * A_WIDTH, 2 * A_WIDTH, 2 * A_WIDTH, A_WIDTH, A_WIDTH,
              B_WIDTH, KV_WIDTH, KV_WIDTH, C_WIDTH)
    return pl.pallas_call(
        functools.partial(_in_kernel, tm=tm, bounds=bounds),
        grid=(t // tm,),
        in_specs=[
            rows(D_MODEL), full((8, 6 * D_MODEL)), full((D_MODEL, IN_COLS)),
            full((128, 2 * A_WIDTH)), full((1, 2 * A_WIDTH)), full((128, 2 * A_WIDTH)), full((1, 2 * A_WIDTH)),
            full((LORA_G, A_WIDTH)), full((1, A_WIDTH)), full((1, 2 * A_WIDTH)), full((1, A_WIDTH)),
            full((1, B_WIDTH)), full((1, KV_WIDTH)), rows(128), rows(128), full((A_WIDTH, A_WIDTH)),
        ],
        out_specs=[rows(w) for w in widths],
        out_shape=[jax.ShapeDtypeStruct((t, w), F32) for w in widths],
        compiler_params=_cparams(("arbitrary",)),
        name="in_projection",
    )(h, mods_l, lw['w_in'], lw['w2blk'], lw['w0'], lw['a2blk'], lw['a0'], lw['g2'], lw['kkw'], lw['ka'],
      lw['rk'], lw['qn'], lw['knw'], cos, sin, ones)


def _split3(x):
    hi = x.astype(BF16)
    r1 = x - hi.astype(F32)
    mid = r1.astype(BF16)
    lo = (r1 - mid.astype(F32)).astype(BF16)
    return hi, mid, lo


def _scan_kernel(*refs, sb, nb, zero_init, want_state):
    fwd = refs[0:6]
    bwd = refs[6:12]
    pos = 12
    if not zero_init:
        s0_ref = refs[pos]
        pos += 1
    y_outs = (refs[pos], refs[pos + 1])
    pos += 2
    if want_state:
        sfin_o = refs[pos]
        pos += 1
    h_ref = refs[pos]

    step = pl.program_id(1)
    nstep = pl.num_programs(1)
    npair = A_HEADS // 2
    pw = 2 * HEAD_DIM
    pairs = [(q, d, hp) for q in range(sb) for d in range(N_DIR) for hp in range(npair)]

    @pl.when(step == 0)
    def _():
        h_ref[...] = jnp.zeros_like(h_ref)
        if not zero_init:
            for (q, d, hp) in pairs:
                for e in range(2):
                    lo = e * HEAD_DIM
                    h_ref[q, d, hp, lo:lo + HEAD_DIM, lo:lo + HEAD_DIM] = s0_ref[q, d, 2 * hp + e].T

    def iota(shape, dim):
        return lax.broadcasted_iota(I32, shape, dim)

    first64 = iota((CHUNK, pw), 1) < HEAD_DIM
    eye_p = ((iota((CHUNK, pw), 1) % HEAD_DIM) == iota((CHUNK, pw), 0)).astype(F32)
    eye2 = (iota((pw, pw), 0) == iota((pw, pw), 1)).astype(F32)
    same_head = (iota((pw, pw), 0) < HEAD_DIM) == (iota((pw, pw), 1) < HEAD_DIM)
    ti = iota((CHUNK, CHUNK), 0)
    si = iota((CHUNK, CHUNK), 1)
    incl16 = ((si <= ti).astype(BF16), (si >= ti).astype(BF16))
    t4 = iota((2 * CHUNK, 2 * pw), 0)
    s4 = iota((2 * CHUNK, 2 * pw), 1) % CHUNK
    incl_off = jnp.where(t4 < CHUNK, 0, 1)
    t4 = t4 % CHUNK
    mask4 = (s4 < t4 + incl_off, s4 > t4 - incl_off)

    def bdiag(x):
        return jnp.concatenate([jnp.where(first64, x, 0.0), jnp.where(first64, 0.0, x)], axis=0)

    lhs, bkt, gcol, v_p, amat = {}, {}, {}, {}, {}
    for q in range(sb):
        for d, ins in enumerate((fwd, bwd)):
            r_ref, v_ref, kk_ref, lw_ref, kd_ref, b_ref = ins
            for j in range(nb):
                rows = pl.ds(j * CHUNK, CHUNK)
                lw = lw_ref[q, rows, :]
                hi, mid, lo = _split3(lw)
                cs = _dot(incl16[d], hi) + _dot(incl16[d], mid) + _dot(incl16[d], lo)
                gam = jnp.exp(cs)
                gam_prev = jnp.exp(cs - lw)
                gam_inv = jnp.exp(-cs)
                gam_end = jnp.exp(cs[CHUNK - 1:CHUNK, :] if d == 0 else cs[0:1, :])
                a_hat = -kk_ref[q, rows, :] * gam_prev
                r_hat = r_ref[q, rows, :] * gam
                b_hat = b_ref[q, rows, :] * gam_inv
                k_hat = kd_ref[q, rows, :] * gam_inv
                v_all = v_ref[q, rows, :]
                for hp in range(npair):
                    key = (q, d, hp, j)
                    sl = slice(hp * pw, (hp + 1) * pw)
                    lhs[key] = jnp.concatenate([a_hat[:, sl], r_hat[:, sl]], axis=0).astype(BF16)
                    bh, kh, ge = b_hat[:, sl], k_hat[:, sl], gam_end[:, sl]
                    rhs = jnp.concatenate([bdiag(bh), bdiag(kh)], axis=0).astype(BF16)
                    amat[key] = _dot_nt(lhs[key], rhs)
                    gcol[key] = jnp.sum(eye2 * ge, axis=1, keepdims=True)
                    bkt[key] = (jnp.concatenate([bh, kh], axis=0) * ge).T.astype(BF16)
                    v_p[key] = v_all[:, sl]
    keys = list(amat.keys())
    npow, tcur, a_rb, xa, ya, vbd = {}, {}, {}, {}, {}, {}
    for key in keys:
        am = jnp.where(mask4[key[1]], amat[key], 0.0)
        npow[key] = am[0:CHUNK, 0:pw]
        tcur[key] = eye_p + npow[key]
        a_rb[key] = am[CHUNK:2 * CHUNK, 0:pw].astype(BF16)
        vbd[key] = bdiag(v_p[key]).astype(BF16)
        xa[key] = _dot(am[0:CHUNK, pw:2 * pw].astype(BF16), vbd[key])
        ya[key] = _dot(am[CHUNK:2 * CHUNK, pw:2 * pw].astype(BF16), vbd[key])
    nsq = int(math.log2(CHUNK)) - 1
    for key in keys:
        npow[key] = _dot(npow[key].astype(BF16), bdiag(npow[key]).astype(BF16))
    for _ in range(nsq - 1):
        for key in keys:
            rhs = jnp.concatenate([bdiag(npow[key]), bdiag(tcur[key])], axis=1).astype(BF16)
            res = _dot(npow[key].astype(BF16), rhs)
            npow[key] = res[:, 0:pw]
            tcur[key] = tcur[key] + res[:, pw:2 * pw]
    t_inv = {}
    for key in keys:
        t_inv[key] = (tcur[key] + _dot(npow[key].astype(BF16), bdiag(tcur[key]).astype(BF16))).astype(BF16)

    hcur = {pr: h_ref[pr[0], pr[1], pr[2]] for pr in pairs}
    for jj in range(nb):
        z, u = {}, {}
        kof = lambda pr: (pr[0], pr[1], pr[2], jj if pr[1] == 0 else nb - 1 - jj)
        for pr in pairs:
            z[pr] = _dot(lhs[kof(pr)], hcur[pr].astype(BF16))
        for pr in pairs:
            key = kof(pr)
            x = z[pr][0:CHUNK] + xa[key]
            u[pr] = _dot(t_inv[key], bdiag(x).astype(BF16))
        for pr in pairs:
            key = kof(pr)
            q, d, hp, j = key
            y = z[pr][CHUNK:2 * CHUNK] + ya[key] + _dot(a_rb[key], bdiag(u[pr]).astype(BF16))
            y_outs[d][q, pl.ds(j * CHUNK, CHUNK), hp * pw:(hp + 1) * pw] = y
            uv = jnp.concatenate([u[pr], v_p[key]], axis=0).astype(BF16)
            hcur[pr] = hcur[pr] * gcol[key] + jnp.where(same_head, _dot(bkt[key], uv), 0.0)
    for pr in pairs:
        h_ref[pr[0], pr[1], pr[2]] = hcur[pr]

    if want_state:
        @pl.when(step == nstep - 1)
        def _():
            for (q, d, hp) in pairs:
                for e in range(2):
                    lo = e * HEAD_DIM
                    sfin_o[q, d, 2 * hp + e] = hcur[(q, d, hp)][lo:lo + HEAD_DIM, lo:lo + HEAD_DIM].T


def _rwkv_scan(ops, s0, *, row0, nseq, seqlen, sb, nb, want_state):
    g = nb * CHUNK
    nstep = seqlen // g
    zero_init = s0 is None
    ops3 = [a.reshape(a.shape[0] // seqlen, seqlen, a.shape[1]) for a in ops]
    base = row0 // (seqlen * sb)
    fidx = lambda col: (lambda s, c: (base + s, c, col))
    bidx = lambda col: (lambda s, c: (base + s, nstep - 1 - c, col))
    blk = lambda im: pl.BlockSpec((sb, g, A_WIDTH), im)
    in_specs = [blk(fidx(0))] * 6 + [blk(bidx(0))] * 3 + [blk(bidx(1))] * 3
    args = ops3 + ops3
    state_spec = pl.BlockSpec((sb, N_DIR, A_HEADS, HEAD_DIM, HEAD_DIM), lambda s, c: (s, 0, 0, 0, 0))
    if not zero_init:
        in_specs.append(state_spec)
        args.append(s0)
    out_specs = [blk(lambda s, c: (s, c, 0)), blk(lambda s, c: (s, nstep - 1 - c, 0))]
    out_shape = [jax.ShapeDtypeStruct((nseq, seqlen, A_WIDTH), F32)] * 2
    if want_state:
        out_specs.append(state_spec)
        out_shape.append(jax.ShapeDtypeStruct((nseq, N_DIR, A_HEADS, HEAD_DIM, HEAD_DIM), F32))
    return pl.pallas_call(
        functools.partial(_scan_kernel, sb=sb, nb=nb, zero_init=zero_init, want_state=want_state),
        grid=(nseq // sb, nstep),
        in_specs=in_specs,
        out_specs=out_specs,
        out_shape=out_shape,
        scratch_shapes=[pltpu.VMEM((sb, N_DIR, A_HEADS // 2, 2 * HEAD_DIM, 2 * HEAD_DIM), F32)],
        compiler_params=_cparams(("arbitrary", "arbitrary")),
        name="rwkv_scan_ctx" if zero_init else "rwkv_scan_dec",
    )(*args)


def _attn_kernel(*refs, tq, with_cache):
    grp = B_Q_HEADS // B_KV_HEADS
    if with_cache:
        q_ref, k_ref, v_ref, ck_ref, cv_ref, o_ref = refs
        kall = jnp.concatenate([k_ref[...], ck_ref[...]], axis=0)
        vall = jnp.concatenate([v_ref[...], cv_ref[...]], axis=0)
    else:
        q_ref, k_ref, v_ref, o_ref = refs
        kall = k_ref[...]
        vall = v_ref[...]
    q = q_ref[...]
    outs = []
    for hk in range(B_KV_HEADS):
        ks = kall[:, hk * HEAD_DIM:(hk + 1) * HEAD_DIM].astype(BF16)
        vs = vall[:, hk * HEAD_DIM:(hk + 1) * HEAD_DIM].astype(BF16)
        qs = jnp.concatenate(
            [q[:, (hk * grp + g) * HEAD_DIM:(hk * grp + g + 1) * HEAD_DIM] for g in range(grp)], axis=0)
        s = _dot_nt(qs.astype(BF16), ks)
        m = jnp.max(s, axis=-1, keepdims=True)
        p = jnp.exp(s - m)
        den = jnp.sum(p, axis=-1, keepdims=True)
        o = _dot(p.astype(BF16), vs) / den
        outs.extend(o[g * tq:(g + 1) * tq] for g in range(grp))
    o_ref[...] = jnp.concatenate(outs, axis=1)


def _attention(q, k_tok, v_tok, cache, *, row0, nseq, seqlen, tq):
    nq = seqlen // tq
    base = row0 // tq
    kbase = row0 // seqlen
    kspec = pl.BlockSpec((seqlen, KV_WIDTH), lambda s, i: (kbase + s, 0))
    in_specs = [pl.BlockSpec((tq, B_WIDTH), lambda s, i: (base + s * nq + i, 0)), kspec, kspec]
    args = [q, k_tok, v_tok]
    if cache is not None:
        ck, cv, layer = cache
        cspec = pl.BlockSpec((None, None, ck.shape[2], KV_WIDTH), lambda s, i: (s, layer, 0, 0))
        in_specs += [cspec, cspec]
        args += [ck, cv]
    return pl.pallas_call(
        functools.partial(_attn_kernel, tq=tq, with_cache=cache is not None),
        grid=(nseq, nq),
        in_specs=in_specs,
        out_specs=pl.BlockSpec((tq, B_WIDTH), lambda s, i: (s * nq + i, 0)),
        out_shape=jax.ShapeDtypeStruct((nseq * seqlen, B_WIDTH), F32),
        compiler_params=_cparams(("arbitrary", "arbitrary")),
        name="attention_ctx" if cache is None else "attention_dec",
    )(*args)


def _pool_kernel(p_ref, w_ref, sc_ref, o_ref, *, seqlen):
    x = p_ref[...]
    t = lax.broadcasted_iota(I32, x.shape, 0)
    lane = lax.broadcasted_iota(I32, x.shape, 1)

    def down(a, k):
        return jnp.where(t >= k, pltpu.roll(a, k, 0), 0.0)

    def up(a, k):
        return jnp.where(t < seqlen - k, pltpu.roll(a, seqlen - k, 0), 0.0)

    past = [x]
    futr = [x]
    for j in range(len(POOL_WINDOWS) - 1):
        past.append(past[j] + down(past[j], 2 ** j))
        futr.append(futr[j] + up(futr[j], 2 ** j))
    win_sum = jnp.zeros_like(x)
    cnt = jnp.zeros_like(x)
    for gi, win in enumerate(POOL_WINDOWS):
        half = win // 2
        in_group = (lane // HEAD_DIM) == gi
        win_sum = jnp.where(in_group, down(past[gi], 1) + futr[gi], win_sum)
        n = (jnp.minimum(t + half, seqlen) - jnp.maximum(t - half, 0)).astype(F32)
        cnt = jnp.where(in_group, n, cnt)
    d = win_sum / cnt - x
    o_ref[...] = _dot(d.astype(BF16), w_ref[...]) * sc_ref[...]


def _pool(p, w_blk, scale, *, row0, nseq, seqlen):
    base = row0 // seqlen
    return pl.pallas_call(
        functools.partial(_pool_kernel, seqlen=seqlen),
        grid=(nseq,),
        in_specs=[pl.BlockSpec((seqlen, C_WIDTH), lambda s: (base + s, 0)),
                  pl.BlockSpec((C_WIDTH, C_WIDTH), lambda s: (0, 0)),
                  pl.BlockSpec((1, C_WIDTH), lambda s: (0, 0))],
        out_specs=pl.BlockSpec((seqlen, C_WIDTH), lambda s: (s, 0)),
        out_shape=jax.ShapeDtypeStruct((nseq * seqlen, C_WIDTH), F32),
        compiler_params=_cparams(("arbitrary",)),
        name="pool_mixer",
    )(p, w_blk, scale)


def _out_kernel(yfc_ref, yfd_ref, ybc_ref, ybd_ref, g_ref, bon_ref, attc_ref, attd_ref, poolc_ref, poold_ref,
                h_ref, mod_ref, w_ref, lng_ref, lnb_ref, gng_ref, gnb_ref, ones_ref, wrh_ref, wrl_ref, br_ref,
                h1_o, u2_o, lg_o, *, tm, bounds, alpha, n_ctx_tiles):
    grp = _group_of_row(pl.program_id(0) * tm, bounds)
    mod = mod_ref[pl.ds(grp, 1), :]
    g1 = mod[:, 2 * D_MODEL:3 * D_MODEL]
    sh2 = mod[:, 3 * D_MODEL:4 * D_MODEL]
    sc2 = mod[:, 4 * D_MODEL:5 * D_MODEL]
    ones = ones_ref[...]
    is_ctx = pl.program_id(0) < n_ctx_tiles
    pick = lambda c_ref, d_ref: jnp.where(is_ctx, c_ref[...], d_ref[...])
    y = pick(yfc_ref, yfd_ref) + pick(ybc_ref, ybd_ref)
    mu = _segsum(y, ones) * (1.0 / HEAD_DIM)
    yc = y - mu
    var = _segsum(yc * yc, ones) * (1.0 / HEAD_DIM)
    yn = yc * lax.rsqrt(var + GN_EPS) * gng_ref[...] + gnb_ref[...]
    a_out = (yn + bon_ref[...]) * g_ref[...]
    m = (_dot(a_out.astype(BF16), w_ref[0:A_WIDTH, :])
         + _dot(pick(attc_ref, attd_ref).astype(BF16), w_ref[A_WIDTH:A_WIDTH + B_WIDTH, :])
         + _dot(pick(poolc_ref, poold_ref).astype(BF16), w_ref[A_WIDTH + B_WIDTH:D_MODEL, :]))
    h1 = _layer_norm(alpha * h_ref[...] + g1 * m, lng_ref[...], lnb_ref[...])
    h1_o[...] = h1
    u2 = h1 * (1.0 + sc2) + sh2
    u2_o[...] = u2.astype(BF16)
    u_hi = u2.astype(BF16)
    u_lo = (u2 - u_hi.astype(F32)).astype(BF16)
    lg_o[...] = (_dot(u_hi, wrh_ref[...]) + _dot(u_lo, wrh_ref[...]) + _dot(u_hi, wrl_ref[...])) + br_ref[...]


def _out_projection(yf, yb, g, bonus, att, pool, h, mods_l, lw, ones, *, tm, bounds, alpha):
    t = h.shape[0]
    nct = yf[0].shape[0] // tm
    full = lambda shape: pl.BlockSpec(shape, lambda i: (0,) * len(shape))
    rows = lambda w: pl.BlockSpec((tm, w), lambda i: (i, 0))
    rows_c = lambda w: pl.BlockSpec((tm, w), lambda i: (jnp.minimum(i, nct - 1), 0))
    rows_d = lambda w: pl.BlockSpec((tm, w), lambda i: (jnp.maximum(i - nct, 0), 0))
    return pl.pallas_call(
        functools.partial(_out_kernel, tm=tm, bounds=bounds, alpha=alpha, n_ctx_tiles=nct),
        grid=(t // tm,),
        in_specs=[rows_c(A_WIDTH), rows_d(A_WIDTH), rows_c(A_WIDTH), rows_d(A_WIDTH), rows(A_WIDTH), rows(A_WIDTH),
                  rows_c(B_WIDTH), rows_d(B_WIDTH), rows_c(C_WIDTH), rows_d(C_WIDTH),
                  rows(D_MODEL), full((8, 6 * D_MODEL)), full((D_MODEL, D_MODEL)), full((1, D_MODEL)),
                  full((1, D_MODEL)), full((1, A_WIDTH)), full((1, A_WIDTH)), full((A_WIDTH, A_WIDTH)),
                  full((D_MODEL, ROUTER_LANES)), full((D_MODEL, ROUTER_LANES)), full((1, ROUTER_LANES))],
        out_specs=[rows(D_MODEL), rows(D_MODEL), rows(ROUTER_LANES)],
        out_shape=[jax.ShapeDtypeStruct((t, D_MODEL), F32), jax.ShapeDtypeStruct((t, D_MODEL), BF16),
                   jax.ShapeDtypeStruct((t, ROUTER_LANES), F32)],
        compiler_params=_cparams(("arbitrary",)),
        name="out_projection",
    )(yf[0], yf[1], yb[0], yb[1], g, bonus, att[0], att[1], pool[0], pool[1], h, mods_l, lw['w_out'],
      lw['ln1_g'], lw['ln1_b'], lw['gn_g'], lw['gn_b'], ones, lw['w_router_hi'], lw['w_router_lo'], lw['b_router'])


def _router_gates(logits):
    lane = lax.broadcasted_iota(I32, logits.shape, 1)
    lane_f = lane.astype(F32)
    neg = -jnp.inf
    big = 1e9

    def first_lane(mask):
        return jnp.min(jnp.where(mask, lane_f, big), axis=-1, keepdims=True).astype(I32)

    is_grp = lane < N_GROUPS
    gmax = jnp.max(jnp.where(is_grp, logits, neg), axis=-1, keepdims=True)
    sel = first_lane(is_grp & (logits == gmax))
    gsum = jnp.sum(jnp.where(is_grp, jnp.exp(logits - gmax), 0.0), axis=-1, keepdims=True)
    gsel = 1.0 / gsum
    first_exp = N_GROUPS + sel * EXP_PER_GROUP
    in_sel = (lane >= first_exp) & (lane < first_exp + EXP_PER_GROUP)
    v1 = jnp.max(jnp.where(in_sel, logits, neg), axis=-1, keepdims=True)
    i1 = first_lane(in_sel & (logits == v1))
    rest = in_sel & (lane != i1)
    v2 = jnp.max(jnp.where(rest, logits, neg), axis=-1, keepdims=True)
    i2 = first_lane(rest & (logits == v2))
    e2 = jnp.exp(v2 - v1)
    t1 = (1.0 / (1.0 + e2)) * gsel
    t2 = (e2 / (1.0 + e2)) * gsel
    return jnp.where(lane == i1, t1, 0.0) + jnp.where(lane == i2, t2, 0.0)


def _moe_kernel(x_ref, lg_ref, w1_ref, w3_ref, w2_ref, h1_ref, mod_ref, lng_ref, lnb_ref, o_ref,
                acc_ref, gate_ref, *, tm, bounds, alpha):
    eg = pl.program_id(1)

    @pl.when(eg == 0)
    def _():
        gate_ref[...] = _router_gates(lg_ref[...])
        acc_ref[...] = jnp.zeros_like(acc_ref)

    gates = gate_ref[...]
    lane = lax.broadcasted_iota(I32, gates.shape, 1)
    x = x_ref[...]
    hids = []
    for j in range(EXP_PER_GROUP):
        e_lane = N_GROUPS + eg * EXP_PER_GROUP + j
        ge = jnp.sum(jnp.where(lane == e_lane, gates, 0.0), axis=-1, keepdims=True)
        h1 = _dot(x, w1_ref[j])
        h3 = _dot(x, w3_ref[j])
        hids.append(((h1 * _sigmoid(h1)) * h3 * ge).astype(BF16))
    hid = jnp.concatenate(hids, axis=1)
    acc_ref[...] += _dot(hid, w2_ref[...].reshape(EXP_PER_GROUP * D_EXPERT, D_MODEL))

    @pl.when(eg == pl.num_programs(1) - 1)
    def _():
        grp = _group_of_row(pl.program_id(0) * tm, bounds)
        g2 = mod_ref[pl.ds(grp, 1), 5 * D_MODEL:6 * D_MODEL]
        o_ref[...] = _layer_norm(alpha * h1_ref[...] + g2 * acc_ref[...], lng_ref[...], lnb_ref[...])


def _moe(u2, logits, h1, mods_l, lw, *, tm, bounds, alpha):
    t = h1.shape[0]
    full = lambda shape: pl.BlockSpec(shape, lambda i, e: (0,) * len(shape))
    rows = lambda w: pl.BlockSpec((tm, w), lambda i, e: (i, 0))
    return pl.pallas_call(
        functools.partial(_moe_kernel, tm=tm, bounds=bounds, alpha=alpha),
        grid=(t // tm, N_GROUPS),
        in_specs=[rows(D_MODEL), rows(ROUTER_LANES),
                  pl.BlockSpec((EXP_PER_GROUP, D_MODEL, D_EXPERT), lambda i, e: (e, 0, 0)),
                  pl.BlockSpec((EXP_PER_GROUP, D_MODEL, D_EXPERT), lambda i, e: (e, 0, 0)),
                  pl.BlockSpec((EXP_PER_GROUP, D_EXPERT, D_MODEL), lambda i, e: (e, 0, 0)),
                  rows(D_MODEL), full((8, 6 * D_MODEL)), full((1, D_MODEL)), full((1, D_MODEL))],
        out_specs=rows(D_MODEL),
        out_shape=jax.ShapeDtypeStruct((t, D_MODEL), F32),
        scratch_shapes=[pltpu.VMEM((tm, D_MODEL), F32), pltpu.VMEM((tm, ROUTER_LANES), F32)],
        compiler_params=_cparams(("arbitrary", "arbitrary")),
        name="moe",
    )(u2, logits, lw['moe_w1'], lw['moe_w3'], lw['moe_w2'], h1, mods_l, lw['ln2_g'], lw['ln2_b'])


def _rope_tables(n_ctx, dec_batch, dec_seq):
    n = HEAD_DIM // 4
    pos = jnp.arange(dec_seq)
    row = (pos // GRID_W).astype(F32)
    col = (pos % GRID_W).astype(F32)
    inv = 1.0 / (ROPE_THETA ** (jnp.arange(n, dtype=F32) / n))
    ra, ca = row[:, None] * inv, col[:, None] * inv
    cos64 = jnp.concatenate([jnp.cos(ra), jnp.cos(ra), jnp.cos(ca), jnp.cos(ca)], axis=-1)
    sin64 = jnp.concatenate([-jnp.sin(ra), jnp.sin(ra), -jnp.sin(ca), jnp.sin(ca)], axis=-1)
    cos_s = jnp.tile(jnp.concatenate([cos64, cos64], axis=-1), (dec_batch, 1))
    sin_s = jnp.tile(jnp.concatenate([sin64, sin64], axis=-1), (dec_batch, 1))
    cos = jnp.concatenate([jnp.ones((n_ctx, 128), F32), cos_s], axis=0)
    sin = jnp.concatenate([jnp.zeros((n_ctx, 128), F32), sin_s], axis=0)
    return cos, sin


def _block_diag2(w):
    z = jnp.zeros_like(w[0])
    return jnp.concatenate([jnp.concatenate([w[0], z], axis=1), jnp.concatenate([z, w[1]], axis=1)], axis=0)


def _layer_weights(l, p):
    d = D_MODEL
    pool_blk = jnp.zeros((C_WIDTH, C_WIDTH), F32)
    for gi in range(len(POOL_WINDOWS)):
        pool_blk = pool_blk.at[gi * 64:(gi + 1) * 64, gi * 64:(gi + 1) * 64].set(p['pool_w'][l, gi])
    w_router = jnp.zeros((d, ROUTER_LANES), F32)
    w_router = w_router.at[:, 0:N_GROUPS].set(p['rt_grp_w'][l])
    w_router = w_router.at[:, N_GROUPS:N_GROUPS + N_EXPERTS].set(p['rt_exp_w'][l])
    b_router = jnp.zeros((1, ROUTER_LANES), F32)
    b_router = b_router.at[0, 0:N_GROUPS].set(p['rt_grp_b'][l])
    b_router = b_router.at[0, N_GROUPS:N_GROUPS + N_EXPERTS].set(p['rt_exp_b'][l])
    return {
        'w_in': p['w_in'][l].astype(BF16),
        'w_out': p['w_out'][l].astype(BF16),
        'w2blk': _block_diag2(p['rw_w2'][l]).astype(BF16),
        'w0': p['rw_w0'][l].reshape(1, 2 * A_WIDTH),
        'a2blk': _block_diag2(p['rw_a2'][l]).astype(BF16),
        'a0': p['rw_a0'][l].reshape(1, 2 * A_WIDTH),
        'g2': p['rw_g2'][l].astype(BF16),
        'kkw': p['rw_kk'][l].reshape(1, A_WIDTH),
        'ka': jnp.tile(p['rw_ka'][l].reshape(1, A_WIDTH), (1, 2)),
        'rk': p['rw_rk'][l].reshape(1, A_WIDTH),
        'gn_g': p['rw_gn_g'][l].reshape(1, A_WIDTH),
        'gn_b': p['rw_gn_b'][l].reshape(1, A_WIDTH),
        'qn': jnp.tile(p['q_norm'][l].reshape(1, HEAD_DIM), (1, B_Q_HEADS)),
        'knw': jnp.tile(p['k_norm'][l].reshape(1, HEAD_DIM), (1, B_KV_HEADS)),
        'pool_blk': pool_blk.astype(BF16),
        'pool_scale': p['pool_scale'][l].reshape(1, C_WIDTH),
        'ln1_g': p['ln1_g'][l].reshape(1, d), 'ln1_b': p['ln1_b'][l].reshape(1, d),
        'ln2_g': p['ln2_g'][l].reshape(1, d), 'ln2_b': p['ln2_b'][l].reshape(1, d),
        'w_router_hi': w_router.astype(BF16),
        'w_router_lo': (w_router - w_router.astype(BF16).astype(F32)).astype(BF16),
        'b_router': b_router,
        'moe_w1': p['moe_w1'][l].astype(BF16),
        'moe_w3': p['moe_w3'][l].astype(BF16),
        'moe_w2': p['moe_w2'][l].astype(BF16),
    }


def _forward(x_prompt, x_sample, state_rwkv, cache_k, cache_v, c, c_ctx, p):
    batch, seq, d = x_prompt.shape
    dec_batch, dec_seq, _ = x_sample.shape
    depth = p['w_in'].shape[0]
    past = cache_k.shape[2]
    n_ctx = batch * seq
    n_tok = n_ctx + dec_batch * dec_seq
    alpha = (2 * depth) ** 0.25
    bounds = tuple(n_ctx + j * dec_seq for j in range(dec_batch))
    tm = min(256, seq)
    tm_moe = math.gcd(n_ctx, dec_seq, 1024)
    tq_ctx = min(256, seq)
    tq_dec = min(256, dec_seq)
    sb_ctx = 2

    cond8 = jnp.zeros((8, d), F32).at[0].set(c_ctx).at[1:1 + dec_batch].set(c)
    mods = _modulation(cond8, p['w_mod'], p['b_mod'])
    cos, sin = _rope_tables(n_ctx, dec_batch, dec_seq)
    head_id = jnp.arange(A_WIDTH) // HEAD_DIM
    ones = (head_id[:, None] == head_id[None, :]).astype(BF16)

    h = jnp.concatenate([x_prompt.reshape(n_ctx, d), x_sample.reshape(dec_batch * dec_seq, d)], axis=0)
    ck4 = cache_k.reshape(dec_batch, depth, past, KV_WIDTH)
    cv4 = cache_v.reshape(dec_batch, depth, past, KV_WIDTH)
    new_s, new_k, new_v = [], [], []
    for l in range(depth):
        lw = _layer_weights(l, p)
        (r, v, kk, logw, kd, bvec, g, bonus, q, k_att, v_att, pp) = _in_projection(
            h, mods[l], lw, cos, sin, ones, tm=tm, bounds=bounds)
        scan_ops = (r, v, kk, logw, kd, bvec)
        yf_c, yb_c, s_fin = _rwkv_scan(scan_ops, None, row0=0, nseq=batch, seqlen=seq, sb=sb_ctx,
                                       nb=seq // CHUNK, want_state=True)
        yf_d, yb_d = _rwkv_scan(scan_ops, state_rwkv[:, l], row0=n_ctx, nseq=dec_batch, seqlen=dec_seq,
                                sb=dec_batch, nb=4, want_state=False)
        yf = (yf_c.reshape(n_ctx, A_WIDTH), yf_d.reshape(n_tok - n_ctx, A_WIDTH))
        yb = (yb_c.reshape(n_ctx, A_WIDTH), yb_d.reshape(n_tok - n_ctx, A_WIDTH))
        att = (_attention(q, k_att, v_att, None, row0=0, nseq=batch, seqlen=seq, tq=tq_ctx),
               _attention(q, k_att, v_att, (ck4, cv4, l), row0=n_ctx, nseq=dec_batch, seqlen=dec_seq, tq=tq_dec))
        pool = (_pool(pp, lw['pool_blk'], lw['pool_scale'], row0=0, nseq=batch, seqlen=seq),
                _pool(pp, lw['pool_blk'], lw['pool_scale'], row0=n_ctx, nseq=dec_batch, seqlen=dec_seq))
        h1, u2, logits = _out_projection(yf, yb, g, bonus, att, pool, h, mods[l], lw, ones,
                                         tm=tm, bounds=bounds, alpha=alpha)
        h = _moe(u2, logits, h1, mods[l], lw, tm=tm_moe, bounds=bounds, alpha=alpha)
        new_s.append(s_fin)
        new_k.append(k_att[:n_ctx].reshape(batch, seq, B_KV_HEADS, HEAD_DIM))
        new_v.append(v_att[:n_ctx].reshape(batch, seq, B_KV_HEADS, HEAD_DIM))
    return (h[:n_ctx].reshape(batch, seq, d), h[n_ctx:].reshape(dec_batch, dec_seq, d),
            jnp.stack(new_s, axis=1), jnp.stack(new_k, axis=1), jnp.stack(new_v, axis=1))


def kernel(x_prompt, x_sample, state_rwkv, cache_k, cache_v, c, c_ctx, w_in, w_out, w_mod, b_mod, ln1_g, ln1_b, ln2_g, ln2_b, rw_w0, rw_w2, rw_a0, rw_a2, rw_g2, rw_kk, rw_ka, rw_rk, rw_gn_g, rw_gn_b, q_norm, k_norm, pool_w, pool_scale, rt_grp_w, rt_grp_b, rt_exp_w, rt_exp_b, moe_w1, moe_w3, moe_w2):
    p = dict(w_in=w_in, w_out=w_out, w_mod=w_mod, b_mod=b_mod, ln1_g=ln1_g, ln1_b=ln1_b, ln2_g=ln2_g,
             ln2_b=ln2_b, rw_w0=rw_w0, rw_w2=rw_w2, rw_a0=rw_a0, rw_a2=rw_a2, rw_g2=rw_g2, rw_kk=rw_kk,
             rw_ka=rw_ka, rw_rk=rw_rk, rw_gn_g=rw_gn_g, rw_gn_b=rw_gn_b, q_norm=q_norm, k_norm=k_norm,
             pool_w=pool_w, pool_scale=pool_scale, rt_grp_w=rt_grp_w, rt_grp_b=rt_grp_b, rt_exp_w=rt_exp_w,
             rt_exp_b=rt_exp_b, moe_w1=moe_w1, moe_w3=moe_w3, moe_w2=moe_w2)
    return _forward(x_prompt, x_sample, state_rwkv, cache_k, cache_v, c, c_ctx, p)
```

```python
import functools
import math

import jax
import jax.numpy as jnp
from jax import lax
from jax.experimental import pallas as pl
from jax.experimental.pallas import tpu as pltpu

F32 = jnp.float32
BF16 = jnp.bfloat16
I32 = jnp.int32
HIGHEST = lax.Precision.HIGHEST

D_MODEL = 1024
GRID_W = 64
HEAD_DIM = 64
A_HEADS = 6
A_WIDTH = A_HEADS * HEAD_DIM
LORA_W = 64
LORA_A = 64
LORA_G = 128
N_DIR = 2
DECAY_SCALE = 0.606531
B_Q_HEADS = 6
B_KV_HEADS = 2
B_WIDTH = B_Q_HEADS * HEAD_DIM
KV_WIDTH = B_KV_HEADS * HEAD_DIM
ROPE_THETA = 10000.0
POOL_WINDOWS = (2, 4, 8, 16)
C_WIDTH = 256
IN_COLS = 2432
N_GROUPS = 4
EXP_PER_GROUP = 4
N_EXPERTS = 16
D_EXPERT = 256
LN_EPS = 1e-5
GN_EPS = 64e-5
QK_EPS = 1e-6

_O_R, _O_K, _O_V = 0, 384, 768
_O_LW, _O_LA, _O_LG = 1152, 1280, 1408
_O_Q, _O_KB, _O_VB, _O_P = 1536, 1920, 2048, 2176

CHUNK = 64
ROUTER_LANES = 128
VMEM_LIMIT = 56 * 1024 * 1024


def _cparams(sem):
    return pltpu.CompilerParams(dimension_semantics=sem, vmem_limit_bytes=VMEM_LIMIT)


def _sigmoid(x):
    return 1.0 / (1.0 + jnp.exp(-x))


def _dot(a, b):
    return jnp.dot(a, b, preferred_element_type=F32)


def _dot_nt(a, b):
    return lax.dot_general(a, b, (((1,), (1,)), ((), ())), preferred_element_type=F32)


def _segsum(x, g_ones):
    hi = x.astype(BF16)
    lo = (x - hi.astype(F32)).astype(BF16)
    return _dot(hi, g_ones) + _dot(lo, g_ones)


def _rope128(x, cos, sin):
    lane = lax.broadcasted_iota(I32, x.shape, 1)
    first = (lane % 32) < 16
    up = pltpu.roll(x, 128 - 16, 1)
    dn = pltpu.roll(x, 16, 1)
    return x * cos + jnp.where(first, up, dn) * sin


def _layer_norm(z, g, b):
    mu = jnp.mean(z, axis=-1, keepdims=True)
    zc = z - mu
    var = jnp.mean(zc * zc, axis=-1, keepdims=True)
    return zc * lax.rsqrt(var + LN_EPS) * g + b


def _mod_kernel(c_ref, w_ref, b_ref, o_ref):
    x = c_ref[...]
    x = x * _sigmoid(x)
    o_ref[...] = jnp.dot(x, w_ref[...], precision=HIGHEST, preferred_element_type=F32) + b_ref[...]


def _modulation(cond8, w_mod, b_mod):
    depth = w_mod.shape[0]
    tn = 1536
    return pl.pallas_call(
        _mod_kernel,
        grid=(depth, 6 * D_MODEL // tn),
        in_specs=[
            pl.BlockSpec((8, D_MODEL), lambda l, j: (0, 0)),
            pl.BlockSpec((None, D_MODEL, tn), lambda l, j: (l, 0, j)),
            pl.BlockSpec((None, 1, tn), lambda l, j: (l, 0, j)),
        ],
        out_specs=pl.BlockSpec((None, 8, tn), lambda l, j: (l, 0, j)),
        out_shape=jax.ShapeDtypeStruct((depth, 8, 6 * D_MODEL), F32),
        compiler_params=_cparams(("arbitrary", "arbitrary")),
        name="modulation",
    )(cond8, w_mod, b_mod.reshape(depth, 1, 6 * D_MODEL))


def _group_of_row(row0, bounds):
    g = jnp.zeros((), I32)
    for b in bounds:
        g = g + jnp.where(row0 >= b, 1, 0).astype(I32)
    return g


def _in_kernel(h_ref, mod_ref, w_ref, w2_ref, w0_ref, a2_ref, a0_ref, g2_ref, kkw_ref, ka_ref,
               rk_ref, qn_ref, knw_ref, cos_ref, sin_ref, ones_ref,
               r_o, v_o, kk_o, lw_o, kd_o, b_o, g_o, bon_o, q_o, k_o, vb_o, p_o, *, tm, bounds):
    grp = _group_of_row(pl.program_id(0) * tm, bounds)
    mod = mod_ref[pl.ds(grp, 1), :]
    sh1 = mod[:, 0:D_MODEL]
    sc1 = mod[:, D_MODEL:2 * D_MODEL]
    u = h_ref[...] * (1.0 + sc1) + sh1
    proj = _dot(u.astype(BF16), w_ref[...])
    ones = ones_ref[...]

    r = proj[:, _O_R:_O_R + A_WIDTH]
    k = proj[:, _O_K:_O_K + A_WIDTH]
    v = proj[:, _O_V:_O_V + A_WIDTH]
    lo_w = proj[:, _O_LW:_O_LW + 128]
    lo_a = proj[:, _O_LA:_O_LA + 128]
    lo_g = proj[:, _O_LG:_O_LG + 128]

    w_pre = w0_ref[...] + _dot(jnp.tanh(lo_w).astype(BF16), w2_ref[...])
    logw = -DECAY_SCALE * _sigmoid(w_pre)
    a = _sigmoid(a0_ref[...] + _dot(lo_a.astype(BF16), a2_ref[...]))
    g = _dot(_sigmoid(lo_g).astype(BF16), g2_ref[...])

    kk = k * kkw_ref[...]
    kk = kk * lax.rsqrt(jnp.maximum(_segsum(kk * kk, ones), 1e-24))
    k2 = jnp.concatenate([k, k], axis=1)
    kd = k2 * (1.0 + (a - 1.0) * ka_ref[...])
    bvec = jnp.concatenate([kk, kk], axis=1) * a
    kd_sum = kd[:, 0:A_WIDTH] + kd[:, A_WIDTH:2 * A_WIDTH]
    bonus = _segsum(r * kd_sum * rk_ref[...], ones) * v

    r_o[...] = r
    v_o[...] = v
    kk_o[...] = kk
    lw_o[...] = logw
    kd_o[...] = kd
    b_o[...] = bvec
    g_o[...] = g
    bon_o[...] = bonus

    cos = cos_ref[...]
    sin = sin_ref[...]
    q = proj[:, _O_Q:_O_Q + B_WIDTH]
    qn = q * lax.rsqrt(_segsum(q * q, ones) * (1.0 / HEAD_DIM) + QK_EPS) * qn_ref[...]
    scale = 1.0 / math.sqrt(HEAD_DIM)
    q_o[...] = jnp.concatenate(
        [_rope128(qn[:, j * 128:(j + 1) * 128], cos, sin) for j in range(B_WIDTH // 128)], axis=1) * scale
    kb = proj[:, _O_KB:_O_KB + KV_WIDTH]
    kn = kb * lax.rsqrt(_segsum(kb * kb, ones[0:KV_WIDTH, 0:KV_WIDTH]) * (1.0 / HEAD_DIM) + QK_EPS) * knw_ref[...]
    k_o[...] = _rope128(kn, cos, sin)
    vb_o[...] = proj[:, _O_VB:_O_VB + KV_WIDTH]
    p_o[...] = proj[:, _O_P:_O_P + C_WIDTH]


def _in_projection(h, mods_l, lw, cos, sin, ones, *, tm, bounds):
    t = h.shape[0]
    full = lambda shape: pl.BlockSpec(shape, lambda i: (0,) * len(shape))
    rows = lambda w: pl.BlockSpec((tm, w), lambda i: (i, 0))
    widths = (A_WIDTH, A_WIDTH, A_WIDTH, 2 * A_WIDTH, 2 * A_WIDTH, 2 * A_WIDTH, A_WIDTH, A_WIDTH,
              B_WIDTH, KV_WIDTH, KV_WIDTH, C_WIDTH)
    return pl.pallas_call(
        functools.partial(_in_kernel, tm=tm, bounds=bounds),
        grid=(t // tm,),
        in_specs=[
            rows(D_MODEL), full((8, 6 * D_MODEL)), full((D_MODEL, IN_COLS)),
            full((128, 2 * A_WIDTH)), full((1, 2 * A_WIDTH)), full((128, 2 * A_WIDTH)), full((1, 2 * A_WIDTH)),
            full((LORA_G, A_WIDTH)), full((1, A_WIDTH)), full((1, 2 * A_WIDTH)), full((1, A_WIDTH)),
            full((1, B_WIDTH)), full((1, KV_WIDTH)), rows(128), rows(128), full((A_WIDTH, A_WIDTH)),
        ],
        out_specs=[rows(w) for w in widths],
        out_shape=[jax.ShapeDtypeStruct((t, w), F32) for w in widths],
        compiler_params=_cparams(("arbitrary",)),
        name="in_projection",
    )(h, mods_l, lw['w_in'], lw['w2blk'], lw['w0'], lw['a2blk'], lw['a0'], lw['g2'], lw['kkw'], lw['ka'],
      lw['rk'], lw['qn'], lw['knw'], cos, sin, ones)


def _split3(x):
    hi = x.astype(BF16)
    r1 = x - hi.astype(F32)
    mid = r1.astype(BF16)
    lo = (r1 - mid.astype(F32)).astype(BF16)
    return hi, mid, lo


def _scan_kernel(*refs, sb, nb, zero_init, want_state):
    fwd = refs[0:6]
    bwd = refs[6:12]
    pos = 12
    if not zero_init:
        s0_ref = refs[pos]
        pos += 1
    y_outs = (refs[pos], refs[pos + 1])
    pos += 2
    if want_state:
        sfin_o = refs[pos]
        pos += 1
    h_ref = refs[pos]

    step = pl.program_id(1)
    nstep = pl.num_programs(1)
    npair = A_HEADS // 2
    pw = 2 * HEAD_DIM
    pairs = [(q, d, hp) for q in range(sb) for d in range(N_DIR) for hp in range(npair)]

    @pl.when(step == 0)
    def _():
        h_ref[...] = jnp.zeros_like(h_ref)
        if not zero_init:
            for (q, d, hp) in pairs:
                for e in range(2):
                    lo = e * HEAD_DIM
                    h_ref[q, d, hp, lo:lo + HEAD_DIM, lo:lo + HEAD_DIM] = s0_ref[q, d, 2 * hp + e].T

    def iota(shape, dim):
        return lax.broadcasted_iota(I32, shape, dim)

    first64 = iota((CHUNK, pw), 1) < HEAD_DIM
    eye_p = ((iota((CHUNK, pw), 1) % HEAD_DIM) == iota((CHUNK, pw), 0)).astype(F32)
    eye2 = (iota((pw, pw), 0) == iota((pw, pw), 1)).astype(F32)
    same_head = (iota((pw, pw), 0) < HEAD_DIM) == (iota((pw, pw), 1) < HEAD_DIM)
    ti = iota((CHUNK, CHUNK), 0)
    si = iota((CHUNK, CHUNK), 1)
    incl16 = ((si <= ti).astype(BF16), (si >= ti).astype(BF16))
    t4 = iota((2 * CHUNK, 2 * pw), 0)
    s4 = iota((2 * CHUNK, 2 * pw), 1) % CHUNK
    incl_off = jnp.where(t4 < CHUNK, 0, 1)
    t4 = t4 % CHUNK
    mask4 = (s4 < t4 + incl_off, s4 > t4 - incl_off)

    def bdiag(x):
        return jnp.concatenate([jnp.where(first64, x, 0.0), jnp.where(first64, 0.0, x)], axis=0)

    lhs, bkt, gcol, v_p, amat = {}, {}, {}, {}, {}
    for q in range(sb):
        for d, ins in enumerate((fwd, bwd)):
            r_ref, v_ref, kk_ref, lw_ref, kd_ref, b_ref = ins
            for j in range(nb):
                rows = pl.ds(j * CHUNK, CHUNK)
                lw = lw_ref[q, rows, :]
                hi, mid, lo = _split3(lw)
                cs = _dot(incl16[d], hi) + _dot(incl16[d], mid) + _dot(incl16[d], lo)
                gam = jnp.exp(cs)
                gam_prev = jnp.exp(cs - lw)
                gam_inv = jnp.exp(-cs)
                gam_end = jnp.exp(cs[CHUNK - 1:CHUNK, :] if d == 0 else cs[0:1, :])
                a_hat = -kk_ref[q, rows, :] * gam_prev
                r_hat = r_ref[q, rows, :] * gam
                b_hat = b_ref[q, rows, :] * gam_inv
                k_hat = kd_ref[q, rows, :] * gam_inv
                v_all = v_ref[q, rows, :]
                for hp in range(npair):
                    key = (q, d, hp, j)
                    sl = slice(hp * pw, (hp + 1) * pw)
                    lhs[key] = jnp.concatenate([a_hat[:, sl], r_hat[:, sl]], axis=0).astype(BF16)
                    bh, kh, ge = b_hat[:, sl], k_hat[:, sl], gam_end[:, sl]
                    rhs = jnp.concatenate([bdiag(bh), bdiag(kh)], axis=0).astype(BF16)
                    amat[key] = _dot_nt(lhs[key], rhs)
                    gcol[key] = jnp.sum(eye2 * ge, axis=1, keepdims=True)
                    bkt[key] = (jnp.concatenate([bh, kh], axis=0) * ge).T.astype(BF16)
                    v_p[key] = v_all[:, sl]
    keys = list(amat.keys())
    npow, tcur, a_rb, xa, ya, vbd = {}, {}, {}, {}, {}, {}
    for key in keys:
        am = jnp.where(mask4[key[1]], amat[key], 0.0)
        npow[key] = am[0:CHUNK, 0:pw]
        tcur[key] = eye_p + npow[key]
        a_rb[key] = am[CHUNK:2 * CHUNK, 0:pw].astype(BF16)
        vbd[key] = bdiag(v_p[key]).astype(BF16)
        xa[key] = _dot(am[0:CHUNK, pw:2 * pw].astype(BF16), vbd[key])
        ya[key] = _dot(am[CHUNK:2 * CHUNK, pw:2 * pw].astype(BF16), vbd[key])
    nsq = int(math.log2(CHUNK)) - 1
    for key in keys:
        npow[key] = _dot(npow[key].astype(BF16), bdiag(npow[key]).astype(BF16))
    for _ in range(nsq - 1):
        for key in keys:
            rhs = jnp.concatenate([bdiag(npow[key]), bdiag(tcur[key])], axis=1).astype(BF16)
            res = _dot(npow[key].astype(BF16), rhs)
            npow[key] = res[:, 0:pw]
            tcur[key] = tcur[key] + res[:, pw:2 * pw]
    t_inv = {}
    for key in keys:
        t_inv[key] = (tcur[key] + _dot(npow[key].astype(BF16), bdiag(tcur[key]).astype(BF16))).astype(BF16)

    hcur = {pr: h_ref[pr[0], pr[1], pr[2]] for pr in pairs}
    for jj in range(nb):
        z, u = {}, {}
        kof = lambda pr: (pr[0], pr[1], pr[2], jj if pr[1] == 0 else nb - 1 - jj)
        for pr in pairs:
            z[pr] = _dot(lhs[kof(pr)], hcur[pr].astype(BF16))
        for pr in pairs:
            key = kof(pr)
            x = z[pr][0:CHUNK] + xa[key]
            u[pr] = _dot(t_inv[key], bdiag(x).astype(BF16))
        for pr in pairs:
            key = kof(pr)
            q, d, hp, j = key
            y = z[pr][CHUNK:2 * CHUNK] + ya[key] + _dot(a_rb[key], bdiag(u[pr]).astype(BF16))
            y_outs[d][q, pl.ds(j * CHUNK, CHUNK), hp * pw:(hp + 1) * pw] = y
            uv = jnp.concatenate([u[pr], v_p[key]], axis=0).astype(BF16)
            hcur[pr] = hcur[pr] * gcol[key] + jnp.where(same_head, _dot(bkt[key], uv), 0.0)
    for pr in pairs:
        h_ref[pr[0], pr[1], pr[2]] = hcur[pr]

    if want_state:
        @pl.when(step == nstep - 1)
        def _():
            for (q, d, hp) in pairs:
                for e in range(2):
                    lo = e * HEAD_DIM
                    sfin_o[q, d, 2 * hp + e] = hcur[(q, d, hp)][lo:lo + HEAD_DIM, lo:lo + HEAD_DIM].T


def _rwkv_scan(ops, s0, *, row0, nseq, seqlen, sb, nb, want_state):
    g = nb * CHUNK
    nstep = seqlen // g
    zero_init = s0 is None
    ops3 = [a.reshape(a.shape[0] // seqlen, seqlen, a.shape[1]) for a in ops]
    base = row0 // (seqlen * sb)
    fidx = lambda col: (lambda s, c: (base + s, c, col))
    bidx = lambda col: (lambda s, c: (base + s, nstep - 1 - c, col))
    blk = lambda im: pl.BlockSpec((sb, g, A_WIDTH), im)
    in_specs = [blk(fidx(0))] * 6 + [blk(bidx(0))] * 3 + [blk(bidx(1))] * 3
    args = ops3 + ops3
    state_spec = pl.BlockSpec((sb, N_DIR, A_HEADS, HEAD_DIM, HEAD_DIM), lambda s, c: (s, 0, 0, 0, 0))
    if not zero_init:
        in_specs.append(state_spec)
        args.append(s0)
    out_specs = [blk(lambda s, c: (s, c, 0)), blk(lambda s, c: (s, nstep - 1 - c, 0))]
    out_shape = [jax.ShapeDtypeStruct((nseq, seqlen, A_WIDTH), F32)] * 2
    if want_state:
        out_specs.append(state_spec)
        out_shape.append(jax.ShapeDtypeStruct((nseq, N_DIR, A_HEADS, HEAD_DIM, HEAD_DIM), F32))
    return pl.pallas_call(
        functools.partial(_scan_kernel, sb=sb, nb=nb, zero_init=zero_init, want_state=want_state),
        grid=(nseq // sb, nstep),
        in_specs=in_specs,
        out_specs=out_specs,
        out_shape=out_shape,
        scratch_shapes=[pltpu.VMEM((sb, N_DIR, A_HEADS // 2, 2 * HEAD_DIM, 2 * HEAD_DIM), F32)],
        compiler_params=_cparams(("arbitrary", "arbitrary")),
        name="rwkv_scan_ctx" if zero_init else "rwkv_scan_dec",
    )(*args)


def _attn_kernel(*refs, tq, with_cache):
    grp = B_Q_HEADS // B_KV_HEADS
    if with_cache:
        q_ref, k_ref, v_ref, ck_ref, cv_ref, o_ref = refs
        kall = jnp.concatenate([k_ref[...], ck_ref[...]], axis=0)
        vall = jnp.concatenate([v_ref[...], cv_ref[...]], axis=0)
    else:
        q_ref, k_ref, v_ref, o_ref = refs
        kall = k_ref[...]
        vall = v_ref[...]
    q = q_ref[...]
    outs = []
    for hk in range(B_KV_HEADS):
        ks = kall[:, hk * HEAD_DIM:(hk + 1) * HEAD_DIM].astype(BF16)
        vs = vall[:, hk * HEAD_DIM:(hk + 1) * HEAD_DIM].astype(BF16)
        qs = jnp.concatenate(
            [q[:, (hk * grp + g) * HEAD_DIM:(hk * grp + g + 1) * HEAD_DIM] for g in range(grp)], axis=0)
        s = _dot_nt(qs.astype(BF16), ks)
        m = jnp.max(s, axis=-1, keepdims=True)
        p = jnp.exp(s - m)
        den = jnp.sum(p, axis=-1, keepdims=True)
        o = _dot(p.astype(BF16), vs) / den
        outs.extend(o[g * tq:(g + 1) * tq] for g in range(grp))
    o_ref[...] = jnp.concatenate(outs, axis=1)


def _attention(q, k_tok, v_tok, cache, *, row0, nseq, seqlen, tq):
    nq = seqlen // tq
    base = row0 // tq
    kbase = row0 // seqlen
    kspec = pl.BlockSpec((seqlen, KV_WIDTH), lambda s, i: (kbase + s, 0))
    in_specs = [pl.BlockSpec((tq, B_WIDTH), lambda s, i: (base + s * nq + i, 0)), kspec, kspec]
    args = [q, k_tok, v_tok]
    if cache is not None:
        ck, cv, layer = cache
        cspec = pl.BlockSpec((None, None, ck.shape[2], KV_WIDTH), lambda s, i: (s, layer, 0, 0))
        in_specs += [cspec, cspec]
        args += [ck, cv]
    return pl.pallas_call(
        functools.partial(_attn_kernel, tq=tq, with_cache=cache is not None),
        grid=(nseq, nq),
        in_specs=in_specs,
        out_specs=pl.BlockSpec((tq, B_WIDTH), lambda s, i: (s * nq + i, 0)),
        out_shape=jax.ShapeDtypeStruct((nseq * seqlen, B_WIDTH), F32),
        compiler_params=_cparams(("arbitrary", "arbitrary")),
        name="attention_ctx" if cache is None else "attention_dec",
    )(*args)


def _pool_kernel(p_ref, w_ref, sc_ref, o_ref, *, seqlen):
    x = p_ref[...]
    t = lax.broadcasted_iota(I32, x.shape, 0)
    lane = lax.broadcasted_iota(I32, x.shape, 1)

    def down(a, k):
        return jnp.where(t >= k, pltpu.roll(a, k, 0), 0.0)

    def up(a, k):
        return jnp.where(t < seqlen - k, pltpu.roll(a, seqlen - k, 0), 0.0)

    past = [x]
    futr = [x]
    for j in range(len(POOL_WINDOWS) - 1):
        past.append(past[j] + down(past[j], 2 ** j))
        futr.append(futr[j] + up(futr[j], 2 ** j))
    win_sum = jnp.zeros_like(x)
    cnt = jnp.zeros_like(x)
    for gi, win in enumerate(POOL_WINDOWS):
        half = win // 2
        in_group = (lane // HEAD_DIM) == gi
        win_sum = jnp.where(in_group, down(past[gi], 1) + futr[gi], win_sum)
        n = (jnp.minimum(t + half, seqlen) - jnp.maximum(t - half, 0)).astype(F32)
        cnt = jnp.where(in_group, n, cnt)
    d = win_sum / cnt - x
    o_ref[...] = _dot(d.astype(BF16), w_ref[...]) * sc_ref[...]


def _pool(p, w_blk, scale, *, row0, nseq, seqlen):
    base = row0 // seqlen
    return pl.pallas_call(
        functools.partial(_pool_kernel, seqlen=seqlen),
        grid=(nseq,),
        in_specs=[pl.BlockSpec((seqlen, C_WIDTH), lambda s: (base + s, 0)),
                  pl.BlockSpec((C_WIDTH, C_WIDTH), lambda s: (0, 0)),
                  pl.BlockSpec((1, C_WIDTH), lambda s: (0, 0))],
        out_specs=pl.BlockSpec((seqlen, C_WIDTH), lambda s: (s, 0)),
        out_shape=jax.ShapeDtypeStruct((nseq * seqlen, C_WIDTH), F32),
        compiler_params=_cparams(("arbitrary",)),
        name="pool_mixer",
    )(p, w_blk, scale)


def _out_kernel(yfc_ref, yfd_ref, ybc_ref, ybd_ref, g_ref, bon_ref, attc_ref, attd_ref, poolc_ref, poold_ref,
                h_ref, mod_ref, w_ref, lng_ref, lnb_ref, gng_ref, gnb_ref, ones_ref, wrh_ref, wrl_ref, br_ref,
                h1_o, u2_o, lg_o, *, tm, bounds, alpha, n_ctx_tiles):
    grp = _group_of_row(pl.program_id(0) * tm, bounds)
    mod = mod_ref[pl.ds(grp, 1), :]
    g1 = mod[:, 2 * D_MODEL:3 * D_MODEL]
    sh2 = mod[:, 3 * D_MODEL:4 * D_MODEL]
    sc2 = mod[:, 4 * D_MODEL:5 * D_MODEL]
    ones = ones_ref[...]
    is_ctx = pl.program_id(0) < n_ctx_tiles
    pick = lambda c_ref, d_ref: jnp.where(is_ctx, c_ref[...], d_ref[...])
    y = pick(yfc_ref, yfd_ref) + pick(ybc_ref, ybd_ref)
    mu = _segsum(y, ones) * (1.0 / HEAD_DIM)
    yc = y - mu
    var = _segsum(yc * yc, ones) * (1.0 / HEAD_DIM)
    yn = yc * lax.rsqrt(var + GN_EPS) * gng_ref[...] + gnb_ref[...]
    a_out = (yn + bon_ref[...]) * g_ref[...]
    m = (_dot(a_out.astype(BF16), w_ref[0:A_WIDTH, :])
         + _dot(pick(attc_ref, attd_ref).astype(BF16), w_ref[A_WIDTH:A_WIDTH + B_WIDTH, :])
         + _dot(pick(poolc_ref, poold_ref).astype(BF16), w_ref[A_WIDTH + B_WIDTH:D_MODEL, :]))
    h1 = _layer_norm(alpha * h_ref[...] + g1 * m, lng_ref[...], lnb_ref[...])
    h1_o[...] = h1
    u2 = h1 * (1.0 + sc2) + sh2
    u2_o[...] = u2.astype(BF16)
    u_hi = u2.astype(BF16)
    u_lo = (u2 - u_hi.astype(F32)).astype(BF16)
    lg_o[...] = (_dot(u_hi, wrh_ref[...]) + _dot(u_lo, wrh_ref[...]) + _dot(u_hi, wrl_ref[...])) + br_ref[...]


def _out_projection(yf, yb, g, bonus, att, pool, h, mods_l, lw, ones, *, tm, bounds, alpha):
    t = h.shape[0]
    nct = yf[0].shape[0] // tm
    full = lambda shape: pl.BlockSpec(shape, lambda i: (0,) * len(shape))
    rows = lambda w: pl.BlockSpec((tm, w), lambda i: (i, 0))
    rows_c = lambda w: pl.BlockSpec((tm, w), lambda i: (jnp.minimum(i, nct - 1), 0))
    rows_d = lambda w: pl.BlockSpec((tm, w), lambda i: (jnp.maximum(i - nct, 0), 0))
    return pl.pallas_call(
        functools.partial(_out_kernel, tm=tm, bounds=bounds, alpha=alpha, n_ctx_tiles=nct),
        grid=(t // tm,),
        in_specs=[rows_c(A_WIDTH), rows_d(A_WIDTH), rows_c(A_WIDTH), rows_d(A_WIDTH), rows(A_WIDTH), rows(A_WIDTH),
                  rows_c(B_WIDTH), rows_d(B_WIDTH), rows_c(C_WIDTH), rows_d(C_WIDTH),
                  rows(D_MODEL), full((8, 6 * D_MODEL)), full((D_MODEL, D_MODEL)), full((1, D_MODEL)),
                  full((1, D_MODEL)), full((1, A_WIDTH)), full((1, A_WIDTH)), full((A_WIDTH, A_WIDTH)),
                  full((D_MODEL, ROUTER_LANES)), full((D_MODEL, ROUTER_LANES)), full((1, ROUTER_LANES))],
        out_specs=[rows(D_MODEL), rows(D_MODEL), rows(ROUTER_LANES)],
        out_shape=[jax.ShapeDtypeStruct((t, D_MODEL), F32), jax.ShapeDtypeStruct((t, D_MODEL), BF16),
                   jax.ShapeDtypeStruct((t, ROUTER_LANES), F32)],
        compiler_params=_cparams(("arbitrary",)),
        name="out_projection",
    )(yf[0], yf[1], yb[0], yb[1], g, bonus, att[0], att[1], pool[0], pool[1], h, mods_l, lw['w_out'],
      lw['ln1_g'], lw['ln1_b'], lw['gn_g'], lw['gn_b'], ones, lw['w_router_hi'], lw['w_router_lo'], lw['b_router'])


def _router_gates(logits):
    lane = lax.broadcasted_iota(I32, logits.shape, 1)
    lane_f = lane.astype(F32)
    neg = -jnp.inf
    big = 1e9

    def first_lane(mask):
        return jnp.min(jnp.where(mask, lane_f, big), axis=-1, keepdims=True).astype(I32)

    is_grp = lane < N_GROUPS
    gmax = jnp.max(jnp.where(is_grp, logits, neg), axis=-1, keepdims=True)
    sel = first_lane(is_grp & (logits == gmax))
    gsum = jnp.sum(jnp.where(is_grp, jnp.exp(logits - gmax), 0.0), axis=-1, keepdims=True)
    gsel = 1.0 / gsum
    first_exp = N_GROUPS + sel * EXP_PER_GROUP
    in_sel = (lane >= first_exp) & (lane < first_exp + EXP_PER_GROUP)
    v1 = jnp.max(jnp.where(in_sel, logits, neg), axis=-1, keepdims=True)
    i1 = first_lane(in_sel & (logits == v1))
    rest = in_sel & (lane != i1)
    v2 = jnp.max(jnp.where(rest, logits, neg), axis=-1, keepdims=True)
    i2 = first_lane(rest & (logits == v2))
    e2 = jnp.exp(v2 - v1)
    t1 = (1.0 / (1.0 + e2)) * gsel
    t2 = (e2 / (1.0 + e2)) * gsel
    return jnp.where(lane == i1, t1, 0.0) + jnp.where(lane == i2, t2, 0.0), sel


def _moe_kernel(x_ref, lg_ref, w1_ref, w3_ref, w2_ref, h1_ref, mod_ref, lng_ref, lnb_ref, o_ref,
                xs_ref, gs_ref, acc_ref, *, tm, sub, bounds, alpha):
    gates, sel = _router_gates(lg_ref[...])
    lane = lax.broadcasted_iota(I32, gates.shape, 1)
    onehot = jnp.where(lane == sel, 1.0, 0.0)
    row_i = lax.broadcasted_iota(I32, (tm, tm), 0)
    col_i = lax.broadcasted_iota(I32, (tm, tm), 1)
    earlier = jnp.where(col_i < row_i, 1.0, 0.0).astype(BF16)
    rank = _dot(earlier, onehot.astype(BF16))
    lane1 = lax.broadcasted_iota(I32, (1, ROUTER_LANES), 1)
    counts = jnp.sum(onehot, axis=0, keepdims=True)
    cnt = [jnp.sum(jnp.where(lane1 == g, counts, 0.0)).astype(I32) for g in range(N_GROUPS)]
    off = [jnp.zeros((), I32)]
    for g in range(N_GROUPS - 1):
        off.append(off[g] + cnt[g])
    offs = jnp.zeros((1, ROUTER_LANES), F32)
    for g in range(N_GROUPS):
        offs = jnp.where(lane1 == g, off[g].astype(F32), offs)
    pos = jnp.sum(onehot * (rank + offs), axis=1, keepdims=True).astype(I32)
    pos_row = jnp.broadcast_to(pos, (tm, ROUTER_LANES)).T[0:1, :]
    perm = jnp.where(row_i == pos_row, 1.0, 0.0).astype(BF16)
    perm_t = jnp.where(col_i == pos, 1.0, 0.0).astype(BF16)
    xs_ref[...] = _dot(perm, x_ref[...]).astype(BF16)
    g_hi = gates.astype(BF16)
    g_lo = (gates - g_hi.astype(F32)).astype(BF16)
    gs_ref[...] = _dot(perm, g_hi) + _dot(perm, g_lo)
    acc_ref[...] = jnp.zeros_like(acc_ref)

    lane_s = lax.broadcasted_iota(I32, (sub, ROUTER_LANES), 1)
    for s in range(tm // sub):
        rows = pl.ds(s * sub, sub)
        for g in range(N_GROUPS):
            @pl.when((off[g] < (s + 1) * sub) & (off[g] + cnt[g] > s * sub))
            def _():
                xb = xs_ref[rows, :]
                gsb = gs_ref[rows, :]
                hids = []
                for j in range(EXP_PER_GROUP):
                    e = g * EXP_PER_GROUP + j
                    ge = jnp.sum(jnp.where(lane_s == N_GROUPS + e, gsb, 0.0), axis=-1, keepdims=True)
                    h1 = _dot(xb, w1_ref[e])
                    h3 = _dot(xb, w3_ref[e])
                    hids.append(((h1 * _sigmoid(h1)) * h3 * ge).astype(BF16))
                hid = jnp.concatenate(hids, axis=1)
                w2g = w2_ref[g * EXP_PER_GROUP:(g + 1) * EXP_PER_GROUP].reshape(EXP_PER_GROUP * D_EXPERT, D_MODEL)
                acc_ref[rows, :] += _dot(hid, w2g)

    f = _dot(perm_t, acc_ref[...].astype(BF16))
    grp = _group_of_row(pl.program_id(0) * tm, bounds)
    g2 = mod_ref[pl.ds(grp, 1), 5 * D_MODEL:6 * D_MODEL]
    o_ref[...] = _layer_norm(alpha * h1_ref[...] + g2 * f, lng_ref[...], lnb_ref[...])


def _moe(u2, logits, h1, mods_l, lw, *, tm, sub, bounds, alpha):
    t = h1.shape[0]
    full = lambda shape: pl.BlockSpec(shape, lambda i: (0,) * len(shape))
    once = lambda shape: pl.BlockSpec(shape, lambda i: (0,) * len(shape), pipeline_mode=pl.Buffered(1))
    rows = lambda w: pl.BlockSpec((tm, w), lambda i: (i, 0))
    return pl.pallas_call(
        functools.partial(_moe_kernel, tm=tm, sub=sub, bounds=bounds, alpha=alpha),
        grid=(t // tm,),
        in_specs=[rows(D_MODEL), rows(ROUTER_LANES),
                  once((N_EXPERTS, D_MODEL, D_EXPERT)), once((N_EXPERTS, D_MODEL, D_EXPERT)),
                  once((N_EXPERTS, D_EXPERT, D_MODEL)),
                  rows(D_MODEL), full((8, 6 * D_MODEL)), full((1, D_MODEL)), full((1, D_MODEL))],
        out_specs=rows(D_MODEL),
        out_shape=jax.ShapeDtypeStruct((t, D_MODEL), F32),
        scratch_shapes=[pltpu.VMEM((tm, D_MODEL), BF16), pltpu.VMEM((tm, ROUTER_LANES), F32),
                        pltpu.VMEM((tm, D_MODEL), F32)],
        compiler_params=_cparams(("arbitrary",)),
        name="moe",
    )(u2, logits, lw['moe_w1'], lw['moe_w3'], lw['moe_w2'], h1, mods_l, lw['ln2_g'], lw['ln2_b'])


def _rope_tables(n_ctx, dec_batch, dec_seq):
    n = HEAD_DIM // 4
    pos = jnp.arange(dec_seq)
    row = (pos // GRID_W).astype(F32)
    col = (pos % GRID_W).astype(F32)
    inv = 1.0 / (ROPE_THETA ** (jnp.arange(n, dtype=F32) / n))
    ra, ca = row[:, None] * inv, col[:, None] * inv
    cos64 = jnp.concatenate([jnp.cos(ra), jnp.cos(ra), jnp.cos(ca), jnp.cos(ca)], axis=-1)
    sin64 = jnp.concatenate([-jnp.sin(ra), jnp.sin(ra), -jnp.sin(ca), jnp.sin(ca)], axis=-1)
    cos_s = jnp.tile(jnp.concatenate([cos64, cos64], axis=-1), (dec_batch, 1))
    sin_s = jnp.tile(jnp.concatenate([sin64, sin64], axis=-1), (dec_batch, 1))
    cos = jnp.concatenate([jnp.ones((n_ctx, 128), F32), cos_s], axis=0)
    sin = jnp.concatenate([jnp.zeros((n_ctx, 128), F32), sin_s], axis=0)
    return cos, sin


def _block_diag2(w):
    z = jnp.zeros_like(w[0])
    return jnp.concatenate([jnp.concatenate([w[0], z], axis=1), jnp.concatenate([z, w[1]], axis=1)], axis=0)


def _layer_weights(l, p):
    d = D_MODEL
    pool_blk = jnp.zeros((C_WIDTH, C_WIDTH), F32)
    for gi in range(len(POOL_WINDOWS)):
        pool_blk = pool_blk.at[gi * 64:(gi + 1) * 64, gi * 64:(gi + 1) * 64].set(p['pool_w'][l, gi])
    w_router = jnp.zeros((d, ROUTER_LANES), F32)
    w_router = w_router.at[:, 0:N_GROUPS].set(p['rt_grp_w'][l])
    w_router = w_router.at[:, N_GROUPS:N_GROUPS + N_EXPERTS].set(p['rt_exp_w'][l])
    b_router = jnp.zeros((1, ROUTER_LANES), F32)
    b_router = b_router.at[0, 0:N_GROUPS].set(p['rt_grp_b'][l])
    b_router = b_router.at[0, N_GROUPS:N_GROUPS + N_EXPERTS].set(p['rt_exp_b'][l])
    return {
        'w_in': p['w_in'][l].astype(BF16),
        'w_out': p['w_out'][l].astype(BF16),
        'w2blk': _block_diag2(p['rw_w2'][l]).astype(BF16),
        'w0': p['rw_w0'][l].reshape(1, 2 * A_WIDTH),
        'a2blk': _block_diag2(p['rw_a2'][l]).astype(BF16),
        'a0': p['rw_a0'][l].reshape(1, 2 * A_WIDTH),
        'g2': p['rw_g2'][l].astype(BF16),
        'kkw': p['rw_kk'][l].reshape(1, A_WIDTH),
        'ka': jnp.tile(p['rw_ka'][l].reshape(1, A_WIDTH), (1, 2)),
        'rk': p['rw_rk'][l].reshape(1, A_WIDTH),
        'gn_g': p['rw_gn_g'][l].reshape(1, A_WIDTH),
        'gn_b': p['rw_gn_b'][l].reshape(1, A_WIDTH),
        'qn': jnp.tile(p['q_norm'][l].reshape(1, HEAD_DIM), (1, B_Q_HEADS)),
        'knw': jnp.tile(p['k_norm'][l].reshape(1, HEAD_DIM), (1, B_KV_HEADS)),
        'pool_blk': pool_blk.astype(BF16),
        'pool_scale': p['pool_scale'][l].reshape(1, C_WIDTH),
        'ln1_g': p['ln1_g'][l].reshape(1, d), 'ln1_b': p['ln1_b'][l].reshape(1, d),
        'ln2_g': p['ln2_g'][l].reshape(1, d), 'ln2_b': p['ln2_b'][l].reshape(1, d),
        'w_router_hi': w_router.astype(BF16),
        'w_router_lo': (w_router - w_router.astype(BF16).astype(F32)).astype(BF16),
        'b_router': b_router,
        'moe_w1': p['moe_w1'][l].astype(BF16),
        'moe_w3': p['moe_w3'][l].astype(BF16),
        'moe_w2': p['moe_w2'][l].astype(BF16),
    }


def _forward(x_prompt, x_sample, state_rwkv, cache_k, cache_v, c, c_ctx, p):
    batch, seq, d = x_prompt.shape
    dec_batch, dec_seq, _ = x_sample.shape
    depth = p['w_in'].shape[0]
    past = cache_k.shape[2]
    n_ctx = batch * seq
    n_tok = n_ctx + dec_batch * dec_seq
    alpha = (2 * depth) ** 0.25
    bounds = tuple(n_ctx + j * dec_seq for j in range(dec_batch))
    tm = min(256, seq)
    tm_moe = math.gcd(n_ctx, dec_seq, 512)
    tq_ctx = min(256, seq)
    tq_dec = min(256, dec_seq)
    sb_ctx = 2

    cond8 = jnp.zeros((8, d), F32).at[0].set(c_ctx).at[1:1 + dec_batch].set(c)
    mods = _modulation(cond8, p['w_mod'], p['b_mod'])
    cos, sin = _rope_tables(n_ctx, dec_batch, dec_seq)
    head_id = jnp.arange(A_WIDTH) // HEAD_DIM
    ones = (head_id[:, None] == head_id[None, :]).astype(BF16)

    h = jnp.concatenate([x_prompt.reshape(n_ctx, d), x_sample.reshape(dec_batch * dec_seq, d)], axis=0)
    ck4 = cache_k.reshape(dec_batch, depth, past, KV_WIDTH)
    cv4 = cache_v.reshape(dec_batch, depth, past, KV_WIDTH)
    new_s, new_k, new_v = [], [], []
    for l in range(depth):
        lw = _layer_weights(l, p)
        (r, v, kk, logw, kd, bvec, g, bonus, q, k_att, v_att, pp) = _in_projection(
            h, mods[l], lw, cos, sin, ones, tm=tm, bounds=bounds)
        scan_ops = (r, v, kk, logw, kd, bvec)
        yf_c, yb_c, s_fin = _rwkv_scan(scan_ops, None, row0=0, nseq=batch, seqlen=seq, sb=sb_ctx,
                                       nb=seq // CHUNK, want_state=True)
        yf_d, yb_d = _rwkv_scan(scan_ops, state_rwkv[:, l], row0=n_ctx, nseq=dec_batch, seqlen=dec_seq,
                                sb=dec_batch, nb=4, want_state=False)
        yf = (yf_c.reshape(n_ctx, A_WIDTH), yf_d.reshape(n_tok - n_ctx, A_WIDTH))
        yb = (yb_c.reshape(n_ctx, A_WIDTH), yb_d.reshape(n_tok - n_ctx, A_WIDTH))
        att = (_attention(q, k_att, v_att, None, row0=0, nseq=batch, seqlen=seq, tq=tq_ctx),
               _attention(q, k_att, v_att, (ck4, cv4, l), row0=n_ctx, nseq=dec_batch, seqlen=dec_seq, tq=tq_dec))
        pool = (_pool(pp, lw['pool_blk'], lw['pool_scale'], row0=0, nseq=batch, seqlen=seq),
                _pool(pp, lw['pool_blk'], lw['pool_scale'], row0=n_ctx, nseq=dec_batch, seqlen=dec_seq))
        h1, u2, logits = _out_projection(yf, yb, g, bonus, att, pool, h, mods[l], lw, ones,
                                         tm=tm, bounds=bounds, alpha=alpha)
        h = _moe(u2, logits, h1, mods[l], lw, tm=tm_moe, sub=128, bounds=bounds, alpha=alpha)
        new_s.append(s_fin)
        new_k.append(k_att[:n_ctx].reshape(batch, seq, B_KV_HEADS, HEAD_DIM))
        new_v.append(v_att[:n_ctx].reshape(batch, seq, B_KV_HEADS, HEAD_DIM))
    return (h[:n_ctx].reshape(batch, seq, d), h[n_ctx:].reshape(dec_batch, dec_seq, d),
            jnp.stack(new_s, axis=1), jnp.stack(new_k, axis=1), jnp.stack(new_v, axis=1))


def kernel(x_prompt, x_sample, state_rwkv, cache_k, cache_v, c, c_ctx, w_in, w_out, w_mod, b_mod, ln1_g, ln1_b, ln2_g, ln2_b, rw_w0, rw_w2, rw_a0, rw_a2, rw_g2, rw_kk, rw_ka, rw_rk, rw_gn_g, rw_gn_b, q_norm, k_norm, pool_w, pool_scale, rt_grp_w, rt_grp_b, rt_exp_w, rt_exp_b, moe_w1, moe_w3, moe_w2):
    p = dict(w_in=w_in, w_out=w_out, w_mod=w_mod, b_mod=b_mod, ln1_g=ln1_g, ln1_b=ln1_b, ln2_g=ln2_g,
             ln2_b=ln2_b, rw_w0=rw_w0, rw_w2=rw_w2, rw_a0=rw_a0, rw_a2=rw_a2, rw_g2=rw_g2, rw_kk=rw_kk,
             rw_ka=rw_ka, rw_rk=rw_rk, rw_gn_g=rw_gn_g, rw_gn_b=rw_gn_b, q_norm=q_norm, k_norm=k_norm,
             pool_w=pool_w, pool_scale=pool_scale, rt_grp_w=rt_grp_w, rt_grp_b=rt_grp_b, rt_exp_w=rt_exp_w,
             rt_exp_b=rt_exp_b, moe_w1=moe_w1, moe_w3=moe_w3, moe_w2=moe_w2)
    return _forward(x_prompt, x_sample, state_rwkv, cache_k, cache_v, c, c_ctx, p)
```

```python
import functools
import math

import jax
import jax.numpy as jnp
from jax import lax
from jax.experimental import pallas as pl
from jax.experimental.pallas import tpu as pltpu

F32 = jnp.float32
BF16 = jnp.bfloat16
I32 = jnp.int32
HIGHEST = lax.Precision.HIGHEST

D_MODEL = 1024
GRID_W = 64
HEAD_DIM = 64
A_HEADS = 6
A_WIDTH = A_HEADS * HEAD_DIM
LORA_W = 64
LORA_A = 64
LORA_G = 128
N_DIR = 2
DECAY_SCALE = 0.606531
B_Q_HEADS = 6
B_KV_HEADS = 2
B_WIDTH = B_Q_HEADS * HEAD_DIM
KV_WIDTH = B_KV_HEADS * HEAD_DIM
ROPE_THETA = 10000.0
POOL_WINDOWS = (2, 4, 8, 16)
C_WIDTH = 256
IN_COLS = 2432
N_GROUPS = 4
EXP_PER_GROUP = 4
N_EXPERTS = 16
D_EXPERT = 256
LN_EPS = 1e-5
GN_EPS = 64e-5
QK_EPS = 1e-6

_O_R, _O_K, _O_V = 0, 384, 768
_O_LW, _O_LA, _O_LG = 1152, 1280, 1408
_O_Q, _O_KB, _O_VB, _O_P = 1536, 1920, 2048, 2176

CHUNK = 64
ROUTER_LANES = 128
VMEM_LIMIT = 56 * 1024 * 1024


def _cparams(sem):
    return pltpu.CompilerParams(dimension_semantics=sem, vmem_limit_bytes=VMEM_LIMIT)


def _sigmoid(x):
    return 1.0 / (1.0 + jnp.exp(-x))


def _dot(a, b):
    return jnp.dot(a, b, preferred_element_type=F32)


def _dot_nt(a, b):
    return lax.dot_general(a, b, (((1,), (1,)), ((), ())), preferred_element_type=F32)


def _segsum(x, g_ones):
    return _dot(x.astype(BF16), g_ones)


def _rope128(x, cos, sin):
    lane = lax.broadcasted_iota(I32, x.shape, 1)
    first = (lane % 32) < 16
    up = pltpu.roll(x, 128 - 16, 1)
    dn = pltpu.roll(x, 16, 1)
    return x * cos + jnp.where(first, up, dn) * sin


def _layer_norm(z, g, b):
    mu = jnp.mean(z, axis=-1, keepdims=True)
    zc = z - mu
    var = jnp.mean(zc * zc, axis=-1, keepdims=True)
    return zc * lax.rsqrt(var + LN_EPS) * g + b


def _mod_kernel(c_ref, w_ref, b_ref, o_ref):
    x = c_ref[...]
    x = x * _sigmoid(x)
    o_ref[...] = jnp.dot(x, w_ref[...], precision=HIGHEST, preferred_element_type=F32) + b_ref[...]


def _modulation(cond8, w_mod, b_mod):
    depth = w_mod.shape[0]
    tn = 1536
    return pl.pallas_call(
        _mod_kernel,
        grid=(depth, 6 * D_MODEL // tn),
        in_specs=[
            pl.BlockSpec((8, D_MODEL), lambda l, j: (0, 0)),
            pl.BlockSpec((None, D_MODEL, tn), lambda l, j: (l, 0, j)),
            pl.BlockSpec((None, 1, tn), lambda l, j: (l, 0, j)),
        ],
        out_specs=pl.BlockSpec((None, 8, tn), lambda l, j: (l, 0, j)),
        out_shape=jax.ShapeDtypeStruct((depth, 8, 6 * D_MODEL), F32),
        compiler_params=_cparams(("arbitrary", "arbitrary")),
        name="modulation",
    )(cond8, w_mod, b_mod.reshape(depth, 1, 6 * D_MODEL))


def _group_of_row(row0, bounds):
    g = jnp.zeros((), I32)
    for b in bounds:
        g = g + jnp.where(row0 >= b, 1, 0).astype(I32)
    return g


def _in_kernel(h_ref, mod_ref, w_ref, w2_ref, w0_ref, a2_ref, a0_ref, g2_ref, kkw_ref, ka_ref,
               rk_ref, qn_ref, knw_ref, cos_ref, sin_ref, ones_ref,
               r_o, v_o, kk_o, lw_o, kd_o, b_o, g_o, bon_o, q_o, k_o, vb_o, p_o, *, tm, bounds):
    grp = _group_of_row(pl.program_id(0) * tm, bounds)
    mod = mod_ref[pl.ds(grp, 1), :]
    sh1 = mod[:, 0:D_MODEL]
    sc1 = mod[:, D_MODEL:2 * D_MODEL]
    u = h_ref[...] * (1.0 + sc1) + sh1
    proj = _dot(u.astype(BF16), w_ref[...])
    ones = ones_ref[...]

    r = proj[:, _O_R:_O_R + A_WIDTH]
    k = proj[:, _O_K:_O_K + A_WIDTH]
    v = proj[:, _O_V:_O_V + A_WIDTH]
    lo_w = proj[:, _O_LW:_O_LW + 128]
    lo_a = proj[:, _O_LA:_O_LA + 128]
    lo_g = proj[:, _O_LG:_O_LG + 128]

    w_pre = w0_ref[...] + _dot(jnp.tanh(lo_w).astype(BF16), w2_ref[...])
    logw = -DECAY_SCALE * _sigmoid(w_pre)
    a = _sigmoid(a0_ref[...] + _dot(lo_a.astype(BF16), a2_ref[...]))
    g = _dot(_sigmoid(lo_g).astype(BF16), g2_ref[...])

    kk = k * kkw_ref[...]
    kk = kk * lax.rsqrt(jnp.maximum(_segsum(kk * kk, ones), 1e-24))
    k2 = jnp.concatenate([k, k], axis=1)
    kd = k2 * (1.0 + (a - 1.0) * ka_ref[...])
    bvec = jnp.concatenate([kk, kk], axis=1) * a
    kd_sum = kd[:, 0:A_WIDTH] + kd[:, A_WIDTH:2 * A_WIDTH]
    bonus = _segsum(r * kd_sum * rk_ref[...], ones) * v

    r_o[...] = r
    v_o[...] = v
    kk_o[...] = kk
    lw_o[...] = logw
    kd_o[...] = kd
    b_o[...] = bvec
    g_o[...] = g
    bon_o[...] = bonus

    cos = cos_ref[...]
    sin = sin_ref[...]
    q = proj[:, _O_Q:_O_Q + B_WIDTH]
    qn = q * lax.rsqrt(_segsum(q * q, ones) * (1.0 / HEAD_DIM) + QK_EPS) * qn_ref[...]
    scale = 1.0 / math.sqrt(HEAD_DIM)
    q_o[...] = jnp.concatenate(
        [_rope128(qn[:, j * 128:(j + 1) * 128], cos, sin) for j in range(B_WIDTH // 128)], axis=1) * scale
    kb = proj[:, _O_KB:_O_KB + KV_WIDTH]
    kn = kb * lax.rsqrt(_segsum(kb * kb, ones[0:KV_WIDTH, 0:KV_WIDTH]) * (1.0 / HEAD_DIM) + QK_EPS) * knw_ref[...]
    k_o[...] = _rope128(kn, cos, sin)
    vb_o[...] = proj[:, _O_VB:_O_VB + KV_WIDTH]
    p_o[...] = proj[:, _O_P:_O_P + C_WIDTH]


def _in_projection(h, mods_l, lw, cos, sin, ones, *, tm, bounds):
    t = h.shape[0]
    full = lambda shape: pl.BlockSpec(shape, lambda i: (0,) * len(shape))
    rows = lambda w: pl.BlockSpec((tm, w), lambda i: (i, 0))
    widths = (A_WIDTH, A_WIDTH, A_WIDTH, 2 * A_WIDTH, 2 * A_WIDTH, 2 * A_WIDTH, A_WIDTH, A_WIDTH,
              B_WIDTH, KV_WIDTH, KV_WIDTH, C_WIDTH)
    return pl.pallas_call(
        functools.partial(_in_kernel, tm=tm, bounds=bounds),
        grid=(t // tm,),
        in_specs=[
            rows(D_MODEL), full((8, 6 * D_MODEL)), full((D_MODEL, IN_COLS)),
            full((128, 2 * A_WIDTH)), full((1, 2 * A_WIDTH)), full((128, 2 * A_WIDTH)), full((1, 2 * A_WIDTH)),
            full((LORA_G, A_WIDTH)), full((1, A_WIDTH)), full((1, 2 * A_WIDTH)), full((1, A_WIDTH)),
            full((1, B_WIDTH)), full((1, KV_WIDTH)), rows(128), rows(128), full((A_WIDTH, A_WIDTH)),
        ],
        out_specs=[rows(w) for w in widths],
        out_shape=[jax.ShapeDtypeStruct((t, w), F32) for w in widths],
        compiler_params=_cparams(("arbitrary",)),
        name="in_projection",
    )(h, mods_l, lw['w_in'], lw['w2blk'], lw['w0'], lw['a2blk'], lw['a0'], lw['g2'], lw['kkw'], lw['ka'],
      lw['rk'], lw['qn'], lw['knw'], cos, sin, ones)


def _split2(x):
    hi = x.astype(BF16)
    lo = (x - hi.astype(F32)).astype(BF16)
    return hi, lo


def _scan_kernel(*refs, sb, nb, zero_init, want_state):
    fwd = refs[0:6]
    bwd = refs[6:12]
    pos = 12
    if not zero_init:
        s0_ref = refs[pos]
        pos += 1
    y_outs = (refs[pos], refs[pos + 1])
    pos += 2
    if want_state:
        sfin_o = refs[pos]
        pos += 1
    h_ref = refs[pos]

    step = pl.program_id(1)
    nstep = pl.num_programs(1)
    npair = A_HEADS // 2
    pw = 2 * HEAD_DIM
    pairs = [(q, d, hp) for q in range(sb) for d in range(N_DIR) for hp in range(npair)]

    @pl.when(step == 0)
    def _():
        h_ref[...] = jnp.zeros_like(h_ref)
        if not zero_init:
            for (q, d, hp) in pairs:
                for e in range(2):
                    lo = e * HEAD_DIM
                    h_ref[q, d, hp, lo:lo + HEAD_DIM, lo:lo + HEAD_DIM] = s0_ref[q, d, 2 * hp + e].T

    def iota(shape, dim):
        return lax.broadcasted_iota(I32, shape, dim)

    first64 = iota((CHUNK, pw), 1) < HEAD_DIM
    eye_p = ((iota((CHUNK, pw), 1) % HEAD_DIM) == iota((CHUNK, pw), 0)).astype(F32)
    eye2 = (iota((pw, pw), 0) == iota((pw, pw), 1)).astype(F32)
    same_head = (iota((pw, pw), 0) < HEAD_DIM) == (iota((pw, pw), 1) < HEAD_DIM)
    ti = iota((CHUNK, CHUNK), 0)
    si = iota((CHUNK, CHUNK), 1)
    incl16 = ((si <= ti).astype(BF16), (si >= ti).astype(BF16))
    t4 = iota((2 * CHUNK, 2 * pw), 0)
    s4 = iota((2 * CHUNK, 2 * pw), 1) % CHUNK
    incl_off = jnp.where(t4 < CHUNK, 0, 1)
    t4 = t4 % CHUNK
    mask4 = (s4 < t4 + incl_off, s4 > t4 - incl_off)

    def bdiag(x):
        return jnp.concatenate([jnp.where(first64, x, 0.0), jnp.where(first64, 0.0, x)], axis=0)

    lhs, rhs1, bkt, gcol, v_p, amat = {}, {}, {}, {}, {}, {}
    blocks = [(q, d, j) for q in range(sb) for d in range(N_DIR) for j in range(nb)]
    csum = {}
    for (q, d, j) in blocks:
        hi, lo = _split2((fwd, bwd)[d][3][q, pl.ds(j * CHUNK, CHUNK), :])
        csum[(q, d, j)] = _dot(incl16[d], hi) + _dot(incl16[d], lo)
    for (q, d, j) in blocks:
        r_ref, v_ref, kk_ref, lw_ref, kd_ref, b_ref = (fwd, bwd)[d]
        rows = pl.ds(j * CHUNK, CHUNK)
        lw = lw_ref[q, rows, :]
        cs = csum[(q, d, j)]
        gam = jnp.exp(cs)
        gam_prev = jnp.exp(cs - lw)
        gam_inv = jnp.exp(-cs)
        gam_end = jnp.exp(cs[CHUNK - 1:CHUNK, :] if d == 0 else cs[0:1, :])
        a_hat = -kk_ref[q, rows, :] * gam_prev
        r_hat = r_ref[q, rows, :] * gam
        b_hat = b_ref[q, rows, :] * gam_inv
        k_hat = kd_ref[q, rows, :] * gam_inv
        v_all = v_ref[q, rows, :]
        for hp in range(npair):
            key = (q, d, hp, j)
            sl = slice(hp * pw, (hp + 1) * pw)
            lhs[key] = jnp.concatenate([a_hat[:, sl], r_hat[:, sl]], axis=0).astype(BF16)
            bh, kh, ge = b_hat[:, sl], k_hat[:, sl], gam_end[:, sl]
            rhs1[key] = jnp.concatenate([bdiag(bh), bdiag(kh)], axis=0).astype(BF16)
            gcol[key] = jnp.sum(eye2 * ge, axis=1, keepdims=True)
            bkt[key] = (jnp.concatenate([bh, kh], axis=0) * ge).T.astype(BF16)
            v_p[key] = v_all[:, sl]
    keys = list(lhs.keys())
    for key in keys:
        amat[key] = _dot_nt(lhs[key], rhs1[key])
    npow, tcur, a_rb, lhs2, vbd = {}, {}, {}, {}, {}
    for key in keys:
        am = jnp.where(mask4[key[1]], amat[key], 0.0)
        npow[key] = am[0:CHUNK, 0:pw]
        tcur[key] = eye_p + npow[key]
        a_rb[key] = am[CHUNK:2 * CHUNK, 0:pw].astype(BF16)
        vbd[key] = bdiag(v_p[key]).astype(BF16)
        lhs2[key] = jnp.concatenate([lhs[key], am[:, pw:2 * pw].astype(BF16)], axis=1)
    nsq = int(math.log2(CHUNK)) - 1
    for key in keys:
        npow[key] = _dot(npow[key].astype(BF16), bdiag(npow[key]).astype(BF16))
    for _ in range(nsq - 1):
        for key in keys:
            rhs = jnp.concatenate([bdiag(npow[key]), bdiag(tcur[key])], axis=1).astype(BF16)
            res = _dot(npow[key].astype(BF16), rhs)
            npow[key] = res[:, 0:pw]
            tcur[key] = tcur[key] + res[:, pw:2 * pw]
    t_inv = {}
    for key in keys:
        t_inv[key] = (tcur[key] + _dot(npow[key].astype(BF16), bdiag(tcur[key]).astype(BF16))).astype(BF16)

    hcur = {pr: h_ref[pr[0], pr[1], pr[2]] for pr in pairs}
    for jj in range(nb):
        z, u = {}, {}
        kof = lambda pr: (pr[0], pr[1], pr[2], jj if pr[1] == 0 else nb - 1 - jj)
        for pr in pairs:
            key = kof(pr)
            hv = jnp.concatenate([hcur[pr].astype(BF16), vbd[key]], axis=0)
            z[pr] = _dot(lhs2[key], hv)
        for pr in pairs:
            u[pr] = _dot(t_inv[kof(pr)], bdiag(z[pr][0:CHUNK]).astype(BF16))
        for pr in pairs:
            key = kof(pr)
            q, d, hp, j = key
            y = z[pr][CHUNK:2 * CHUNK] + _dot(a_rb[key], bdiag(u[pr]).astype(BF16))
            y_outs[d][q, pl.ds(j * CHUNK, CHUNK), hp * pw:(hp + 1) * pw] = y
            uv = jnp.concatenate([u[pr], v_p[key]], axis=0).astype(BF16)
            hcur[pr] = hcur[pr] * gcol[key] + jnp.where(same_head, _dot(bkt[key], uv), 0.0)
    for pr in pairs:
        h_ref[pr[0], pr[1], pr[2]] = hcur[pr]

    if want_state:
        @pl.when(step == nstep - 1)
        def _():
            for (q, d, hp) in pairs:
                for e in range(2):
                    lo = e * HEAD_DIM
                    sfin_o[q, d, 2 * hp + e] = hcur[(q, d, hp)][lo:lo + HEAD_DIM, lo:lo + HEAD_DIM].T


def _rwkv_scan(ops, s0, *, row0, nseq, seqlen, sb, nb, want_state):
    g = nb * CHUNK
    nstep = seqlen // g
    zero_init = s0 is None
    ops3 = [a.reshape(a.shape[0] // seqlen, seqlen, a.shape[1]) for a in ops]
    base = row0 // (seqlen * sb)
    fidx = lambda col: (lambda s, c: (base + s, c, col))
    bidx = lambda col: (lambda s, c: (base + s, nstep - 1 - c, col))
    blk = lambda im: pl.BlockSpec((sb, g, A_WIDTH), im)
    in_specs = [blk(fidx(0))] * 6 + [blk(bidx(0))] * 3 + [blk(bidx(1))] * 3
    args = ops3 + ops3
    state_spec = pl.BlockSpec((sb, N_DIR, A_HEADS, HEAD_DIM, HEAD_DIM), lambda s, c: (s, 0, 0, 0, 0))
    if not zero_init:
        in_specs.append(state_spec)
        args.append(s0)
    out_specs = [blk(lambda s, c: (s, c, 0)), blk(lambda s, c: (s, nstep - 1 - c, 0))]
    out_shape = [jax.ShapeDtypeStruct((nseq, seqlen, A_WIDTH), F32)] * 2
    if want_state:
        out_specs.append(state_spec)
        out_shape.append(jax.ShapeDtypeStruct((nseq, N_DIR, A_HEADS, HEAD_DIM, HEAD_DIM), F32))
    return pl.pallas_call(
        functools.partial(_scan_kernel, sb=sb, nb=nb, zero_init=zero_init, want_state=want_state),
        grid=(nseq // sb, nstep),
        in_specs=in_specs,
        out_specs=out_specs,
        out_shape=out_shape,
        scratch_shapes=[pltpu.VMEM((sb, N_DIR, A_HEADS // 2, 2 * HEAD_DIM, 2 * HEAD_DIM), F32)],
        compiler_params=_cparams(("arbitrary", "arbitrary")),
        name="rwkv_scan_ctx" if zero_init else "rwkv_scan_dec",
    )(*args)


def _attn_kernel(*refs, with_cache):
    grp = B_Q_HEADS // B_KV_HEADS
    if with_cache:
        q_ref, k_ref, v_ref, ck_ref, cv_ref, o_ref = refs
        kall = jnp.concatenate([k_ref[...], ck_ref[...]], axis=0)
        vall = jnp.concatenate([v_ref[...], cv_ref[...]], axis=0)
    else:
        q_ref, k_ref, v_ref, o_ref = refs
        kall = k_ref[...]
        vall = v_ref[...]
    q = q_ref[...].astype(BF16)
    k16 = [kall[:, hk * HEAD_DIM:(hk + 1) * HEAD_DIM].astype(BF16) for hk in range(B_KV_HEADS)]
    v16 = [vall[:, hk * HEAD_DIM:(hk + 1) * HEAD_DIM].astype(BF16) for hk in range(B_KV_HEADS)]

    def scores(j):
        return _dot_nt(q[:, j * HEAD_DIM:(j + 1) * HEAD_DIM], k16[j // grp])

    outs = []
    s_next = scores(0)
    for j in range(B_Q_HEADS):
        s = s_next
        if j + 1 < B_Q_HEADS:
            s_next = scores(j + 1)
        m = jnp.max(s, axis=-1, keepdims=True)
        p = jnp.exp(s - m)
        den = jnp.sum(p, axis=-1, keepdims=True)
        outs.append(_dot(p.astype(BF16), v16[j // grp]) / den)
    o_ref[...] = jnp.concatenate(outs, axis=1)


def _attention(q, k_tok, v_tok, cache, *, row0, nseq, seqlen, tq):
    nq = seqlen // tq
    base = row0 // tq
    kbase = row0 // seqlen
    kspec = pl.BlockSpec((seqlen, KV_WIDTH), lambda s, i: (kbase + s, 0))
    in_specs = [pl.BlockSpec((tq, B_WIDTH), lambda s, i: (base + s * nq + i, 0)), kspec, kspec]
    args = [q, k_tok, v_tok]
    if cache is not None:
        ck, cv, layer = cache
        cspec = pl.BlockSpec((None, None, ck.shape[2], KV_WIDTH), lambda s, i: (s, layer, 0, 0))
        in_specs += [cspec, cspec]
        args += [ck, cv]
    return pl.pallas_call(
        functools.partial(_attn_kernel, with_cache=cache is not None),
        grid=(nseq, nq),
        in_specs=in_specs,
        out_specs=pl.BlockSpec((tq, B_WIDTH), lambda s, i: (s * nq + i, 0)),
        out_shape=jax.ShapeDtypeStruct((nseq * seqlen, B_WIDTH), F32),
        compiler_params=_cparams(("arbitrary", "arbitrary")),
        name="attention_ctx" if cache is None else "attention_dec",
    )(*args)


def _pool_kernel(p_ref, w_ref, sc_ref, o_ref, *, seqlen):
    x = p_ref[...]
    t = lax.broadcasted_iota(I32, x.shape, 0)
    lane = lax.broadcasted_iota(I32, x.shape, 1)

    def down(a, k):
        return jnp.where(t >= k, pltpu.roll(a, k, 0), 0.0)

    def up(a, k):
        return jnp.where(t < seqlen - k, pltpu.roll(a, seqlen - k, 0), 0.0)

    past = [x]
    futr = [x]
    for j in range(len(POOL_WINDOWS) - 1):
        past.append(past[j] + down(past[j], 2 ** j))
        futr.append(futr[j] + up(futr[j], 2 ** j))
    win_sum = jnp.zeros_like(x)
    cnt = jnp.zeros_like(x)
    for gi, win in enumerate(POOL_WINDOWS):
        half = win // 2
        in_group = (lane // HEAD_DIM) == gi
        win_sum = jnp.where(in_group, down(past[gi], 1) + futr[gi], win_sum)
        n = (jnp.minimum(t + half, seqlen) - jnp.maximum(t - half, 0)).astype(F32)
        cnt = jnp.where(in_group, n, cnt)
    d = win_sum / cnt - x
    o_ref[...] = _dot(d.astype(BF16), w_ref[...]) * sc_ref[...]


def _pool(p, w_blk, scale, *, row0, nseq, seqlen):
    base = row0 // seqlen
    return pl.pallas_call(
        functools.partial(_pool_kernel, seqlen=seqlen),
        grid=(nseq,),
        in_specs=[pl.BlockSpec((seqlen, C_WIDTH), lambda s: (base + s, 0)),
                  pl.BlockSpec((C_WIDTH, C_WIDTH), lambda s: (0, 0)),
                  pl.BlockSpec((1, C_WIDTH), lambda s: (0, 0))],
        out_specs=pl.BlockSpec((seqlen, C_WIDTH), lambda s: (s, 0)),
        out_shape=jax.ShapeDtypeStruct((nseq * seqlen, C_WIDTH), F32),
        compiler_params=_cparams(("arbitrary",)),
        name="pool_mixer",
    )(p, w_blk, scale)


def _out_kernel(yfc_ref, yfd_ref, ybc_ref, ybd_ref, g_ref, bon_ref, attc_ref, attd_ref, poolc_ref, poold_ref,
                h_ref, mod_ref, w_ref, lng_ref, lnb_ref, gng_ref, gnb_ref, ones_ref, wrh_ref, wrl_ref, br_ref,
                h1_o, u2_o, lg_o, *, tm, bounds, alpha, n_ctx_tiles):
    grp = _group_of_row(pl.program_id(0) * tm, bounds)
    mod = mod_ref[pl.ds(grp, 1), :]
    g1 = mod[:, 2 * D_MODEL:3 * D_MODEL]
    sh2 = mod[:, 3 * D_MODEL:4 * D_MODEL]
    sc2 = mod[:, 4 * D_MODEL:5 * D_MODEL]
    ones = ones_ref[...]
    is_ctx = pl.program_id(0) < n_ctx_tiles
    pick = lambda c_ref, d_ref: jnp.where(is_ctx, c_ref[...], d_ref[...])
    y = pick(yfc_ref, yfd_ref) + pick(ybc_ref, ybd_ref)
    mu = _segsum(y, ones) * (1.0 / HEAD_DIM)
    yc = y - mu
    var = _segsum(yc * yc, ones) * (1.0 / HEAD_DIM)
    yn = yc * lax.rsqrt(var + GN_EPS) * gng_ref[...] + gnb_ref[...]
    a_out = (yn + bon_ref[...]) * g_ref[...]
    m = (_dot(a_out.astype(BF16), w_ref[0:A_WIDTH, :])
         + _dot(pick(attc_ref, attd_ref).astype(BF16), w_ref[A_WIDTH:A_WIDTH + B_WIDTH, :])
         + _dot(pick(poolc_ref, poold_ref).astype(BF16), w_ref[A_WIDTH + B_WIDTH:D_MODEL, :]))
    h1 = _layer_norm(alpha * h_ref[...] + g1 * m, lng_ref[...], lnb_ref[...])
    h1_o[...] = h1
    u2 = h1 * (1.0 + sc2) + sh2
    u2_o[...] = u2.astype(BF16)
    u_hi = u2.astype(BF16)
    u_lo = (u2 - u_hi.astype(F32)).astype(BF16)
    lg_o[...] = (_dot(u_hi, wrh_ref[...]) + _dot(u_lo, wrh_ref[...]) + _dot(u_hi, wrl_ref[...])) + br_ref[...]


def _out_projection(yf, yb, g, bonus, att, pool, h, mods_l, lw, ones, *, tm, bounds, alpha):
    t = h.shape[0]
    nct = yf[0].shape[0] // tm
    full = lambda shape: pl.BlockSpec(shape, lambda i: (0,) * len(shape))
    rows = lambda w: pl.BlockSpec((tm, w), lambda i: (i, 0))
    rows_c = lambda w: pl.BlockSpec((tm, w), lambda i: (jnp.minimum(i, nct - 1), 0))
    rows_d = lambda w: pl.BlockSpec((tm, w), lambda i: (jnp.maximum(i - nct, 0), 0))
    return pl.pallas_call(
        functools.partial(_out_kernel, tm=tm, bounds=bounds, alpha=alpha, n_ctx_tiles=nct),
        grid=(t // tm,),
        in_specs=[rows_c(A_WIDTH), rows_d(A_WIDTH), rows_c(A_WIDTH), rows_d(A_WIDTH), rows(A_WIDTH), rows(A_WIDTH),
                  rows_c(B_WIDTH), rows_d(B_WIDTH), rows_c(C_WIDTH), rows_d(C_WIDTH),
                  rows(D_MODEL), full((8, 6 * D_MODEL)), full((D_MODEL, D_MODEL)), full((1, D_MODEL)),
                  full((1, D_MODEL)), full((1, A_WIDTH)), full((1, A_WIDTH)), full((A_WIDTH, A_WIDTH)),
                  full((D_MODEL, ROUTER_LANES)), full((D_MODEL, ROUTER_LANES)), full((1, ROUTER_LANES))],
        out_specs=[rows(D_MODEL), rows(D_MODEL), rows(ROUTER_LANES)],
        out_shape=[jax.ShapeDtypeStruct((t, D_MODEL), F32), jax.ShapeDtypeStruct((t, D_MODEL), BF16),
                   jax.ShapeDtypeStruct((t, ROUTER_LANES), F32)],
        compiler_params=_cparams(("arbitrary",)),
        name="out_projection",
    )(yf[0], yf[1], yb[0], yb[1], g, bonus, att[0], att[1], pool[0], pool[1], h, mods_l, lw['w_out'],
      lw['ln1_g'], lw['ln1_b'], lw['gn_g'], lw['gn_b'], ones, lw['w_router_hi'], lw['w_router_lo'], lw['b_router'])


def _router_gates(logits):
    lane = lax.broadcasted_iota(I32, logits.shape, 1)
    lane_f = lane.astype(F32)
    neg = -jnp.inf
    big = 1e9

    def first_lane(mask):
        return jnp.min(jnp.where(mask, lane_f, big), axis=-1, keepdims=True).astype(I32)

    is_grp = lane < N_GROUPS
    gmax = jnp.max(jnp.where(is_grp, logits, neg), axis=-1, keepdims=True)
    sel = first_lane(is_grp & (logits == gmax))
    gsum = jnp.sum(jnp.where(is_grp, jnp.exp(logits - gmax), 0.0), axis=-1, keepdims=True)
    gsel = 1.0 / gsum
    first_exp = N_GROUPS + sel * EXP_PER_GROUP
    in_sel = (lane >= first_exp) & (lane < first_exp + EXP_PER_GROUP)
    v1 = jnp.max(jnp.where(in_sel, logits, neg), axis=-1, keepdims=True)
    i1 = first_lane(in_sel & (logits == v1))
    rest = in_sel & (lane != i1)
    v2 = jnp.max(jnp.where(rest, logits, neg), axis=-1, keepdims=True)
    i2 = first_lane(rest & (logits == v2))
    e2 = jnp.exp(v2 - v1)
    t1 = (1.0 / (1.0 + e2)) * gsel
    t2 = (e2 / (1.0 + e2)) * gsel
    return jnp.where(lane == i1, t1, 0.0) + jnp.where(lane == i2, t2, 0.0), sel


def _moe_kernel(x_ref, lg_ref, w1_ref, w3_ref, w2_ref, h1_ref, mod_ref, lng_ref, lnb_ref, o_ref,
                xs_ref, gs_ref, acc_ref, *, tm, sub, bounds, alpha):
    gates, sel = _router_gates(lg_ref[...])
    lane = lax.broadcasted_iota(I32, gates.shape, 1)
    onehot = jnp.where(lane == sel, 1.0, 0.0)
    row_i = lax.broadcasted_iota(I32, (tm, tm), 0)
    col_i = lax.broadcasted_iota(I32, (tm, tm), 1)
    earlier = jnp.where(col_i < row_i, 1.0, 0.0).astype(BF16)
    rank = _dot(earlier, onehot.astype(BF16))
    lane1 = lax.broadcasted_iota(I32, (1, ROUTER_LANES), 1)
    counts = jnp.sum(onehot, axis=0, keepdims=True)
    cnt = [jnp.sum(jnp.where(lane1 == g, counts, 0.0)).astype(I32) for g in range(N_GROUPS)]
    off = [jnp.zeros((), I32)]
    for g in range(N_GROUPS - 1):
        off.append(off[g] + cnt[g])
    offs = jnp.zeros((1, ROUTER_LANES), F32)
    for g in range(N_GROUPS):
        offs = jnp.where(lane1 == g, off[g].astype(F32), offs)
    pos = jnp.sum(onehot * (rank + offs), axis=1, keepdims=True).astype(I32)
    pos_row = jnp.broadcast_to(pos, (tm, ROUTER_LANES)).T[0:1, :]
    perm = jnp.where(row_i == pos_row, 1.0, 0.0).astype(BF16)
    perm_t = jnp.where(col_i == pos, 1.0, 0.0).astype(BF16)
    xs_ref[...] = _dot(perm, x_ref[...]).astype(BF16)
    g_hi = gates.astype(BF16)
    g_lo = (gates - g_hi.astype(F32)).astype(BF16)
    gs_ref[...] = _dot(perm, g_hi) + _dot(perm, g_lo)
    acc_ref[...] = jnp.zeros_like(acc_ref)

    lane_s = lax.broadcasted_iota(I32, (sub, ROUTER_LANES), 1)
    for s in range(tm // sub):
        rows = pl.ds(s * sub, sub)
        for g in range(N_GROUPS):
            @pl.when((off[g] < (s + 1) * sub) & (off[g] + cnt[g] > s * sub))
            def _():
                xb = xs_ref[rows, :]
                gsb = gs_ref[rows, :]
                hids = []
                for j in range(EXP_PER_GROUP):
                    e = g * EXP_PER_GROUP + j
                    ge = jnp.sum(jnp.where(lane_s == N_GROUPS + e, gsb, 0.0), axis=-1, keepdims=True)
                    h1 = _dot(xb, w1_ref[e])
                    h3 = _dot(xb, w3_ref[e])
                    hids.append(((h1 * _sigmoid(h1)) * h3 * ge).astype(BF16))
                hid = jnp.concatenate(hids, axis=1)
                w2g = w2_ref[g * EXP_PER_GROUP:(g + 1) * EXP_PER_GROUP].reshape(EXP_PER_GROUP * D_EXPERT, D_MODEL)
                acc_ref[rows, :] += _dot(hid, w2g)

    f = _dot(perm_t, acc_ref[...].astype(BF16))
    grp = _group_of_row(pl.program_id(0) * tm, bounds)
    g2 = mod_ref[pl.ds(grp, 1), 5 * D_MODEL:6 * D_MODEL]
    o_ref[...] = _layer_norm(alpha * h1_ref[...] + g2 * f, lng_ref[...], lnb_ref[...])


def _moe(u2, logits, h1, mods_l, lw, *, tm, sub, bounds, alpha):
    t = h1.shape[0]
    full = lambda shape: pl.BlockSpec(shape, lambda i: (0,) * len(shape))
    once = lambda shape: pl.BlockSpec(shape, lambda i: (0,) * len(shape), pipeline_mode=pl.Buffered(1))
    rows = lambda w: pl.BlockSpec((tm, w), lambda i: (i, 0))
    return pl.pallas_call(
        functools.partial(_moe_kernel, tm=tm, sub=sub, bounds=bounds, alpha=alpha),
        grid=(t // tm,),
        in_specs=[rows(D_MODEL), rows(ROUTER_LANES),
                  once((N_EXPERTS, D_MODEL, D_EXPERT)), once((N_EXPERTS, D_MODEL, D_EXPERT)),
                  once((N_EXPERTS, D_EXPERT, D_MODEL)),
                  rows(D_MODEL), full((8, 6 * D_MODEL)), full((1, D_MODEL)), full((1, D_MODEL))],
        out_specs=rows(D_MODEL),
        out_shape=jax.ShapeDtypeStruct((t, D_MODEL), F32),
        scratch_shapes=[pltpu.VMEM((tm, D_MODEL), BF16), pltpu.VMEM((tm, ROUTER_LANES), F32),
                        pltpu.VMEM((tm, D_MODEL), F32)],
        compiler_params=_cparams(("arbitrary",)),
        name="moe",
    )(u2, logits, lw['moe_w1'], lw['moe_w3'], lw['moe_w2'], h1, mods_l, lw['ln2_g'], lw['ln2_b'])


def _rope_tables(n_ctx, dec_batch, dec_seq):
    n = HEAD_DIM // 4
    pos = jnp.arange(dec_seq)
    row = (pos // GRID_W).astype(F32)
    col = (pos % GRID_W).astype(F32)
    inv = 1.0 / (ROPE_THETA ** (jnp.arange(n, dtype=F32) / n))
    ra, ca = row[:, None] * inv, col[:, None] * inv
    cos64 = jnp.concatenate([jnp.cos(ra), jnp.cos(ra), jnp.cos(ca), jnp.cos(ca)], axis=-1)
    sin64 = jnp.concatenate([-jnp.sin(ra), jnp.sin(ra), -jnp.sin(ca), jnp.sin(ca)], axis=-1)
    cos_s = jnp.tile(jnp.concatenate([cos64, cos64], axis=-1), (dec_batch, 1))
    sin_s = jnp.tile(jnp.concatenate([sin64, sin64], axis=-1), (dec_batch, 1))
    cos = jnp.concatenate([jnp.ones((n_ctx, 128), F32), cos_s], axis=0)
    sin = jnp.concatenate([jnp.zeros((n_ctx, 128), F32), sin_s], axis=0)
    return cos, sin


def _block_diag2(w):
    z = jnp.zeros_like(w[0])
    return jnp.concatenate([jnp.concatenate([w[0], z], axis=1), jnp.concatenate([z, w[1]], axis=1)], axis=0)


def _layer_weights(l, p):
    d = D_MODEL
    pool_blk = jnp.zeros((C_WIDTH, C_WIDTH), F32)
    for gi in range(len(POOL_WINDOWS)):
        pool_blk = pool_blk.at[gi * 64:(gi + 1) * 64, gi * 64:(gi + 1) * 64].set(p['pool_w'][l, gi])
    w_router = jnp.zeros((d, ROUTER_LANES), F32)
    w_router = w_router.at[:, 0:N_GROUPS].set(p['rt_grp_w'][l])
    w_router = w_router.at[:, N_GROUPS:N_GROUPS + N_EXPERTS].set(p['rt_exp_w'][l])
    b_router = jnp.zeros((1, ROUTER_LANES), F32)
    b_router = b_router.at[0, 0:N_GROUPS].set(p['rt_grp_b'][l])
    b_router = b_router.at[0, N_GROUPS:N_GROUPS + N_EXPERTS].set(p['rt_exp_b'][l])
    return {
        'w_in': p['w_in'][l].astype(BF16),
        'w_out': p['w_out'][l].astype(BF16),
        'w2blk': _block_diag2(p['rw_w2'][l]).astype(BF16),
        'w0': p['rw_w0'][l].reshape(1, 2 * A_WIDTH),
        'a2blk': _block_diag2(p['rw_a2'][l]).astype(BF16),
        'a0': p['rw_a0'][l].reshape(1, 2 * A_WIDTH),
        'g2': p['rw_g2'][l].astype(BF16),
        'kkw': p['rw_kk'][l].reshape(1, A_WIDTH),
        'ka': jnp.tile(p['rw_ka'][l].reshape(1, A_WIDTH), (1, 2)),
        'rk': p['rw_rk'][l].reshape(1, A_WIDTH),
        'gn_g': p['rw_gn_g'][l].reshape(1, A_WIDTH),
        'gn_b': p['rw_gn_b'][l].reshape(1, A_WIDTH),
        'qn': jnp.tile(p['q_norm'][l].reshape(1, HEAD_DIM), (1, B_Q_HEADS)),
        'knw': jnp.tile(p['k_norm'][l].reshape(1, HEAD_DIM), (1, B_KV_HEADS)),
        'pool_blk': pool_blk.astype(BF16),
        'pool_scale': p['pool_scale'][l].reshape(1, C_WIDTH),
        'ln1_g': p['ln1_g'][l].reshape(1, d), 'ln1_b': p['ln1_b'][l].reshape(1, d),
        'ln2_g': p['ln2_g'][l].reshape(1, d), 'ln2_b': p['ln2_b'][l].reshape(1, d),
        'w_router_hi': w_router.astype(BF16),
        'w_router_lo': (w_router - w_router.astype(BF16).astype(F32)).astype(BF16),
        'b_router': b_router,
        'moe_w1': p['moe_w1'][l].astype(BF16),
        'moe_w3': p['moe_w3'][l].astype(BF16),
        'moe_w2': p['moe_w2'][l].astype(BF16),
    }


def _forward(x_prompt, x_sample, state_rwkv, cache_k, cache_v, c, c_ctx, p):
    batch, seq, d = x_prompt.shape
    dec_batch, dec_seq, _ = x_sample.shape
    depth = p['w_in'].shape[0]
    past = cache_k.shape[2]
    n_ctx = batch * seq
    n_tok = n_ctx + dec_batch * dec_seq
    alpha = (2 * depth) ** 0.25
    bounds = tuple(n_ctx + j * dec_seq for j in range(dec_batch))
    tm = min(256, seq)
    tm_moe = math.gcd(n_ctx, dec_seq, 512)
    tq_ctx = min(256, seq)
    tq_dec = min(256, dec_seq)
    sb_ctx = 2

    cond8 = jnp.zeros((8, d), F32).at[0].set(c_ctx).at[1:1 + dec_batch].set(c)
    mods = _modulation(cond8, p['w_mod'], p['b_mod'])
    cos, sin = _rope_tables(n_ctx, dec_batch, dec_seq)
    head_id = jnp.arange(A_WIDTH) // HEAD_DIM
    ones = (head_id[:, None] == head_id[None, :]).astype(BF16)

    h = jnp.concatenate([x_prompt.reshape(n_ctx, d), x_sample.reshape(dec_batch * dec_seq, d)], axis=0)
    ck4 = cache_k.reshape(dec_batch, depth, past, KV_WIDTH)
    cv4 = cache_v.reshape(dec_batch, depth, past, KV_WIDTH)
    new_s, new_k, new_v = [], [], []
    for l in range(depth):
        lw = _layer_weights(l, p)
        (r, v, kk, logw, kd, bvec, g, bonus, q, k_att, v_att, pp) = _in_projection(
            h, mods[l], lw, cos, sin, ones, tm=tm, bounds=bounds)
        scan_ops = (r, v, kk, logw, kd, bvec)
        yf_c, yb_c, s_fin = _rwkv_scan(scan_ops, None, row0=0, nseq=batch, seqlen=seq, sb=sb_ctx,
                                       nb=seq // CHUNK, want_state=True)
        yf_d, yb_d = _rwkv_scan(scan_ops, state_rwkv[:, l], row0=n_ctx, nseq=dec_batch, seqlen=dec_seq,
                                sb=dec_batch, nb=4, want_state=False)
        yf = (yf_c.reshape(n_ctx, A_WIDTH), yf_d.reshape(n_tok - n_ctx, A_WIDTH))
        yb = (yb_c.reshape(n_ctx, A_WIDTH), yb_d.reshape(n_tok - n_ctx, A_WIDTH))
        att = (_attention(q, k_att, v_att, None, row0=0, nseq=batch, seqlen=seq, tq=tq_ctx),
               _attention(q, k_att, v_att, (ck4, cv4, l), row0=n_ctx, nseq=dec_batch, seqlen=dec_seq, tq=tq_dec))
        pool = (_pool(pp, lw['pool_blk'], lw['pool_scale'], row0=0, nseq=batch, seqlen=seq),
                _pool(pp, lw['pool_blk'], lw['pool_scale'], row0=n_ctx, nseq=dec_batch, seqlen=dec_seq))
        h1, u2, logits = _out_projection(yf, yb, g, bonus, att, pool, h, mods[l], lw, ones,
                                         tm=tm, bounds=bounds, alpha=alpha)
        h = _moe(u2, logits, h1, mods[l], lw, tm=tm_moe, sub=128, bounds=bounds, alpha=alpha)
        new_s.append(s_fin)
        new_k.append(k_att[:n_ctx].reshape(batch, seq, B_KV_HEADS, HEAD_DIM))
        new_v.append(v_att[:n_ctx].reshape(batch, seq, B_KV_HEADS, HEAD_DIM))
    return (h[:n_ctx].reshape(batch, seq, d), h[n_ctx:].reshape(dec_batch, dec_seq, d),
            jnp.stack(new_s, axis=1), jnp.stack(new_k, axis=1), jnp.stack(new_v, axis=1))


def kernel(x_prompt, x_sample, state_rwkv, cache_k, cache_v, c, c_ctx, w_in, w_out, w_mod, b_mod, ln1_g, ln1_b, ln2_g, ln2_b, rw_w0, rw_w2, rw_a0, rw_a2, rw_g2, rw_kk, rw_ka, rw_rk, rw_gn_g, rw_gn_b, q_norm, k_norm, pool_w, pool_scale, rt_grp_w, rt_grp_b, rt_exp_w, rt_exp_b, moe_w1, moe_w3, moe_w2):
    p = dict(w_in=w_in, w_out=w_out, w_mod=w_mod, b_mod=b_mod, ln1_g=ln1_g, ln1_b=ln1_b, ln2_g=ln2_g,
             ln2_b=ln2_b, rw_w0=rw_w0, rw_w2=rw_w2, rw_a0=rw_a0, rw_a2=rw_a2, rw_g2=rw_g2, rw_kk=rw_kk,
             rw_ka=rw_ka, rw_rk=rw_rk, rw_gn_g=rw_gn_g, rw_gn_b=rw_gn_b, q_norm=q_norm, k_norm=k_norm,
             pool_w=pool_w, pool_scale=pool_scale, rt_grp_w=rt_grp_w, rt_grp_b=rt_grp_b, rt_exp_w=rt_exp_w,
             rt_exp_b=rt_exp_b, moe_w1=moe_w1, moe_w3=moe_w3, moe_w2=moe_w2)
    return _forward(x_prompt, x_sample, state_rwkv, cache_k, cache_v, c, c_ctx, p)
```

```python
import functools
import math

import jax
import jax.numpy as jnp
from jax import lax
from jax.experimental import pallas as pl
from jax.experimental.pallas import tpu as pltpu

F32 = jnp.float32
BF16 = jnp.bfloat16
I32 = jnp.int32

D_MODEL = 1024
GRID_W = 64
HEAD_DIM = 64
A_HEADS = 6
A_WIDTH = A_HEADS * HEAD_DIM
LORA_W = 64
LORA_A = 64
LORA_G = 128
N_DIR = 2
DECAY_SCALE = 0.606531
B_Q_HEADS = 6
B_KV_HEADS = 2
B_WIDTH = B_Q_HEADS * HEAD_DIM
KV_WIDTH = B_KV_HEADS * HEAD_DIM
ROPE_THETA = 10000.0
POOL_WINDOWS = (2, 4, 8, 16)
C_WIDTH = 256
IN_COLS = 2432
N_GROUPS = 4
EXP_PER_GROUP = 4
N_EXPERTS = 16
D_EXPERT = 256
LN_EPS = 1e-5
GN_EPS = 64e-5
QK_EPS = 1e-6

_O_R, _O_K, _O_V = 0, 384, 768
_O_LW, _O_LA, _O_LG = 1152, 1280, 1408
_O_Q, _O_KB, _O_VB, _O_P = 1536, 1920, 2048, 2176

CHUNK = 64
ROUTER_LANES = 128
VMEM_LIMIT = 56 * 1024 * 1024


def _cparams(sem):
    return pltpu.CompilerParams(dimension_semantics=sem, vmem_limit_bytes=VMEM_LIMIT)


def _sigmoid(x):
    return 1.0 / (1.0 + jnp.exp(-x))


def _dot(a, b):
    return jnp.dot(a, b, preferred_element_type=F32)


def _dot_nt(a, b):
    return lax.dot_general(a, b, (((1,), (1,)), ((), ())), preferred_element_type=F32)


def _segsum(x, ones2):
    x16 = x.astype(BF16)
    slabs = [_dot(x16[:, j:j + 128], ones2) for j in range(0, x.shape[1], 128)]
    return slabs[0] if len(slabs) == 1 else jnp.concatenate(slabs, axis=1)


def _rope128(x, cos, sin):
    lane = lax.broadcasted_iota(I32, x.shape, 1)
    first = (lane % 32) < 16
    up = pltpu.roll(x, 128 - 16, 1)
    dn = pltpu.roll(x, 16, 1)
    return x * cos + jnp.where(first, up, dn) * sin


def _layer_norm(z, g, b):
    mu = jnp.mean(z, axis=-1, keepdims=True)
    zc = z - mu
    var = jnp.mean(zc * zc, axis=-1, keepdims=True)
    return zc * lax.rsqrt(var + LN_EPS) * g + b


def _mod_kernel(c_ref, w_ref, b_ref, o_ref):
    x = c_ref[...]
    x = x * _sigmoid(x)
    w = w_ref[...]
    x_hi = x.astype(BF16)
    x_lo = (x - x_hi.astype(F32)).astype(BF16)
    w_hi = w.astype(BF16)
    w_lo = (w - w_hi.astype(F32)).astype(BF16)
    o_ref[...] = (_dot(x_hi, w_hi) + _dot(x_lo, w_hi) + _dot(x_hi, w_lo)) + b_ref[...]


def _modulation(cond8, w_mod, b_mod):
    depth = w_mod.shape[0]
    tn = 1536
    return pl.pallas_call(
        _mod_kernel,
        grid=(depth, 6 * D_MODEL // tn),
        in_specs=[
            pl.BlockSpec((8, D_MODEL), lambda l, j: (0, 0)),
            pl.BlockSpec((None, D_MODEL, tn), lambda l, j: (l, 0, j)),
            pl.BlockSpec((None, 1, tn), lambda l, j: (l, 0, j)),
        ],
        out_specs=pl.BlockSpec((None, 8, tn), lambda l, j: (l, 0, j)),
        out_shape=jax.ShapeDtypeStruct((depth, 8, 6 * D_MODEL), F32),
        compiler_params=_cparams(("arbitrary", "arbitrary")),
        name="modulation",
    )(cond8, w_mod, b_mod.reshape(depth, 1, 6 * D_MODEL))


def _group_of_row(row0, bounds):
    g = jnp.zeros((), I32)
    for b in bounds:
        g = g + jnp.where(row0 >= b, 1, 0).astype(I32)
    return g


def _in_kernel(hc_ref, hd_ref, mod_ref, w_ref, w2_ref, w0_ref, a2_ref, a0_ref, g2_ref, kkw_ref, ka_ref,
               rk_ref, qn_ref, knw_ref, cos_ref, sin_ref, ones_ref,
               r_o, v_o, kk_o, lw_o, kd_o, b_o, g_o, bon_o, q_o, k_o, vb_o, p_o, *, tm, bounds, n_ctx_tiles):
    grp = _group_of_row(pl.program_id(0) * tm, bounds)
    mod = mod_ref[pl.ds(grp, 1), :]
    sh1 = mod[:, 0:D_MODEL]
    sc1 = mod[:, D_MODEL:2 * D_MODEL]
    h = jnp.where(pl.program_id(0) < n_ctx_tiles, hc_ref[...], hd_ref[...])
    u = h * (1.0 + sc1) + sh1
    proj = _dot(u.astype(BF16), w_ref[...])
    ones = ones_ref[...]

    r = proj[:, _O_R:_O_R + A_WIDTH]
    k = proj[:, _O_K:_O_K + A_WIDTH]
    v = proj[:, _O_V:_O_V + A_WIDTH]
    lo_w = proj[:, _O_LW:_O_LW + 128]
    lo_a = proj[:, _O_LA:_O_LA + 128]
    lo_g = proj[:, _O_LG:_O_LG + 128]

    w_pre = w0_ref[...] + _dot(jnp.tanh(lo_w).astype(BF16), w2_ref[...])
    logw = -DECAY_SCALE * _sigmoid(w_pre)
    a = _sigmoid(a0_ref[...] + _dot(lo_a.astype(BF16), a2_ref[...]))
    g = _dot(_sigmoid(lo_g).astype(BF16), g2_ref[...])

    kk = k * kkw_ref[...]
    kk = kk * lax.rsqrt(jnp.maximum(_segsum(kk * kk, ones), 1e-24))
    k2 = jnp.concatenate([k, k], axis=1)
    kd = k2 * (1.0 + (a - 1.0) * ka_ref[...])
    bvec = jnp.concatenate([kk, kk], axis=1) * a
    kd_sum = kd[:, 0:A_WIDTH] + kd[:, A_WIDTH:2 * A_WIDTH]
    bonus = _segsum(r * kd_sum * rk_ref[...], ones) * v

    r_o[...] = r
    v_o[...] = v
    kk_o[...] = kk
    lw_o[...] = logw
    kd_o[...] = kd
    b_o[...] = bvec
    g_o[...] = g
    bon_o[...] = bonus

    cos = cos_ref[...]
    sin = sin_ref[...]
    q = proj[:, _O_Q:_O_Q + B_WIDTH]
    qn = q * lax.rsqrt(_segsum(q * q, ones) * (1.0 / HEAD_DIM) + QK_EPS) * qn_ref[...]
    scale = 1.0 / math.sqrt(HEAD_DIM)
    q_o[...] = jnp.concatenate(
        [_rope128(qn[:, j * 128:(j + 1) * 128], cos, sin) for j in range(B_WIDTH // 128)], axis=1) * scale
    kb = proj[:, _O_KB:_O_KB + KV_WIDTH]
    kn = kb * lax.rsqrt(_segsum(kb * kb, ones) * (1.0 / HEAD_DIM) + QK_EPS) * knw_ref[...]
    k_o[...] = _rope128(kn, cos, sin)
    vb_o[...] = proj[:, _O_VB:_O_VB + KV_WIDTH]
    p_o[...] = proj[:, _O_P:_O_P + C_WIDTH]


def _pair_specs(tm, nct):
    rows_c = lambda w: pl.BlockSpec((tm, w), lambda i: (jnp.minimum(i, nct - 1), 0))
    rows_d = lambda w: pl.BlockSpec((tm, w), lambda i: (jnp.maximum(i - nct, 0), 0))
    return rows_c, rows_d


def _in_projection(h, mods_l, lw, cos, sin, ones, *, tm, bounds):
    nct = h[0].shape[0] // tm
    t = h[0].shape[0] + h[1].shape[0]
    full = lambda shape: pl.BlockSpec(shape, lambda i: (0,) * len(shape))
    rows = lambda w: pl.BlockSpec((tm, w), lambda i: (i, 0))
    rows_c, rows_d = _pair_specs(tm, nct)
    widths = (A_WIDTH, A_WIDTH, A_WIDTH, 2 * A_WIDTH, 2 * A_WIDTH, 2 * A_WIDTH, A_WIDTH, A_WIDTH,
              B_WIDTH, KV_WIDTH, KV_WIDTH, C_WIDTH)
    return pl.pallas_call(
        functools.partial(_in_kernel, tm=tm, bounds=bounds, n_ctx_tiles=nct),
        grid=(t // tm,),
        in_specs=[
            rows_c(D_MODEL), rows_d(D_MODEL), full((8, 6 * D_MODEL)), full((D_MODEL, IN_COLS)),
            full((128, 2 * A_WIDTH)), full((1, 2 * A_WIDTH)), full((128, 2 * A_WIDTH)), full((1, 2 * A_WIDTH)),
            full((LORA_G, A_WIDTH)), full((1, A_WIDTH)), full((1, 2 * A_WIDTH)), full((1, A_WIDTH)),
            full((1, B_WIDTH)), full((1, KV_WIDTH)), rows(128), rows(128), full((2 * HEAD_DIM, 2 * HEAD_DIM)),
        ],
        out_specs=[rows(w) for w in widths],
        out_shape=[jax.ShapeDtypeStruct((t, w), F32) for w in widths],
        compiler_params=_cparams(("arbitrary",)),
        name="in_projection",
    )(h[0], h[1], mods_l, lw['w_in'], lw['w2blk'], lw['w0'], lw['a2blk'], lw['a0'], lw['g2'], lw['kkw'], lw['ka'],
      lw['rk'], lw['qn'], lw['knw'], cos, sin, ones)


def _split2(x):
    hi = x.astype(BF16)
    lo = (x - hi.astype(F32)).astype(BF16)
    return hi, lo


def _scan_kernel(*refs, sb, nb, zero_init, want_state):
    fwd = refs[0:6]
    bwd = refs[6:12]
    pos = 12
    if not zero_init:
        s0_ref = refs[pos]
        pos += 1
    y_outs = (refs[pos], refs[pos + 1])
    pos += 2
    if want_state:
        sfin_o = refs[pos]
        pos += 1
    h_ref = refs[pos]

    step = pl.program_id(1)
    nstep = pl.num_programs(1)
    npair = A_HEADS // 2
    pw = 2 * HEAD_DIM
    pairs = [(q, d, hp) for q in range(sb) for d in range(N_DIR) for hp in range(npair)]

    @pl.when(step == 0)
    def _():
        h_ref[...] = jnp.zeros_like(h_ref)
        if not zero_init:
            for (q, d, hp) in pairs:
                for e in range(2):
                    lo = e * HEAD_DIM
                    h_ref[q, d, hp, lo:lo + HEAD_DIM, lo:lo + HEAD_DIM] = s0_ref[q, d, 2 * hp + e].T

    def iota(shape, dim):
        return lax.broadcasted_iota(I32, shape, dim)

    first64 = iota((CHUNK, pw), 1) < HEAD_DIM
    eye_p = ((iota((CHUNK, pw), 1) % HEAD_DIM) == iota((CHUNK, pw), 0)).astype(F32)
    eye2 = (iota((pw, pw), 0) == iota((pw, pw), 1)).astype(F32)
    same_head = (iota((pw, pw), 0) < HEAD_DIM) == (iota((pw, pw), 1) < HEAD_DIM)
    ti = iota((CHUNK, CHUNK), 0)
    si = iota((CHUNK, CHUNK), 1)
    incl16 = ((si <= ti).astype(BF16), (si >= ti).astype(BF16))
    t4 = iota((2 * CHUNK, 2 * pw), 0)
    s4 = iota((2 * CHUNK, 2 * pw), 1) % CHUNK
    incl_off = jnp.where(t4 < CHUNK, 0, 1)
    t4 = t4 % CHUNK
    mask4 = (s4 < t4 + incl_off, s4 > t4 - incl_off)

    def bdiag(x):
        return jnp.concatenate([jnp.where(first64, x, 0.0), jnp.where(first64, 0.0, x)], axis=0)

    lhs, rhs1, bkt, gcol, v_p, amat = {}, {}, {}, {}, {}, {}
    blocks = [(q, d, j) for q in range(sb) for d in range(N_DIR) for j in range(nb)]
    csum = {}
    for (q, d, j) in blocks:
        hi, lo = _split2((fwd, bwd)[d][3][q, pl.ds(j * CHUNK, CHUNK), :])
        csum[(q, d, j)] = _dot(incl16[d], hi) + _dot(incl16[d], lo)
    for (q, d, j) in blocks:
        r_ref, v_ref, kk_ref, lw_ref, kd_ref, b_ref = (fwd, bwd)[d]
        rows = pl.ds(j * CHUNK, CHUNK)
        lw = lw_ref[q, rows, :]
        cs = csum[(q, d, j)]
        gam = jnp.exp(cs)
        gam_prev = jnp.exp(cs - lw)
        gam_inv = jnp.exp(-cs)
        gam_end = jnp.exp(cs[CHUNK - 1:CHUNK, :] if d == 0 else cs[0:1, :])
        a_hat = -kk_ref[q, rows, :] * gam_prev
        r_hat = r_ref[q, rows, :] * gam
        b_hat = b_ref[q, rows, :] * gam_inv
        k_hat = kd_ref[q, rows, :] * gam_inv
        v_all = v_ref[q, rows, :]
        for hp in range(npair):
            key = (q, d, hp, j)
            sl = slice(hp * pw, (hp + 1) * pw)
            lhs[key] = jnp.concatenate([a_hat[:, sl], r_hat[:, sl]], axis=0).astype(BF16)
            bh, kh, ge = b_hat[:, sl], k_hat[:, sl], gam_end[:, sl]
            rhs1[key] = jnp.concatenate([bdiag(bh), bdiag(kh)], axis=0).astype(BF16)
            gcol[key] = jnp.sum(eye2 * ge, axis=1, keepdims=True)
            bkt[key] = (jnp.concatenate([bh, kh], axis=0) * ge).T.astype(BF16)
            v_p[key] = v_all[:, sl]
    keys = list(lhs.keys())
    for key in keys:
        amat[key] = _dot_nt(lhs[key], rhs1[key])
    npow, tcur, a_rb, lhs2, vbd = {}, {}, {}, {}, {}
    for key in keys:
        am = jnp.where(mask4[key[1]], amat[key], 0.0)
        npow[key] = am[0:CHUNK, 0:pw]
        tcur[key] = eye_p + npow[key]
        a_rb[key] = am[CHUNK:2 * CHUNK, 0:pw].astype(BF16)
        vbd[key] = bdiag(v_p[key]).astype(BF16)
        lhs2[key] = jnp.concatenate([lhs[key], am[:, pw:2 * pw].astype(BF16)], axis=1)
    nsq = int(math.log2(CHUNK)) - 1
    for key in keys:
        npow[key] = _dot(npow[key].astype(BF16), bdiag(npow[key]).astype(BF16))
    for _ in range(nsq - 1):
        for key in keys:
            rhs = jnp.concatenate([bdiag(npow[key]), bdiag(tcur[key])], axis=1).astype(BF16)
            res = _dot(npow[key].astype(BF16), rhs)
            npow[key] = res[:, 0:pw]
            tcur[key] = tcur[key] + res[:, pw:2 * pw]
    t_inv = {}
    for key in keys:
        t_inv[key] = (tcur[key] + _dot(npow[key].astype(BF16), bdiag(tcur[key]).astype(BF16))).astype(BF16)

    hcur = {pr: h_ref[pr[0], pr[1], pr[2]] for pr in pairs}
    for jj in range(nb):
        z, u = {}, {}
        kof = lambda pr: (pr[0], pr[1], pr[2], jj if pr[1] == 0 else nb - 1 - jj)
        for pr in pairs:
            key = kof(pr)
            hv = jnp.concatenate([hcur[pr].astype(BF16), vbd[key]], axis=0)
            z[pr] = _dot(lhs2[key], hv)
        for pr in pairs:
            u[pr] = _dot(t_inv[kof(pr)], bdiag(z[pr][0:CHUNK]).astype(BF16))
        for pr in pairs:
            key = kof(pr)
            q, d, hp, j = key
            y = z[pr][CHUNK:2 * CHUNK] + _dot(a_rb[key], bdiag(u[pr]).astype(BF16))
            y_outs[d][q, pl.ds(j * CHUNK, CHUNK), hp * pw:(hp + 1) * pw] = y
            uv = jnp.concatenate([u[pr], v_p[key]], axis=0).astype(BF16)
            hcur[pr] = hcur[pr] * gcol[key] + jnp.where(same_head, _dot(bkt[key], uv), 0.0)
    for pr in pairs:
        h_ref[pr[0], pr[1], pr[2]] = hcur[pr]

    if want_state:
        @pl.when(step == nstep - 1)
        def _():
            for (q, d, hp) in pairs:
                for e in range(2):
                    lo = e * HEAD_DIM
                    sfin_o[q, d, 2 * hp + e] = hcur[(q, d, hp)][lo:lo + HEAD_DIM, lo:lo + HEAD_DIM].T


def _rwkv_scan(ops, s0, *, row0, nseq, seqlen, sb, nb, want_state):
    g = nb * CHUNK
    nstep = seqlen // g
    zero_init = s0 is None
    ops3 = [a.reshape(a.shape[0] // seqlen, seqlen, a.shape[1]) for a in ops]
    base = row0 // (seqlen * sb)
    fidx = lambda col: (lambda s, c: (base + s, c, col))
    bidx = lambda col: (lambda s, c: (base + s, nstep - 1 - c, col))
    blk = lambda im: pl.BlockSpec((sb, g, A_WIDTH), im)
    in_specs = [blk(fidx(0))] * 6 + [blk(bidx(0))] * 3 + [blk(bidx(1))] * 3
    args = ops3 + ops3
    state_spec = pl.BlockSpec((sb, N_DIR, A_HEADS, HEAD_DIM, HEAD_DIM), lambda s, c: (s, 0, 0, 0, 0))
    if not zero_init:
        in_specs.append(state_spec)
        args.append(s0)
    out_specs = [blk(lambda s, c: (s, c, 0)), blk(lambda s, c: (s, nstep - 1 - c, 0))]
    out_shape = [jax.ShapeDtypeStruct((nseq, seqlen, A_WIDTH), F32)] * 2
    if want_state:
        out_specs.append(state_spec)
        out_shape.append(jax.ShapeDtypeStruct((nseq, N_DIR, A_HEADS, HEAD_DIM, HEAD_DIM), F32))
    return pl.pallas_call(
        functools.partial(_scan_kernel, sb=sb, nb=nb, zero_init=zero_init, want_state=want_state),
        grid=(nseq // sb, nstep),
        in_specs=in_specs,
        out_specs=out_specs,
        out_shape=out_shape,
        scratch_shapes=[pltpu.VMEM((sb, N_DIR, A_HEADS // 2, 2 * HEAD_DIM, 2 * HEAD_DIM), F32)],
        compiler_params=_cparams(("arbitrary", "arbitrary")),
        name="rwkv_scan_ctx" if zero_init else "rwkv_scan_dec",
    )(*args)


def _attn_kernel(*refs, with_cache):
    grp = B_Q_HEADS // B_KV_HEADS
    if with_cache:
        q_ref, k_ref, v_ref, ck_ref, cv_ref, o_ref = refs
        kall = jnp.concatenate([k_ref[...], ck_ref[...]], axis=0)
        vall = jnp.concatenate([v_ref[...], cv_ref[...]], axis=0)
    else:
        q_ref, k_ref, v_ref, o_ref = refs
        kall = k_ref[...]
        vall = v_ref[...]
    q = q_ref[...].astype(BF16)
    k16 = [kall[:, hk * HEAD_DIM:(hk + 1) * HEAD_DIM].astype(BF16) for hk in range(B_KV_HEADS)]
    v16 = [vall[:, hk * HEAD_DIM:(hk + 1) * HEAD_DIM].astype(BF16) for hk in range(B_KV_HEADS)]

    def scores(j):
        return _dot_nt(q[:, j * HEAD_DIM:(j + 1) * HEAD_DIM], k16[j // grp])

    outs = []
    s_next = scores(0)
    for j in range(B_Q_HEADS):
        s = s_next
        if j + 1 < B_Q_HEADS:
            s_next = scores(j + 1)
        m = jnp.max(s, axis=-1, keepdims=True)
        p = jnp.exp(s - m)
        den = jnp.sum(p, axis=-1, keepdims=True)
        outs.append(_dot(p.astype(BF16), v16[j // grp]) / den)
    o_ref[...] = jnp.concatenate(outs, axis=1)


def _attention(q, k_tok, v_tok, cache, *, row0, nseq, seqlen, tq):
    nq = seqlen // tq
    base = row0 // tq
    kbase = row0 // seqlen
    kspec = pl.BlockSpec((seqlen, KV_WIDTH), lambda s, i: (kbase + s, 0))
    in_specs = [pl.BlockSpec((tq, B_WIDTH), lambda s, i: (base + s * nq + i, 0)), kspec, kspec]
    args = [q, k_tok, v_tok]
    if cache is not None:
        ck, cv, layer = cache
        cspec = pl.BlockSpec((None, None, ck.shape[2], KV_WIDTH), lambda s, i: (s, layer, 0, 0))
        in_specs += [cspec, cspec]
        args += [ck, cv]
    return pl.pallas_call(
        functools.partial(_attn_kernel, with_cache=cache is not None),
        grid=(nseq, nq),
        in_specs=in_specs,
        out_specs=pl.BlockSpec((tq, B_WIDTH), lambda s, i: (s * nq + i, 0)),
        out_shape=jax.ShapeDtypeStruct((nseq * seqlen, B_WIDTH), F32),
        compiler_params=_cparams(("arbitrary", "arbitrary")),
        name="attention_ctx" if cache is None else "attention_dec",
    )(*args)


def _pool_kernel(p_ref, w_ref, sc_ref, o_ref, *, seqlen):
    x = p_ref[...]
    t = lax.broadcasted_iota(I32, x.shape, 0)
    lane = lax.broadcasted_iota(I32, x.shape, 1)

    def down(a, k):
        return jnp.where(t >= k, pltpu.roll(a, k, 0), 0.0)

    def up(a, k):
        return jnp.where(t < seqlen - k, pltpu.roll(a, seqlen - k, 0), 0.0)

    past = [x]
    futr = [x]
    for j in range(len(POOL_WINDOWS) - 1):
        past.append(past[j] + down(past[j], 2 ** j))
        futr.append(futr[j] + up(futr[j], 2 ** j))
    win_sum = jnp.zeros_like(x)
    cnt = jnp.zeros_like(x)
    for gi, win in enumerate(POOL_WINDOWS):
        half = win // 2
        in_group = (lane // HEAD_DIM) == gi
        win_sum = jnp.where(in_group, down(past[gi], 1) + futr[gi], win_sum)
        n = (jnp.minimum(t + half, seqlen) - jnp.maximum(t - half, 0)).astype(F32)
        cnt = jnp.where(in_group, n, cnt)
    d = win_sum / cnt - x
    o_ref[...] = _dot(d.astype(BF16), w_ref[...]) * sc_ref[...]


def _pool(p, w_blk, scale, *, row0, nseq, seqlen):
    base = row0 // seqlen
    return pl.pallas_call(
        functools.partial(_pool_kernel, seqlen=seqlen),
        grid=(nseq,),
        in_specs=[pl.BlockSpec((seqlen, C_WIDTH), lambda s: (base + s, 0)),
                  pl.BlockSpec((C_WIDTH, C_WIDTH), lambda s: (0, 0)),
                  pl.BlockSpec((1, C_WIDTH), lambda s: (0, 0))],
        out_specs=pl.BlockSpec((seqlen, C_WIDTH), lambda s: (s, 0)),
        out_shape=jax.ShapeDtypeStruct((nseq * seqlen, C_WIDTH), F32),
        compiler_params=_cparams(("arbitrary",)),
        name="pool_mixer",
    )(p, w_blk, scale)


def _out_kernel(yfc_ref, yfd_ref, ybc_ref, ybd_ref, g_ref, bon_ref, attc_ref, attd_ref, poolc_ref, poold_ref,
                hc_ref, hd_ref, mod_ref, w_ref, lng_ref, lnb_ref, gng_ref, gnb_ref, ones_ref, wrh_ref, wrl_ref,
                br_ref, h1_o, u2_o, lg_o, *, tm, bounds, alpha, n_ctx_tiles):
    grp = _group_of_row(pl.program_id(0) * tm, bounds)
    mod = mod_ref[pl.ds(grp, 1), :]
    g1 = mod[:, 2 * D_MODEL:3 * D_MODEL]
    sh2 = mod[:, 3 * D_MODEL:4 * D_MODEL]
    sc2 = mod[:, 4 * D_MODEL:5 * D_MODEL]
    ones = ones_ref[...]
    is_ctx = pl.program_id(0) < n_ctx_tiles
    pick = lambda c_ref, d_ref: jnp.where(is_ctx, c_ref[...], d_ref[...])
    y = pick(yfc_ref, yfd_ref) + pick(ybc_ref, ybd_ref)
    mu = _segsum(y, ones) * (1.0 / HEAD_DIM)
    yc = y - mu
    var = _segsum(yc * yc, ones) * (1.0 / HEAD_DIM)
    yn = yc * lax.rsqrt(var + GN_EPS) * gng_ref[...] + gnb_ref[...]
    a_out = (yn + bon_ref[...]) * g_ref[...]
    m = (_dot(a_out.astype(BF16), w_ref[0:A_WIDTH, :])
         + _dot(pick(attc_ref, attd_ref).astype(BF16), w_ref[A_WIDTH:A_WIDTH + B_WIDTH, :])
         + _dot(pick(poolc_ref, poold_ref).astype(BF16), w_ref[A_WIDTH + B_WIDTH:D_MODEL, :]))
    h1 = _layer_norm(alpha * pick(hc_ref, hd_ref) + g1 * m, lng_ref[...], lnb_ref[...])
    h1_o[...] = h1
    u2 = h1 * (1.0 + sc2) + sh2
    u2_o[...] = u2.astype(BF16)
    u_hi = u2.astype(BF16)
    u_lo = (u2 - u_hi.astype(F32)).astype(BF16)
    lg_o[...] = (_dot(u_hi, wrh_ref[...]) + _dot(u_lo, wrh_ref[...]) + _dot(u_hi, wrl_ref[...])) + br_ref[...]


def _out_projection(yf, yb, g, bonus, att, pool, h, mods_l, lw, ones, *, tm, bounds, alpha):
    t = h[0].shape[0] + h[1].shape[0]
    nct = h[0].shape[0] // tm
    full = lambda shape: pl.BlockSpec(shape, lambda i: (0,) * len(shape))
    rows = lambda w: pl.BlockSpec((tm, w), lambda i: (i, 0))
    rows_c, rows_d = _pair_specs(tm, nct)
    return pl.pallas_call(
        functools.partial(_out_kernel, tm=tm, bounds=bounds, alpha=alpha, n_ctx_tiles=nct),
        grid=(t // tm,),
        in_specs=[rows_c(A_WIDTH), rows_d(A_WIDTH), rows_c(A_WIDTH), rows_d(A_WIDTH), rows(A_WIDTH), rows(A_WIDTH),
                  rows_c(B_WIDTH), rows_d(B_WIDTH), rows_c(C_WIDTH), rows_d(C_WIDTH), rows_c(D_MODEL),
                  rows_d(D_MODEL), full((8, 6 * D_MODEL)), full((D_MODEL, D_MODEL)), full((1, D_MODEL)),
                  full((1, D_MODEL)), full((1, A_WIDTH)), full((1, A_WIDTH)), full((2 * HEAD_DIM, 2 * HEAD_DIM)),
                  full((D_MODEL, ROUTER_LANES)), full((D_MODEL, ROUTER_LANES)), full((1, ROUTER_LANES))],
        out_specs=[rows(D_MODEL), rows(D_MODEL), rows(ROUTER_LANES)],
        out_shape=[jax.ShapeDtypeStruct((t, D_MODEL), F32), jax.ShapeDtypeStruct((t, D_MODEL), BF16),
                   jax.ShapeDtypeStruct((t, ROUTER_LANES), F32)],
        compiler_params=_cparams(("arbitrary",)),
        name="out_projection",
    )(yf[0], yf[1], yb[0], yb[1], g, bonus, att[0], att[1], pool[0], pool[1], h[0], h[1], mods_l, lw['w_out'],
      lw['ln1_g'], lw['ln1_b'], lw['gn_g'], lw['gn_b'], ones, lw['w_router_hi'], lw['w_router_lo'], lw['b_router'])


def _router_gates(logits):
    lane = lax.broadcasted_iota(I32, logits.shape, 1)
    lane_f = lane.astype(F32)
    neg = -jnp.inf
    big = 1e9

    def first_lane(mask):
        return jnp.min(jnp.where(mask, lane_f, big), axis=-1, keepdims=True).astype(I32)

    is_grp = lane < N_GROUPS
    gmax = jnp.max(jnp.where(is_grp, logits, neg), axis=-1, keepdims=True)
    sel = first_lane(is_grp & (logits == gmax))
    gsum = jnp.sum(jnp.where(is_grp, jnp.exp(logits - gmax), 0.0), axis=-1, keepdims=True)
    gsel = 1.0 / gsum
    first_exp = N_GROUPS + sel * EXP_PER_GROUP
    in_sel = (lane >= first_exp) & (lane < first_exp + EXP_PER_GROUP)
    v1 = jnp.max(jnp.where(in_sel, logits, neg), axis=-1, keepdims=True)
    i1 = first_lane(in_sel & (logits == v1))
    rest = in_sel & (lane != i1)
    v2 = jnp.max(jnp.where(rest, logits, neg), axis=-1, keepdims=True)
    i2 = first_lane(rest & (logits == v2))
    e2 = jnp.exp(v2 - v1)
    t1 = (1.0 / (1.0 + e2)) * gsel
    t2 = (e2 / (1.0 + e2)) * gsel
    return jnp.where(lane == i1, t1, 0.0) + jnp.where(lane == i2, t2, 0.0), sel


def _moe_kernel(x_ref, lg_ref, w1_ref, w3_ref, w2_ref, h1_ref, mod_ref, lng_ref, lnb_ref, oc_ref, od_ref,
                xs_ref, gs_ref, acc_ref, *, tm, sub, bounds, alpha, n_ctx_tiles):
    gates, sel = _router_gates(lg_ref[...])
    lane = lax.broadcasted_iota(I32, gates.shape, 1)
    onehot = jnp.where(lane == sel, 1.0, 0.0)
    row_i = lax.broadcasted_iota(I32, (tm, tm), 0)
    col_i = lax.broadcasted_iota(I32, (tm, tm), 1)
    earlier = jnp.where(col_i < row_i, 1.0, 0.0).astype(BF16)
    rank = _dot(earlier, onehot.astype(BF16))
    lane1 = lax.broadcasted_iota(I32, (1, ROUTER_LANES), 1)
    counts = jnp.sum(onehot, axis=0, keepdims=True)
    cnt = [jnp.sum(jnp.where(lane1 == g, counts, 0.0)).astype(I32) for g in range(N_GROUPS)]
    off = [jnp.zeros((), I32)]
    for g in range(N_GROUPS - 1):
        off.append(off[g] + cnt[g])
    offs = jnp.zeros((1, ROUTER_LANES), F32)
    for g in range(N_GROUPS):
        offs = jnp.where(lane1 == g, off[g].astype(F32), offs)
    pos = jnp.sum(onehot * (rank + offs), axis=1, keepdims=True).astype(I32)
    pos_row = jnp.broadcast_to(pos, (tm, ROUTER_LANES)).T[0:1, :]
    perm = jnp.where(row_i == pos_row, 1.0, 0.0).astype(BF16)
    perm_t = jnp.where(col_i == pos, 1.0, 0.0).astype(BF16)
    xs_ref[...] = _dot(perm, x_ref[...]).astype(BF16)
    gs_ref[...] = _dot(perm, gates.astype(BF16))
    acc_ref[...] = jnp.zeros_like(acc_ref)

    lane_s = lax.broadcasted_iota(I32, (sub, ROUTER_LANES), 1)
    for s in range(tm // sub):
        rows = pl.ds(s * sub, sub)
        for g in range(N_GROUPS):
            @pl.when((off[g] < (s + 1) * sub) & (off[g] + cnt[g] > s * sub))
            def _():
                xb = xs_ref[rows, :]
                gsb = gs_ref[rows, :]
                hids = []
                for j in range(EXP_PER_GROUP):
                    e = g * EXP_PER_GROUP + j
                    ge = jnp.sum(jnp.where(lane_s == N_GROUPS + e, gsb, 0.0), axis=-1, keepdims=True)
                    h1 = _dot(xb, w1_ref[e])
                    h3 = _dot(xb, w3_ref[e])
                    hids.append(((h1 * _sigmoid(h1)) * h3 * ge).astype(BF16))
                hid = jnp.concatenate(hids, axis=1)
                w2g = w2_ref[g * EXP_PER_GROUP:(g + 1) * EXP_PER_GROUP].reshape(EXP_PER_GROUP * D_EXPERT, D_MODEL)
                acc_ref[rows, :] += _dot(hid, w2g)

    f = _dot(perm_t, acc_ref[...].astype(BF16))
    grp = _group_of_row(pl.program_id(0) * tm, bounds)
    g2 = mod_ref[pl.ds(grp, 1), 5 * D_MODEL:6 * D_MODEL]
    out = _layer_norm(alpha * h1_ref[...] + g2 * f, lng_ref[...], lnb_ref[...])
    is_ctx = pl.program_id(0) < n_ctx_tiles

    @pl.when(is_ctx)
    def _():
        oc_ref[...] = out

    @pl.when(jnp.logical_not(is_ctx))
    def _():
        od_ref[...] = out


def _moe(u2, logits, h1, mods_l, lw, *, tm, sub, bounds, alpha, n_ctx):
    t = h1.shape[0]
    nct = n_ctx // tm
    rows_c, rows_d = _pair_specs(tm, nct)
    full = lambda shape: pl.BlockSpec(shape, lambda i: (0,) * len(shape))
    once = lambda shape: pl.BlockSpec(shape, lambda i: (0,) * len(shape), pipeline_mode=pl.Buffered(1))
    rows = lambda w: pl.BlockSpec((tm, w), lambda i: (i, 0))
    return pl.pallas_call(
        functools.partial(_moe_kernel, tm=tm, sub=sub, bounds=bounds, alpha=alpha, n_ctx_tiles=nct),
        grid=(t // tm,),
        in_specs=[rows(D_MODEL), rows(ROUTER_LANES),
                  once((N_EXPERTS, D_MODEL, D_EXPERT)), once((N_EXPERTS, D_MODEL, D_EXPERT)),
                  once((N_EXPERTS, D_EXPERT, D_MODEL)),
                  rows(D_MODEL), full((8, 6 * D_MODEL)), full((1, D_MODEL)), full((1, D_MODEL))],
        out_specs=[rows_c(D_MODEL), rows_d(D_MODEL)],
        out_shape=[jax.ShapeDtypeStruct((n_ctx, D_MODEL), F32), jax.ShapeDtypeStruct((t - n_ctx, D_MODEL), F32)],
        scratch_shapes=[pltpu.VMEM((tm, D_MODEL), BF16), pltpu.VMEM((tm, ROUTER_LANES), F32),
                        pltpu.VMEM((tm, D_MODEL), F32)],
        compiler_params=_cparams(("arbitrary",)),
        name="moe",
    )(u2, logits, lw['moe_w1'], lw['moe_w3'], lw['moe_w2'], h1, mods_l, lw['ln2_g'], lw['ln2_b'])


def _rope_tables(n_ctx, dec_batch, dec_seq):
    n = HEAD_DIM // 4
    pos = jnp.arange(dec_seq)
    row = (pos // GRID_W).astype(F32)
    col = (pos % GRID_W).astype(F32)
    inv = 1.0 / (ROPE_THETA ** (jnp.arange(n, dtype=F32) / n))
    ra, ca = row[:, None] * inv, col[:, None] * inv
    cos64 = jnp.concatenate([jnp.cos(ra), jnp.cos(ra), jnp.cos(ca), jnp.cos(ca)], axis=-1)
    sin64 = jnp.concatenate([-jnp.sin(ra), jnp.sin(ra), -jnp.sin(ca), jnp.sin(ca)], axis=-1)
    cos_s = jnp.tile(jnp.concatenate([cos64, cos64], axis=-1), (dec_batch, 1))
    sin_s = jnp.tile(jnp.concatenate([sin64, sin64], axis=-1), (dec_batch, 1))
    cos = jnp.concatenate([jnp.ones((n_ctx, 128), F32), cos_s], axis=0)
    sin = jnp.concatenate([jnp.zeros((n_ctx, 128), F32), sin_s], axis=0)
    return cos, sin


def _block_diag2(w):
    z = jnp.zeros_like(w[0])
    return jnp.concatenate([jnp.concatenate([w[0], z], axis=1), jnp.concatenate([z, w[1]], axis=1)], axis=0)


def _layer_weights(l, p):
    d = D_MODEL
    pool_blk = jnp.zeros((C_WIDTH, C_WIDTH), F32)
    for gi in range(len(POOL_WINDOWS)):
        pool_blk = pool_blk.at[gi * 64:(gi + 1) * 64, gi * 64:(gi + 1) * 64].set(p['pool_w'][l, gi])
    w_router = jnp.zeros((d, ROUTER_LANES), F32)
    w_router = w_router.at[:, 0:N_GROUPS].set(p['rt_grp_w'][l])
    w_router = w_router.at[:, N_GROUPS:N_GROUPS + N_EXPERTS].set(p['rt_exp_w'][l])
    b_router = jnp.zeros((1, ROUTER_LANES), F32)
    b_router = b_router.at[0, 0:N_GROUPS].set(p['rt_grp_b'][l])
    b_router = b_router.at[0, N_GROUPS:N_GROUPS + N_EXPERTS].set(p['rt_exp_b'][l])
    return {
        'w_in': p['w_in'][l].astype(BF16),
        'w_out': p['w_out'][l].astype(BF16),
        'w2blk': _block_diag2(p['rw_w2'][l]).astype(BF16),
        'w0': p['rw_w0'][l].reshape(1, 2 * A_WIDTH),
        'a2blk': _block_diag2(p['rw_a2'][l]).astype(BF16),
        'a0': p['rw_a0'][l].reshape(1, 2 * A_WIDTH),
        'g2': p['rw_g2'][l].astype(BF16),
        'kkw': p['rw_kk'][l].reshape(1, A_WIDTH),
        'ka': jnp.tile(p['rw_ka'][l].reshape(1, A_WIDTH), (1, 2)),
        'rk': p['rw_rk'][l].reshape(1, A_WIDTH),
        'gn_g': p['rw_gn_g'][l].reshape(1, A_WIDTH),
        'gn_b': p['rw_gn_b'][l].reshape(1, A_WIDTH),
        'qn': jnp.tile(p['q_norm'][l].reshape(1, HEAD_DIM), (1, B_Q_HEADS)),
        'knw': jnp.tile(p['k_norm'][l].reshape(1, HEAD_DIM), (1, B_KV_HEADS)),
        'pool_blk': pool_blk.astype(BF16),
        'pool_scale': p['pool_scale'][l].reshape(1, C_WIDTH),
        'ln1_g': p['ln1_g'][l].reshape(1, d), 'ln1_b': p['ln1_b'][l].reshape(1, d),
        'ln2_g': p['ln2_g'][l].reshape(1, d), 'ln2_b': p['ln2_b'][l].reshape(1, d),
        'w_router_hi': w_router.astype(BF16),
        'w_router_lo': (w_router - w_router.astype(BF16).astype(F32)).astype(BF16),
        'b_router': b_router,
        'moe_w1': p['moe_w1'][l].astype(BF16),
        'moe_w3': p['moe_w3'][l].astype(BF16),
        'moe_w2': p['moe_w2'][l].astype(BF16),
    }


def _forward(x_prompt, x_sample, state_rwkv, cache_k, cache_v, c, c_ctx, p):
    batch, seq, d = x_prompt.shape
    dec_batch, dec_seq, _ = x_sample.shape
    depth = p['w_in'].shape[0]
    past = cache_k.shape[2]
    n_ctx = batch * seq
    n_tok = n_ctx + dec_batch * dec_seq
    alpha = (2 * depth) ** 0.25
    bounds = tuple(n_ctx + j * dec_seq for j in range(dec_batch))
    tm = math.gcd(n_ctx, dec_seq, 512)
    tm_moe = math.gcd(n_ctx, dec_seq, 512)
    tq_ctx = min(256, seq)
    tq_dec = min(256, dec_seq)
    sb_ctx = 2

    cond8 = jnp.zeros((8, d), F32).at[0].set(c_ctx).at[1:1 + dec_batch].set(c)
    mods = _modulation(cond8, p['w_mod'], p['b_mod'])
    cos, sin = _rope_tables(n_ctx, dec_batch, dec_seq)
    head_id = jnp.arange(2 * HEAD_DIM) // HEAD_DIM
    ones = (head_id[:, None] == head_id[None, :]).astype(BF16)

    h = (x_prompt.reshape(n_ctx, d), x_sample.reshape(dec_batch * dec_seq, d))
    ck4 = cache_k.reshape(dec_batch, depth, past, KV_WIDTH)
    cv4 = cache_v.reshape(dec_batch, depth, past, KV_WIDTH)
    new_s, new_k, new_v = [], [], []
    for l in range(depth):
        lw = _layer_weights(l, p)
        (r, v, kk, logw, kd, bvec, g, bonus, q, k_att, v_att, pp) = _in_projection(
            h, mods[l], lw, cos, sin, ones, tm=tm, bounds=bounds)
        scan_ops = (r, v, kk, logw, kd, bvec)
        yf_c, yb_c, s_fin = _rwkv_scan(scan_ops, None, row0=0, nseq=batch, seqlen=seq, sb=sb_ctx,
                                       nb=seq // CHUNK, want_state=True)
        yf_d, yb_d = _rwkv_scan(scan_ops, state_rwkv[:, l], row0=n_ctx, nseq=dec_batch, seqlen=dec_seq,
                                sb=dec_batch, nb=4, want_state=False)
        yf = (yf_c.reshape(n_ctx, A_WIDTH), yf_d.reshape(n_tok - n_ctx, A_WIDTH))
        yb = (yb_c.reshape(n_ctx, A_WIDTH), yb_d.reshape(n_tok - n_ctx, A_WIDTH))
        att = (_attention(q, k_att, v_att, None, row0=0, nseq=batch, seqlen=seq, tq=tq_ctx),
               _attention(q, k_att, v_att, (ck4, cv4, l), row0=n_ctx, nseq=dec_batch, seqlen=dec_seq, tq=tq_dec))
        pool = (_pool(pp, lw['pool_blk'], lw['pool_scale'], row0=0, nseq=batch, seqlen=seq),
                _pool(pp, lw['pool_blk'], lw['pool_scale'], row0=n_ctx, nseq=dec_batch, seqlen=dec_seq))
        h1, u2, logits = _out_projection(yf, yb, g, bonus, att, pool, h, mods[l], lw, ones,
                                         tm=tm, bounds=bounds, alpha=alpha)
        h = _moe(u2, logits, h1, mods[l], lw, tm=tm_moe, sub=128, bounds=bounds, alpha=alpha, n_ctx=n_ctx)
        new_s.append(s_fin)
        new_k.append(k_att[:n_ctx].reshape(batch, seq, B_KV_HEADS, HEAD_DIM))
        new_v.append(v_att[:n_ctx].reshape(batch, seq, B_KV_HEADS, HEAD_DIM))
    return (h[0].reshape(batch, seq, d), h[1].reshape(dec_batch, dec_seq, d),
            jnp.stack(new_s, axis=1), jnp.stack(new_k, axis=1), jnp.stack(new_v, axis=1))


def kernel(x_prompt, x_sample, state_rwkv, cache_k, cache_v, c, c_ctx, w_in, w_out, w_mod, b_mod, ln1_g, ln1_b, ln2_g, ln2_b, rw_w0, rw_w2, rw_a0, rw_a2, rw_g2, rw_kk, rw_ka, rw_rk, rw_gn_g, rw_gn_b, q_norm, k_norm, pool_w, pool_scale, rt_grp_w, rt_grp_b, rt_exp_w, rt_exp_b, moe_w1, moe_w3, moe_w2):
    p = dict(w_in=w_in, w_out=w_out, w_mod=w_mod, b_mod=b_mod, ln1_g=ln1_g, ln1_b=ln1_b, ln2_g=ln2_g,
             ln2_b=ln2_b, rw_w0=rw_w0, rw_w2=rw_w2, rw_a0=rw_a0, rw_a2=rw_a2, rw_g2=rw_g2, rw_kk=rw_kk,
             rw_ka=rw_ka, rw_rk=rw_rk, rw_gn_g=rw_gn_g, rw_gn_b=rw_gn_b, q_norm=q_norm, k_norm=k_norm,
             pool_w=pool_w, pool_scale=pool_scale, rt_grp_w=rt_grp_w, rt_grp_b=rt_grp_b, rt_exp_w=rt_exp_w,
             rt_exp_b=rt_exp_b, moe_w1=moe_w1, moe_w3=moe_w3, moe_w2=moe_w2)
    return _forward(x_prompt, x_sample, state_rwkv, cache_k, cache_v, c, c_ctx, p)
```

```python
import functools
import math

import jax
import jax.numpy as jnp
from jax import lax
from jax.experimental import pallas as pl
from jax.experimental.pallas import tpu as pltpu

F32 = jnp.float32
BF16 = jnp.bfloat16
I32 = jnp.int32

D_MODEL = 1024
GRID_W = 64
HEAD_DIM = 64
A_HEADS = 6
A_WIDTH = A_HEADS * HEAD_DIM
LORA_W = 64
LORA_A = 64
LORA_G = 128
N_DIR = 2
DECAY_SCALE = 0.606531
B_Q_HEADS = 6
B_KV_HEADS = 2
B_WIDTH = B_Q_HEADS * HEAD_DIM
KV_WIDTH = B_KV_HEADS * HEAD_DIM
ROPE_THETA = 10000.0
POOL_WINDOWS = (2, 4, 8, 16)
C_WIDTH = 256
IN_COLS = 2432
N_GROUPS = 4
EXP_PER_GROUP = 4
N_EXPERTS = 16
D_EXPERT = 256
LN_EPS = 1e-5
GN_EPS = 64e-5
QK_EPS = 1e-6

_O_R, _O_K, _O_V = 0, 384, 768
_O_LW, _O_LA, _O_LG = 1152, 1280, 1408
_O_Q, _O_KB, _O_VB, _O_P = 1536, 1920, 2048, 2176

CHUNK = 64
ROUTER_LANES = 128
VMEM_LIMIT = 56 * 1024 * 1024


def _cparams(sem):
    return pltpu.CompilerParams(dimension_semantics=sem, vmem_limit_bytes=VMEM_LIMIT)


def _sigmoid(x):
    return 1.0 / (1.0 + jnp.exp(-x))


def _dot(a, b):
    return jnp.dot(a, b, preferred_element_type=F32)


def _dot_nt(a, b):
    return lax.dot_general(a, b, (((1,), (1,)), ((), ())), preferred_element_type=F32)


def _segsum(x, ones2):
    x16 = x.astype(BF16)
    slabs = [_dot(x16[:, j:j + 128], ones2) for j in range(0, x.shape[1], 128)]
    return slabs[0] if len(slabs) == 1 else jnp.concatenate(slabs, axis=1)


def _rope128(x, cos, sin):
    lane = lax.broadcasted_iota(I32, x.shape, 1)
    first = (lane % 32) < 16
    up = pltpu.roll(x, 128 - 16, 1)
    dn = pltpu.roll(x, 16, 1)
    return x * cos + jnp.where(first, up, dn) * sin


def _layer_norm(z, g, b):
    mu = jnp.mean(z, axis=-1, keepdims=True)
    zc = z - mu
    var = jnp.mean(zc * zc, axis=-1, keepdims=True)
    return zc * lax.rsqrt(var + LN_EPS) * g + b


def _mod_kernel(c_ref, w_ref, b_ref, o_ref):
    x = c_ref[...]
    x = x * _sigmoid(x)
    w = w_ref[...]
    x_hi = x.astype(BF16)
    x_lo = (x - x_hi.astype(F32)).astype(BF16)
    w_hi = w.astype(BF16)
    w_lo = (w - w_hi.astype(F32)).astype(BF16)
    o_ref[...] = (_dot(x_hi, w_hi) + _dot(x_lo, w_hi) + _dot(x_hi, w_lo)) + b_ref[...]


def _modulation(cond8, w_mod, b_mod):
    depth = w_mod.shape[0]
    tn = 1536
    return pl.pallas_call(
        _mod_kernel,
        grid=(depth, 6 * D_MODEL // tn),
        in_specs=[
            pl.BlockSpec((8, D_MODEL), lambda l, j: (0, 0)),
            pl.BlockSpec((None, D_MODEL, tn), lambda l, j: (l, 0, j)),
            pl.BlockSpec((None, 1, tn), lambda l, j: (l, 0, j)),
        ],
        out_specs=pl.BlockSpec((None, 8, tn), lambda l, j: (l, 0, j)),
        out_shape=jax.ShapeDtypeStruct((depth, 8, 6 * D_MODEL), F32),
        compiler_params=_cparams(("arbitrary", "arbitrary")),
        name="modulation",
    )(cond8, w_mod, b_mod.reshape(depth, 1, 6 * D_MODEL))


def _group_of_row(row0, bounds):
    g = jnp.zeros((), I32)
    for b in bounds:
        g = g + jnp.where(row0 >= b, 1, 0).astype(I32)
    return g


def _in_kernel(hc_ref, hd_ref, mod_ref, w_ref, w2_ref, w0_ref, a2_ref, a0_ref, g2_ref, kkw_ref, ka_ref,
               rk_ref, qn_ref, knw_ref, cos_ref, sin_ref, ones_ref,
               r_o, v_o, kk_o, lw_o, kd_o, b_o, g_o, bon_o, q_o, k_o, vb_o, p_o, *, tm, bounds, n_ctx_tiles):
    grp = _group_of_row(pl.program_id(0) * tm, bounds)
    mod = mod_ref[pl.ds(grp, 1), :]
    sh1 = mod[:, 0:D_MODEL]
    sc1 = mod[:, D_MODEL:2 * D_MODEL]
    is_ctx = pl.program_id(0) < n_ctx_tiles
    h = jnp.where(is_ctx, hc_ref[...], hd_ref[...])
    u = h * (1.0 + sc1) + sh1
    proj = _dot(u.astype(BF16), w_ref[...])
    ones = ones_ref[...]

    r = proj[:, _O_R:_O_R + A_WIDTH]
    k = proj[:, _O_K:_O_K + A_WIDTH]
    v = proj[:, _O_V:_O_V + A_WIDTH]
    lo_w = proj[:, _O_LW:_O_LW + 128]
    lo_a = proj[:, _O_LA:_O_LA + 128]
    lo_g = proj[:, _O_LG:_O_LG + 128]

    w_pre = w0_ref[...] + _dot(jnp.tanh(lo_w).astype(BF16), w2_ref[...])
    logw = -DECAY_SCALE * _sigmoid(w_pre)
    a = _sigmoid(a0_ref[...] + _dot(lo_a.astype(BF16), a2_ref[...]))
    g = _dot(_sigmoid(lo_g).astype(BF16), g2_ref[...])

    kk = k * kkw_ref[...]
    kk = kk * lax.rsqrt(jnp.maximum(_segsum(kk * kk, ones), 1e-24))
    k2 = jnp.concatenate([k, k], axis=1)
    kd = k2 * (1.0 + (a - 1.0) * ka_ref[...])
    bvec = jnp.concatenate([kk, kk], axis=1) * a
    kd_sum = kd[:, 0:A_WIDTH] + kd[:, A_WIDTH:2 * A_WIDTH]
    bonus = _segsum(r * kd_sum * rk_ref[...], ones) * v

    r_o[...] = r
    v_o[...] = v
    kk_o[...] = kk
    lw_o[...] = logw
    kd_o[...] = kd
    b_o[...] = bvec
    g_o[...] = g
    bon_o[...] = bonus

    cos = cos_ref[...]
    sin = sin_ref[...]
    rope = lambda x: jnp.where(is_ctx, x, _rope128(x, cos, sin))
    q = proj[:, _O_Q:_O_Q + B_WIDTH]
    qn = q * lax.rsqrt(_segsum(q * q, ones) * (1.0 / HEAD_DIM) + QK_EPS) * qn_ref[...]
    scale = 1.0 / math.sqrt(HEAD_DIM)
    q_o[...] = jnp.concatenate([rope(qn[:, j * 128:(j + 1) * 128]) for j in range(B_WIDTH // 128)], axis=1) * scale
    kb = proj[:, _O_KB:_O_KB + KV_WIDTH]
    kn = kb * lax.rsqrt(_segsum(kb * kb, ones) * (1.0 / HEAD_DIM) + QK_EPS) * knw_ref[...]
    k_o[...] = rope(kn)
    vb_o[...] = proj[:, _O_VB:_O_VB + KV_WIDTH]
    p_o[...] = proj[:, _O_P:_O_P + C_WIDTH]


def _pair_specs(tm, nct):
    rows_c = lambda w: pl.BlockSpec((tm, w), lambda i: (jnp.minimum(i, nct - 1), 0))
    rows_d = lambda w: pl.BlockSpec((tm, w), lambda i: (jnp.maximum(i - nct, 0), 0))
    return rows_c, rows_d


def _layer_spec(layer):
    return lambda *shape: pl.BlockSpec((None,) + shape, lambda *_: (layer,) + (0,) * len(shape))


def _in_projection(h, mods, w_in, layer, lw, cos, sin, ones, *, tm, bounds):
    nct = h[0].shape[0] // tm
    t = h[0].shape[0] + h[1].shape[0]
    full = lambda shape: pl.BlockSpec(shape, lambda i: (0,) * len(shape))
    rows = lambda w: pl.BlockSpec((tm, w), lambda i: (i, 0))
    rows_c, rows_d = _pair_specs(tm, nct)
    lay = _layer_spec(layer)
    tiles_per_req = cos.shape[0] // tm
    table = pl.BlockSpec((tm, 128), lambda i: (jnp.maximum(i - nct, 0) % tiles_per_req, 0))
    widths = (A_WIDTH, A_WIDTH, A_WIDTH, 2 * A_WIDTH, 2 * A_WIDTH, 2 * A_WIDTH, A_WIDTH, A_WIDTH,
              B_WIDTH, KV_WIDTH, KV_WIDTH, C_WIDTH)
    return pl.pallas_call(
        functools.partial(_in_kernel, tm=tm, bounds=bounds, n_ctx_tiles=nct),
        grid=(t // tm,),
        in_specs=[
            rows_c(D_MODEL), rows_d(D_MODEL), lay(8, 6 * D_MODEL), lay(D_MODEL, IN_COLS),
            lay(128, 2 * A_WIDTH), lay(1, 2 * A_WIDTH), lay(128, 2 * A_WIDTH), lay(1, 2 * A_WIDTH),
            lay(LORA_G, A_WIDTH), lay(1, A_WIDTH), lay(1, 2 * A_WIDTH), lay(1, A_WIDTH),
            lay(1, B_WIDTH), lay(1, KV_WIDTH), table, table, full((2 * HEAD_DIM, 2 * HEAD_DIM)),
        ],
        out_specs=[rows(w) for w in widths],
        out_shape=[jax.ShapeDtypeStruct((t, w), F32) for w in widths],
        compiler_params=_cparams(("arbitrary",)),
        name="in_projection",
    )(h[0], h[1], mods, w_in, lw['w2blk'], lw['w0'], lw['a2blk'], lw['a0'], lw['g2'], lw['kkw'], lw['ka'],
      lw['rk'], lw['qn'], lw['knw'], cos, sin, ones)


def _split2(x):
    hi = x.astype(BF16)
    lo = (x - hi.astype(F32)).astype(BF16)
    return hi, lo


def _scan_kernel(*refs, sb, nb, zero_init, want_state):
    fwd = refs[0:6]
    bwd = refs[6:12]
    pos = 12
    if not zero_init:
        s0_ref = refs[pos]
        pos += 1
    y_outs = (refs[pos], refs[pos + 1])
    pos += 2
    if want_state:
        sfin_o = refs[pos]
        pos += 1
    h_ref = refs[pos]

    step = pl.program_id(1)
    nstep = pl.num_programs(1)
    npair = A_HEADS // 2
    pw = 2 * HEAD_DIM
    pairs = [(q, d, hp) for q in range(sb) for d in range(N_DIR) for hp in range(npair)]

    @pl.when(step == 0)
    def _():
        h_ref[...] = jnp.zeros_like(h_ref)
        if not zero_init:
            for (q, d, hp) in pairs:
                for e in range(2):
                    lo = e * HEAD_DIM
                    h_ref[q, d, hp, lo:lo + HEAD_DIM, lo:lo + HEAD_DIM] = s0_ref[q, d, 2 * hp + e].T

    def iota(shape, dim):
        return lax.broadcasted_iota(I32, shape, dim)

    first64 = iota((CHUNK, pw), 1) < HEAD_DIM
    eye_p = ((iota((CHUNK, pw), 1) % HEAD_DIM) == iota((CHUNK, pw), 0)).astype(F32)
    eye2 = (iota((pw, pw), 0) == iota((pw, pw), 1)).astype(F32)
    same_head = (iota((pw, pw), 0) < HEAD_DIM) == (iota((pw, pw), 1) < HEAD_DIM)
    ti = iota((CHUNK, CHUNK), 0)
    si = iota((CHUNK, CHUNK), 1)
    incl16 = ((si <= ti).astype(BF16), (si >= ti).astype(BF16))
    t4 = iota((2 * CHUNK, 2 * pw), 0)
    s4 = iota((2 * CHUNK, 2 * pw), 1) % CHUNK
    incl_off = jnp.where(t4 < CHUNK, 0, 1)
    t4 = t4 % CHUNK
    mask4 = (s4 < t4 + incl_off, s4 > t4 - incl_off)

    def bdiag(x):
        return jnp.concatenate([jnp.where(first64, x, 0.0), jnp.where(first64, 0.0, x)], axis=0)

    lhs, rhs1, bkt, gcol, v_p, amat = {}, {}, {}, {}, {}, {}
    blocks = [(q, d, j) for q in range(sb) for d in range(N_DIR) for j in range(nb)]
    csum = {}
    for (q, d, j) in blocks:
        hi, lo = _split2((fwd, bwd)[d][3][q, pl.ds(j * CHUNK, CHUNK), :])
        csum[(q, d, j)] = _dot(incl16[d], hi) + _dot(incl16[d], lo)
    for (q, d, j) in blocks:
        r_ref, v_ref, kk_ref, lw_ref, kd_ref, b_ref = (fwd, bwd)[d]
        rows = pl.ds(j * CHUNK, CHUNK)
        lw = lw_ref[q, rows, :]
        cs = csum[(q, d, j)]
        gam = jnp.exp(cs)
        gam_prev = jnp.exp(cs - lw)
        gam_inv = jnp.exp(-cs)
        gam_end = jnp.exp(cs[CHUNK - 1:CHUNK, :] if d == 0 else cs[0:1, :])
        a_hat = -kk_ref[q, rows, :] * gam_prev
        r_hat = r_ref[q, rows, :] * gam
        b_hat = b_ref[q, rows, :] * gam_inv
        k_hat = kd_ref[q, rows, :] * gam_inv
        v_all = v_ref[q, rows, :]
        for hp in range(npair):
            key = (q, d, hp, j)
            sl = slice(hp * pw, (hp + 1) * pw)
            lhs[key] = jnp.concatenate([a_hat[:, sl], r_hat[:, sl]], axis=0).astype(BF16)
            bh, kh, ge = b_hat[:, sl], k_hat[:, sl], gam_end[:, sl]
            rhs1[key] = jnp.concatenate([bdiag(bh), bdiag(kh)], axis=0).astype(BF16)
            gcol[key] = jnp.sum(eye2 * ge, axis=1, keepdims=True)
            bkt[key] = (jnp.concatenate([bh, kh], axis=0) * ge).T.astype(BF16)
            v_p[key] = v_all[:, sl]
    keys = list(lhs.keys())
    for key in keys:
        amat[key] = _dot_nt(lhs[key], rhs1[key])
    npow, tcur, a_rb, lhs2, vbd = {}, {}, {}, {}, {}
    for key in keys:
        am = jnp.where(mask4[key[1]], amat[key], 0.0)
        npow[key] = am[0:CHUNK, 0:pw]
        tcur[key] = eye_p + npow[key]
        a_rb[key] = am[CHUNK:2 * CHUNK, 0:pw].astype(BF16)
        vbd[key] = bdiag(v_p[key]).astype(BF16)
        lhs2[key] = jnp.concatenate([lhs[key], am[:, pw:2 * pw].astype(BF16)], axis=1)
    nsq = int(math.log2(CHUNK)) - 1
    for key in keys:
        npow[key] = _dot(npow[key].astype(BF16), bdiag(npow[key]).astype(BF16))
    for _ in range(nsq - 1):
        for key in keys:
            rhs = jnp.concatenate([bdiag(npow[key]), bdiag(tcur[key])], axis=1).astype(BF16)
            res = _dot(npow[key].astype(BF16), rhs)
            npow[key] = res[:, 0:pw]
            tcur[key] = tcur[key] + res[:, pw:2 * pw]
    t_inv = {}
    for key in keys:
        t_inv[key] = (tcur[key] + _dot(npow[key].astype(BF16), bdiag(tcur[key]).astype(BF16))).astype(BF16)

    hcur = {pr: h_ref[pr[0], pr[1], pr[2]] for pr in pairs}
    for jj in range(nb):
        z, u = {}, {}
        kof = lambda pr: (pr[0], pr[1], pr[2], jj if pr[1] == 0 else nb - 1 - jj)
        for pr in pairs:
            key = kof(pr)
            hv = jnp.concatenate([hcur[pr].astype(BF16), vbd[key]], axis=0)
            z[pr] = _dot(lhs2[key], hv)
        for pr in pairs:
            u[pr] = _dot(t_inv[kof(pr)], bdiag(z[pr][0:CHUNK]).astype(BF16))
        for pr in pairs:
            key = kof(pr)
            q, d, hp, j = key
            y = z[pr][CHUNK:2 * CHUNK] + _dot(a_rb[key], bdiag(u[pr]).astype(BF16))
            y_outs[d][q, pl.ds(j * CHUNK, CHUNK), hp * pw:(hp + 1) * pw] = y
            uv = jnp.concatenate([u[pr], v_p[key]], axis=0).astype(BF16)
            hcur[pr] = hcur[pr] * gcol[key] + jnp.where(same_head, _dot(bkt[key], uv), 0.0)
    for pr in pairs:
        h_ref[pr[0], pr[1], pr[2]] = hcur[pr]

    if want_state:
        @pl.when(step == nstep - 1)
        def _():
            for (q, d, hp) in pairs:
                for e in range(2):
                    lo = e * HEAD_DIM
                    sfin_o[q, d, 2 * hp + e] = hcur[(q, d, hp)][lo:lo + HEAD_DIM, lo:lo + HEAD_DIM].T


def _rwkv_scan(ops, s0, *, row0, nseq, seqlen, sb, nb, want_state):
    g = nb * CHUNK
    nstep = seqlen // g
    zero_init = s0 is None
    ops3 = [a.reshape(a.shape[0] // seqlen, seqlen, a.shape[1]) for a in ops]
    base = row0 // (seqlen * sb)
    fidx = lambda col: (lambda s, c: (base + s, c, col))
    bidx = lambda col: (lambda s, c: (base + s, nstep - 1 - c, col))
    blk = lambda im: pl.BlockSpec((sb, g, A_WIDTH), im)
    in_specs = [blk(fidx(0))] * 6 + [blk(bidx(0))] * 3 + [blk(bidx(1))] * 3
    args = ops3 + ops3
    state_spec = pl.BlockSpec((sb, N_DIR, A_HEADS, HEAD_DIM, HEAD_DIM), lambda s, c: (s, 0, 0, 0, 0))
    if not zero_init:
        in_specs.append(state_spec)
        args.append(s0)
    out_specs = [blk(lambda s, c: (s, c, 0)), blk(lambda s, c: (s, nstep - 1 - c, 0))]
    out_shape = [jax.ShapeDtypeStruct((nseq, seqlen, A_WIDTH), F32)] * 2
    if want_state:
        out_specs.append(state_spec)
        out_shape.append(jax.ShapeDtypeStruct((nseq, N_DIR, A_HEADS, HEAD_DIM, HEAD_DIM), F32))
    return pl.pallas_call(
        functools.partial(_scan_kernel, sb=sb, nb=nb, zero_init=zero_init, want_state=want_state),
        grid=(nseq // sb, nstep),
        in_specs=in_specs,
        out_specs=out_specs,
        out_shape=out_shape,
        scratch_shapes=[pltpu.VMEM((sb, N_DIR, A_HEADS // 2, 2 * HEAD_DIM, 2 * HEAD_DIM), F32)],
        compiler_params=_cparams(("arbitrary", "arbitrary")),
        name="rwkv_scan_ctx" if zero_init else "rwkv_scan_dec",
    )(*args)


def _attn_kernel(*refs, with_cache):
    grp = B_Q_HEADS // B_KV_HEADS
    if with_cache:
        q_ref, k_ref, v_ref, ck_ref, cv_ref, o_ref = refs
        kall = jnp.concatenate([k_ref[...], ck_ref[...]], axis=0)
        vall = jnp.concatenate([v_ref[...], cv_ref[...]], axis=0)
    else:
        q_ref, k_ref, v_ref, o_ref = refs
        kall = k_ref[...]
        vall = v_ref[...]
    q = q_ref[...].astype(BF16)
    k16 = [kall[:, hk * HEAD_DIM:(hk + 1) * HEAD_DIM].astype(BF16) for hk in range(B_KV_HEADS)]
    v16 = [vall[:, hk * HEAD_DIM:(hk + 1) * HEAD_DIM].astype(BF16) for hk in range(B_KV_HEADS)]

    def scores(j):
        return _dot_nt(q[:, j * HEAD_DIM:(j + 1) * HEAD_DIM], k16[j // grp])

    outs = []
    s_next = scores(0)
    for j in range(B_Q_HEADS):
        s = s_next
        if j + 1 < B_Q_HEADS:
            s_next = scores(j + 1)
        m = jnp.max(s, axis=-1, keepdims=True)
        p = jnp.exp(s - m)
        den = jnp.sum(p, axis=-1, keepdims=True)
        outs.append(_dot(p.astype(BF16), v16[j // grp]) / den)
    o_ref[...] = jnp.concatenate(outs, axis=1)


def _attention(q, k_tok, v_tok, cache, *, row0, nseq, seqlen, tq):
    nq = seqlen // tq
    base = row0 // tq
    kbase = row0 // seqlen
    kspec = pl.BlockSpec((seqlen, KV_WIDTH), lambda s, i: (kbase + s, 0))
    in_specs = [pl.BlockSpec((tq, B_WIDTH), lambda s, i: (base + s * nq + i, 0)), kspec, kspec]
    args = [q, k_tok, v_tok]
    if cache is not None:
        ck, cv, layer = cache
        cspec = pl.BlockSpec((None, None, ck.shape[2], KV_WIDTH), lambda s, i: (s, layer, 0, 0))
        in_specs += [cspec, cspec]
        args += [ck, cv]
    return pl.pallas_call(
        functools.partial(_attn_kernel, with_cache=cache is not None),
        grid=(nseq, nq),
        in_specs=in_specs,
        out_specs=pl.BlockSpec((tq, B_WIDTH), lambda s, i: (s * nq + i, 0)),
        out_shape=jax.ShapeDtypeStruct((nseq * seqlen, B_WIDTH), F32),
        compiler_params=_cparams(("arbitrary", "arbitrary")),
        name="attention_ctx" if cache is None else "attention_dec",
    )(*args)


def _pool_kernel(p_ref, w_ref, sc_ref, o_ref, *, seqlen):
    x = p_ref[...]
    t = lax.broadcasted_iota(I32, x.shape, 0)
    lane = lax.broadcasted_iota(I32, x.shape, 1)

    def down(a, k):
        return jnp.where(t >= k, pltpu.roll(a, k, 0), 0.0)

    def up(a, k):
        return jnp.where(t < seqlen - k, pltpu.roll(a, seqlen - k, 0), 0.0)

    past = [x]
    futr = [x]
    for j in range(len(POOL_WINDOWS) - 1):
        past.append(past[j] + down(past[j], 2 ** j))
        futr.append(futr[j] + up(futr[j], 2 ** j))
    win_sum = jnp.zeros_like(x)
    cnt = jnp.zeros_like(x)
    for gi, win in enumerate(POOL_WINDOWS):
        half = win // 2
        in_group = (lane // HEAD_DIM) == gi
        win_sum = jnp.where(in_group, down(past[gi], 1) + futr[gi], win_sum)
        n = (jnp.minimum(t + half, seqlen) - jnp.maximum(t - half, 0)).astype(F32)
        cnt = jnp.where(in_group, n, cnt)
    d = win_sum / cnt - x
    o_ref[...] = _dot(d.astype(BF16), w_ref[...]) * sc_ref[...]


def _pool(p, w_blk, scale, layer, *, row0, nseq, seqlen):
    base = row0 // seqlen
    lay = _layer_spec(layer)
    return pl.pallas_call(
        functools.partial(_pool_kernel, seqlen=seqlen),
        grid=(nseq,),
        in_specs=[pl.BlockSpec((seqlen, C_WIDTH), lambda s: (base + s, 0)), lay(C_WIDTH, C_WIDTH), lay(1, C_WIDTH)],
        out_specs=pl.BlockSpec((seqlen, C_WIDTH), lambda s: (s, 0)),
        out_shape=jax.ShapeDtypeStruct((nseq * seqlen, C_WIDTH), F32),
        compiler_params=_cparams(("arbitrary",)),
        name="pool_mixer",
    )(p, w_blk, scale)


def _out_kernel(yfc_ref, yfd_ref, ybc_ref, ybd_ref, g_ref, bon_ref, attc_ref, attd_ref, poolc_ref, poold_ref,
                hc_ref, hd_ref, mod_ref, w_ref, lng_ref, lnb_ref, gng_ref, gnb_ref, ones_ref, wrh_ref, wrl_ref,
                br_ref, h1_o, u2_o, lg_o, *, tm, bounds, alpha, n_ctx_tiles):
    grp = _group_of_row(pl.program_id(0) * tm, bounds)
    mod = mod_ref[pl.ds(grp, 1), :]
    g1 = mod[:, 2 * D_MODEL:3 * D_MODEL]
    sh2 = mod[:, 3 * D_MODEL:4 * D_MODEL]
    sc2 = mod[:, 4 * D_MODEL:5 * D_MODEL]
    ones = ones_ref[...]
    is_ctx = pl.program_id(0) < n_ctx_tiles
    pick = lambda c_ref, d_ref: jnp.where(is_ctx, c_ref[...], d_ref[...])
    y = pick(yfc_ref, yfd_ref) + pick(ybc_ref, ybd_ref)
    mu = _segsum(y, ones) * (1.0 / HEAD_DIM)
    yc = y - mu
    var = _segsum(yc * yc, ones) * (1.0 / HEAD_DIM)
    yn = yc * lax.rsqrt(var + GN_EPS) * gng_ref[...] + gnb_ref[...]
    a_out = (yn + bon_ref[...]) * g_ref[...]
    m = (_dot(a_out.astype(BF16), w_ref[0:A_WIDTH, :])
         + _dot(pick(attc_ref, attd_ref).astype(BF16), w_ref[A_WIDTH:A_WIDTH + B_WIDTH, :])
         + _dot(pick(poolc_ref, poold_ref).astype(BF16), w_ref[A_WIDTH + B_WIDTH:D_MODEL, :]))
    h1 = _layer_norm(alpha * pick(hc_ref, hd_ref) + g1 * m, lng_ref[...], lnb_ref[...])
    h1_o[...] = h1
    u2 = h1 * (1.0 + sc2) + sh2
    u2_o[...] = u2.astype(BF16)
    u_hi = u2.astype(BF16)
    u_lo = (u2 - u_hi.astype(F32)).astype(BF16)
    lg_o[...] = (_dot(u_hi, wrh_ref[...]) + _dot(u_lo, wrh_ref[...]) + _dot(u_hi, wrl_ref[...])) + br_ref[...]


def _out_projection(yf, yb, g, bonus, att, pool, h, mods, w_out, layer, lw, ones, *, tm, bounds, alpha):
    t = h[0].shape[0] + h[1].shape[0]
    nct = h[0].shape[0] // tm
    full = lambda shape: pl.BlockSpec(shape, lambda i: (0,) * len(shape))
    rows = lambda w: pl.BlockSpec((tm, w), lambda i: (i, 0))
    rows_c, rows_d = _pair_specs(tm, nct)
    lay = _layer_spec(layer)
    return pl.pallas_call(
        functools.partial(_out_kernel, tm=tm, bounds=bounds, alpha=alpha, n_ctx_tiles=nct),
        grid=(t // tm,),
        in_specs=[rows_c(A_WIDTH), rows_d(A_WIDTH), rows_c(A_WIDTH), rows_d(A_WIDTH), rows(A_WIDTH), rows(A_WIDTH),
                  rows_c(B_WIDTH), rows_d(B_WIDTH), rows_c(C_WIDTH), rows_d(C_WIDTH), rows_c(D_MODEL),
                  rows_d(D_MODEL), lay(8, 6 * D_MODEL), lay(D_MODEL, D_MODEL), lay(1, D_MODEL),
                  lay(1, D_MODEL), lay(1, A_WIDTH), lay(1, A_WIDTH), full((2 * HEAD_DIM, 2 * HEAD_DIM)),
                  lay(D_MODEL, ROUTER_LANES), lay(D_MODEL, ROUTER_LANES), lay(1, ROUTER_LANES)],
        out_specs=[rows(D_MODEL), rows(D_MODEL), rows(ROUTER_LANES)],
        out_shape=[jax.ShapeDtypeStruct((t, D_MODEL), F32), jax.ShapeDtypeStruct((t, D_MODEL), BF16),
                   jax.ShapeDtypeStruct((t, ROUTER_LANES), F32)],
        compiler_params=_cparams(("arbitrary",)),
        name="out_projection",
    )(yf[0], yf[1], yb[0], yb[1], g, bonus, att[0], att[1], pool[0], pool[1], h[0], h[1], mods, w_out,
      lw['ln1_g'], lw['ln1_b'], lw['gn_g'], lw['gn_b'], ones, lw['w_router_hi'], lw['w_router_lo'], lw['b_router'])


def _router_gates(logits):
    lane = lax.broadcasted_iota(I32, logits.shape, 1)
    lane_f = lane.astype(F32)
    neg = -jnp.inf
    big = 1e9

    def first_lane(mask):
        return jnp.min(jnp.where(mask, lane_f, big), axis=-1, keepdims=True).astype(I32)

    is_grp = lane < N_GROUPS
    gmax = jnp.max(jnp.where(is_grp, logits, neg), axis=-1, keepdims=True)
    sel = first_lane(is_grp & (logits == gmax))
    gsum = jnp.sum(jnp.where(is_grp, jnp.exp(logits - gmax), 0.0), axis=-1, keepdims=True)
    gsel = 1.0 / gsum
    first_exp = N_GROUPS + sel * EXP_PER_GROUP
    in_sel = (lane >= first_exp) & (lane < first_exp + EXP_PER_GROUP)
    v1 = jnp.max(jnp.where(in_sel, logits, neg), axis=-1, keepdims=True)
    i1 = first_lane(in_sel & (logits == v1))
    rest = in_sel & (lane != i1)
    v2 = jnp.max(jnp.where(rest, logits, neg), axis=-1, keepdims=True)
    i2 = first_lane(rest & (logits == v2))
    e2 = jnp.exp(v2 - v1)
    t1 = (1.0 / (1.0 + e2)) * gsel
    t2 = (e2 / (1.0 + e2)) * gsel
    return jnp.where(lane == i1, t1, 0.0) + jnp.where(lane == i2, t2, 0.0), sel


def _moe_kernel(x_ref, lg_ref, w1_ref, w3_ref, w2_ref, h1_ref, mod_ref, lng_ref, lnb_ref, oc_ref, od_ref,
                xs_ref, gs_ref, acc_ref, *, tm, sub, bounds, alpha, n_ctx_tiles):
    gates, sel = _router_gates(lg_ref[...])
    lane = lax.broadcasted_iota(I32, gates.shape, 1)
    onehot = jnp.where(lane == sel, 1.0, 0.0)
    row_i = lax.broadcasted_iota(I32, (tm, tm), 0)
    col_i = lax.broadcasted_iota(I32, (tm, tm), 1)
    earlier = jnp.where(col_i < row_i, 1.0, 0.0).astype(BF16)
    rank = _dot(earlier, onehot.astype(BF16))
    lane1 = lax.broadcasted_iota(I32, (1, ROUTER_LANES), 1)
    counts = jnp.sum(onehot, axis=0, keepdims=True)
    cnt = [jnp.sum(jnp.where(lane1 == g, counts, 0.0)).astype(I32) for g in range(N_GROUPS)]
    off = [jnp.zeros((), I32)]
    for g in range(N_GROUPS - 1):
        off.append(off[g] + cnt[g])
    offs = jnp.zeros((1, ROUTER_LANES), F32)
    for g in range(N_GROUPS):
        offs = jnp.where(lane1 == g, off[g].astype(F32), offs)
    pos = jnp.sum(onehot * (rank + offs), axis=1, keepdims=True).astype(I32)
    pos_row = jnp.broadcast_to(pos, (tm, ROUTER_LANES)).T[0:1, :]
    perm = jnp.where(row_i == pos_row, 1.0, 0.0).astype(BF16)
    perm_t = jnp.where(col_i == pos, 1.0, 0.0).astype(BF16)
    xs_ref[...] = _dot(perm, x_ref[...]).astype(BF16)
    gs_ref[...] = _dot(perm, gates.astype(BF16))
    acc_ref[...] = jnp.zeros_like(acc_ref)

    lane_s = lax.broadcasted_iota(I32, (sub, ROUTER_LANES), 1)
    for s in range(tm // sub):
        rows = pl.ds(s * sub, sub)
        for g in range(N_GROUPS):
            @pl.when((off[g] < (s + 1) * sub) & (off[g] + cnt[g] > s * sub))
            def _():
                xb = xs_ref[rows, :]
                gsb = gs_ref[rows, :]
                hids = []
                for j in range(EXP_PER_GROUP):
                    e = g * EXP_PER_GROUP + j
                    ge = jnp.sum(jnp.where(lane_s == N_GROUPS + e, gsb, 0.0), axis=-1, keepdims=True)
                    h1 = _dot(xb, w1_ref[e])
                    h3 = _dot(xb, w3_ref[e])
                    hids.append(((h1 * _sigmoid(h1)) * h3 * ge).astype(BF16))
                hid = jnp.concatenate(hids, axis=1)
                w2g = w2_ref[g * EXP_PER_GROUP:(g + 1) * EXP_PER_GROUP].reshape(EXP_PER_GROUP * D_EXPERT, D_MODEL)
                acc_ref[rows, :] += _dot(hid, w2g)

    f = _dot(perm_t, acc_ref[...].astype(BF16))
    grp = _group_of_row(pl.program_id(0) * tm, bounds)
    g2 = mod_ref[pl.ds(grp, 1), 5 * D_MODEL:6 * D_MODEL]
    out = _layer_norm(alpha * h1_ref[...] + g2 * f, lng_ref[...], lnb_ref[...])
    is_ctx = pl.program_id(0) < n_ctx_tiles

    @pl.when(is_ctx)
    def _():
        oc_ref[...] = out

    @pl.when(jnp.logical_not(is_ctx))
    def _():
        od_ref[...] = out


def _moe(u2, logits, h1, mods, experts, layer, lw, *, tm, sub, bounds, alpha, n_ctx):
    t = h1.shape[0]
    nct = n_ctx // tm
    rows_c, rows_d = _pair_specs(tm, nct)
    lay = _layer_spec(layer)
    once = lambda shape: pl.BlockSpec(shape, lambda i: (layer,) + (0,) * (len(shape) - 1),
                                      pipeline_mode=pl.Buffered(1))
    rows = lambda w: pl.BlockSpec((tm, w), lambda i: (i, 0))
    return pl.pallas_call(
        functools.partial(_moe_kernel, tm=tm, sub=sub, bounds=bounds, alpha=alpha, n_ctx_tiles=nct),
        grid=(t // tm,),
        in_specs=[rows(D_MODEL), rows(ROUTER_LANES),
                  once((None, N_EXPERTS, D_MODEL, D_EXPERT)), once((None, N_EXPERTS, D_MODEL, D_EXPERT)),
                  once((None, N_EXPERTS, D_EXPERT, D_MODEL)),
                  rows(D_MODEL), lay(8, 6 * D_MODEL), lay(1, D_MODEL), lay(1, D_MODEL)],
        out_specs=[rows_c(D_MODEL), rows_d(D_MODEL)],
        out_shape=[jax.ShapeDtypeStruct((n_ctx, D_MODEL), F32), jax.ShapeDtypeStruct((t - n_ctx, D_MODEL), F32)],
        scratch_shapes=[pltpu.VMEM((tm, D_MODEL), BF16), pltpu.VMEM((tm, ROUTER_LANES), F32),
                        pltpu.VMEM((tm, D_MODEL), F32)],
        compiler_params=_cparams(("arbitrary",)),
        name="moe",
    )(u2, logits, experts[0], experts[1], experts[2], h1, mods, lw['ln2_g'], lw['ln2_b'])


def _rope_tables(dec_seq):
    n = HEAD_DIM // 4
    pos = jnp.arange(dec_seq)
    row = (pos // GRID_W).astype(F32)
    col = (pos % GRID_W).astype(F32)
    inv = 1.0 / (ROPE_THETA ** (jnp.arange(n, dtype=F32) / n))
    ra, ca = row[:, None] * inv, col[:, None] * inv
    cos64 = jnp.concatenate([jnp.cos(ra), jnp.cos(ra), jnp.cos(ca), jnp.cos(ca)], axis=-1)
    sin64 = jnp.concatenate([-jnp.sin(ra), jnp.sin(ra), -jnp.sin(ca), jnp.sin(ca)], axis=-1)
    return jnp.concatenate([cos64, cos64], axis=-1), jnp.concatenate([sin64, sin64], axis=-1)


def _block_diag2(w):
    z = jnp.zeros_like(w[:, 0])
    return jnp.concatenate([jnp.concatenate([w[:, 0], z], axis=2), jnp.concatenate([z, w[:, 1]], axis=2)], axis=1)


def _derived_weights(p):
    d = D_MODEL
    depth = p['w_in'].shape[0]
    row = lambda a, w: a.reshape(depth, 1, w)
    pool_blk = jnp.zeros((depth, C_WIDTH, C_WIDTH), F32)
    for gi in range(len(POOL_WINDOWS)):
        pool_blk = pool_blk.at[:, gi * 64:(gi + 1) * 64, gi * 64:(gi + 1) * 64].set(p['pool_w'][:, gi])
    w_router = jnp.zeros((depth, d, ROUTER_LANES), F32)
    w_router = w_router.at[:, :, 0:N_GROUPS].set(p['rt_grp_w'])
    w_router = w_router.at[:, :, N_GROUPS:N_GROUPS + N_EXPERTS].set(p['rt_exp_w'])
    b_router = jnp.zeros((depth, 1, ROUTER_LANES), F32)
    b_router = b_router.at[:, 0, 0:N_GROUPS].set(p['rt_grp_b'])
    b_router = b_router.at[:, 0, N_GROUPS:N_GROUPS + N_EXPERTS].set(p['rt_exp_b'])
    return {
        'w2blk': _block_diag2(p['rw_w2']).astype(BF16),
        'w0': row(p['rw_w0'], 2 * A_WIDTH),
        'a2blk': _block_diag2(p['rw_a2']).astype(BF16),
        'a0': row(p['rw_a0'], 2 * A_WIDTH),
        'g2': p['rw_g2'].astype(BF16),
        'kkw': row(p['rw_kk'], A_WIDTH),
        'ka': jnp.tile(row(p['rw_ka'], A_WIDTH), (1, 1, 2)),
        'rk': row(p['rw_rk'], A_WIDTH),
        'gn_g': row(p['rw_gn_g'], A_WIDTH),
        'gn_b': row(p['rw_gn_b'], A_WIDTH),
        'qn': jnp.tile(row(p['q_norm'], HEAD_DIM), (1, 1, B_Q_HEADS)),
        'knw': jnp.tile(row(p['k_norm'], HEAD_DIM), (1, 1, B_KV_HEADS)),
        'pool_blk': pool_blk.astype(BF16),
        'pool_scale': row(p['pool_scale'], C_WIDTH),
        'ln1_g': row(p['ln1_g'], d), 'ln1_b': row(p['ln1_b'], d),
        'ln2_g': row(p['ln2_g'], d), 'ln2_b': row(p['ln2_b'], d),
        'w_router_hi': w_router.astype(BF16),
        'w_router_lo': (w_router - w_router.astype(BF16).astype(F32)).astype(BF16),
        'b_router': b_router,
    }


def _forward(x_prompt, x_sample, state_rwkv, cache_k, cache_v, c, c_ctx, p):
    batch, seq, d = x_prompt.shape
    dec_batch, dec_seq, _ = x_sample.shape
    depth = p['w_in'].shape[0]
    past = cache_k.shape[2]
    n_ctx = batch * seq
    n_tok = n_ctx + dec_batch * dec_seq
    alpha = (2 * depth) ** 0.25
    bounds = tuple(n_ctx + j * dec_seq for j in range(dec_batch))
    tm = math.gcd(n_ctx, dec_seq, 512)
    tm_moe = math.gcd(n_ctx, dec_seq, 512)
    tq_ctx = min(256, seq)
    tq_dec = min(256, dec_seq)
    sb_ctx = 2

    cond8 = jnp.zeros((8, d), F32).at[0].set(c_ctx).at[1:1 + dec_batch].set(c)
    mods = _modulation(cond8, p['w_mod'], p['b_mod'])
    cos, sin = _rope_tables(dec_seq)
    w_in16, w_out16 = p['w_in'].astype(BF16), p['w_out'].astype(BF16)
    experts16 = (p['moe_w1'].astype(BF16), p['moe_w3'].astype(BF16), p['moe_w2'].astype(BF16))
    lw = _derived_weights(p)
    head_id = jnp.arange(2 * HEAD_DIM) // HEAD_DIM
    ones = (head_id[:, None] == head_id[None, :]).astype(BF16)

    h = (x_prompt.reshape(n_ctx, d), x_sample.reshape(dec_batch * dec_seq, d))
    ck4 = cache_k.reshape(dec_batch, depth, past, KV_WIDTH)
    cv4 = cache_v.reshape(dec_batch, depth, past, KV_WIDTH)
    new_s, new_k, new_v = [], [], []
    for l in range(depth):
        (r, v, kk, logw, kd, bvec, g, bonus, q, k_att, v_att, pp) = _in_projection(
            h, mods, w_in16, l, lw, cos, sin, ones, tm=tm, bounds=bounds)
        scan_ops = (r, v, kk, logw, kd, bvec)
        yf_c, yb_c, s_fin = _rwkv_scan(scan_ops, None, row0=0, nseq=batch, seqlen=seq, sb=sb_ctx,
                                       nb=seq // CHUNK, want_state=True)
        yf_d, yb_d = _rwkv_scan(scan_ops, state_rwkv[:, l], row0=n_ctx, nseq=dec_batch, seqlen=dec_seq,
                                sb=dec_batch, nb=4, want_state=False)
        yf = (yf_c.reshape(n_ctx, A_WIDTH), yf_d.reshape(n_tok - n_ctx, A_WIDTH))
        yb = (yb_c.reshape(n_ctx, A_WIDTH), yb_d.reshape(n_tok - n_ctx, A_WIDTH))
        att = (_attention(q, k_att, v_att, None, row0=0, nseq=batch, seqlen=seq, tq=tq_ctx),
               _attention(q, k_att, v_att, (ck4, cv4, l), row0=n_ctx, nseq=dec_batch, seqlen=dec_seq, tq=tq_dec))
        pool = (_pool(pp, lw['pool_blk'], lw['pool_scale'], l, row0=0, nseq=batch, seqlen=seq),
                _pool(pp, lw['pool_blk'], lw['pool_scale'], l, row0=n_ctx, nseq=dec_batch, seqlen=dec_seq))
        h1, u2, logits = _out_projection(yf, yb, g, bonus, att, pool, h, mods, w_out16, l, lw, ones,
                                         tm=tm, bounds=bounds, alpha=alpha)
        h = _moe(u2, logits, h1, mods, experts16, l, lw, tm=tm_moe, sub=128, bounds=bounds, alpha=alpha,
                 n_ctx=n_ctx)
        new_s.append(s_fin)
        new_k.append(k_att[:n_ctx].reshape(batch, seq, B_KV_HEADS, HEAD_DIM))
        new_v.append(v_att[:n_ctx].reshape(batch, seq, B_KV_HEADS, HEAD_DIM))
    return (h[0].reshape(batch, seq, d), h[1].reshape(dec_batch, dec_seq, d),
            jnp.stack(new_s, axis=1), jnp.stack(new_k, axis=1), jnp.stack(new_v, axis=1))


def kernel(x_prompt, x_sample, state_rwkv, cache_k, cache_v, c, c_ctx, w_in, w_out, w_mod, b_mod, ln1_g, ln1_b, ln2_g, ln2_b, rw_w0, rw_w2, rw_a0, rw_a2, rw_g2, rw_kk, rw_ka, rw_rk, rw_gn_g, rw_gn_b, q_norm, k_norm, pool_w, pool_scale, rt_grp_w, rt_grp_b, rt_exp_w, rt_exp_b, moe_w1, moe_w3, moe_w2):
    p = dict(w_in=w_in, w_out=w_out, w_mod=w_mod, b_mod=b_mod, ln1_g=ln1_g, ln1_b=ln1_b, ln2_g=ln2_g,
             ln2_b=ln2_b, rw_w0=rw_w0, rw_w2=rw_w2, rw_a0=rw_a0, rw_a2=rw_a2, rw_g2=rw_g2, rw_kk=rw_kk,
             rw_ka=rw_ka, rw_rk=rw_rk, rw_gn_g=rw_gn_g, rw_gn_b=rw_gn_b, q_norm=q_norm, k_norm=k_norm,
             pool_w=pool_w, pool_scale=pool_scale, rt_grp_w=rt_grp_w, rt_grp_b=rt_grp_b, rt_exp_w=rt_exp_w,
             rt_exp_b=rt_exp_b, moe_w1=moe_w1, moe_w3=moe_w3, moe_w2=moe_w2)
    return _forward(x_prompt, x_sample, state_rwkv, cache_k, cache_v, c, c_ctx, p)
```

```python
import functools
import math

import jax
import jax.numpy as jnp
from jax import lax
from jax.experimental import pallas as pl
from jax.experimental.pallas import tpu as pltpu

F32 = jnp.float32
BF16 = jnp.bfloat16
I32 = jnp.int32

D_MODEL = 1024
GRID_W = 64
HEAD_DIM = 64
A_HEADS = 6
A_WIDTH = A_HEADS * HEAD_DIM
LORA_W = 64
LORA_A = 64
LORA_G = 128
N_DIR = 2
DECAY_SCALE = 0.606531
B_Q_HEADS = 6
B_KV_HEADS = 2
B_WIDTH = B_Q_HEADS * HEAD_DIM
KV_WIDTH = B_KV_HEADS * HEAD_DIM
ROPE_THETA = 10000.0
POOL_WINDOWS = (2, 4, 8, 16)
C_WIDTH = 256
IN_COLS = 2432
N_GROUPS = 4
EXP_PER_GROUP = 4
N_EXPERTS = 16
D_EXPERT = 256
LN_EPS = 1e-5
GN_EPS = 64e-5
QK_EPS = 1e-6

_O_R, _O_K, _O_V = 0, 384, 768
_O_LW, _O_LA, _O_LG = 1152, 1280, 1408
_O_Q, _O_KB, _O_VB, _O_P = 1536, 1920, 2048, 2176

CHUNK = 64
ROUTER_LANES = 128
VMEM_LIMIT = 56 * 1024 * 1024


def _cparams(sem):
    return pltpu.CompilerParams(dimension_semantics=sem, vmem_limit_bytes=VMEM_LIMIT)


def _sigmoid(x):
    return 1.0 / (1.0 + jnp.exp(-x))


def _dot(a, b):
    return jnp.dot(a, b, preferred_element_type=F32)


def _dot_nt(a, b):
    return lax.dot_general(a, b, (((1,), (1,)), ((), ())), preferred_element_type=F32)


def _segsum(x, ones2):
    x16 = x.astype(BF16)
    slabs = [_dot(x16[:, j:j + 128], ones2) for j in range(0, x.shape[1], 128)]
    return slabs[0] if len(slabs) == 1 else jnp.concatenate(slabs, axis=1)


def _rope128(x, cos, sin):
    lane = lax.broadcasted_iota(I32, x.shape, 1)
    first = (lane % 32) < 16
    up = pltpu.roll(x, 128 - 16, 1)
    dn = pltpu.roll(x, 16, 1)
    return x * cos + jnp.where(first, up, dn) * sin


def _layer_norm(z, g, b):
    mu = jnp.mean(z, axis=-1, keepdims=True)
    zc = z - mu
    var = jnp.mean(zc * zc, axis=-1, keepdims=True)
    return zc * lax.rsqrt(var + LN_EPS) * g + b


def _mod_kernel(c_ref, w_ref, b_ref, o_ref):
    x = c_ref[...]
    x = x * _sigmoid(x)
    w = w_ref[...]
    x_hi = x.astype(BF16)
    x_lo = (x - x_hi.astype(F32)).astype(BF16)
    w_hi = w.astype(BF16)
    w_lo = (w - w_hi.astype(F32)).astype(BF16)
    o_ref[...] = (_dot(x_hi, w_hi) + _dot(x_lo, w_hi) + _dot(x_hi, w_lo)) + b_ref[...]


def _modulation(cond8, w_mod, b_mod):
    depth = w_mod.shape[0]
    tn = 1536
    return pl.pallas_call(
        _mod_kernel,
        grid=(depth, 6 * D_MODEL // tn),
        in_specs=[
            pl.BlockSpec((8, D_MODEL), lambda l, j: (0, 0)),
            pl.BlockSpec((None, D_MODEL, tn), lambda l, j: (l, 0, j)),
            pl.BlockSpec((None, 1, tn), lambda l, j: (l, 0, j)),
        ],
        out_specs=pl.BlockSpec((None, 8, tn), lambda l, j: (l, 0, j)),
        out_shape=jax.ShapeDtypeStruct((depth, 8, 6 * D_MODEL), F32),
        compiler_params=_cparams(("arbitrary", "arbitrary")),
        name="modulation",
    )(cond8, w_mod, b_mod.reshape(depth, 1, 6 * D_MODEL))


def _group_of_row(row0, bounds):
    g = jnp.zeros((), I32)
    for b in bounds:
        g = g + jnp.where(row0 >= b, 1, 0).astype(I32)
    return g


def _in_kernel(hc_ref, hd_ref, mod_ref, w_ref, w2_ref, w0_ref, a2_ref, a0_ref, g2_ref, kkw_ref, ka_ref,
               rk_ref, qn_ref, knw_ref, cos_ref, sin_ref, ones_ref,
               r_o, v_o, kk_o, lw_o, kd_o, b_o, g_o, bon_o, q_o, k_o, vb_o, p_o, *, tm, bounds, n_ctx_tiles):
    grp = _group_of_row(pl.program_id(0) * tm, bounds)
    mod = mod_ref[pl.ds(grp, 1), :]
    sh1 = mod[:, 0:D_MODEL]
    sc1 = mod[:, D_MODEL:2 * D_MODEL]
    is_ctx = pl.program_id(0) < n_ctx_tiles
    h = jnp.where(is_ctx, hc_ref[...], hd_ref[...])
    u = h * (1.0 + sc1) + sh1
    ones = ones_ref[...]
    scale = 1.0 / math.sqrt(HEAD_DIM)
    proj = _dot(u.astype(BF16), w_ref[...])

    r = proj[:, _O_R:_O_R + A_WIDTH]
    k = proj[:, _O_K:_O_K + A_WIDTH]
    v = proj[:, _O_V:_O_V + A_WIDTH]
    lo_w = proj[:, _O_LW:_O_LW + 128]
    lo_a = proj[:, _O_LA:_O_LA + 128]
    lo_g = proj[:, _O_LG:_O_LG + 128]

    w_pre = w0_ref[...] + _dot(jnp.tanh(lo_w).astype(BF16), w2_ref[...])
    logw = -DECAY_SCALE * _sigmoid(w_pre)
    a = _sigmoid(a0_ref[...] + _dot(lo_a.astype(BF16), a2_ref[...]))
    g = _dot(_sigmoid(lo_g).astype(BF16), g2_ref[...])

    kk = k * kkw_ref[...]
    kk = kk * lax.rsqrt(jnp.maximum(_segsum(kk * kk, ones), 1e-24))
    k2 = jnp.concatenate([k, k], axis=1)
    kd = k2 * (1.0 + (a - 1.0) * ka_ref[...])
    bvec = jnp.concatenate([kk, kk], axis=1) * a
    kd_sum = kd[:, 0:A_WIDTH] + kd[:, A_WIDTH:2 * A_WIDTH]
    bonus = _segsum(r * kd_sum * rk_ref[...], ones) * v

    r_o[...] = r
    v_o[...] = v
    kk_o[...] = kk
    lw_o[...] = logw
    kd_o[...] = kd
    b_o[...] = bvec
    g_o[...] = g
    bon_o[...] = bonus

    cos = cos_ref[...]
    sin = sin_ref[...]
    rope = lambda x: jnp.where(is_ctx, x, _rope128(x, cos, sin))
    q = proj[:, _O_Q:_O_Q + B_WIDTH]
    qn = q * lax.rsqrt(_segsum(q * q, ones) * (1.0 / HEAD_DIM) + QK_EPS) * qn_ref[...]
    q_o[...] = jnp.concatenate([rope(qn[:, j:j + 128]) for j in range(0, B_WIDTH, 128)], axis=1) * scale
    kb = proj[:, _O_KB:_O_KB + KV_WIDTH]
    kn = kb * lax.rsqrt(_segsum(kb * kb, ones) * (1.0 / HEAD_DIM) + QK_EPS) * knw_ref[...]
    k_o[...] = rope(kn)
    vb_o[...] = proj[:, _O_VB:_O_VB + KV_WIDTH]
    p_o[...] = proj[:, _O_P:_O_P + C_WIDTH]


def _pair_specs(tm, nct):
    rows_c = lambda w: pl.BlockSpec((tm, w), lambda i: (jnp.minimum(i, nct - 1), 0))
    rows_d = lambda w: pl.BlockSpec((tm, w), lambda i: (jnp.maximum(i - nct, 0), 0))
    return rows_c, rows_d


def _layer_spec(layer):
    return lambda *shape: pl.BlockSpec((None,) + shape, lambda *_: (layer,) + (0,) * len(shape))


def _in_projection(h, mods, w_in, layer, lw, cos, sin, ones, *, tm, bounds):
    nct = h[0].shape[0] // tm
    t = h[0].shape[0] + h[1].shape[0]
    full = lambda shape: pl.BlockSpec(shape, lambda i: (0,) * len(shape))
    rows = lambda w: pl.BlockSpec((tm, w), lambda i: (i, 0))
    rows_c, rows_d = _pair_specs(tm, nct)
    lay = _layer_spec(layer)
    tiles_per_req = cos.shape[0] // tm
    table = pl.BlockSpec((tm, 128), lambda i: (jnp.maximum(i - nct, 0) % tiles_per_req, 0))
    widths = (A_WIDTH, A_WIDTH, A_WIDTH, 2 * A_WIDTH, 2 * A_WIDTH, 2 * A_WIDTH, A_WIDTH, A_WIDTH,
              B_WIDTH, KV_WIDTH, KV_WIDTH, C_WIDTH)
    return pl.pallas_call(
        functools.partial(_in_kernel, tm=tm, bounds=bounds, n_ctx_tiles=nct),
        grid=(t // tm,),
        in_specs=[
            rows_c(D_MODEL), rows_d(D_MODEL), lay(8, 6 * D_MODEL), lay(D_MODEL, IN_COLS),
            lay(128, 2 * A_WIDTH), lay(1, 2 * A_WIDTH), lay(128, 2 * A_WIDTH), lay(1, 2 * A_WIDTH),
            lay(LORA_G, A_WIDTH), lay(1, A_WIDTH), lay(1, 2 * A_WIDTH), lay(1, A_WIDTH),
            lay(1, B_WIDTH), lay(1, KV_WIDTH), table, table, full((2 * HEAD_DIM, 2 * HEAD_DIM)),
        ],
        out_specs=[rows(w) for w in widths],
        out_shape=[jax.ShapeDtypeStruct((t, w), F32) for w in widths],
        compiler_params=_cparams(("arbitrary",)),
        name="in_projection",
    )(h[0], h[1], mods, w_in, lw['w2blk'], lw['w0'], lw['a2blk'], lw['a0'], lw['g2'], lw['kkw'], lw['ka'],
      lw['rk'], lw['qn'], lw['knw'], cos, sin, ones)


def _split2(x):
    hi = x.astype(BF16)
    lo = (x - hi.astype(F32)).astype(BF16)
    return hi, lo


def _scan_kernel(*refs, sb, nb, zero_init, want_state):
    fwd = refs[0:6]
    bwd = refs[6:12]
    pos = 12
    if not zero_init:
        s0_ref = refs[pos]
        pos += 1
    y_outs = (refs[pos], refs[pos + 1])
    pos += 2
    if want_state:
        sfin_o = refs[pos]
        pos += 1
    h_ref = refs[pos]

    step = pl.program_id(1)
    nstep = pl.num_programs(1)
    npair = A_HEADS // 2
    pw = 2 * HEAD_DIM
    pairs = [(q, d, hp) for q in range(sb) for d in range(N_DIR) for hp in range(npair)]

    @pl.when(step == 0)
    def _():
        h_ref[...] = jnp.zeros_like(h_ref)
        if not zero_init:
            for (q, d, hp) in pairs:
                for e in range(2):
                    lo = e * HEAD_DIM
                    h_ref[q, d, hp, lo:lo + HEAD_DIM, lo:lo + HEAD_DIM] = s0_ref[q, d, 2 * hp + e].T

    def iota(shape, dim):
        return lax.broadcasted_iota(I32, shape, dim)

    first64 = iota((CHUNK, pw), 1) < HEAD_DIM
    eye_p = ((iota((CHUNK, pw), 1) % HEAD_DIM) == iota((CHUNK, pw), 0)).astype(F32)
    eye2 = (iota((pw, pw), 0) == iota((pw, pw), 1)).astype(F32)
    same_head = (iota((pw, pw), 0) < HEAD_DIM) == (iota((pw, pw), 1) < HEAD_DIM)
    ti = iota((CHUNK, CHUNK), 0)
    si = iota((CHUNK, CHUNK), 1)
    incl16 = ((si <= ti).astype(BF16), (si >= ti).astype(BF16))
    t4 = iota((2 * CHUNK, 2 * pw), 0)
    s4 = iota((2 * CHUNK, 2 * pw), 1) % CHUNK
    incl_off = jnp.where(t4 < CHUNK, 0, 1)
    t4 = t4 % CHUNK
    mask4 = (s4 < t4 + incl_off, s4 > t4 - incl_off)

    def bdiag(x):
        return jnp.concatenate([jnp.where(first64, x, 0.0), jnp.where(first64, 0.0, x)], axis=0)

    lhs, rhs1, bkt, gcol, v_p, amat = {}, {}, {}, {}, {}, {}
    blocks = [(q, d, j) for q in range(sb) for d in range(N_DIR) for j in range(nb)]
    csum = {}
    for (q, d, j) in blocks:
        hi, lo = _split2((fwd, bwd)[d][3][q, pl.ds(j * CHUNK, CHUNK), :])
        csum[(q, d, j)] = _dot(incl16[d], hi) + _dot(incl16[d], lo)
    for (q, d, j) in blocks:
        r_ref, v_ref, kk_ref, lw_ref, kd_ref, b_ref = (fwd, bwd)[d]
        rows = pl.ds(j * CHUNK, CHUNK)
        lw = lw_ref[q, rows, :]
        cs = csum[(q, d, j)]
        gam = jnp.exp(cs)
        gam_prev = jnp.exp(cs - lw)
        gam_inv = jnp.exp(-cs)
        gam_end = jnp.exp(cs[CHUNK - 1:CHUNK, :] if d == 0 else cs[0:1, :])
        a_hat = -kk_ref[q, rows, :] * gam_prev
        r_hat = r_ref[q, rows, :] * gam
        b_hat = b_ref[q, rows, :] * gam_inv
        k_hat = kd_ref[q, rows, :] * gam_inv
        v_all = v_ref[q, rows, :]
        for hp in range(npair):
            key = (q, d, hp, j)
            sl = slice(hp * pw, (hp + 1) * pw)
            lhs[key] = jnp.concatenate([a_hat[:, sl], r_hat[:, sl]], axis=0).astype(BF16)
            bh, kh, ge = b_hat[:, sl], k_hat[:, sl], gam_end[:, sl]
            rhs1[key] = jnp.concatenate([bdiag(bh), bdiag(kh)], axis=0).astype(BF16)
            gcol[key] = jnp.sum(eye2 * ge, axis=1, keepdims=True)
            bkt[key] = (jnp.concatenate([bh, kh], axis=0) * ge).T.astype(BF16)
            v_p[key] = v_all[:, sl]
    keys = list(lhs.keys())
    for key in keys:
        amat[key] = _dot_nt(lhs[key], rhs1[key])
    npow, tcur, a_rb, lhs2, vbd = {}, {}, {}, {}, {}
    for key in keys:
        am = jnp.where(mask4[key[1]], amat[key], 0.0)
        npow[key] = am[0:CHUNK, 0:pw]
        tcur[key] = eye_p + npow[key]
        a_rb[key] = am[CHUNK:2 * CHUNK, 0:pw].astype(BF16)
        vbd[key] = bdiag(v_p[key]).astype(BF16)
        lhs2[key] = jnp.concatenate([lhs[key], am[:, pw:2 * pw].astype(BF16)], axis=1)
    nsq = int(math.log2(CHUNK)) - 1
    for key in keys:
        npow[key] = _dot(npow[key].astype(BF16), bdiag(npow[key]).astype(BF16))
    for _ in range(nsq - 1):
        for key in keys:
            rhs = jnp.concatenate([bdiag(npow[key]), bdiag(tcur[key])], axis=1).astype(BF16)
            res = _dot(npow[key].astype(BF16), rhs)
            npow[key] = res[:, 0:pw]
            tcur[key] = tcur[key] + res[:, pw:2 * pw]
    t_inv = {}
    for key in keys:
        t_inv[key] = (tcur[key] + _dot(npow[key].astype(BF16), bdiag(tcur[key]).astype(BF16))).astype(BF16)

    hcur = {pr: h_ref[pr[0], pr[1], pr[2]] for pr in pairs}
    for jj in range(nb):
        z, u = {}, {}
        kof = lambda pr: (pr[0], pr[1], pr[2], jj if pr[1] == 0 else nb - 1 - jj)
        for pr in pairs:
            key = kof(pr)
            hv = jnp.concatenate([hcur[pr].astype(BF16), vbd[key]], axis=0)
            z[pr] = _dot(lhs2[key], hv)
        for pr in pairs:
            u[pr] = _dot(t_inv[kof(pr)], bdiag(z[pr][0:CHUNK]).astype(BF16))
        for pr in pairs:
            key = kof(pr)
            q, d, hp, j = key
            y = z[pr][CHUNK:2 * CHUNK] + _dot(a_rb[key], bdiag(u[pr]).astype(BF16))
            y_outs[d][q, pl.ds(j * CHUNK, CHUNK), hp * pw:(hp + 1) * pw] = y
            uv = jnp.concatenate([u[pr], v_p[key]], axis=0).astype(BF16)
            hcur[pr] = hcur[pr] * gcol[key] + jnp.where(same_head, _dot(bkt[key], uv), 0.0)
    for pr in pairs:
        h_ref[pr[0], pr[1], pr[2]] = hcur[pr]

    if want_state:
        @pl.when(step == nstep - 1)
        def _():
            for (q, d, hp) in pairs:
                for e in range(2):
                    lo = e * HEAD_DIM
                    sfin_o[q, d, 2 * hp + e] = hcur[(q, d, hp)][lo:lo + HEAD_DIM, lo:lo + HEAD_DIM].T


def _rwkv_scan(ops, s0, *, row0, nseq, seqlen, sb, nb, want_state):
    g = nb * CHUNK
    nstep = seqlen // g
    zero_init = s0 is None
    ops3 = [a.reshape(a.shape[0] // seqlen, seqlen, a.shape[1]) for a in ops]
    base = row0 // (seqlen * sb)
    fidx = lambda col: (lambda s, c: (base + s, c, col))
    bidx = lambda col: (lambda s, c: (base + s, nstep - 1 - c, col))
    blk = lambda im: pl.BlockSpec((sb, g, A_WIDTH), im)
    in_specs = [blk(fidx(0))] * 6 + [blk(bidx(0))] * 3 + [blk(bidx(1))] * 3
    args = ops3 + ops3
    state_spec = pl.BlockSpec((sb, N_DIR, A_HEADS, HEAD_DIM, HEAD_DIM), lambda s, c: (s, 0, 0, 0, 0))
    if not zero_init:
        in_specs.append(state_spec)
        args.append(s0)
    out_specs = [blk(lambda s, c: (s, c, 0)), blk(lambda s, c: (s, nstep - 1 - c, 0))]
    out_shape = [jax.ShapeDtypeStruct((nseq, seqlen, A_WIDTH), F32)] * 2
    if want_state:
        out_specs.append(state_spec)
        out_shape.append(jax.ShapeDtypeStruct((nseq, N_DIR, A_HEADS, HEAD_DIM, HEAD_DIM), F32))
    return pl.pallas_call(
        functools.partial(_scan_kernel, sb=sb, nb=nb, zero_init=zero_init, want_state=want_state),
        grid=(nseq // sb, nstep),
        in_specs=in_specs,
        out_specs=out_specs,
        out_shape=out_shape,
        scratch_shapes=[pltpu.VMEM((sb, N_DIR, A_HEADS // 2, 2 * HEAD_DIM, 2 * HEAD_DIM), F32)],
        compiler_params=_cparams(("arbitrary", "arbitrary")),
        name="rwkv_scan_ctx" if zero_init else "rwkv_scan_dec",
    )(*args)


def _attn_kernel(*refs, with_cache):
    grp = B_Q_HEADS // B_KV_HEADS
    if with_cache:
        q_ref, k_ref, v_ref, ck_ref, cv_ref, p_ref, pw_ref, psc_ref, o_ref, pool_o = refs
        kall = jnp.concatenate([k_ref[...], ck_ref[...]], axis=0)
        vall = jnp.concatenate([v_ref[...], cv_ref[...]], axis=0)
    else:
        q_ref, k_ref, v_ref, p_ref, pw_ref, psc_ref, o_ref, pool_o = refs
        kall = k_ref[...]
        vall = v_ref[...]

    @pl.when(pl.program_id(1) == 0)
    def _():
        pool_o[...] = _pool_mix(p_ref[...], pw_ref[...], psc_ref[...])

    q = q_ref[...].astype(BF16)
    k16 = [kall[:, hk * HEAD_DIM:(hk + 1) * HEAD_DIM].astype(BF16) for hk in range(B_KV_HEADS)]
    v16 = [vall[:, hk * HEAD_DIM:(hk + 1) * HEAD_DIM].astype(BF16) for hk in range(B_KV_HEADS)]

    def scores(j):
        return _dot_nt(q[:, j * HEAD_DIM:(j + 1) * HEAD_DIM], k16[j // grp])

    outs = []
    s_next = scores(0)
    for j in range(B_Q_HEADS):
        s = s_next
        if j + 1 < B_Q_HEADS:
            s_next = scores(j + 1)
        m = jnp.max(s, axis=-1, keepdims=True)
        p = jnp.exp(s - m)
        den = jnp.sum(p, axis=-1, keepdims=True)
        outs.append(_dot(p.astype(BF16), v16[j // grp]) / den)
    o_ref[...] = jnp.concatenate(outs, axis=1)


def _attention(q, k_tok, v_tok, cache, p_tok, pool_w, pool_scale, layer, *, row0, nseq, seqlen, tq):
    nq = seqlen // tq
    base = row0 // tq
    kbase = row0 // seqlen
    kspec = pl.BlockSpec((seqlen, KV_WIDTH), lambda s, i: (kbase + s, 0))
    in_specs = [pl.BlockSpec((tq, B_WIDTH), lambda s, i: (base + s * nq + i, 0)), kspec, kspec]
    args = [q, k_tok, v_tok]
    if cache is not None:
        ck, cv = cache
        cspec = pl.BlockSpec((None, None, ck.shape[2], KV_WIDTH), lambda s, i: (s, layer, 0, 0))
        in_specs += [cspec, cspec]
        args += [ck, cv]
    lay = _layer_spec(layer)
    in_specs += [pl.BlockSpec((seqlen, C_WIDTH), lambda s, i: (kbase + s, 0)), lay(C_WIDTH, C_WIDTH), lay(1, C_WIDTH)]
    args += [p_tok, pool_w, pool_scale]
    return pl.pallas_call(
        functools.partial(_attn_kernel, with_cache=cache is not None),
        grid=(nseq, nq),
        in_specs=in_specs,
        out_specs=[pl.BlockSpec((tq, B_WIDTH), lambda s, i: (s * nq + i, 0)),
                   pl.BlockSpec((seqlen, C_WIDTH), lambda s, i: (s, 0))],
        out_shape=[jax.ShapeDtypeStruct((nseq * seqlen, B_WIDTH), F32),
                   jax.ShapeDtypeStruct((nseq * seqlen, C_WIDTH), F32)],
        compiler_params=_cparams(("arbitrary", "arbitrary")),
        name="attention_ctx" if cache is None else "attention_dec",
    )(*args)


def _pool_mix(x, w_blk, scale):
    seqlen = x.shape[0]
    t = lax.broadcasted_iota(I32, x.shape, 0)
    lane = lax.broadcasted_iota(I32, x.shape, 1)

    def down(a, k):
        return jnp.where(t >= k, pltpu.roll(a, k, 0), 0.0)

    def up(a, k):
        return jnp.where(t < seqlen - k, pltpu.roll(a, seqlen - k, 0), 0.0)

    past = [x]
    futr = [x]
    for j in range(len(POOL_WINDOWS) - 1):
        past.append(past[j] + down(past[j], 2 ** j))
        futr.append(futr[j] + up(futr[j], 2 ** j))
    win_sum = jnp.zeros_like(x)
    cnt = jnp.zeros_like(x)
    for gi, win in enumerate(POOL_WINDOWS):
        half = win // 2
        in_group = (lane // HEAD_DIM) == gi
        win_sum = jnp.where(in_group, down(past[gi], 1) + futr[gi], win_sum)
        n = (jnp.minimum(t + half, seqlen) - jnp.maximum(t - half, 0)).astype(F32)
        cnt = jnp.where(in_group, n, cnt)
    d = win_sum / cnt - x
    return _dot(d.astype(BF16), w_blk) * scale


def _out_kernel(yfc_ref, yfd_ref, ybc_ref, ybd_ref, g_ref, bon_ref, attc_ref, attd_ref, poolc_ref, poold_ref,
                hc_ref, hd_ref, mod_ref, w_ref, lng_ref, lnb_ref, gng_ref, gnb_ref, ones_ref, wrh_ref, wrl_ref,
                br_ref, h1_o, u2_o, lg_o, *, tm, bounds, alpha, n_ctx_tiles):
    grp = _group_of_row(pl.program_id(0) * tm, bounds)
    mod = mod_ref[pl.ds(grp, 1), :]
    g1 = mod[:, 2 * D_MODEL:3 * D_MODEL]
    sh2 = mod[:, 3 * D_MODEL:4 * D_MODEL]
    sc2 = mod[:, 4 * D_MODEL:5 * D_MODEL]
    ones = ones_ref[...]
    is_ctx = pl.program_id(0) < n_ctx_tiles
    half = tm // 2
    halves = [pl.ds(s * half, half) for s in range(2)]
    pick = lambda c_ref, d_ref, rows: jnp.where(is_ctx, c_ref[rows, :], d_ref[rows, :])
    y = [pick(yfc_ref, yfd_ref, rows) + pick(ybc_ref, ybd_ref, rows) for rows in halves]
    mu = [_segsum(v, ones) * (1.0 / HEAD_DIM) for v in y]
    yc = [v - m_ for v, m_ in zip(y, mu)]
    var = [_segsum(v * v, ones) * (1.0 / HEAD_DIM) for v in yc]
    m = []
    for rows, v, vr in zip(halves, yc, var):
        yn = v * lax.rsqrt(vr + GN_EPS) * gng_ref[...] + gnb_ref[...]
        a_out = (yn + bon_ref[rows, :]) * g_ref[rows, :]
        m.append(_dot(a_out.astype(BF16), w_ref[0:A_WIDTH, :])
                 + _dot(pick(attc_ref, attd_ref, rows).astype(BF16), w_ref[A_WIDTH:A_WIDTH + B_WIDTH, :])
                 + _dot(pick(poolc_ref, poold_ref, rows).astype(BF16), w_ref[A_WIDTH + B_WIDTH:D_MODEL, :]))
    for rows, m_ in zip(halves, m):
        h1 = _layer_norm(alpha * pick(hc_ref, hd_ref, rows) + g1 * m_, lng_ref[...], lnb_ref[...])
        h1_o[rows, :] = h1
        u2 = h1 * (1.0 + sc2) + sh2
        u_hi = u2.astype(BF16)
        u_lo = (u2 - u_hi.astype(F32)).astype(BF16)
        u2_o[rows, :] = u_hi
        lg_o[rows, :] = (_dot(u_hi, wrh_ref[...]) + _dot(u_lo, wrh_ref[...]) + _dot(u_hi, wrl_ref[...])) + br_ref[...]


def _out_projection(yf, yb, g, bonus, att, pool, h, mods, w_out, layer, lw, ones, *, tm, bounds, alpha):
    t = h[0].shape[0] + h[1].shape[0]
    nct = h[0].shape[0] // tm
    full = lambda shape: pl.BlockSpec(shape, lambda i: (0,) * len(shape))
    rows = lambda w: pl.BlockSpec((tm, w), lambda i: (i, 0))
    rows_c, rows_d = _pair_specs(tm, nct)
    lay = _layer_spec(layer)
    return pl.pallas_call(
        functools.partial(_out_kernel, tm=tm, bounds=bounds, alpha=alpha, n_ctx_tiles=nct),
        grid=(t // tm,),
        in_specs=[rows_c(A_WIDTH), rows_d(A_WIDTH), rows_c(A_WIDTH), rows_d(A_WIDTH), rows(A_WIDTH), rows(A_WIDTH),
                  rows_c(B_WIDTH), rows_d(B_WIDTH), rows_c(C_WIDTH), rows_d(C_WIDTH), rows_c(D_MODEL),
                  rows_d(D_MODEL), lay(8, 6 * D_MODEL), lay(D_MODEL, D_MODEL), lay(1, D_MODEL),
                  lay(1, D_MODEL), lay(1, A_WIDTH), lay(1, A_WIDTH), full((2 * HEAD_DIM, 2 * HEAD_DIM)),
                  lay(D_MODEL, ROUTER_LANES), lay(D_MODEL, ROUTER_LANES), lay(1, ROUTER_LANES)],
        out_specs=[rows(D_MODEL), rows(D_MODEL), rows(ROUTER_LANES)],
        out_shape=[jax.ShapeDtypeStruct((t, D_MODEL), F32), jax.ShapeDtypeStruct((t, D_MODEL), BF16),
                   jax.ShapeDtypeStruct((t, ROUTER_LANES), F32)],
        compiler_params=_cparams(("arbitrary",)),
        name="out_projection",
    )(yf[0], yf[1], yb[0], yb[1], g, bonus, att[0], att[1], pool[0], pool[1], h[0], h[1], mods, w_out,
      lw['ln1_g'], lw['ln1_b'], lw['gn_g'], lw['gn_b'], ones, lw['w_router_hi'], lw['w_router_lo'], lw['b_router'])


def _router_gates(logits):
    lane = lax.broadcasted_iota(I32, logits.shape, 1)
    lane_f = lane.astype(F32)
    neg = -jnp.inf
    big = 1e9

    def first_lane(mask):
        return jnp.min(jnp.where(mask, lane_f, big), axis=-1, keepdims=True).astype(I32)

    is_grp = lane < N_GROUPS
    gmax = jnp.max(jnp.where(is_grp, logits, neg), axis=-1, keepdims=True)
    sel = first_lane(is_grp & (logits == gmax))
    gsum = jnp.sum(jnp.where(is_grp, jnp.exp(logits - gmax), 0.0), axis=-1, keepdims=True)
    gsel = 1.0 / gsum
    first_exp = N_GROUPS + sel * EXP_PER_GROUP
    in_sel = (lane >= first_exp) & (lane < first_exp + EXP_PER_GROUP)
    v1 = jnp.max(jnp.where(in_sel, logits, neg), axis=-1, keepdims=True)
    i1 = first_lane(in_sel & (logits == v1))
    rest = in_sel & (lane != i1)
    v2 = jnp.max(jnp.where(rest, logits, neg), axis=-1, keepdims=True)
    i2 = first_lane(rest & (logits == v2))
    e2 = jnp.exp(v2 - v1)
    t1 = (1.0 / (1.0 + e2)) * gsel
    t2 = (e2 / (1.0 + e2)) * gsel
    return jnp.where(lane == i1, t1, 0.0) + jnp.where(lane == i2, t2, 0.0), sel


def _moe_kernel(x_ref, lg_ref, w1_ref, w3_ref, w2_ref, h1_ref, mod_ref, lng_ref, lnb_ref, oc_ref, od_ref,
                xs_ref, gs_ref, acc_ref, *, tm, sub, bounds, alpha, n_ctx_tiles):
    gates, sel = _router_gates(lg_ref[...])
    lane = lax.broadcasted_iota(I32, gates.shape, 1)
    onehot = jnp.where(lane == sel, 1.0, 0.0)
    row_i = lax.broadcasted_iota(I32, (tm, tm), 0)
    col_i = lax.broadcasted_iota(I32, (tm, tm), 1)
    earlier = jnp.where(col_i < row_i, 1.0, 0.0).astype(BF16)
    rank = _dot(earlier, onehot.astype(BF16))
    lane1 = lax.broadcasted_iota(I32, (1, ROUTER_LANES), 1)
    counts = jnp.sum(onehot, axis=0, keepdims=True)
    cnt = [jnp.sum(jnp.where(lane1 == g, counts, 0.0)).astype(I32) for g in range(N_GROUPS)]
    off = [jnp.zeros((), I32)]
    for g in range(N_GROUPS - 1):
        off.append(off[g] + cnt[g])
    offs = jnp.zeros((1, ROUTER_LANES), F32)
    for g in range(N_GROUPS):
        offs = jnp.where(lane1 == g, off[g].astype(F32), offs)
    pos = jnp.sum(onehot * (rank + offs), axis=1, keepdims=True).astype(I32)
    pos_row = jnp.broadcast_to(pos, (tm, ROUTER_LANES)).T[0:1, :]
    perm = jnp.where(row_i == pos_row, 1.0, 0.0).astype(BF16)
    perm_t = jnp.where(col_i == pos, 1.0, 0.0).astype(BF16)
    xs_ref[...] = _dot(perm, x_ref[...]).astype(BF16)
    gs_ref[...] = _dot(perm, gates.astype(BF16))
    acc_ref[...] = jnp.zeros_like(acc_ref)

    lane_s = lax.broadcasted_iota(I32, (sub, ROUTER_LANES), 1)
    for s in range(tm // sub):
        rows = pl.ds(s * sub, sub)
        for g in range(N_GROUPS):
            @pl.when((off[g] < (s + 1) * sub) & (off[g] + cnt[g] > s * sub))
            def _():
                xb = xs_ref[rows, :]
                gsb = gs_ref[rows, :]
                hids = []
                for j in range(EXP_PER_GROUP):
                    e = g * EXP_PER_GROUP + j
                    ge = jnp.sum(jnp.where(lane_s == N_GROUPS + e, gsb, 0.0), axis=-1, keepdims=True)
                    h1 = _dot(xb, w1_ref[e])
                    h3 = _dot(xb, w3_ref[e])
                    hids.append(((h1 * _sigmoid(h1)) * h3 * ge).astype(BF16))
                hid = jnp.concatenate(hids, axis=1)
                w2g = w2_ref[g * EXP_PER_GROUP:(g + 1) * EXP_PER_GROUP].reshape(EXP_PER_GROUP * D_EXPERT, D_MODEL)
                acc_ref[rows, :] += _dot(hid, w2g)

    f = _dot(perm_t, acc_ref[...].astype(BF16))
    grp = _group_of_row(pl.program_id(0) * tm, bounds)
    g2 = mod_ref[pl.ds(grp, 1), 5 * D_MODEL:6 * D_MODEL]
    out = _layer_norm(alpha * h1_ref[...] + g2 * f, lng_ref[...], lnb_ref[...])
    is_ctx = pl.program_id(0) < n_ctx_tiles

    @pl.when(is_ctx)
    def _():
        oc_ref[...] = out

    @pl.when(jnp.logical_not(is_ctx))
    def _():
        od_ref[...] = out


def _moe(u2, logits, h1, mods, experts, layer, lw, *, tm, sub, bounds, alpha, n_ctx):
    t = h1.shape[0]
    nct = n_ctx // tm
    rows_c, rows_d = _pair_specs(tm, nct)
    lay = _layer_spec(layer)
    once = lambda shape: pl.BlockSpec(shape, lambda i: (layer,) + (0,) * (len(shape) - 1),
                                      pipeline_mode=pl.Buffered(1))
    rows = lambda w: pl.BlockSpec((tm, w), lambda i: (i, 0))
    return pl.pallas_call(
        functools.partial(_moe_kernel, tm=tm, sub=sub, bounds=bounds, alpha=alpha, n_ctx_tiles=nct),
        grid=(t // tm,),
        in_specs=[rows(D_MODEL), rows(ROUTER_LANES),
                  once((None, N_EXPERTS, D_MODEL, D_EXPERT)), once((None, N_EXPERTS, D_MODEL, D_EXPERT)),
                  once((None, N_EXPERTS, D_EXPERT, D_MODEL)),
                  rows(D_MODEL), lay(8, 6 * D_MODEL), lay(1, D_MODEL), lay(1, D_MODEL)],
        out_specs=[rows_c(D_MODEL), rows_d(D_MODEL)],
        out_shape=[jax.ShapeDtypeStruct((n_ctx, D_MODEL), F32), jax.ShapeDtypeStruct((t - n_ctx, D_MODEL), F32)],
        scratch_shapes=[pltpu.VMEM((tm, D_MODEL), BF16), pltpu.VMEM((tm, ROUTER_LANES), F32),
                        pltpu.VMEM((tm, D_MODEL), F32)],
        compiler_params=_cparams(("arbitrary",)),
        name="moe",
    )(u2, logits, experts[0], experts[1], experts[2], h1, mods, lw['ln2_g'], lw['ln2_b'])


def _rope_tables(dec_seq):
    n = HEAD_DIM // 4
    pos = jnp.arange(dec_seq)
    row = (pos // GRID_W).astype(F32)
    col = (pos % GRID_W).astype(F32)
    inv = 1.0 / (ROPE_THETA ** (jnp.arange(n, dtype=F32) / n))
    ra, ca = row[:, None] * inv, col[:, None] * inv
    cos64 = jnp.concatenate([jnp.cos(ra), jnp.cos(ra), jnp.cos(ca), jnp.cos(ca)], axis=-1)
    sin64 = jnp.concatenate([-jnp.sin(ra), jnp.sin(ra), -jnp.sin(ca), jnp.sin(ca)], axis=-1)
    return jnp.concatenate([cos64, cos64], axis=-1), jnp.concatenate([sin64, sin64], axis=-1)


def _block_diag2(w):
    z = jnp.zeros_like(w[:, 0])
    return jnp.concatenate([jnp.concatenate([w[:, 0], z], axis=2), jnp.concatenate([z, w[:, 1]], axis=2)], axis=1)


def _derived_weights(p):
    d = D_MODEL
    depth = p['w_in'].shape[0]
    row = lambda a, w: a.reshape(depth, 1, w)
    ng = len(POOL_WINDOWS)
    pool_blk = (p['pool_w'][:, :, :, None, :] * jnp.eye(ng, dtype=F32)[None, :, None, :, None]).reshape(
        depth, C_WIDTH, C_WIDTH)
    lane_pad = ROUTER_LANES - N_GROUPS - N_EXPERTS
    w_router = jnp.pad(jnp.concatenate([p['rt_grp_w'], p['rt_exp_w']], axis=-1), ((0, 0), (0, 0), (0, lane_pad)))
    b_router = jnp.pad(jnp.concatenate([p['rt_grp_b'], p['rt_exp_b']], axis=-1), ((0, 0), (0, lane_pad)))
    b_router = b_router.reshape(depth, 1, ROUTER_LANES)
    return {
        'w2blk': _block_diag2(p['rw_w2']).astype(BF16),
        'w0': row(p['rw_w0'], 2 * A_WIDTH),
        'a2blk': _block_diag2(p['rw_a2']).astype(BF16),
        'a0': row(p['rw_a0'], 2 * A_WIDTH),
        'g2': p['rw_g2'].astype(BF16),
        'kkw': row(p['rw_kk'], A_WIDTH),
        'ka': jnp.tile(row(p['rw_ka'], A_WIDTH), (1, 1, 2)),
        'rk': row(p['rw_rk'], A_WIDTH),
        'gn_g': row(p['rw_gn_g'], A_WIDTH),
        'gn_b': row(p['rw_gn_b'], A_WIDTH),
        'qn': jnp.tile(row(p['q_norm'], HEAD_DIM), (1, 1, B_Q_HEADS)),
        'knw': jnp.tile(row(p['k_norm'], HEAD_DIM), (1, 1, B_KV_HEADS)),
        'pool_blk': pool_blk.astype(BF16),
        'pool_scale': row(p['pool_scale'], C_WIDTH),
        'ln1_g': row(p['ln1_g'], d), 'ln1_b': row(p['ln1_b'], d),
        'ln2_g': row(p['ln2_g'], d), 'ln2_b': row(p['ln2_b'], d),
        'w_router_hi': w_router.astype(BF16),
        'w_router_lo': (w_router - w_router.astype(BF16).astype(F32)).astype(BF16),
        'b_router': b_router,
    }


def _forward(x_prompt, x_sample, state_rwkv, cache_k, cache_v, c, c_ctx, p):
    batch, seq, d = x_prompt.shape
    dec_batch, dec_seq, _ = x_sample.shape
    depth = p['w_in'].shape[0]
    past = cache_k.shape[2]
    n_ctx = batch * seq
    n_tok = n_ctx + dec_batch * dec_seq
    alpha = (2 * depth) ** 0.25
    bounds = tuple(n_ctx + j * dec_seq for j in range(dec_batch))
    tm = math.gcd(n_ctx, dec_seq, 512)
    tm_moe = math.gcd(n_ctx, dec_seq, 512)
    tq_ctx = min(256, seq)
    tq_dec = min(256, dec_seq)
    sb_ctx = 2

    cond8 = jnp.zeros((8, d), F32).at[0].set(c_ctx).at[1:1 + dec_batch].set(c)
    mods = _modulation(cond8, p['w_mod'], p['b_mod'])
    cos, sin = _rope_tables(dec_seq)
    w_in16, w_out16 = p['w_in'].astype(BF16), p['w_out'].astype(BF16)
    experts16 = (p['moe_w1'].astype(BF16), p['moe_w3'].astype(BF16), p['moe_w2'].astype(BF16))
    lw = _derived_weights(p)
    head_id = jnp.arange(2 * HEAD_DIM) // HEAD_DIM
    ones = (head_id[:, None] == head_id[None, :]).astype(BF16)

    h = (x_prompt.reshape(n_ctx, d), x_sample.reshape(dec_batch * dec_seq, d))
    ck4 = cache_k.reshape(dec_batch, depth, past, KV_WIDTH)
    cv4 = cache_v.reshape(dec_batch, depth, past, KV_WIDTH)
    new_s, new_k, new_v = [], [], []
    for l in range(depth):
        (r, v, kk, logw, kd, bvec, g, bonus, q, k_att, v_att, pp) = _in_projection(
            h, mods, w_in16, l, lw, cos, sin, ones, tm=tm, bounds=bounds)
        scan_ops = (r, v, kk, logw, kd, bvec)
        yf_c, yb_c, s_fin = _rwkv_scan(scan_ops, None, row0=0, nseq=batch, seqlen=seq, sb=sb_ctx,
                                       nb=seq // CHUNK, want_state=True)
        yf_d, yb_d = _rwkv_scan(scan_ops, state_rwkv[:, l], row0=n_ctx, nseq=dec_batch, seqlen=dec_seq,
                                sb=dec_batch, nb=4, want_state=False)
        yf = (yf_c.reshape(n_ctx, A_WIDTH), yf_d.reshape(n_tok - n_ctx, A_WIDTH))
        yb = (yb_c.reshape(n_ctx, A_WIDTH), yb_d.reshape(n_tok - n_ctx, A_WIDTH))
        att_c, pool_c = _attention(q, k_att, v_att, None, pp, lw['pool_blk'], lw['pool_scale'], l,
                                   row0=0, nseq=batch, seqlen=seq, tq=tq_ctx)
        att_d, pool_d = _attention(q, k_att, v_att, (ck4, cv4), pp, lw['pool_blk'], lw['pool_scale'], l,
                                   row0=n_ctx, nseq=dec_batch, seqlen=dec_seq, tq=tq_dec)
        att, pool = (att_c, att_d), (pool_c, pool_d)
        h1, u2, logits = _out_projection(yf, yb, g, bonus, att, pool, h, mods, w_out16, l, lw, ones,
                                         tm=tm, bounds=bounds, alpha=alpha)
        h = _moe(u2, logits, h1, mods, experts16, l, lw, tm=tm_moe, sub=128, bounds=bounds, alpha=alpha,
                 n_ctx=n_ctx)
        new_s.append(s_fin)
        new_k.append(k_att[:n_ctx].reshape(batch, seq, B_KV_HEADS, HEAD_DIM))
        new_v.append(v_att[:n_ctx].reshape(batch, seq, B_KV_HEADS, HEAD_DIM))
    return (h[0].reshape(batch, seq, d), h[1].reshape(dec_batch, dec_seq, d),
            jnp.stack(new_s, axis=1), jnp.stack(new_k, axis=1), jnp.stack(new_v, axis=1))


def kernel(x_prompt, x_sample, state_rwkv, cache_k, cache_v, c, c_ctx, w_in, w_out, w_mod, b_mod, ln1_g, ln1_b, ln2_g, ln2_b, rw_w0, rw_w2, rw_a0, rw_a2, rw_g2, rw_kk, rw_ka, rw_rk, rw_gn_g, rw_gn_b, q_norm, k_norm, pool_w, pool_scale, rt_grp_w, rt_grp_b, rt_exp_w, rt_exp_b, moe_w1, moe_w3, moe_w2):
    p = dict(w_in=w_in, w_out=w_out, w_mod=w_mod, b_mod=b_mod, ln1_g=ln1_g, ln1_b=ln1_b, ln2_g=ln2_g,
             ln2_b=ln2_b, rw_w0=rw_w0, rw_w2=rw_w2, rw_a0=rw_a0, rw_a2=rw_a2, rw_g2=rw_g2, rw_kk=rw_kk,
             rw_ka=rw_ka, rw_rk=rw_rk, rw_gn_g=rw_gn_g, rw_gn_b=rw_gn_b, q_norm=q_norm, k_norm=k_norm,
             pool_w=pool_w, pool_scale=pool_scale, rt_grp_w=rt_grp_w, rt_grp_b=rt_grp_b, rt_exp_w=rt_exp_w,
             rt_exp_b=rt_exp_b, moe_w1=moe_w1, moe_w3=moe_w3, moe_w2=moe_w2)
    return _forward(x_prompt, x_sample, state_rwkv, cache_k, cache_v, c, c_ctx, p)
```

```python
import functools
import math

import jax
import jax.numpy as jnp
from jax import lax
from jax.experimental import pallas as pl
from jax.experimental.pallas import tpu as pltpu

F32 = jnp.float32
BF16 = jnp.bfloat16
I32 = jnp.int32

D_MODEL = 1024
GRID_W = 64
HEAD_DIM = 64
A_HEADS = 6
A_WIDTH = A_HEADS * HEAD_DIM
LORA_W = 64
LORA_A = 64
LORA_G = 128
N_DIR = 2
DECAY_SCALE = 0.606531
B_Q_HEADS = 6
B_KV_HEADS = 2
B_WIDTH = B_Q_HEADS * HEAD_DIM
KV_WIDTH = B_KV_HEADS * HEAD_DIM
ROPE_THETA = 10000.0
POOL_WINDOWS = (2, 4, 8, 16)
C_WIDTH = 256
IN_COLS = 2432
N_GROUPS = 4
EXP_PER_GROUP = 4
N_EXPERTS = 16
D_EXPERT = 256
LN_EPS = 1e-5
GN_EPS = 64e-5
QK_EPS = 1e-6

_O_R, _O_K, _O_V = 0, 384, 768
_O_LW, _O_LA, _O_LG = 1152, 1280, 1408
_O_Q, _O_KB, _O_VB, _O_P = 1536, 1920, 2048, 2176

CHUNK = 64
ROUTER_LANES = 128
VMEM_LIMIT = 56 * 1024 * 1024


def _cparams(sem):
    return pltpu.CompilerParams(dimension_semantics=sem, vmem_limit_bytes=VMEM_LIMIT)


def _sigmoid(x):
    return 1.0 / (1.0 + jnp.exp(-x))


def _dot(a, b):
    return jnp.dot(a, b, preferred_element_type=F32)


def _dot_nt(a, b):
    return lax.dot_general(a, b, (((1,), (1,)), ((), ())), preferred_element_type=F32)


def _segsum(x, ones2):
    x16 = x.astype(BF16)
    slabs = [_dot(x16[:, j:j + 128], ones2) for j in range(0, x.shape[1], 128)]
    return slabs[0] if len(slabs) == 1 else jnp.concatenate(slabs, axis=1)


def _rope128(x, cos, sin):
    lane = lax.broadcasted_iota(I32, x.shape, 1)
    first = (lane % 32) < 16
    up = pltpu.roll(x, 128 - 16, 1)
    dn = pltpu.roll(x, 16, 1)
    return x * cos + jnp.where(first, up, dn) * sin


def _layer_norm(z, g, b):
    mu = jnp.mean(z, axis=-1, keepdims=True)
    zc = z - mu
    var = jnp.mean(zc * zc, axis=-1, keepdims=True)
    return zc * lax.rsqrt(var + LN_EPS) * g + b


def _mod_kernel(c_ref, w_ref, b_ref, o_ref):
    x = c_ref[...]
    x = x * _sigmoid(x)
    w = w_ref[...]
    x_hi = x.astype(BF16)
    x_lo = (x - x_hi.astype(F32)).astype(BF16)
    w_hi = w.astype(BF16)
    w_lo = (w - w_hi.astype(F32)).astype(BF16)
    o_ref[...] = (_dot(x_hi, w_hi) + _dot(x_lo, w_hi) + _dot(x_hi, w_lo)) + b_ref[...]


def _modulation(cond8, w_mod, b_mod):
    depth = w_mod.shape[0]
    tn = 1536
    return pl.pallas_call(
        _mod_kernel,
        grid=(depth, 6 * D_MODEL // tn),
        in_specs=[
            pl.BlockSpec((8, D_MODEL), lambda l, j: (0, 0)),
            pl.BlockSpec((None, D_MODEL, tn), lambda l, j: (l, 0, j)),
            pl.BlockSpec((None, 1, tn), lambda l, j: (l, 0, j)),
        ],
        out_specs=pl.BlockSpec((None, 8, tn), lambda l, j: (l, 0, j)),
        out_shape=jax.ShapeDtypeStruct((depth, 8, 6 * D_MODEL), F32),
        compiler_params=_cparams(("arbitrary", "arbitrary")),
        name="modulation",
    )(cond8, w_mod, b_mod.reshape(depth, 1, 6 * D_MODEL))


def _group_of_row(row0, bounds):
    g = jnp.zeros((), I32)
    for b in bounds:
        g = g + jnp.where(row0 >= b, 1, 0).astype(I32)
    return g


def _in_kernel(hc_ref, hd_ref, mod_ref, w_ref, w2_ref, w0_ref, a2_ref, a0_ref, g2_ref, kkw_ref, ka_ref,
               rk_ref, qn_ref, knw_ref, cos_ref, sin_ref, ones_ref,
               r_o, v_o, kk_o, lw_o, kd_o, b_o, g_o, bon_o, q_o, k_o, vb_o, p_o, *, tm, bounds, n_ctx_tiles):
    grp = _group_of_row(pl.program_id(0) * tm, bounds)
    mod = mod_ref[pl.ds(grp, 1), :]
    sh1 = mod[:, 0:D_MODEL]
    sc1 = mod[:, D_MODEL:2 * D_MODEL]
    is_ctx = pl.program_id(0) < n_ctx_tiles
    h = jnp.where(is_ctx, hc_ref[...], hd_ref[...])
    u = h * (1.0 + sc1) + sh1
    ones = ones_ref[...]
    scale = 1.0 / math.sqrt(HEAD_DIM)
    proj = _dot(u.astype(BF16), w_ref[...])

    r = proj[:, _O_R:_O_R + A_WIDTH]
    k = proj[:, _O_K:_O_K + A_WIDTH]
    v = proj[:, _O_V:_O_V + A_WIDTH]
    lo_w = proj[:, _O_LW:_O_LW + 128]
    lo_a = proj[:, _O_LA:_O_LA + 128]
    lo_g = proj[:, _O_LG:_O_LG + 128]

    w_pre = w0_ref[...] + _dot(jnp.tanh(lo_w).astype(BF16), w2_ref[...])
    logw = -DECAY_SCALE * _sigmoid(w_pre)
    a = _sigmoid(a0_ref[...] + _dot(lo_a.astype(BF16), a2_ref[...]))
    g = _dot(_sigmoid(lo_g).astype(BF16), g2_ref[...])

    kk = k * kkw_ref[...]
    kk = kk * lax.rsqrt(jnp.maximum(_segsum(kk * kk, ones), 1e-24))
    k2 = jnp.concatenate([k, k], axis=1)
    kd = k2 * (1.0 + (a - 1.0) * ka_ref[...])
    bvec = jnp.concatenate([kk, kk], axis=1) * a
    kd_sum = kd[:, 0:A_WIDTH] + kd[:, A_WIDTH:2 * A_WIDTH]
    bonus = _segsum(r * kd_sum * rk_ref[...], ones) * v

    r_o[...] = r
    v_o[...] = v.astype(BF16)
    kk_o[...] = kk
    lw_o[...] = logw
    kd_o[...] = kd
    b_o[...] = bvec
    g_o[...] = g
    bon_o[...] = bonus

    cos = cos_ref[...]
    sin = sin_ref[...]
    rope = lambda x: jnp.where(is_ctx, x, _rope128(x, cos, sin))
    q = proj[:, _O_Q:_O_Q + B_WIDTH]
    qn = q * lax.rsqrt(_segsum(q * q, ones) * (1.0 / HEAD_DIM) + QK_EPS) * qn_ref[...]
    q_o[...] = (jnp.concatenate([rope(qn[:, j:j + 128]) for j in range(0, B_WIDTH, 128)], axis=1)
                * scale).astype(BF16)
    kb = proj[:, _O_KB:_O_KB + KV_WIDTH]
    kn = kb * lax.rsqrt(_segsum(kb * kb, ones) * (1.0 / HEAD_DIM) + QK_EPS) * knw_ref[...]
    k_o[...] = rope(kn)
    vb_o[...] = proj[:, _O_VB:_O_VB + KV_WIDTH]
    p_o[...] = proj[:, _O_P:_O_P + C_WIDTH]


def _pair_specs(tm, nct):
    rows_c = lambda w: pl.BlockSpec((tm, w), lambda i: (jnp.minimum(i, nct - 1), 0))
    rows_d = lambda w: pl.BlockSpec((tm, w), lambda i: (jnp.maximum(i - nct, 0), 0))
    return rows_c, rows_d


def _layer_spec(layer):
    return lambda *shape: pl.BlockSpec((None,) + shape, lambda *_: (layer,) + (0,) * len(shape))


def _in_projection(h, mods, w_in, layer, lw, cos, sin, ones, *, tm, bounds):
    nct = h[0].shape[0] // tm
    t = h[0].shape[0] + h[1].shape[0]
    full = lambda shape: pl.BlockSpec(shape, lambda i: (0,) * len(shape))
    rows = lambda w: pl.BlockSpec((tm, w), lambda i: (i, 0))
    rows_c, rows_d = _pair_specs(tm, nct)
    lay = _layer_spec(layer)
    tiles_per_req = cos.shape[0] // tm
    table = pl.BlockSpec((tm, 128), lambda i: (jnp.maximum(i - nct, 0) % tiles_per_req, 0))
    widths = (A_WIDTH, A_WIDTH, A_WIDTH, 2 * A_WIDTH, 2 * A_WIDTH, 2 * A_WIDTH, A_WIDTH, A_WIDTH,
              B_WIDTH, KV_WIDTH, KV_WIDTH, C_WIDTH)
    dtypes = (F32, BF16, F32, F32, F32, F32, F32, F32, BF16, F32, F32, F32)
    return pl.pallas_call(
        functools.partial(_in_kernel, tm=tm, bounds=bounds, n_ctx_tiles=nct),
        grid=(t // tm,),
        in_specs=[
            rows_c(D_MODEL), rows_d(D_MODEL), lay(8, 6 * D_MODEL), lay(D_MODEL, IN_COLS),
            lay(128, 2 * A_WIDTH), lay(1, 2 * A_WIDTH), lay(128, 2 * A_WIDTH), lay(1, 2 * A_WIDTH),
            lay(LORA_G, A_WIDTH), lay(1, A_WIDTH), lay(1, 2 * A_WIDTH), lay(1, A_WIDTH),
            lay(1, B_WIDTH), lay(1, KV_WIDTH), table, table, full((2 * HEAD_DIM, 2 * HEAD_DIM)),
        ],
        out_specs=[rows(w) for w in widths],
        out_shape=[jax.ShapeDtypeStruct((t, w), dt) for w, dt in zip(widths, dtypes)],
        compiler_params=_cparams(("arbitrary",)),
        name="in_projection",
    )(h[0], h[1], mods, w_in, lw['w2blk'], lw['w0'], lw['a2blk'], lw['a0'], lw['g2'], lw['kkw'], lw['ka'],
      lw['rk'], lw['qn'], lw['knw'], cos, sin, ones)


def _split2(x):
    hi = x.astype(BF16)
    lo = (x - hi.astype(F32)).astype(BF16)
    return hi, lo


def _scan_kernel(*refs, sb, nb, zero_init, want_state):
    fwd = refs[0:6]
    bwd = refs[6:12]
    pos = 12
    if not zero_init:
        s0_ref = refs[pos]
        pos += 1
    y_outs = (refs[pos], refs[pos + 1])
    pos += 2
    if want_state:
        sfin_o = refs[pos]
        pos += 1
    h_ref = refs[pos]

    step = pl.program_id(1)
    nstep = pl.num_programs(1)
    npair = A_HEADS // 2
    pw = 2 * HEAD_DIM
    pairs = [(q, d, hp) for q in range(sb) for d in range(N_DIR) for hp in range(npair)]

    @pl.when(step == 0)
    def _():
        h_ref[...] = jnp.zeros_like(h_ref)
        if not zero_init:
            for (q, d, hp) in pairs:
                for e in range(2):
                    lo = e * HEAD_DIM
                    h_ref[q, d, hp, lo:lo + HEAD_DIM, lo:lo + HEAD_DIM] = s0_ref[q, d, 2 * hp + e].T

    def iota(shape, dim):
        return lax.broadcasted_iota(I32, shape, dim)

    first64 = iota((CHUNK, pw), 1) < HEAD_DIM
    eye_p = ((iota((CHUNK, pw), 1) % HEAD_DIM) == iota((CHUNK, pw), 0)).astype(F32)
    eye2 = (iota((pw, pw), 0) == iota((pw, pw), 1)).astype(F32)
    same_head = (iota((pw, pw), 0) < HEAD_DIM) == (iota((pw, pw), 1) < HEAD_DIM)
    ti = iota((CHUNK, CHUNK), 0)
    si = iota((CHUNK, CHUNK), 1)
    incl16 = ((si <= ti).astype(BF16), (si >= ti).astype(BF16))
    t4 = iota((2 * CHUNK, 2 * pw), 0)
    s4 = iota((2 * CHUNK, 2 * pw), 1) % CHUNK
    incl_off = jnp.where(t4 < CHUNK, 0, 1)
    t4 = t4 % CHUNK
    mask4 = (s4 < t4 + incl_off, s4 > t4 - incl_off)

    def bdiag(x):
        return jnp.concatenate([jnp.where(first64, x, 0.0), jnp.where(first64, 0.0, x)], axis=0)

    lhs, rhs1, bkt, gcol, v_p, amat = {}, {}, {}, {}, {}, {}
    blocks = [(q, d, j) for q in range(sb) for d in range(N_DIR) for j in range(nb)]
    csum = {}
    for (q, d, j) in blocks:
        hi, lo = _split2((fwd, bwd)[d][3][q, pl.ds(j * CHUNK, CHUNK), :])
        csum[(q, d, j)] = _dot(incl16[d], hi) + _dot(incl16[d], lo)
    for (q, d, j) in blocks:
        r_ref, v_ref, kk_ref, lw_ref, kd_ref, b_ref = (fwd, bwd)[d]
        rows = pl.ds(j * CHUNK, CHUNK)
        lw = lw_ref[q, rows, :]
        cs = csum[(q, d, j)]
        gam = jnp.exp(cs)
        gam_prev = jnp.exp(cs - lw)
        gam_inv = jnp.exp(-cs)
        gam_end = jnp.exp(cs[CHUNK - 1:CHUNK, :] if d == 0 else cs[0:1, :])
        a_hat = -kk_ref[q, rows, :] * gam_prev
        r_hat = r_ref[q, rows, :] * gam
        b_hat = b_ref[q, rows, :] * gam_inv
        k_hat = kd_ref[q, rows, :] * gam_inv
        v_all = v_ref[q, rows, :]
        for hp in range(npair):
            key = (q, d, hp, j)
            sl = slice(hp * pw, (hp + 1) * pw)
            lhs[key] = jnp.concatenate([a_hat[:, sl], r_hat[:, sl]], axis=0).astype(BF16)
            bh, kh, ge = b_hat[:, sl], k_hat[:, sl], gam_end[:, sl]
            rhs1[key] = jnp.concatenate([bdiag(bh), bdiag(kh)], axis=0).astype(BF16)
            gcol[key] = jnp.sum(eye2 * ge, axis=1, keepdims=True)
            bkt[key] = (jnp.concatenate([bh, kh], axis=0) * ge).T.astype(BF16)
            v_p[key] = v_all[:, sl]
    keys = list(lhs.keys())
    for key in keys:
        amat[key] = _dot_nt(lhs[key], rhs1[key])
    npow, tcur, a_rb, lhs2, vbd = {}, {}, {}, {}, {}
    for key in keys:
        am = jnp.where(mask4[key[1]], amat[key], 0.0)
        npow[key] = am[0:CHUNK, 0:pw]
        tcur[key] = eye_p + npow[key]
        a_rb[key] = am[CHUNK:2 * CHUNK, 0:pw].astype(BF16)
        vbd[key] = bdiag(v_p[key])
        lhs2[key] = jnp.concatenate([lhs[key], am[:, pw:2 * pw].astype(BF16)], axis=1)
    nsq = int(math.log2(CHUNK)) - 1
    for key in keys:
        npow[key] = _dot(npow[key].astype(BF16), bdiag(npow[key]).astype(BF16))
    for _ in range(nsq - 1):
        for key in keys:
            rhs = jnp.concatenate([bdiag(npow[key]), bdiag(tcur[key])], axis=1).astype(BF16)
            res = _dot(npow[key].astype(BF16), rhs)
            npow[key] = res[:, 0:pw]
            tcur[key] = tcur[key] + res[:, pw:2 * pw]
    t_inv = {}
    for key in keys:
        t_inv[key] = (tcur[key] + _dot(npow[key].astype(BF16), bdiag(tcur[key]).astype(BF16))).astype(BF16)

    hcur = {pr: h_ref[pr[0], pr[1], pr[2]] for pr in pairs}
    for jj in range(nb):
        z, u = {}, {}
        kof = lambda pr: (pr[0], pr[1], pr[2], jj if pr[1] == 0 else nb - 1 - jj)
        for pr in pairs:
            key = kof(pr)
            hv = jnp.concatenate([hcur[pr].astype(BF16), vbd[key]], axis=0)
            z[pr] = _dot(lhs2[key], hv)
        for pr in pairs:
            u[pr] = _dot(t_inv[kof(pr)], bdiag(z[pr][0:CHUNK]).astype(BF16))
        for pr in pairs:
            key = kof(pr)
            q, d, hp, j = key
            y = z[pr][CHUNK:2 * CHUNK] + _dot(a_rb[key], bdiag(u[pr]).astype(BF16))
            y_outs[d][q, pl.ds(j * CHUNK, CHUNK), hp * pw:(hp + 1) * pw] = y
            uv = jnp.concatenate([u[pr].astype(BF16), v_p[key]], axis=0)
            hcur[pr] = hcur[pr] * gcol[key] + jnp.where(same_head, _dot(bkt[key], uv), 0.0)
    for pr in pairs:
        h_ref[pr[0], pr[1], pr[2]] = hcur[pr]

    if want_state:
        @pl.when(step == nstep - 1)
        def _():
            for (q, d, hp) in pairs:
                for e in range(2):
                    lo = e * HEAD_DIM
                    sfin_o[q, d, 2 * hp + e] = hcur[(q, d, hp)][lo:lo + HEAD_DIM, lo:lo + HEAD_DIM].T


def _rwkv_scan(ops, s0, *, row0, nseq, seqlen, sb, nb, want_state):
    g = nb * CHUNK
    nstep = seqlen // g
    zero_init = s0 is None
    ops3 = [a.reshape(a.shape[0] // seqlen, seqlen, a.shape[1]) for a in ops]
    base = row0 // (seqlen * sb)
    fidx = lambda col: (lambda s, c: (base + s, c, col))
    bidx = lambda col: (lambda s, c: (base + s, nstep - 1 - c, col))
    blk = lambda im: pl.BlockSpec((sb, g, A_WIDTH), im)
    in_specs = [blk(fidx(0))] * 6 + [blk(bidx(0))] * 3 + [blk(bidx(1))] * 3
    args = ops3 + ops3
    state_spec = pl.BlockSpec((sb, N_DIR, A_HEADS, HEAD_DIM, HEAD_DIM), lambda s, c: (s, 0, 0, 0, 0))
    if not zero_init:
        in_specs.append(state_spec)
        args.append(s0)
    out_specs = [blk(lambda s, c: (s, c, 0)), blk(lambda s, c: (s, nstep - 1 - c, 0))]
    out_shape = [jax.ShapeDtypeStruct((nseq, seqlen, A_WIDTH), F32)] * 2
    if want_state:
        out_specs.append(state_spec)
        out_shape.append(jax.ShapeDtypeStruct((nseq, N_DIR, A_HEADS, HEAD_DIM, HEAD_DIM), F32))
    return pl.pallas_call(
        functools.partial(_scan_kernel, sb=sb, nb=nb, zero_init=zero_init, want_state=want_state),
        grid=(nseq // sb, nstep),
        in_specs=in_specs,
        out_specs=out_specs,
        out_shape=out_shape,
        scratch_shapes=[pltpu.VMEM((sb, N_DIR, A_HEADS // 2, 2 * HEAD_DIM, 2 * HEAD_DIM), F32)],
        compiler_params=_cparams(("arbitrary", "arbitrary")),
        name="rwkv_scan_ctx" if zero_init else "rwkv_scan_dec",
    )(*args)


def _attn_kernel(*refs, with_cache):
    grp = B_Q_HEADS // B_KV_HEADS
    if with_cache:
        q_ref, k_ref, v_ref, ck_ref, cv_ref, p_ref, pw_ref, psc_ref, o_ref, pool_o = refs
        kall = jnp.concatenate([k_ref[...], ck_ref[...]], axis=0)
        vall = jnp.concatenate([v_ref[...], cv_ref[...]], axis=0)
    else:
        q_ref, k_ref, v_ref, p_ref, pw_ref, psc_ref, o_ref, pool_o = refs
        kall = k_ref[...]
        vall = v_ref[...]

    @pl.when(pl.program_id(1) == 0)
    def _():
        pool_o[...] = _pool_mix(p_ref[...], pw_ref[...], psc_ref[...]).astype(BF16)

    q = q_ref[...]
    k16 =[kall[:, hk * HEAD_DIM:(hk + 1) * HEAD_DIM].astype(BF16) for hk in range(B_KV_HEADS)]
    v16 = [vall[:, hk * HEAD_DIM:(hk + 1) * HEAD_DIM].astype(BF16) for hk in range(B_KV_HEADS)]

    def scores(j):
        return _dot_nt(q[:, j * HEAD_DIM:(j + 1) * HEAD_DIM], k16[j // grp])

    outs = []
    s_next = scores(0)
    for j in range(B_Q_HEADS):
        s = s_next
        if j + 1 < B_Q_HEADS:
            s_next = scores(j + 1)
        m = jnp.max(s, axis=-1, keepdims=True)
        p = jnp.exp(s - m)
        den = jnp.sum(p, axis=-1, keepdims=True)
        outs.append(_dot(p.astype(BF16), v16[j // grp]) / den)
    o_ref[...] = jnp.concatenate(outs, axis=1).astype(BF16)


def _attention(q, k_tok, v_tok, cache, p_tok, pool_w, pool_scale, layer, *, row0, nseq, seqlen, tq):
    nq = seqlen // tq
    base = row0 // tq
    kbase = row0 // seqlen
    kspec = pl.BlockSpec((seqlen, KV_WIDTH), lambda s, i: (kbase + s, 0))
    in_specs = [pl.BlockSpec((tq, B_WIDTH), lambda s, i: (base + s * nq + i, 0)), kspec, kspec]
    args = [q, k_tok, v_tok]
    if cache is not None:
        ck, cv = cache
        cspec = pl.BlockSpec((None, None, ck.shape[2], KV_WIDTH), lambda s, i: (s, layer, 0, 0))
        in_specs += [cspec, cspec]
        args += [ck, cv]
    lay = _layer_spec(layer)
    in_specs += [pl.BlockSpec((seqlen, C_WIDTH), lambda s, i: (kbase + s, 0)), lay(C_WIDTH, C_WIDTH), lay(1, C_WIDTH)]
    args += [p_tok, pool_w, pool_scale]
    return pl.pallas_call(
        functools.partial(_attn_kernel, with_cache=cache is not None),
        grid=(nseq, nq),
        in_specs=in_specs,
        out_specs=[pl.BlockSpec((tq, B_WIDTH), lambda s, i: (s * nq + i, 0)),
                   pl.BlockSpec((seqlen, C_WIDTH), lambda s, i: (s, 0))],
        out_shape=[jax.ShapeDtypeStruct((nseq * seqlen, B_WIDTH), BF16),
                   jax.ShapeDtypeStruct((nseq * seqlen, C_WIDTH), BF16)],
        compiler_params=_cparams(("arbitrary", "arbitrary")),
        name="attention_ctx" if cache is None else "attention_dec",
    )(*args)


def _pool_mix(x, w_blk, scale):
    seqlen = x.shape[0]
    t = lax.broadcasted_iota(I32, x.shape, 0)
    lane = lax.broadcasted_iota(I32, x.shape, 1)

    def down(a, k):
        return jnp.where(t >= k, pltpu.roll(a, k, 0), 0.0)

    def up(a, k):
        return jnp.where(t < seqlen - k, pltpu.roll(a, seqlen - k, 0), 0.0)

    past = [x]
    futr = [x]
    for j in range(len(POOL_WINDOWS) - 1):
        past.append(past[j] + down(past[j], 2 ** j))
        futr.append(futr[j] + up(futr[j], 2 ** j))
    win_sum = jnp.zeros_like(x)
    cnt = jnp.zeros_like(x)
    for gi, win in enumerate(POOL_WINDOWS):
        half = win // 2
        in_group = (lane // HEAD_DIM) == gi
        win_sum = jnp.where(in_group, down(past[gi], 1) + futr[gi], win_sum)
        n = (jnp.minimum(t + half, seqlen) - jnp.maximum(t - half, 0)).astype(F32)
        cnt = jnp.where(in_group, n, cnt)
    d = win_sum / cnt - x
    return _dot(d.astype(BF16), w_blk) * scale


def _out_kernel(yfc_ref, yfd_ref, ybc_ref, ybd_ref, g_ref, bon_ref, attc_ref, attd_ref, poolc_ref, poold_ref,
                hc_ref, hd_ref, mod_ref, w_ref, lng_ref, lnb_ref, gng_ref, gnb_ref, ones_ref, wrh_ref, wrl_ref,
                br_ref, h1_o, u2_o, lg_o, *, tm, bounds, alpha, n_ctx_tiles):
    grp = _group_of_row(pl.program_id(0) * tm, bounds)
    mod = mod_ref[pl.ds(grp, 1), :]
    g1 = mod[:, 2 * D_MODEL:3 * D_MODEL]
    sh2 = mod[:, 3 * D_MODEL:4 * D_MODEL]
    sc2 = mod[:, 4 * D_MODEL:5 * D_MODEL]
    ones = ones_ref[...]
    is_ctx = pl.program_id(0) < n_ctx_tiles
    half = tm // 2
    halves = [pl.ds(s * half, half) for s in range(2)]
    pick = lambda c_ref, d_ref, rows: jnp.where(is_ctx, c_ref[rows, :], d_ref[rows, :])
    y = [pick(yfc_ref, yfd_ref, rows) + pick(ybc_ref, ybd_ref, rows) for rows in halves]
    mu = [_segsum(v, ones) * (1.0 / HEAD_DIM) for v in y]
    yc = [v - m_ for v, m_ in zip(y, mu)]
    var = [_segsum(v * v, ones) * (1.0 / HEAD_DIM) for v in yc]
    m = []
    for rows, v, vr in zip(halves, yc, var):
        yn = v * lax.rsqrt(vr + GN_EPS) * gng_ref[...] + gnb_ref[...]
        a_out = (yn + bon_ref[rows, :]) * g_ref[rows, :]
        m.append(_dot(a_out.astype(BF16), w_ref[0:A_WIDTH, :])
                 + _dot(pick(attc_ref, attd_ref, rows), w_ref[A_WIDTH:A_WIDTH + B_WIDTH, :])
                 + _dot(pick(poolc_ref, poold_ref, rows), w_ref[A_WIDTH + B_WIDTH:D_MODEL, :]))
    for rows, m_ in zip(halves, m):
        h1 = _layer_norm(alpha * pick(hc_ref, hd_ref, rows) + g1 * m_, lng_ref[...], lnb_ref[...])
        h1_o[rows, :] = h1
        u2 = h1 * (1.0 + sc2) + sh2
        u_hi = u2.astype(BF16)
        u_lo = (u2 - u_hi.astype(F32)).astype(BF16)
        u2_o[rows, :] = u_hi
        lg_o[rows, :] = (_dot(u_hi, wrh_ref[...]) + _dot(u_lo, wrh_ref[...]) + _dot(u_hi, wrl_ref[...])) + br_ref[...]


def _out_projection(yf, yb, g, bonus, att, pool, h, mods, w_out, layer, lw, ones, *, tm, bounds, alpha):
    t = h[0].shape[0] + h[1].shape[0]
    nct = h[0].shape[0] // tm
    full = lambda shape: pl.BlockSpec(shape, lambda i: (0,) * len(shape))
    rows = lambda w: pl.BlockSpec((tm, w), lambda i: (i, 0))
    rows_c, rows_d = _pair_specs(tm, nct)
    lay = _layer_spec(layer)
    return pl.pallas_call(
        functools.partial(_out_kernel, tm=tm, bounds=bounds, alpha=alpha, n_ctx_tiles=nct),
        grid=(t // tm,),
        in_specs=[rows_c(A_WIDTH), rows_d(A_WIDTH), rows_c(A_WIDTH), rows_d(A_WIDTH), rows(A_WIDTH), rows(A_WIDTH),
                  rows_c(B_WIDTH), rows_d(B_WIDTH), rows_c(C_WIDTH), rows_d(C_WIDTH), rows_c(D_MODEL),
                  rows_d(D_MODEL), lay(8, 6 * D_MODEL), lay(D_MODEL, D_MODEL), lay(1, D_MODEL),
                  lay(1, D_MODEL), lay(1, A_WIDTH), lay(1, A_WIDTH), full((2 * HEAD_DIM, 2 * HEAD_DIM)),
                  lay(D_MODEL, ROUTER_LANES), lay(D_MODEL, ROUTER_LANES), lay(1, ROUTER_LANES)],
        out_specs=[rows(D_MODEL), rows(D_MODEL), rows(ROUTER_LANES)],
        out_shape=[jax.ShapeDtypeStruct((t, D_MODEL), F32), jax.ShapeDtypeStruct((t, D_MODEL), BF16),
                   jax.ShapeDtypeStruct((t, ROUTER_LANES), F32)],
        compiler_params=_cparams(("arbitrary",)),
        name="out_projection",
    )(yf[0], yf[1], yb[0], yb[1], g, bonus, att[0], att[1], pool[0], pool[1], h[0], h[1], mods, w_out,
      lw['ln1_g'], lw['ln1_b'], lw['gn_g'], lw['gn_b'], ones, lw['w_router_hi'], lw['w_router_lo'], lw['b_router'])


def _router_gates(logits):
    lane = lax.broadcasted_iota(I32, logits.shape, 1)
    lane_f = lane.astype(F32)
    neg = -jnp.inf
    big = 1e9

    def first_lane(mask):
        return jnp.min(jnp.where(mask, lane_f, big), axis=-1, keepdims=True).astype(I32)

    is_grp = lane < N_GROUPS
    gmax = jnp.max(jnp.where(is_grp, logits, neg), axis=-1, keepdims=True)
    sel = first_lane(is_grp & (logits == gmax))
    gsum = jnp.sum(jnp.where(is_grp, jnp.exp(logits - gmax), 0.0), axis=-1, keepdims=True)
    gsel = 1.0 / gsum
    first_exp = N_GROUPS + sel * EXP_PER_GROUP
    in_sel = (lane >= first_exp) & (lane < first_exp + EXP_PER_GROUP)
    v1 = jnp.max(jnp.where(in_sel, logits, neg), axis=-1, keepdims=True)
    i1 = first_lane(in_sel & (logits == v1))
    rest = in_sel & (lane != i1)
    v2 = jnp.max(jnp.where(rest, logits, neg), axis=-1, keepdims=True)
    i2 = first_lane(rest & (logits == v2))
    e2 = jnp.exp(v2 - v1)
    t1 = (1.0 / (1.0 + e2)) * gsel
    t2 = (e2 / (1.0 + e2)) * gsel
    return jnp.where(lane == i1, t1, 0.0) + jnp.where(lane == i2, t2, 0.0), sel


def _load_expert_weights(src_hbm, layer, dst_ref, stage_ref, sem):
    copy = lambda e: pltpu.make_async_copy(src_hbm.at[layer, e], stage_ref.at[e % 2], sem.at[e % 2])
    copy(0).start()
    copy(1).start()
    for e in range(N_EXPERTS):
        copy(e).wait()
        dst_ref[e] = stage_ref[e % 2].astype(BF16)
        if e + 2 < N_EXPERTS:
            copy(e + 2).start()


def _moe_kernel(x_ref, lg_ref, w1_hbm, w3_hbm, w2_hbm, h1_ref, mod_ref, lng_ref, lnb_ref, oc_ref, od_ref,
                xs_ref, gs_ref, acc_ref, w1_ref, w3_ref, w2_ref, stage_in_ref, stage_out_ref, sem,
                *, tm, sub, bounds, alpha, n_ctx_tiles, layer):
    @pl.when(pl.program_id(0) == 0)
    def _():
        _load_expert_weights(w1_hbm, layer, w1_ref, stage_in_ref, sem)
        _load_expert_weights(w3_hbm, layer, w3_ref, stage_in_ref, sem)
        _load_expert_weights(w2_hbm, layer, w2_ref, stage_out_ref, sem)

    gates, sel = _router_gates(lg_ref[...])
    lane = lax.broadcasted_iota(I32, gates.shape, 1)
    onehot = jnp.where(lane == sel, 1.0, 0.0)
    row_i = lax.broadcasted_iota(I32, (tm, tm), 0)
    col_i = lax.broadcasted_iota(I32, (tm, tm), 1)
    earlier = jnp.where(col_i < row_i, 1.0, 0.0).astype(BF16)
    rank = _dot(earlier, onehot.astype(BF16))
    lane1 = lax.broadcasted_iota(I32, (1, ROUTER_LANES), 1)
    counts = jnp.sum(onehot, axis=0, keepdims=True)
    cnt = [jnp.sum(jnp.where(lane1 == g, counts, 0.0)).astype(I32) for g in range(N_GROUPS)]
    off = [jnp.zeros((), I32)]
    for g in range(N_GROUPS - 1):
        off.append(off[g] + cnt[g])
    offs = jnp.zeros((1, ROUTER_LANES), F32)
    for g in range(N_GROUPS):
        offs = jnp.where(lane1 == g, off[g].astype(F32), offs)
    pos = jnp.sum(onehot * (rank + offs), axis=1, keepdims=True).astype(I32)
    pos_row = jnp.broadcast_to(pos, (tm, ROUTER_LANES)).T[0:1, :]
    perm = jnp.where(row_i == pos_row, 1.0, 0.0).astype(BF16)
    perm_t = jnp.where(col_i == pos, 1.0, 0.0).astype(BF16)
    xs_ref[...] = _dot(perm, x_ref[...]).astype(BF16)
    gs_ref[...] = _dot(perm, gates.astype(BF16))
    acc_ref[...] = jnp.zeros_like(acc_ref)

    lane_s = lax.broadcasted_iota(I32, (sub, ROUTER_LANES), 1)
    for s in range(tm // sub):
        rows = pl.ds(s * sub, sub)
        for g in range(N_GROUPS):
            @pl.when((off[g] < (s + 1) * sub) & (off[g] + cnt[g] > s * sub))
            def _():
                xb = xs_ref[rows, :]
                gsb = gs_ref[rows, :]
                hids = []
                for j in range(EXP_PER_GROUP):
                    e = g * EXP_PER_GROUP + j
                    ge = jnp.sum(jnp.where(lane_s == N_GROUPS + e, gsb, 0.0), axis=-1, keepdims=True)
                    h1 = _dot(xb, w1_ref[e])
                    h3 = _dot(xb, w3_ref[e])
                    hids.append(((h1 * _sigmoid(h1)) * h3 * ge).astype(BF16))
                hid = jnp.concatenate(hids, axis=1)
                w2g = w2_ref[g * EXP_PER_GROUP:(g + 1) * EXP_PER_GROUP].reshape(EXP_PER_GROUP * D_EXPERT, D_MODEL)
                acc_ref[rows, :] += _dot(hid, w2g)

    f = _dot(perm_t, acc_ref[...].astype(BF16))
    grp = _group_of_row(pl.program_id(0) * tm, bounds)
    g2 = mod_ref[pl.ds(grp, 1), 5 * D_MODEL:6 * D_MODEL]
    out = _layer_norm(alpha * h1_ref[...] + g2 * f, lng_ref[...], lnb_ref[...])
    is_ctx = pl.program_id(0) < n_ctx_tiles

    @pl.when(is_ctx)
    def _():
        oc_ref[...] = out

    @pl.when(jnp.logical_not(is_ctx))
    def _():
        od_ref[...] = out


def _moe(u2, logits, h1, mods, experts, layer, lw, *, tm, sub, bounds, alpha, n_ctx):
    t = h1.shape[0]
    nct = n_ctx // tm
    rows_c, rows_d = _pair_specs(tm, nct)
    lay = _layer_spec(layer)
    hbm = pl.BlockSpec(memory_space=pl.ANY)
    rows = lambda w: pl.BlockSpec((tm, w), lambda i: (i, 0))
    return pl.pallas_call(
        functools.partial(_moe_kernel, tm=tm, sub=sub, bounds=bounds, alpha=alpha, n_ctx_tiles=nct, layer=layer),
        grid=(t // tm,),
        in_specs=[rows(D_MODEL), rows(ROUTER_LANES), hbm, hbm, hbm,
                  rows(D_MODEL), lay(8, 6 * D_MODEL), lay(1, D_MODEL), lay(1, D_MODEL)],
        out_specs=[rows_c(D_MODEL), rows_d(D_MODEL)],
        out_shape=[jax.ShapeDtypeStruct((n_ctx, D_MODEL), F32), jax.ShapeDtypeStruct((t - n_ctx, D_MODEL), F32)],
        scratch_shapes=[pltpu.VMEM((tm, D_MODEL), BF16), pltpu.VMEM((tm, ROUTER_LANES), F32),
                        pltpu.VMEM((tm, D_MODEL), F32),
                        pltpu.VMEM((N_EXPERTS, D_MODEL, D_EXPERT), BF16),
                        pltpu.VMEM((N_EXPERTS, D_MODEL, D_EXPERT), BF16),
                        pltpu.VMEM((N_EXPERTS, D_EXPERT, D_MODEL), BF16),
                        pltpu.VMEM((2, D_MODEL, D_EXPERT), F32), pltpu.VMEM((2, D_EXPERT, D_MODEL), F32),
                        pltpu.SemaphoreType.DMA((2,))],
        compiler_params=_cparams(("arbitrary",)),
        name="moe",
    )(u2, logits, experts[0], experts[1], experts[2], h1, mods, lw['ln2_g'], lw['ln2_b'])


def _rope_tables(dec_seq):
    n = HEAD_DIM // 4
    pos = jnp.arange(dec_seq)
    row = (pos // GRID_W).astype(F32)
    col = (pos % GRID_W).astype(F32)
    inv = 1.0 / (ROPE_THETA ** (jnp.arange(n, dtype=F32) / n))
    ra, ca = row[:, None] * inv, col[:, None] * inv
    cos64 = jnp.concatenate([jnp.cos(ra), jnp.cos(ra), jnp.cos(ca), jnp.cos(ca)], axis=-1)
    sin64 = jnp.concatenate([-jnp.sin(ra), jnp.sin(ra), -jnp.sin(ca), jnp.sin(ca)], axis=-1)
    return jnp.concatenate([cos64, cos64], axis=-1), jnp.concatenate([sin64, sin64], axis=-1)


def _block_diag2(w):
    z = jnp.zeros_like(w[:, 0])
    return jnp.concatenate([jnp.concatenate([w[:, 0], z], axis=2), jnp.concatenate([z, w[:, 1]], axis=2)], axis=1)


def _derived_weights(p):
    d = D_MODEL
    depth = p['w_in'].shape[0]
    row = lambda a, w: a.reshape(depth, 1, w)
    ng = len(POOL_WINDOWS)
    pool_blk = (p['pool_w'][:, :, :, None, :] * jnp.eye(ng, dtype=F32)[None, :, None, :, None]).reshape(
        depth, C_WIDTH, C_WIDTH)
    lane_pad = ROUTER_LANES - N_GROUPS - N_EXPERTS
    w_router = jnp.pad(jnp.concatenate([p['rt_grp_w'], p['rt_exp_w']], axis=-1), ((0, 0), (0, 0), (0, lane_pad)))
    b_router = jnp.pad(jnp.concatenate([p['rt_grp_b'], p['rt_exp_b']], axis=-1), ((0, 0), (0, lane_pad)))
    b_router = b_router.reshape(depth, 1, ROUTER_LANES)
    return {
        'w2blk': _block_diag2(p['rw_w2']).astype(BF16),
        'w0': row(p['rw_w0'], 2 * A_WIDTH),
        'a2blk': _block_diag2(p['rw_a2']).astype(BF16),
        'a0': row(p['rw_a0'], 2 * A_WIDTH),
        'g2': p['rw_g2'].astype(BF16),
        'kkw': row(p['rw_kk'], A_WIDTH),
        'ka': jnp.tile(row(p['rw_ka'], A_WIDTH), (1, 1, 2)),
        'rk': row(p['rw_rk'], A_WIDTH),
        'gn_g': row(p['rw_gn_g'], A_WIDTH),
        'gn_b': row(p['rw_gn_b'], A_WIDTH),
        'qn': jnp.tile(row(p['q_norm'], HEAD_DIM), (1, 1, B_Q_HEADS)),
        'knw': jnp.tile(row(p['k_norm'], HEAD_DIM), (1, 1, B_KV_HEADS)),
        'pool_blk': pool_blk.astype(BF16),
        'pool_scale': row(p['pool_scale'], C_WIDTH),
        'ln1_g': row(p['ln1_g'], d), 'ln1_b': row(p['ln1_b'], d),
        'ln2_g': row(p['ln2_g'], d), 'ln2_b': row(p['ln2_b'], d),
        'w_router_hi': w_router.astype(BF16),
        'w_router_lo': (w_router - w_router.astype(BF16).astype(F32)).astype(BF16),
        'b_router': b_router,
    }


def _forward(x_prompt, x_sample, state_rwkv, cache_k, cache_v, c, c_ctx, p):
    batch, seq, d = x_prompt.shape
    dec_batch, dec_seq, _ = x_sample.shape
    depth = p['w_in'].shape[0]
    past = cache_k.shape[2]
    n_ctx = batch * seq
    n_tok = n_ctx + dec_batch * dec_seq
    alpha = (2 * depth) ** 0.25
    bounds = tuple(n_ctx + j * dec_seq for j in range(dec_batch))
    tm = math.gcd(n_ctx, dec_seq, 512)
    tm_moe = math.gcd(n_ctx, dec_seq, 512)
    tq_ctx = min(256, seq)
    tq_dec = min(256, dec_seq)
    sb_ctx = 2

    cond8 = jnp.zeros((8, d), F32).at[0].set(c_ctx).at[1:1 + dec_batch].set(c)
    mods = _modulation(cond8, p['w_mod'], p['b_mod'])
    cos, sin = _rope_tables(dec_seq)
    w_in16, w_out16 = p['w_in'].astype(BF16), p['w_out'].astype(BF16)
    experts = (p['moe_w1'], p['moe_w3'], p['moe_w2'])
    lw = _derived_weights(p)
    head_id = jnp.arange(2 * HEAD_DIM) // HEAD_DIM
    ones = (head_id[:, None] == head_id[None, :]).astype(BF16)

    h = (x_prompt.reshape(n_ctx, d), x_sample.reshape(dec_batch * dec_seq, d))
    ck4 = cache_k.reshape(dec_batch, depth, past, KV_WIDTH)
    cv4 = cache_v.reshape(dec_batch, depth, past, KV_WIDTH)
    new_s, new_k, new_v = [], [], []
    for l in range(depth):
        (r, v, kk, logw, kd, bvec, g, bonus, q, k_att, v_att, pp) = _in_projection(
            h, mods, w_in16, l, lw, cos, sin, ones, tm=tm, bounds=bounds)
        scan_ops = (r, v, kk, logw, kd, bvec)
        yf_c, yb_c, s_fin = _rwkv_scan(scan_ops, None, row0=0, nseq=batch, seqlen=seq, sb=sb_ctx,
                                       nb=seq // CHUNK, want_state=True)
        yf_d, yb_d = _rwkv_scan(scan_ops, state_rwkv[:, l], row0=n_ctx, nseq=dec_batch, seqlen=dec_seq,
                                sb=dec_batch, nb=4, want_state=False)
        yf = (yf_c.reshape(n_ctx, A_WIDTH), yf_d.reshape(n_tok - n_ctx, A_WIDTH))
        yb = (yb_c.reshape(n_ctx, A_WIDTH), yb_d.reshape(n_tok - n_ctx, A_WIDTH))
        att_c, pool_c = _attention(q, k_att, v_att, None, pp, lw['pool_blk'], lw['pool_scale'], l,
                                   row0=0, nseq=batch, seqlen=seq, tq=tq_ctx)
        att_d, pool_d = _attention(q, k_att, v_att, (ck4, cv4), pp, lw['pool_blk'], lw['pool_scale'], l,
                                   row0=n_ctx, nseq=dec_batch, seqlen=dec_seq, tq=tq_dec)
        att, pool = (att_c, att_d), (pool_c, pool_d)
        h1, u2, logits = _out_projection(yf, yb, g, bonus, att, pool, h, mods, w_out16, l, lw, ones,
                                         tm=tm, bounds=bounds, alpha=alpha)
        h = _moe(u2, logits, h1, mods, experts, l, lw, tm=tm_moe, sub=128, bounds=bounds, alpha=alpha,
                 n_ctx=n_ctx)
        new_s.append(s_fin)
        new_k.append(k_att[:n_ctx].reshape(batch, seq, B_KV_HEADS, HEAD_DIM))
        new_v.append(v_att[:n_ctx].reshape(batch, seq, B_KV_HEADS, HEAD_DIM))
    return (h[0].reshape(batch, seq, d), h[1].reshape(dec_batch, dec_seq, d),
            jnp.stack(new_s, axis=1), jnp.stack(new_k, axis=1), jnp.stack(new_v, axis=1))


def kernel(x_prompt, x_sample, state_rwkv, cache_k, cache_v, c, c_ctx, w_in, w_out, w_mod, b_mod, ln1_g, ln1_b, ln2_g, ln2_b, rw_w0, rw_w2, rw_a0, rw_a2, rw_g2, rw_kk, rw_ka, rw_rk, rw_gn_g, rw_gn_b, q_norm, k_norm, pool_w, pool_scale, rt_grp_w, rt_grp_b, rt_exp_w, rt_exp_b, moe_w1, moe_w3, moe_w2):
    p = dict(w_in=w_in, w_out=w_out, w_mod=w_mod, b_mod=b_mod, ln1_g=ln1_g, ln1_b=ln1_b, ln2_g=ln2_g,
             ln2_b=ln2_b, rw_w0=rw_w0, rw_w2=rw_w2, rw_a0=rw_a0, rw_a2=rw_a2, rw_g2=rw_g2, rw_kk=rw_kk,
             rw_ka=rw_ka, rw_rk=rw_rk, rw_gn_g=rw_gn_g, rw_gn_b=rw_gn_b, q_norm=q_norm, k_norm=k_norm,
             pool_w=pool_w, pool_scale=pool_scale, rt_grp_w=rt_grp_w, rt_grp_b=rt_grp_b, rt_exp_w=rt_exp_w,
             rt_exp_b=rt_exp_b, moe_w1=moe_w1, moe_w3=moe_w3, moe_w2=moe_w2)
    return _forward(x_prompt, x_sample, state_rwkv, cache_k, cache_v, c, c_ctx, p)
```

```python
import functools
import math

import jax
import jax.numpy as jnp
from jax import lax
from jax.experimental import pallas as pl
from jax.experimental.pallas import tpu as pltpu

F32 = jnp.float32
BF16 = jnp.bfloat16
I32 = jnp.int32

D_MODEL = 1024
GRID_W = 64
HEAD_DIM = 64
A_HEADS = 6
A_WIDTH = A_HEADS * HEAD_DIM
LORA_W = 64
LORA_A = 64
LORA_G = 128
N_DIR = 2
DECAY_SCALE = 0.606531
B_Q_HEADS = 6
B_KV_HEADS = 2
B_WIDTH = B_Q_HEADS * HEAD_DIM
KV_WIDTH = B_KV_HEADS * HEAD_DIM
ROPE_THETA = 10000.0
POOL_WINDOWS = (2, 4, 8, 16)
C_WIDTH = 256
IN_COLS = 2432
N_GROUPS = 4
EXP_PER_GROUP = 4
N_EXPERTS = 16
D_EXPERT = 256
LN_EPS = 1e-5
GN_EPS = 64e-5
QK_EPS = 1e-6

_O_R, _O_K, _O_V = 0, 384, 768
_O_LW, _O_LA, _O_LG = 1152, 1280, 1408
_O_Q, _O_KB, _O_VB, _O_P = 1536, 1920, 2048, 2176

CHUNK = 64
ROUTER_LANES = 128
WEIGHT_STAGE_SLOTS = 4
VMEM_LIMIT = 56 * 1024 * 1024


def _cparams(sem):
    return pltpu.CompilerParams(dimension_semantics=sem, vmem_limit_bytes=VMEM_LIMIT)


def _sigmoid(x):
    return 1.0 / (1.0 + jnp.exp(-x))


def _dot(a, b):
    return jnp.dot(a, b, preferred_element_type=F32)


def _dot_nt(a, b):
    return lax.dot_general(a, b, (((1,), (1,)), ((), ())), preferred_element_type=F32)


def _segsum(x, ones2):
    x16 = x.astype(BF16)
    slabs = [_dot(x16[:, j:j + 128], ones2) for j in range(0, x.shape[1], 128)]
    return slabs[0] if len(slabs) == 1 else jnp.concatenate(slabs, axis=1)


def _rope128(x, cos, sin):
    lane = lax.broadcasted_iota(I32, x.shape, 1)
    first = (lane % 32) < 16
    up = pltpu.roll(x, 128 - 16, 1)
    dn = pltpu.roll(x, 16, 1)
    return x * cos + jnp.where(first, up, dn) * sin


def _layer_norm(z, g, b):
    mu = jnp.mean(z, axis=-1, keepdims=True)
    zc = z - mu
    var = jnp.mean(zc * zc, axis=-1, keepdims=True)
    return zc * lax.rsqrt(var + LN_EPS) * g + b


def _mod_kernel(c_ref, w_ref, b_ref, o_ref):
    x = c_ref[...]
    x = x * _sigmoid(x)
    w = w_ref[...]
    x_hi = x.astype(BF16)
    x_lo = (x - x_hi.astype(F32)).astype(BF16)
    w_hi = w.astype(BF16)
    w_lo = (w - w_hi.astype(F32)).astype(BF16)
    o_ref[...] = (_dot(x_hi, w_hi) + _dot(x_lo, w_hi) + _dot(x_hi, w_lo)) + b_ref[...]


def _modulation(cond8, w_mod, b_mod):
    depth = w_mod.shape[0]
    tn = 1536
    return pl.pallas_call(
        _mod_kernel,
        grid=(depth, 6 * D_MODEL // tn),
        in_specs=[
            pl.BlockSpec((8, D_MODEL), lambda l, j: (0, 0)),
            pl.BlockSpec((None, D_MODEL, tn), lambda l, j: (l, 0, j)),
            pl.BlockSpec((None, 1, tn), lambda l, j: (l, 0, j)),
        ],
        out_specs=pl.BlockSpec((None, 8, tn), lambda l, j: (l, 0, j)),
        out_shape=jax.ShapeDtypeStruct((depth, 8, 6 * D_MODEL), F32),
        compiler_params=_cparams(("arbitrary", "arbitrary")),
        name="modulation",
    )(cond8, w_mod, b_mod.reshape(depth, 1, 6 * D_MODEL))


def _group_of_row(row0, bounds):
    g = jnp.zeros((), I32)
    for b in bounds:
        g = g + jnp.where(row0 >= b, 1, 0).astype(I32)
    return g


def _in_kernel(hc_ref, hd_ref, mod_ref, w_ref, w2_ref, w0_ref, a2_ref, a0_ref, g2_ref, kkw_ref, ka_ref,
               rk_ref, qn_ref, knw_ref, cos_ref, sin_ref, ones_ref,
               r_o, v_o, kk_o, lw_o, kd_o, b_o, g_o, bon_o, q_o, k_o, vb_o, p_o, *, tm, bounds, n_ctx_tiles):
    grp = _group_of_row(pl.program_id(0) * tm, bounds)
    mod = mod_ref[pl.ds(grp, 1), :]
    sh1 = mod[:, 0:D_MODEL]
    sc1 = mod[:, D_MODEL:2 * D_MODEL]
    is_ctx = pl.program_id(0) < n_ctx_tiles
    h = jnp.where(is_ctx, hc_ref[...], hd_ref[...])
    u = h * (1.0 + sc1) + sh1
    ones = ones_ref[...]
    scale = 1.0 / math.sqrt(HEAD_DIM)
    proj = _dot(u.astype(BF16), w_ref[...])

    r = proj[:, _O_R:_O_R + A_WIDTH]
    k = proj[:, _O_K:_O_K + A_WIDTH]
    v = proj[:, _O_V:_O_V + A_WIDTH]
    lo_w = proj[:, _O_LW:_O_LW + 128]
    lo_a = proj[:, _O_LA:_O_LA + 128]
    lo_g = proj[:, _O_LG:_O_LG + 128]

    w_pre = w0_ref[...] + _dot(jnp.tanh(lo_w).astype(BF16), w2_ref[...])
    logw = -DECAY_SCALE * _sigmoid(w_pre)
    a = _sigmoid(a0_ref[...] + _dot(lo_a.astype(BF16), a2_ref[...]))
    g = _dot(_sigmoid(lo_g).astype(BF16), g2_ref[...])

    kk = k * kkw_ref[...]
    kk = kk * lax.rsqrt(jnp.maximum(_segsum(kk * kk, ones), 1e-24))
    k2 = jnp.concatenate([k, k], axis=1)
    kd = k2 * (1.0 + (a - 1.0) * ka_ref[...])
    bvec = jnp.concatenate([kk, kk], axis=1) * a
    kd_sum = kd[:, 0:A_WIDTH] + kd[:, A_WIDTH:2 * A_WIDTH]
    bonus = _segsum(r * kd_sum * rk_ref[...], ones) * v

    r_o[...] = r
    v_o[...] = v.astype(BF16)
    kk_o[...] = kk
    lw_o[...] = logw
    kd_o[...] = kd
    b_o[...] = bvec
    g_o[...] = g
    bon_o[...] = bonus

    cos = cos_ref[...]
    sin = sin_ref[...]
    rope = lambda x: jnp.where(is_ctx, x, _rope128(x, cos, sin))
    q = proj[:, _O_Q:_O_Q + B_WIDTH]
    qn = q * lax.rsqrt(_segsum(q * q, ones) * (1.0 / HEAD_DIM) + QK_EPS) * qn_ref[...]
    q_o[...] = (jnp.concatenate([rope(qn[:, j:j + 128]) for j in range(0, B_WIDTH, 128)], axis=1)
                * scale).astype(BF16)
    kb = proj[:, _O_KB:_O_KB + KV_WIDTH]
    kn = kb * lax.rsqrt(_segsum(kb * kb, ones) * (1.0 / HEAD_DIM) + QK_EPS) * knw_ref[...]
    k_o[...] = rope(kn)
    vb_o[...] = proj[:, _O_VB:_O_VB + KV_WIDTH]
    p_o[...] = proj[:, _O_P:_O_P + C_WIDTH]


def _pair_specs(tm, nct):
    rows_c = lambda w: pl.BlockSpec((tm, w), lambda i: (jnp.minimum(i, nct - 1), 0))
    rows_d = lambda w: pl.BlockSpec((tm, w), lambda i: (jnp.maximum(i - nct, 0), 0))
    return rows_c, rows_d


def _layer_spec(layer):
    return lambda *shape: pl.BlockSpec((None,) + shape, lambda *_: (layer,) + (0,) * len(shape))


def _in_projection(h, mods, w_in, layer, lw, cos, sin, ones, *, tm, bounds):
    nct = h[0].shape[0] // tm
    t = h[0].shape[0] + h[1].shape[0]
    full = lambda shape: pl.BlockSpec(shape, lambda i: (0,) * len(shape))
    rows = lambda w: pl.BlockSpec((tm, w), lambda i: (i, 0))
    rows_c, rows_d = _pair_specs(tm, nct)
    lay = _layer_spec(layer)
    tiles_per_req = cos.shape[0] // tm
    table = pl.BlockSpec((tm, 128), lambda i: (jnp.maximum(i - nct, 0) % tiles_per_req, 0))
    widths = (A_WIDTH, A_WIDTH, A_WIDTH, 2 * A_WIDTH, 2 * A_WIDTH, 2 * A_WIDTH, A_WIDTH, A_WIDTH,
              B_WIDTH, KV_WIDTH, KV_WIDTH, C_WIDTH)
    dtypes = (F32, BF16, F32, F32, F32, F32, F32, F32, BF16, F32, F32, F32)
    return pl.pallas_call(
        functools.partial(_in_kernel, tm=tm, bounds=bounds, n_ctx_tiles=nct),
        grid=(t // tm,),
        in_specs=[
            rows_c(D_MODEL), rows_d(D_MODEL), lay(8, 6 * D_MODEL), lay(D_MODEL, IN_COLS),
            lay(128, 2 * A_WIDTH), lay(1, 2 * A_WIDTH), lay(128, 2 * A_WIDTH), lay(1, 2 * A_WIDTH),
            lay(LORA_G, A_WIDTH), lay(1, A_WIDTH), lay(1, 2 * A_WIDTH), lay(1, A_WIDTH),
            lay(1, B_WIDTH), lay(1, KV_WIDTH), table, table, full((2 * HEAD_DIM, 2 * HEAD_DIM)),
        ],
        out_specs=[rows(w) for w in widths],
        out_shape=[jax.ShapeDtypeStruct((t, w), dt) for w, dt in zip(widths, dtypes)],
        compiler_params=_cparams(("arbitrary",)),
        name="in_projection",
    )(h[0], h[1], mods, w_in, lw['w2blk'], lw['w0'], lw['a2blk'], lw['a0'], lw['g2'], lw['kkw'], lw['ka'],
      lw['rk'], lw['qn'], lw['knw'], cos, sin, ones)


def _split2(x):
    hi = x.astype(BF16)
    lo = (x - hi.astype(F32)).astype(BF16)
    return hi, lo


def _scan_kernel(*refs, sb, nb, zero_init, want_state):
    fwd = refs[0:6]
    bwd = refs[6:12]
    pos = 12
    if not zero_init:
        s0_ref = refs[pos]
        pos += 1
    y_outs = (refs[pos], refs[pos + 1])
    pos += 2
    if want_state:
        sfin_o = refs[pos]
        pos += 1
    h_ref = refs[pos]

    step = pl.program_id(1)
    nstep = pl.num_programs(1)
    npair = A_HEADS // 2
    pw = 2 * HEAD_DIM
    pairs = [(q, d, hp) for q in range(sb) for d in range(N_DIR) for hp in range(npair)]

    @pl.when(step == 0)
    def _():
        h_ref[...] = jnp.zeros_like(h_ref)
        if not zero_init:
            for (q, d, hp) in pairs:
                for e in range(2):
                    lo = e * HEAD_DIM
                    h_ref[q, d, hp, lo:lo + HEAD_DIM, lo:lo + HEAD_DIM] = s0_ref[q, d, 2 * hp + e].T

    def iota(shape, dim):
        return lax.broadcasted_iota(I32, shape, dim)

    first64 = iota((CHUNK, pw), 1) < HEAD_DIM
    eye_p = ((iota((CHUNK, pw), 1) % HEAD_DIM) == iota((CHUNK, pw), 0)).astype(F32)
    eye2 = (iota((pw, pw), 0) == iota((pw, pw), 1)).astype(F32)
    same_head = (iota((pw, pw), 0) < HEAD_DIM) == (iota((pw, pw), 1) < HEAD_DIM)
    ti = iota((CHUNK, CHUNK), 0)
    si = iota((CHUNK, CHUNK), 1)
    incl16 = ((si <= ti).astype(BF16), (si >= ti).astype(BF16))
    t4 = iota((2 * CHUNK, 2 * pw), 0)
    s4 = iota((2 * CHUNK, 2 * pw), 1) % CHUNK
    incl_off = jnp.where(t4 < CHUNK, 0, 1)
    t4 = t4 % CHUNK
    mask4 = (s4 < t4 + incl_off, s4 > t4 - incl_off)

    def bdiag(x):
        return jnp.concatenate([jnp.where(first64, x, 0.0), jnp.where(first64, 0.0, x)], axis=0)

    lhs, rhs1, bkt, gcol, v_p, amat = {}, {}, {}, {}, {}, {}
    blocks = [(q, d, j) for q in range(sb) for d in range(N_DIR) for j in range(nb)]
    csum = {}
    for (q, d, j) in blocks:
        hi, lo = _split2((fwd, bwd)[d][3][q, pl.ds(j * CHUNK, CHUNK), :])
        csum[(q, d, j)] = _dot(incl16[d], hi) + _dot(incl16[d], lo)
    for (q, d, j) in blocks:
        r_ref, v_ref, kk_ref, lw_ref, kd_ref, b_ref = (fwd, bwd)[d]
        rows = pl.ds(j * CHUNK, CHUNK)
        lw = lw_ref[q, rows, :]
        cs = csum[(q, d, j)]
        gam = jnp.exp(cs)
        gam_prev = jnp.exp(cs - lw)
        gam_inv = jnp.exp(-cs)
        gam_end = jnp.exp(cs[CHUNK - 1:CHUNK, :] if d == 0 else cs[0:1, :])
        a_hat = -kk_ref[q, rows, :] * gam_prev
        r_hat = r_ref[q, rows, :] * gam
        b_hat = b_ref[q, rows, :] * gam_inv
        k_hat = kd_ref[q, rows, :] * gam_inv
        v_all = v_ref[q, rows, :]
        for hp in range(npair):
            key = (q, d, hp, j)
            sl = slice(hp * pw, (hp + 1) * pw)
            lhs[key] = jnp.concatenate([a_hat[:, sl], r_hat[:, sl]], axis=0).astype(BF16)
            bh, kh, ge = b_hat[:, sl], k_hat[:, sl], gam_end[:, sl]
            rhs1[key] = jnp.concatenate([bdiag(bh), bdiag(kh)], axis=0).astype(BF16)
            gcol[key] = jnp.sum(eye2 * ge, axis=1, keepdims=True)
            bkt[key] = (jnp.concatenate([bh, kh], axis=0) * ge).T.astype(BF16)
            v_p[key] = v_all[:, sl]
    keys = list(lhs.keys())
    for key in keys:
        amat[key] = _dot_nt(lhs[key], rhs1[key])
    npow, tcur, a_rb, lhs2, vbd = {}, {}, {}, {}, {}
    for key in keys:
        am = jnp.where(mask4[key[1]], amat[key], 0.0)
        npow[key] = am[0:CHUNK, 0:pw]
        tcur[key] = eye_p + npow[key]
        a_rb[key] = am[CHUNK:2 * CHUNK, 0:pw].astype(BF16)
        vbd[key] = bdiag(v_p[key])
        lhs2[key] = jnp.concatenate([lhs[key], am[:, pw:2 * pw].astype(BF16)], axis=1)
    nsq = int(math.log2(CHUNK)) - 1
    for key in keys:
        npow[key] = _dot(npow[key].astype(BF16), bdiag(npow[key]).astype(BF16))
    for _ in range(nsq - 1):
        for key in keys:
            rhs = jnp.concatenate([bdiag(npow[key]), bdiag(tcur[key])], axis=1).astype(BF16)
            res = _dot(npow[key].astype(BF16), rhs)
            npow[key] = res[:, 0:pw]
            tcur[key] = tcur[key] + res[:, pw:2 * pw]
    t_inv = {}
    for key in keys:
        t_inv[key] = (tcur[key] + _dot(npow[key].astype(BF16), bdiag(tcur[key]).astype(BF16))).astype(BF16)

    hcur = {pr: h_ref[pr[0], pr[1], pr[2]] for pr in pairs}
    for jj in range(nb):
        z, u = {}, {}
        kof = lambda pr: (pr[0], pr[1], pr[2], jj if pr[1] == 0 else nb - 1 - jj)
        for pr in pairs:
            key = kof(pr)
            hv = jnp.concatenate([hcur[pr].astype(BF16), vbd[key]], axis=0)
            z[pr] = _dot(lhs2[key], hv)
        for pr in pairs:
            u[pr] = _dot(t_inv[kof(pr)], bdiag(z[pr][0:CHUNK]).astype(BF16))
        for pr in pairs:
            key = kof(pr)
            q, d, hp, j = key
            y = z[pr][CHUNK:2 * CHUNK] + _dot(a_rb[key], bdiag(u[pr]).astype(BF16))
            y_outs[d][q, pl.ds(j * CHUNK, CHUNK), hp * pw:(hp + 1) * pw] = y
            uv = jnp.concatenate([u[pr].astype(BF16), v_p[key]], axis=0)
            hcur[pr] = hcur[pr] * gcol[key] + jnp.where(same_head, _dot(bkt[key], uv), 0.0)
    for pr in pairs:
        h_ref[pr[0], pr[1], pr[2]] = hcur[pr]

    if want_state:
        @pl.when(step == nstep - 1)
        def _():
            for (q, d, hp) in pairs:
                for e in range(2):
                    lo = e * HEAD_DIM
                    sfin_o[q, d, 2 * hp + e] = hcur[(q, d, hp)][lo:lo + HEAD_DIM, lo:lo + HEAD_DIM].T


def _rwkv_scan(ops, s0, *, row0, nseq, seqlen, sb, nb, want_state):
    g = nb * CHUNK
    nstep = seqlen // g
    zero_init = s0 is None
    ops3 = [a.reshape(a.shape[0] // seqlen, seqlen, a.shape[1]) for a in ops]
    base = row0 // (seqlen * sb)
    fidx = lambda col: (lambda s, c: (base + s, c, col))
    bidx = lambda col: (lambda s, c: (base + s, nstep - 1 - c, col))
    blk = lambda im: pl.BlockSpec((sb, g, A_WIDTH), im)
    in_specs = [blk(fidx(0))] * 6 + [blk(bidx(0))] * 3 + [blk(bidx(1))] * 3
    args = ops3 + ops3
    state_spec = pl.BlockSpec((sb, N_DIR, A_HEADS, HEAD_DIM, HEAD_DIM), lambda s, c: (s, 0, 0, 0, 0))
    if not zero_init:
        in_specs.append(state_spec)
        args.append(s0)
    out_specs = [blk(lambda s, c: (s, c, 0)), blk(lambda s, c: (s, nstep - 1 - c, 0))]
    out_shape = [jax.ShapeDtypeStruct((nseq, seqlen, A_WIDTH), F32)] * 2
    if want_state:
        out_specs.append(state_spec)
        out_shape.append(jax.ShapeDtypeStruct((nseq, N_DIR, A_HEADS, HEAD_DIM, HEAD_DIM), F32))
    return pl.pallas_call(
        functools.partial(_scan_kernel, sb=sb, nb=nb, zero_init=zero_init, want_state=want_state),
        grid=(nseq // sb, nstep),
        in_specs=in_specs,
        out_specs=out_specs,
        out_shape=out_shape,
        scratch_shapes=[pltpu.VMEM((sb, N_DIR, A_HEADS // 2, 2 * HEAD_DIM, 2 * HEAD_DIM), F32)],
        compiler_params=_cparams(("arbitrary", "arbitrary")),
        name="rwkv_scan_ctx" if zero_init else "rwkv_scan_dec",
    )(*args)


def _attn_kernel(*refs, with_cache):
    grp = B_Q_HEADS // B_KV_HEADS
    if with_cache:
        q_ref, k_ref, v_ref, ck_ref, cv_ref, p_ref, pw_ref, psc_ref, o_ref, pool_o = refs
        kall = jnp.concatenate([k_ref[...], ck_ref[...]], axis=0)
        vall = jnp.concatenate([v_ref[...], cv_ref[...]], axis=0)
    else:
        q_ref, k_ref, v_ref, p_ref, pw_ref, psc_ref, o_ref, pool_o = refs
        kall = k_ref[...]
        vall = v_ref[...]

    @pl.when(pl.program_id(1) == 0)
    def _():
        pool_o[...] = _pool_mix(p_ref[...], pw_ref[...], psc_ref[...]).astype(BF16)

    q = q_ref[...]
    k16 =[kall[:, hk * HEAD_DIM:(hk + 1) * HEAD_DIM].astype(BF16) for hk in range(B_KV_HEADS)]
    v16 = [vall[:, hk * HEAD_DIM:(hk + 1) * HEAD_DIM].astype(BF16) for hk in range(B_KV_HEADS)]

    def scores(j):
        return _dot_nt(q[:, j * HEAD_DIM:(j + 1) * HEAD_DIM], k16[j // grp])

    outs = []
    s_next = scores(0)
    for j in range(B_Q_HEADS):
        s = s_next
        if j + 1 < B_Q_HEADS:
            s_next = scores(j + 1)
        m = jnp.max(s, axis=-1, keepdims=True)
        p = jnp.exp(s - m)
        den = jnp.sum(p, axis=-1, keepdims=True)
        outs.append(_dot(p.astype(BF16), v16[j // grp]) / den)
    o_ref[...] = jnp.concatenate(outs, axis=1).astype(BF16)


def _attention(q, k_tok, v_tok, cache, p_tok, pool_w, pool_scale, layer, *, row0, nseq, seqlen, tq):
    nq = seqlen // tq
    base = row0 // tq
    kbase = row0 // seqlen
    kspec = pl.BlockSpec((seqlen, KV_WIDTH), lambda s, i: (kbase + s, 0))
    in_specs = [pl.BlockSpec((tq, B_WIDTH), lambda s, i: (base + s * nq + i, 0)), kspec, kspec]
    args = [q, k_tok, v_tok]
    if cache is not None:
        ck, cv = cache
        cspec = pl.BlockSpec((None, None, ck.shape[2], KV_WIDTH), lambda s, i: (s, layer, 0, 0))
        in_specs += [cspec, cspec]
        args += [ck, cv]
    lay = _layer_spec(layer)
    in_specs += [pl.BlockSpec((seqlen, C_WIDTH), lambda s, i: (kbase + s, 0)), lay(C_WIDTH, C_WIDTH), lay(1, C_WIDTH)]
    args += [p_tok, pool_w, pool_scale]
    return pl.pallas_call(
        functools.partial(_attn_kernel, with_cache=cache is not None),
        grid=(nseq, nq),
        in_specs=in_specs,
        out_specs=[pl.BlockSpec((tq, B_WIDTH), lambda s, i: (s * nq + i, 0)),
                   pl.BlockSpec((seqlen, C_WIDTH), lambda s, i: (s, 0))],
        out_shape=[jax.ShapeDtypeStruct((nseq * seqlen, B_WIDTH), BF16),
                   jax.ShapeDtypeStruct((nseq * seqlen, C_WIDTH), BF16)],
        compiler_params=_cparams(("arbitrary", "arbitrary")),
        name="attention_ctx" if cache is None else "attention_dec",
    )(*args)


def _pool_mix(x, w_blk, scale):
    seqlen = x.shape[0]
    t = lax.broadcasted_iota(I32, x.shape, 0)
    lane = lax.broadcasted_iota(I32, x.shape, 1)

    def down(a, k):
        return jnp.where(t >= k, pltpu.roll(a, k, 0), 0.0)

    def up(a, k):
        return jnp.where(t < seqlen - k, pltpu.roll(a, seqlen - k, 0), 0.0)

    past = [x]
    futr = [x]
    for j in range(len(POOL_WINDOWS) - 1):
        past.append(past[j] + down(past[j], 2 ** j))
        futr.append(futr[j] + up(futr[j], 2 ** j))
    win_sum = jnp.zeros_like(x)
    cnt = jnp.zeros_like(x)
    for gi, win in enumerate(POOL_WINDOWS):
        half = win // 2
        in_group = (lane // HEAD_DIM) == gi
        win_sum = jnp.where(in_group, down(past[gi], 1) + futr[gi], win_sum)
        n = (jnp.minimum(t + half, seqlen) - jnp.maximum(t - half, 0)).astype(F32)
        cnt = jnp.where(in_group, n, cnt)
    d = win_sum / cnt - x
    return _dot(d.astype(BF16), w_blk) * scale


def _out_kernel(yfc_ref, yfd_ref, ybc_ref, ybd_ref, g_ref, bon_ref, attc_ref, attd_ref, poolc_ref, poold_ref,
                hc_ref, hd_ref, mod_ref, w_ref, lng_ref, lnb_ref, gng_ref, gnb_ref, ones_ref, wrh_ref, wrl_ref,
                br_ref, h1_o, u2_o, lg_o, *, tm, bounds, alpha, n_ctx_tiles):
    grp = _group_of_row(pl.program_id(0) * tm, bounds)
    mod = mod_ref[pl.ds(grp, 1), :]
    g1 = mod[:, 2 * D_MODEL:3 * D_MODEL]
    sh2 = mod[:, 3 * D_MODEL:4 * D_MODEL]
    sc2 = mod[:, 4 * D_MODEL:5 * D_MODEL]
    ones = ones_ref[...]
    is_ctx = pl.program_id(0) < n_ctx_tiles
    half = tm // 2
    halves = [pl.ds(s * half, half) for s in range(2)]
    pick = lambda c_ref, d_ref, rows: jnp.where(is_ctx, c_ref[rows, :], d_ref[rows, :])
    y = [pick(yfc_ref, yfd_ref, rows) + pick(ybc_ref, ybd_ref, rows) for rows in halves]
    mu = [_segsum(v, ones) * (1.0 / HEAD_DIM) for v in y]
    yc = [v - m_ for v, m_ in zip(y, mu)]
    var = [_segsum(v * v, ones) * (1.0 / HEAD_DIM) for v in yc]
    m = []
    for rows, v, vr in zip(halves, yc, var):
        yn = v * lax.rsqrt(vr + GN_EPS) * gng_ref[...] + gnb_ref[...]
        a_out = (yn + bon_ref[rows, :]) * g_ref[rows, :]
        m.append(_dot(a_out.astype(BF16), w_ref[0:A_WIDTH, :])
                 + _dot(pick(attc_ref, attd_ref, rows), w_ref[A_WIDTH:A_WIDTH + B_WIDTH, :])
                 + _dot(pick(poolc_ref, poold_ref, rows), w_ref[A_WIDTH + B_WIDTH:D_MODEL, :]))
    for rows, m_ in zip(halves, m):
        h1 = _layer_norm(alpha * pick(hc_ref, hd_ref, rows) + g1 * m_, lng_ref[...], lnb_ref[...])
        h1_o[rows, :] = h1
        u2 = h1 * (1.0 + sc2) + sh2
        u_hi = u2.astype(BF16)
        u_lo = (u2 - u_hi.astype(F32)).astype(BF16)
        u2_o[rows, :] = u_hi
        lg_o[rows, :] = (_dot(u_hi, wrh_ref[...]) + _dot(u_lo, wrh_ref[...]) + _dot(u_hi, wrl_ref[...])) + br_ref[...]


def _out_projection(yf, yb, g, bonus, att, pool, h, mods, w_out, layer, lw, ones, *, tm, bounds, alpha):
    t = h[0].shape[0] + h[1].shape[0]
    nct = h[0].shape[0] // tm
    full = lambda shape: pl.BlockSpec(shape, lambda i: (0,) * len(shape))
    rows = lambda w: pl.BlockSpec((tm, w), lambda i: (i, 0))
    rows_c, rows_d = _pair_specs(tm, nct)
    lay = _layer_spec(layer)
    return pl.pallas_call(
        functools.partial(_out_kernel, tm=tm, bounds=bounds, alpha=alpha, n_ctx_tiles=nct),
        grid=(t // tm,),
        in_specs=[rows_c(A_WIDTH), rows_d(A_WIDTH), rows_c(A_WIDTH), rows_d(A_WIDTH), rows(A_WIDTH), rows(A_WIDTH),
                  rows_c(B_WIDTH), rows_d(B_WIDTH), rows_c(C_WIDTH), rows_d(C_WIDTH), rows_c(D_MODEL),
                  rows_d(D_MODEL), lay(8, 6 * D_MODEL), lay(D_MODEL, D_MODEL), lay(1, D_MODEL),
                  lay(1, D_MODEL), lay(1, A_WIDTH), lay(1, A_WIDTH), full((2 * HEAD_DIM, 2 * HEAD_DIM)),
                  lay(D_MODEL, ROUTER_LANES), lay(D_MODEL, ROUTER_LANES), lay(1, ROUTER_LANES)],
        out_specs=[rows(D_MODEL), rows(D_MODEL), rows(ROUTER_LANES)],
        out_shape=[jax.ShapeDtypeStruct((t, D_MODEL), F32), jax.ShapeDtypeStruct((t, D_MODEL), BF16),
                   jax.ShapeDtypeStruct((t, ROUTER_LANES), F32)],
        compiler_params=_cparams(("arbitrary",)),
        name="out_projection",
    )(yf[0], yf[1], yb[0], yb[1], g, bonus, att[0], att[1], pool[0], pool[1], h[0], h[1], mods, w_out,
      lw['ln1_g'], lw['ln1_b'], lw['gn_g'], lw['gn_b'], ones, lw['w_router_hi'], lw['w_router_lo'], lw['b_router'])


def _router_gates(logits):
    lane = lax.broadcasted_iota(I32, logits.shape, 1)
    lane_f = lane.astype(F32)
    neg = -jnp.inf
    big = 1e9

    def first_lane(mask):
        return jnp.min(jnp.where(mask, lane_f, big), axis=-1, keepdims=True).astype(I32)

    is_grp = lane < N_GROUPS
    gmax = jnp.max(jnp.where(is_grp, logits, neg), axis=-1, keepdims=True)
    sel = first_lane(is_grp & (logits == gmax))
    gsum = jnp.sum(jnp.where(is_grp, jnp.exp(logits - gmax), 0.0), axis=-1, keepdims=True)
    gsel = 1.0 / gsum
    first_exp = N_GROUPS + sel * EXP_PER_GROUP
    in_sel = (lane >= first_exp) & (lane < first_exp + EXP_PER_GROUP)
    v1 = jnp.max(jnp.where(in_sel, logits, neg), axis=-1, keepdims=True)
    i1 = first_lane(in_sel & (logits == v1))
    rest = in_sel & (lane != i1)
    v2 = jnp.max(jnp.where(rest, logits, neg), axis=-1, keepdims=True)
    i2 = first_lane(rest & (logits == v2))
    e2 = jnp.exp(v2 - v1)
    t1 = (1.0 / (1.0 + e2)) * gsel
    t2 = (e2 / (1.0 + e2)) * gsel
    return jnp.where(lane == i1, t1, 0.0) + jnp.where(lane == i2, t2, 0.0), sel


def _weight_stream(jobs, stage_ref, sem):
    nslot = stage_ref.shape[0]
    copy = lambda i: pltpu.make_async_copy(jobs[i][0], stage_ref.at[i % nslot], sem.at[i % nslot])

    def prime():
        for i in range(min(nslot, len(jobs))):
            copy(i).start()

    def drain():
        for i, (_, dst_ref, e) in enumerate(jobs):
            copy(i).wait()
            dst_ref[e] = stage_ref[i % nslot].astype(BF16)
            if i + nslot < len(jobs):
                copy(i + nslot).start()

    return prime, drain


def _moe_kernel(x_ref, lg_ref, w1_hbm, w3_hbm, w2_hbm, h1_ref, mod_ref, lng_ref, lnb_ref, oc_ref, od_ref,
                xs_ref, gs_ref, acc_ref, w1_ref, w3_ref, w2_ref, stage_in_ref, stage_out_ref, sem_in, sem_out,
                *, tm, sub, bounds, alpha, n_ctx_tiles, layer):
    @pl.when(pl.program_id(0) == 0)
    def _():
        jobs_in = [(w.at[layer, e], dst, e) for w, dst in ((w1_hbm, w1_ref), (w3_hbm, w3_ref))
                   for e in range(N_EXPERTS)]
        jobs_out = [(w2_hbm.at[layer, e], w2_ref, e) for e in range(N_EXPERTS)]
        prime_in, drain_in = _weight_stream(jobs_in, stage_in_ref, sem_in)
        prime_out, drain_out = _weight_stream(jobs_out, stage_out_ref, sem_out)
        prime_in()
        prime_out()
        drain_in()
        drain_out()

    gates, sel = _router_gates(lg_ref[...])
    lane = lax.broadcasted_iota(I32, gates.shape, 1)
    onehot = jnp.where(lane == sel, 1.0, 0.0)
    row_i = lax.broadcasted_iota(I32, (tm, tm), 0)
    col_i = lax.broadcasted_iota(I32, (tm, tm), 1)
    earlier = jnp.where(col_i < row_i, 1.0, 0.0).astype(BF16)
    rank = _dot(earlier, onehot.astype(BF16))
    lane1 = lax.broadcasted_iota(I32, (1, ROUTER_LANES), 1)
    counts = jnp.sum(onehot, axis=0, keepdims=True)
    cnt = [jnp.sum(jnp.where(lane1 == g, counts, 0.0)).astype(I32) for g in range(N_GROUPS)]
    off = [jnp.zeros((), I32)]
    for g in range(N_GROUPS - 1):
        off.append(off[g] + cnt[g])
    offs = jnp.zeros((1, ROUTER_LANES), F32)
    for g in range(N_GROUPS):
        offs = jnp.where(lane1 == g, off[g].astype(F32), offs)
    pos = jnp.sum(onehot * (rank + offs), axis=1, keepdims=True).astype(I32)
    pos_row = jnp.broadcast_to(pos, (tm, ROUTER_LANES)).T[0:1, :]
    perm = jnp.where(row_i == pos_row, 1.0, 0.0).astype(BF16)
    perm_t = jnp.where(col_i == pos, 1.0, 0.0).astype(BF16)
    xs_ref[...] = _dot(perm, x_ref[...]).astype(BF16)
    gs_ref[...] = _dot(perm, gates.astype(BF16))
    acc_ref[...] = jnp.zeros_like(acc_ref)

    lane_s = lax.broadcasted_iota(I32, (sub, ROUTER_LANES), 1)
    for s in range(tm // sub):
        rows = pl.ds(s * sub, sub)
        for g in range(N_GROUPS):
            @pl.when((off[g] < (s + 1) * sub) & (off[g] + cnt[g] > s * sub))
            def _():
                xb = xs_ref[rows, :]
                gsb = gs_ref[rows, :]
                hids = []
                for j in range(EXP_PER_GROUP):
                    e = g * EXP_PER_GROUP + j
                    ge = jnp.sum(jnp.where(lane_s == N_GROUPS + e, gsb, 0.0), axis=-1, keepdims=True)
                    h1 = _dot(xb, w1_ref[e])
                    h3 = _dot(xb, w3_ref[e])
                    hids.append(((h1 * _sigmoid(h1)) * h3 * ge).astype(BF16))
                hid = jnp.concatenate(hids, axis=1)
                w2g = w2_ref[g * EXP_PER_GROUP:(g + 1) * EXP_PER_GROUP].reshape(EXP_PER_GROUP * D_EXPERT, D_MODEL)
                acc_ref[rows, :] += _dot(hid, w2g)

    f = _dot(perm_t, acc_ref[...].astype(BF16))
    grp = _group_of_row(pl.program_id(0) * tm, bounds)
    g2 = mod_ref[pl.ds(grp, 1), 5 * D_MODEL:6 * D_MODEL]
    out = _layer_norm(alpha * h1_ref[...] + g2 * f, lng_ref[...], lnb_ref[...])
    is_ctx = pl.program_id(0) < n_ctx_tiles

    @pl.when(is_ctx)
    def _():
        oc_ref[...] = out

    @pl.when(jnp.logical_not(is_ctx))
    def _():
        od_ref[...] = out


def _moe(u2, logits, h1, mods, experts, layer, lw, *, tm, sub, bounds, alpha, n_ctx):
    t = h1.shape[0]
    nct = n_ctx // tm
    rows_c, rows_d = _pair_specs(tm, nct)
    lay = _layer_spec(layer)
    hbm = pl.BlockSpec(memory_space=pl.ANY)
    rows = lambda w: pl.BlockSpec((tm, w), lambda i: (i, 0))
    return pl.pallas_call(
        functools.partial(_moe_kernel, tm=tm, sub=sub, bounds=bounds, alpha=alpha, n_ctx_tiles=nct, layer=layer),
        grid=(t // tm,),
        in_specs=[rows(D_MODEL), rows(ROUTER_LANES), hbm, hbm, hbm,
                  rows(D_MODEL), lay(8, 6 * D_MODEL), lay(1, D_MODEL), lay(1, D_MODEL)],
        out_specs=[rows_c(D_MODEL), rows_d(D_MODEL)],
        out_shape=[jax.ShapeDtypeStruct((n_ctx, D_MODEL), F32), jax.ShapeDtypeStruct((t - n_ctx, D_MODEL), F32)],
        scratch_shapes=[pltpu.VMEM((tm, D_MODEL), BF16), pltpu.VMEM((tm, ROUTER_LANES), F32),
                        pltpu.VMEM((tm, D_MODEL), F32),
                        pltpu.VMEM((N_EXPERTS, D_MODEL, D_EXPERT), BF16),
                        pltpu.VMEM((N_EXPERTS, D_MODEL, D_EXPERT), BF16),
                        pltpu.VMEM((N_EXPERTS, D_EXPERT, D_MODEL), BF16),
                        pltpu.VMEM((WEIGHT_STAGE_SLOTS, D_MODEL, D_EXPERT), F32),
                        pltpu.VMEM((WEIGHT_STAGE_SLOTS, D_EXPERT, D_MODEL), F32),
                        pltpu.SemaphoreType.DMA((WEIGHT_STAGE_SLOTS,)),
                        pltpu.SemaphoreType.DMA((WEIGHT_STAGE_SLOTS,))],
        compiler_params=_cparams(("arbitrary",)),
        name="moe",
    )(u2, logits, experts[0], experts[1], experts[2], h1, mods, lw['ln2_g'], lw['ln2_b'])


def _rope_tables(dec_seq):
    n = HEAD_DIM // 4
    pos = jnp.arange(dec_seq)
    row = (pos // GRID_W).astype(F32)
    col = (pos % GRID_W).astype(F32)
    inv = 1.0 / (ROPE_THETA ** (jnp.arange(n, dtype=F32) / n))
    ra, ca = row[:, None] * inv, col[:, None] * inv
    cos64 = jnp.concatenate([jnp.cos(ra), jnp.cos(ra), jnp.cos(ca), jnp.cos(ca)], axis=-1)
    sin64 = jnp.concatenate([-jnp.sin(ra), jnp.sin(ra), -jnp.sin(ca), jnp.sin(ca)], axis=-1)
    return jnp.concatenate([cos64, cos64], axis=-1), jnp.concatenate([sin64, sin64], axis=-1)


def _block_diag2(w):
    z = jnp.zeros_like(w[:, 0])
    return jnp.concatenate([jnp.concatenate([w[:, 0], z], axis=2), jnp.concatenate([z, w[:, 1]], axis=2)], axis=1)


def _derived_weights(p):
    d = D_MODEL
    depth = p['w_in'].shape[0]
    row = lambda a, w: a.reshape(depth, 1, w)
    ng = len(POOL_WINDOWS)
    pool_blk = (p['pool_w'][:, :, :, None, :] * jnp.eye(ng, dtype=F32)[None, :, None, :, None]).reshape(
        depth, C_WIDTH, C_WIDTH)
    lane_pad = ROUTER_LANES - N_GROUPS - N_EXPERTS
    w_router = jnp.pad(jnp.concatenate([p['rt_grp_w'], p['rt_exp_w']], axis=-1), ((0, 0), (0, 0), (0, lane_pad)))
    b_router = jnp.pad(jnp.concatenate([p['rt_grp_b'], p['rt_exp_b']], axis=-1), ((0, 0), (0, lane_pad)))
    b_router = b_router.reshape(depth, 1, ROUTER_LANES)
    return {
        'w2blk': _block_diag2(p['rw_w2']).astype(BF16),
        'w0': row(p['rw_w0'], 2 * A_WIDTH),
        'a2blk': _block_diag2(p['rw_a2']).astype(BF16),
        'a0': row(p['rw_a0'], 2 * A_WIDTH),
        'g2': p['rw_g2'].astype(BF16),
        'kkw': row(p['rw_kk'], A_WIDTH),
        'ka': jnp.tile(row(p['rw_ka'], A_WIDTH), (1, 1, 2)),
        'rk': row(p['rw_rk'], A_WIDTH),
        'gn_g': row(p['rw_gn_g'], A_WIDTH),
        'gn_b': row(p['rw_gn_b'], A_WIDTH),
        'qn': jnp.tile(row(p['q_norm'], HEAD_DIM), (1, 1, B_Q_HEADS)),
        'knw': jnp.tile(row(p['k_norm'], HEAD_DIM), (1, 1, B_KV_HEADS)),
        'pool_blk': pool_blk.astype(BF16),
        'pool_scale': row(p['pool_scale'], C_WIDTH),
        'ln1_g': row(p['ln1_g'], d), 'ln1_b': row(p['ln1_b'], d),
        'ln2_g': row(p['ln2_g'], d), 'ln2_b': row(p['ln2_b'], d),
        'w_router_hi': w_router.astype(BF16),
        'w_router_lo': (w_router - w_router.astype(BF16).astype(F32)).astype(BF16),
        'b_router': b_router,
    }


def _forward(x_prompt, x_sample, state_rwkv, cache_k, cache_v, c, c_ctx, p):
    batch, seq, d = x_prompt.shape
    dec_batch, dec_seq, _ = x_sample.shape
    depth = p['w_in'].shape[0]
    past = cache_k.shape[2]
    n_ctx = batch * seq
    n_tok = n_ctx + dec_batch * dec_seq
    alpha = (2 * depth) ** 0.25
    bounds = tuple(n_ctx + j * dec_seq for j in range(dec_batch))
    tm = math.gcd(n_ctx, dec_seq, 512)
    tm_moe = math.gcd(n_ctx, dec_seq, 512)
    tq_ctx = min(256, seq)
    tq_dec = min(256, dec_seq)
    sb_ctx = 2

    cond8 = jnp.zeros((8, d), F32).at[0].set(c_ctx).at[1:1 + dec_batch].set(c)
    mods = _modulation(cond8, p['w_mod'], p['b_mod'])
    cos, sin = _rope_tables(dec_seq)
    w_in16, w_out16 = p['w_in'].astype(BF16), p['w_out'].astype(BF16)
    experts = (p['moe_w1'], p['moe_w3'], p['moe_w2'])
    lw = _derived_weights(p)
    head_id = jnp.arange(2 * HEAD_DIM) // HEAD_DIM
    ones = (head_id[:, None] == head_id[None, :]).astype(BF16)

    h = (x_prompt.reshape(n_ctx, d), x_sample.reshape(dec_batch * dec_seq, d))
    ck4 = cache_k.reshape(dec_batch, depth, past, KV_WIDTH)
    cv4 = cache_v.reshape(dec_batch, depth, past, KV_WIDTH)
    new_s, new_k, new_v = [], [], []
    for l in range(depth):
        (r, v, kk, logw, kd, bvec, g, bonus, q, k_att, v_att, pp) = _in_projection(
            h, mods, w_in16, l, lw, cos, sin, ones, tm=tm, bounds=bounds)
        scan_ops = (r, v, kk, logw, kd, bvec)
        yf_c, yb_c, s_fin = _rwkv_scan(scan_ops, None, row0=0, nseq=batch, seqlen=seq, sb=sb_ctx,
                                       nb=seq // CHUNK, want_state=True)
        yf_d, yb_d = _rwkv_scan(scan_ops, state_rwkv[:, l], row0=n_ctx, nseq=dec_batch, seqlen=dec_seq,
                                sb=dec_batch, nb=4, want_state=False)
        yf = (yf_c.reshape(n_ctx, A_WIDTH), yf_d.reshape(n_tok - n_ctx, A_WIDTH))
        yb = (yb_c.reshape(n_ctx, A_WIDTH), yb_d.reshape(n_tok - n_ctx, A_WIDTH))
        att_c, pool_c = _attention(q, k_att, v_att, None, pp, lw['pool_blk'], lw['pool_scale'], l,
                                   row0=0, nseq=batch, seqlen=seq, tq=tq_ctx)
        att_d, pool_d = _attention(q, k_att, v_att, (ck4, cv4), pp, lw['pool_blk'], lw['pool_scale'], l,
                                   row0=n_ctx, nseq=dec_batch, seqlen=dec_seq, tq=tq_dec)
        att, pool = (att_c, att_d), (pool_c, pool_d)
        h1, u2, logits = _out_projection(yf, yb, g, bonus, att, pool, h, mods, w_out16, l, lw, ones,
                                         tm=tm, bounds=bounds, alpha=alpha)
        h = _moe(u2, logits, h1, mods, experts, l, lw, tm=tm_moe, sub=128, bounds=bounds, alpha=alpha,
                 n_ctx=n_ctx)
        new_s.append(s_fin)
        new_k.append(k_att[:n_ctx].reshape(batch, seq, B_KV_HEADS, HEAD_DIM))
        new_v.append(v_att[:n_ctx].reshape(batch, seq, B_KV_HEADS, HEAD_DIM))
    return (h[0].reshape(batch, seq, d), h[1].reshape(dec_batch, dec_seq, d),
            jnp.stack(new_s, axis=1), jnp.stack(new_k, axis=1), jnp.stack(new_v, axis=1))


def kernel(x_prompt, x_sample, state_rwkv, cache_k, cache_v, c, c_ctx, w_in, w_out, w_mod, b_mod, ln1_g, ln1_b, ln2_g, ln2_b, rw_w0, rw_w2, rw_a0, rw_a2, rw_g2, rw_kk, rw_ka, rw_rk, rw_gn_g, rw_gn_b, q_norm, k_norm, pool_w, pool_scale, rt_grp_w, rt_grp_b, rt_exp_w, rt_exp_b, moe_w1, moe_w3, moe_w2):
    p = dict(w_in=w_in, w_out=w_out, w_mod=w_mod, b_mod=b_mod, ln1_g=ln1_g, ln1_b=ln1_b, ln2_g=ln2_g,
             ln2_b=ln2_b, rw_w0=rw_w0, rw_w2=rw_w2, rw_a0=rw_a0, rw_a2=rw_a2, rw_g2=rw_g2, rw_kk=rw_kk,
             rw_ka=rw_ka, rw_rk=rw_rk, rw_gn_g=rw_gn_g, rw_gn_b=rw_gn_b, q_norm=q_norm, k_norm=k_norm,
             pool_w=pool_w, pool_scale=pool_scale, rt_grp_w=rt_grp_w, rt_grp_b=rt_grp_b, rt_exp_w=rt_exp_w,
             rt_exp_b=rt_exp_b, moe_w1=moe_w1, moe_w3=moe_w3, moe_w2=moe_w2)
    return _forward(x_prompt, x_sample, state_rwkv, cache_k, cache_v, c, c_ctx, p)
```

```python
import functools
import math

import jax
import jax.numpy as jnp
from jax import lax
from jax.experimental import pallas as pl
from jax.experimental.pallas import tpu as pltpu

F32 = jnp.float32
BF16 = jnp.bfloat16
I32 = jnp.int32

D_MODEL = 1024
GRID_W = 64
HEAD_DIM = 64
A_HEADS = 6
A_WIDTH = A_HEADS * HEAD_DIM
LORA_W = 64
LORA_A = 64
LORA_G = 128
N_DIR = 2
DECAY_SCALE = 0.606531
B_Q_HEADS = 6
B_KV_HEADS = 2
B_WIDTH = B_Q_HEADS * HEAD_DIM
KV_WIDTH = B_KV_HEADS * HEAD_DIM
ROPE_THETA = 10000.0
POOL_WINDOWS = (2, 4, 8, 16)
C_WIDTH = 256
IN_COLS = 2432
N_GROUPS = 4
EXP_PER_GROUP = 4
N_EXPERTS = 16
D_EXPERT = 256
LN_EPS = 1e-5
GN_EPS = 64e-5
QK_EPS = 1e-6

_O_R, _O_K, _O_V = 0, 384, 768
_O_LW, _O_LA, _O_LG = 1152, 1280, 1408
_O_Q, _O_KB, _O_VB, _O_P = 1536, 1920, 2048, 2176

CHUNK = 64
ROUTER_LANES = 128
WEIGHT_STAGE_SLOTS = 4
VMEM_LIMIT = 56 * 1024 * 1024


def _cparams(sem):
    return pltpu.CompilerParams(dimension_semantics=sem, vmem_limit_bytes=VMEM_LIMIT)


def _sigmoid(x):
    return 1.0 / (1.0 + jnp.exp(-x))


def _dot(a, b):
    return jnp.dot(a, b, preferred_element_type=F32)


def _dot_nt(a, b):
    return lax.dot_general(a, b, (((1,), (1,)), ((), ())), preferred_element_type=F32)


def _segsum(x, ones2):
    x16 = x.astype(BF16)
    slabs = [_dot(x16[:, j:j + 128], ones2) for j in range(0, x.shape[1], 128)]
    return slabs[0] if len(slabs) == 1 else jnp.concatenate(slabs, axis=1)


def _rope128(x, cos, sin):
    lane = lax.broadcasted_iota(I32, x.shape, 1)
    first = (lane % 32) < 16
    up = pltpu.roll(x, 128 - 16, 1)
    dn = pltpu.roll(x, 16, 1)
    return x * cos + jnp.where(first, up, dn) * sin


def _layer_norm(z, g, b):
    mu = jnp.mean(z, axis=-1, keepdims=True)
    zc = z - mu
    var = jnp.mean(zc * zc, axis=-1, keepdims=True)
    return zc * lax.rsqrt(var + LN_EPS) * g + b


def _mod_kernel(c_ref, w_ref, b_ref, o_ref):
    x = c_ref[...]
    x = x * _sigmoid(x)
    w = w_ref[...]
    x_hi = x.astype(BF16)
    x_lo = (x - x_hi.astype(F32)).astype(BF16)
    w_hi = w.astype(BF16)
    w_lo = (w - w_hi.astype(F32)).astype(BF16)
    o_ref[...] = (_dot(x_hi, w_hi) + _dot(x_lo, w_hi) + _dot(x_hi, w_lo)) + b_ref[...]


def _modulation(cond8, w_mod, b_mod):
    depth = w_mod.shape[0]
    tn = 1536
    return pl.pallas_call(
        _mod_kernel,
        grid=(depth, 6 * D_MODEL // tn),
        in_specs=[
            pl.BlockSpec((8, D_MODEL), lambda l, j: (0, 0)),
            pl.BlockSpec((None, D_MODEL, tn), lambda l, j: (l, 0, j)),
            pl.BlockSpec((None, 1, tn), lambda l, j: (l, 0, j)),
        ],
        out_specs=pl.BlockSpec((None, 8, tn), lambda l, j: (l, 0, j)),
        out_shape=jax.ShapeDtypeStruct((depth, 8, 6 * D_MODEL), F32),
        compiler_params=_cparams(("arbitrary", "arbitrary")),
        name="modulation",
    )(cond8, w_mod, b_mod.reshape(depth, 1, 6 * D_MODEL))


def _group_of_row(row0, bounds):
    g = jnp.zeros((), I32)
    for b in bounds:
        g = g + jnp.where(row0 >= b, 1, 0).astype(I32)
    return g


def _in_kernel(hc_ref, hd_ref, mod_ref, w_ref, w2_ref, w0_ref, a2_ref, a0_ref, g2_ref, kkw_ref, ka_ref,
               rk_ref, qn_ref, knw_ref, cos_ref, sin_ref, ones_ref,
               r_o, v_o, kk_o, lw_o, kd_o, b_o, g_o, bon_o, q_o, k_o, vb_o, p_o, *, tm, bounds, n_ctx_tiles):
    grp = _group_of_row(pl.program_id(0) * tm, bounds)
    mod = mod_ref[pl.ds(grp, 1), :]
    sh1 = mod[:, 0:D_MODEL]
    sc1 = mod[:, D_MODEL:2 * D_MODEL]
    is_ctx = pl.program_id(0) < n_ctx_tiles
    h = jnp.where(is_ctx, hc_ref[...], hd_ref[...])
    u = h * (1.0 + sc1) + sh1
    ones = ones_ref[...]
    scale = 1.0 / math.sqrt(HEAD_DIM)
    proj = _dot(u.astype(BF16), w_ref[...])

    r = proj[:, _O_R:_O_R + A_WIDTH]
    k = proj[:, _O_K:_O_K + A_WIDTH]
    v = proj[:, _O_V:_O_V + A_WIDTH]
    lo_w = proj[:, _O_LW:_O_LW + 128]
    lo_a = proj[:, _O_LA:_O_LA + 128]
    lo_g = proj[:, _O_LG:_O_LG + 128]

    w_pre = w0_ref[...] + _dot(jnp.tanh(lo_w).astype(BF16), w2_ref[...])
    logw = -DECAY_SCALE * _sigmoid(w_pre)
    a = _sigmoid(a0_ref[...] + _dot(lo_a.astype(BF16), a2_ref[...]))
    g = _dot(_sigmoid(lo_g).astype(BF16), g2_ref[...])

    kk = k * kkw_ref[...]
    kk = kk * lax.rsqrt(jnp.maximum(_segsum(kk * kk, ones), 1e-24))
    k2 = jnp.concatenate([k, k], axis=1)
    kd = k2 * (1.0 + (a - 1.0) * ka_ref[...])
    bvec = jnp.concatenate([kk, kk], axis=1) * a
    kd_sum = kd[:, 0:A_WIDTH] + kd[:, A_WIDTH:2 * A_WIDTH]
    bonus = _segsum(r * kd_sum * rk_ref[...], ones) * v

    r_o[...] = r.astype(BF16)
    v_o[...] = v.astype(BF16)
    kk_o[...] = kk.astype(BF16)
    lw_o[...] = logw
    kd_o[...] = kd.astype(BF16)
    b_o[...] = bvec.astype(BF16)
    g_o[...] = g
    bon_o[...] = bonus

    cos = cos_ref[...]
    sin = sin_ref[...]
    rope = lambda x: jnp.where(is_ctx, x, _rope128(x, cos, sin))
    q = proj[:, _O_Q:_O_Q + B_WIDTH]
    qn = q * lax.rsqrt(_segsum(q * q, ones) * (1.0 / HEAD_DIM) + QK_EPS) * qn_ref[...]
    q_o[...] = (jnp.concatenate([rope(qn[:, j:j + 128]) for j in range(0, B_WIDTH, 128)], axis=1)
                * scale).astype(BF16)
    kb = proj[:, _O_KB:_O_KB + KV_WIDTH]
    kn = kb * lax.rsqrt(_segsum(kb * kb, ones) * (1.0 / HEAD_DIM) + QK_EPS) * knw_ref[...]
    k_o[...] = rope(kn)
    vb_o[...] = proj[:, _O_VB:_O_VB + KV_WIDTH]
    p_o[...] = proj[:, _O_P:_O_P + C_WIDTH]


def _pair_specs(tm, nct):
    rows_c = lambda w: pl.BlockSpec((tm, w), lambda i: (jnp.minimum(i, nct - 1), 0))
    rows_d = lambda w: pl.BlockSpec((tm, w), lambda i: (jnp.maximum(i - nct, 0), 0))
    return rows_c, rows_d


def _layer_spec(layer):
    return lambda *shape: pl.BlockSpec((None,) + shape, lambda *_: (layer,) + (0,) * len(shape))


def _in_projection(h, mods, w_in, layer, lw, cos, sin, ones, *, tm, bounds):
    nct = h[0].shape[0] // tm
    t = h[0].shape[0] + h[1].shape[0]
    full = lambda shape: pl.BlockSpec(shape, lambda i: (0,) * len(shape))
    rows = lambda w: pl.BlockSpec((tm, w), lambda i: (i, 0))
    rows_c, rows_d = _pair_specs(tm, nct)
    lay = _layer_spec(layer)
    tiles_per_req = cos.shape[0] // tm
    table = pl.BlockSpec((tm, 128), lambda i: (jnp.maximum(i - nct, 0) % tiles_per_req, 0))
    widths = (A_WIDTH, A_WIDTH, A_WIDTH, 2 * A_WIDTH, 2 * A_WIDTH, 2 * A_WIDTH, A_WIDTH, A_WIDTH,
              B_WIDTH, KV_WIDTH, KV_WIDTH, C_WIDTH)
    dtypes = (BF16, BF16, BF16, F32, BF16, BF16, F32, F32, BF16, F32, F32, F32)
    return pl.pallas_call(
        functools.partial(_in_kernel, tm=tm, bounds=bounds, n_ctx_tiles=nct),
        grid=(t // tm,),
        in_specs=[
            rows_c(D_MODEL), rows_d(D_MODEL), lay(8, 6 * D_MODEL), lay(D_MODEL, IN_COLS),
            lay(128, 2 * A_WIDTH), lay(1, 2 * A_WIDTH), lay(128, 2 * A_WIDTH), lay(1, 2 * A_WIDTH),
            lay(LORA_G, A_WIDTH), lay(1, A_WIDTH), lay(1, 2 * A_WIDTH), lay(1, A_WIDTH),
            lay(1, B_WIDTH), lay(1, KV_WIDTH), table, table, full((2 * HEAD_DIM, 2 * HEAD_DIM)),
        ],
        out_specs=[rows(w) for w in widths],
        out_shape=[jax.ShapeDtypeStruct((t, w), dt) for w, dt in zip(widths, dtypes)],
        compiler_params=_cparams(("arbitrary",)),
        name="in_projection",
    )(h[0], h[1], mods, w_in, lw['w2blk'], lw['w0'], lw['a2blk'], lw['a0'], lw['g2'], lw['kkw'], lw['ka'],
      lw['rk'], lw['qn'], lw['knw'], cos, sin, ones)


def _split2(x):
    hi = x.astype(BF16)
    lo = (x - hi.astype(F32)).astype(BF16)
    return hi, lo


def _scan_kernel(*refs, sb, nb, zero_init, want_state):
    fwd = refs[0:6]
    bwd = refs[6:12]
    pos = 12
    if not zero_init:
        s0_ref = refs[pos]
        pos += 1
    y_outs = (refs[pos], refs[pos + 1])
    pos += 2
    if want_state:
        sfin_o = refs[pos]
        pos += 1
    h_ref = refs[pos]

    step = pl.program_id(1)
    nstep = pl.num_programs(1)
    npair = A_HEADS // 2
    pw = 2 * HEAD_DIM
    pairs = [(q, d, hp) for q in range(sb) for d in range(N_DIR) for hp in range(npair)]

    @pl.when(step == 0)
    def _():
        h_ref[...] = jnp.zeros_like(h_ref)
        if not zero_init:
            for (q, d, hp) in pairs:
                for e in range(2):
                    lo = e * HEAD_DIM
                    h_ref[q, d, hp, lo:lo + HEAD_DIM, lo:lo + HEAD_DIM] = s0_ref[q, d, 2 * hp + e].T

    def iota(shape, dim):
        return lax.broadcasted_iota(I32, shape, dim)

    first64 = iota((CHUNK, pw), 1) < HEAD_DIM
    eye_p = ((iota((CHUNK, pw), 1) % HEAD_DIM) == iota((CHUNK, pw), 0)).astype(F32)
    eye2 = (iota((pw, pw), 0) == iota((pw, pw), 1)).astype(F32)
    same_head = (iota((pw, pw), 0) < HEAD_DIM) == (iota((pw, pw), 1) < HEAD_DIM)
    ti = iota((CHUNK, CHUNK), 0)
    si = iota((CHUNK, CHUNK), 1)
    incl16 = ((si <= ti).astype(BF16), (si >= ti).astype(BF16))
    t4 = iota((2 * CHUNK, 2 * pw), 0)
    s4 = iota((2 * CHUNK, 2 * pw), 1) % CHUNK
    incl_off = jnp.where(t4 < CHUNK, 0, 1)
    t4 = t4 % CHUNK
    mask4 = (s4 < t4 + incl_off, s4 > t4 - incl_off)

    def bdiag(x):
        return jnp.concatenate([jnp.where(first64, x, 0.0), jnp.where(first64, 0.0, x)], axis=0)

    lhs, rhs1, bkt, gcol, v_p, amat = {}, {}, {}, {}, {}, {}
    blocks = [(q, d, j) for q in range(sb) for d in range(N_DIR) for j in range(nb)]
    csum = {}
    for (q, d, j) in blocks:
        hi, lo = _split2((fwd, bwd)[d][3][q, pl.ds(j * CHUNK, CHUNK), :])
        csum[(q, d, j)] = _dot(incl16[d], hi) + _dot(incl16[d], lo)
    for (q, d, j) in blocks:
        r_ref, v_ref, kk_ref, lw_ref, kd_ref, b_ref = (fwd, bwd)[d]
        rows = pl.ds(j * CHUNK, CHUNK)
        lw = lw_ref[q, rows, :]
        cs = csum[(q, d, j)]
        gam = jnp.exp(cs)
        gam_prev = jnp.exp(cs - lw)
        gam_inv = jnp.exp(-cs)
        gam_end = jnp.exp(cs[CHUNK - 1:CHUNK, :] if d == 0 else cs[0:1, :])
        a_hat = -kk_ref[q, rows, :] * gam_prev
        r_hat = r_ref[q, rows, :] * gam
        b_hat = b_ref[q, rows, :] * gam_inv
        k_hat = kd_ref[q, rows, :] * gam_inv
        v_all = v_ref[q, rows, :]
        for hp in range(npair):
            key = (q, d, hp, j)
            sl = slice(hp * pw, (hp + 1) * pw)
            lhs[key] = jnp.concatenate([a_hat[:, sl], r_hat[:, sl]], axis=0).astype(BF16)
            bh, kh, ge = b_hat[:, sl], k_hat[:, sl], gam_end[:, sl]
            rhs1[key] = jnp.concatenate([bdiag(bh), bdiag(kh)], axis=0).astype(BF16)
            gcol[key] = jnp.sum(eye2 * ge, axis=1, keepdims=True)
            bkt[key] = (jnp.concatenate([bh, kh], axis=0) * ge).T.astype(BF16)
            v_p[key] = v_all[:, sl]
    keys = list(lhs.keys())
    for key in keys:
        amat[key] = _dot_nt(lhs[key], rhs1[key])
    npow, tcur, a_rb, lhs2, vbd = {}, {}, {}, {}, {}
    for key in keys:
        am = jnp.where(mask4[key[1]], amat[key], 0.0)
        npow[key] = am[0:CHUNK, 0:pw]
        tcur[key] = eye_p + npow[key]
        a_rb[key] = am[CHUNK:2 * CHUNK, 0:pw].astype(BF16)
        vbd[key] = bdiag(v_p[key])
        lhs2[key] = jnp.concatenate([lhs[key], am[:, pw:2 * pw].astype(BF16)], axis=1)
    nsq = int(math.log2(CHUNK)) - 1
    for key in keys:
        npow[key] = _dot(npow[key].astype(BF16), bdiag(npow[key]).astype(BF16))
    for _ in range(nsq - 1):
        for key in keys:
            rhs = jnp.concatenate([bdiag(npow[key]), bdiag(tcur[key])], axis=1).astype(BF16)
            res = _dot(npow[key].astype(BF16), rhs)
            npow[key] = res[:, 0:pw]
            tcur[key] = tcur[key] + res[:, pw:2 * pw]
    t_inv = {}
    for key in keys:
        t_inv[key] = (tcur[key] + _dot(npow[key].astype(BF16), bdiag(tcur[key]).astype(BF16))).astype(BF16)

    hcur = {pr: h_ref[pr[0], pr[1], pr[2]] for pr in pairs}
    for jj in range(nb):
        z, u = {}, {}
        kof = lambda pr: (pr[0], pr[1], pr[2], jj if pr[1] == 0 else nb - 1 - jj)
        for pr in pairs:
            key = kof(pr)
            hv = jnp.concatenate([hcur[pr].astype(BF16), vbd[key]], axis=0)
            z[pr] = _dot(lhs2[key], hv)
        for pr in pairs:
            u[pr] = _dot(t_inv[kof(pr)], bdiag(z[pr][0:CHUNK]).astype(BF16))
        for pr in pairs:
            key = kof(pr)
            q, d, hp, j = key
            y = z[pr][CHUNK:2 * CHUNK] + _dot(a_rb[key], bdiag(u[pr]).astype(BF16))
            y_outs[d][q, pl.ds(j * CHUNK, CHUNK), hp * pw:(hp + 1) * pw] = y
            uv = jnp.concatenate([u[pr].astype(BF16), v_p[key]], axis=0)
            hcur[pr] = hcur[pr] * gcol[key] + jnp.where(same_head, _dot(bkt[key], uv), 0.0)
    for pr in pairs:
        h_ref[pr[0], pr[1], pr[2]] = hcur[pr]

    if want_state:
        @pl.when(step == nstep - 1)
        def _():
            for (q, d, hp) in pairs:
                for e in range(2):
                    lo = e * HEAD_DIM
                    sfin_o[q, d, 2 * hp + e] = hcur[(q, d, hp)][lo:lo + HEAD_DIM, lo:lo + HEAD_DIM].T


def _rwkv_scan(ops, s0, *, row0, nseq, seqlen, sb, nb, want_state):
    g = nb * CHUNK
    nstep = seqlen // g
    zero_init = s0 is None
    ops3 = [a.reshape(a.shape[0] // seqlen, seqlen, a.shape[1]) for a in ops]
    base = row0 // (seqlen * sb)
    fidx = lambda col: (lambda s, c: (base + s, c, col))
    bidx = lambda col: (lambda s, c: (base + s, nstep - 1 - c, col))
    blk = lambda im: pl.BlockSpec((sb, g, A_WIDTH), im)
    in_specs = [blk(fidx(0))] * 6 + [blk(bidx(0))] * 3 + [blk(bidx(1))] * 3
    args = ops3 + ops3
    state_spec = pl.BlockSpec((sb, N_DIR, A_HEADS, HEAD_DIM, HEAD_DIM), lambda s, c: (s, 0, 0, 0, 0))
    if not zero_init:
        in_specs.append(state_spec)
        args.append(s0)
    out_specs = [blk(lambda s, c: (s, c, 0)), blk(lambda s, c: (s, nstep - 1 - c, 0))]
    out_shape = [jax.ShapeDtypeStruct((nseq, seqlen, A_WIDTH), F32)] * 2
    if want_state:
        out_specs.append(state_spec)
        out_shape.append(jax.ShapeDtypeStruct((nseq, N_DIR, A_HEADS, HEAD_DIM, HEAD_DIM), F32))
    return pl.pallas_call(
        functools.partial(_scan_kernel, sb=sb, nb=nb, zero_init=zero_init, want_state=want_state),
        grid=(nseq // sb, nstep),
        in_specs=in_specs,
        out_specs=out_specs,
        out_shape=out_shape,
        scratch_shapes=[pltpu.VMEM((sb, N_DIR, A_HEADS // 2, 2 * HEAD_DIM, 2 * HEAD_DIM), F32)],
        compiler_params=_cparams(("arbitrary", "arbitrary")),
        name="rwkv_scan_ctx" if zero_init else "rwkv_scan_dec",
    )(*args)


def _attn_kernel(*refs, with_cache):
    grp = B_Q_HEADS // B_KV_HEADS
    if with_cache:
        q_ref, k_ref, v_ref, ck_ref, cv_ref, p_ref, pw_ref, psc_ref, o_ref, pool_o = refs
        kall = jnp.concatenate([k_ref[...], ck_ref[...]], axis=0)
        vall = jnp.concatenate([v_ref[...], cv_ref[...]], axis=0)
    else:
        q_ref, k_ref, v_ref, p_ref, pw_ref, psc_ref, o_ref, pool_o = refs
        kall = k_ref[...]
        vall = v_ref[...]

    @pl.when(pl.program_id(1) == 0)
    def _():
        pool_o[...] = _pool_mix(p_ref[...], pw_ref[...], psc_ref[...]).astype(BF16)

    q = q_ref[...]
    k16 =[kall[:, hk * HEAD_DIM:(hk + 1) * HEAD_DIM].astype(BF16) for hk in range(B_KV_HEADS)]
    v16 = [vall[:, hk * HEAD_DIM:(hk + 1) * HEAD_DIM].astype(BF16) for hk in range(B_KV_HEADS)]

    def scores(j):
        return _dot_nt(q[:, j * HEAD_DIM:(j + 1) * HEAD_DIM], k16[j // grp])

    outs = []
    s_next = scores(0)
    for j in range(B_Q_HEADS):
        s = s_next
        if j + 1 < B_Q_HEADS:
            s_next = scores(j + 1)
        m = jnp.max(s, axis=-1, keepdims=True)
        p = jnp.exp(s - m)
        den = jnp.sum(p, axis=-1, keepdims=True)
        outs.append(_dot(p.astype(BF16), v16[j // grp]) / den)
    o_ref[...] = jnp.concatenate(outs, axis=1).astype(BF16)


def _attention(q, k_tok, v_tok, cache, p_tok, pool_w, pool_scale, layer, *, row0, nseq, seqlen, tq):
    nq = seqlen // tq
    base = row0 // tq
    kbase = row0 // seqlen
    kspec = pl.BlockSpec((seqlen, KV_WIDTH), lambda s, i: (kbase + s, 0))
    in_specs = [pl.BlockSpec((tq, B_WIDTH), lambda s, i: (base + s * nq + i, 0)), kspec, kspec]
    args = [q, k_tok, v_tok]
    if cache is not None:
        ck, cv = cache
        cspec = pl.BlockSpec((None, None, ck.shape[2], KV_WIDTH), lambda s, i: (s, layer, 0, 0))
        in_specs += [cspec, cspec]
        args += [ck, cv]
    lay = _layer_spec(layer)
    in_specs += [pl.BlockSpec((seqlen, C_WIDTH), lambda s, i: (kbase + s, 0)), lay(C_WIDTH, C_WIDTH), lay(1, C_WIDTH)]
    args += [p_tok, pool_w, pool_scale]
    return pl.pallas_call(
        functools.partial(_attn_kernel, with_cache=cache is not None),
        grid=(nseq, nq),
        in_specs=in_specs,
        out_specs=[pl.BlockSpec((tq, B_WIDTH), lambda s, i: (s * nq + i, 0)),
                   pl.BlockSpec((seqlen, C_WIDTH), lambda s, i: (s, 0))],
        out_shape=[jax.ShapeDtypeStruct((nseq * seqlen, B_WIDTH), BF16),
                   jax.ShapeDtypeStruct((nseq * seqlen, C_WIDTH), BF16)],
        compiler_params=_cparams(("arbitrary", "arbitrary")),
        name="attention_ctx" if cache is None else "attention_dec",
    )(*args)


def _pool_mix(x, w_blk, scale):
    seqlen = x.shape[0]
    t = lax.broadcasted_iota(I32, x.shape, 0)
    lane = lax.broadcasted_iota(I32, x.shape, 1)

    def down(a, k):
        return jnp.where(t >= k, pltpu.roll(a, k, 0), 0.0)

    def up(a, k):
        return jnp.where(t < seqlen - k, pltpu.roll(a, seqlen - k, 0), 0.0)

    past = [x]
    futr = [x]
    for j in range(len(POOL_WINDOWS) - 1):
        past.append(past[j] + down(past[j], 2 ** j))
        futr.append(futr[j] + up(futr[j], 2 ** j))
    win_sum = jnp.zeros_like(x)
    cnt = jnp.zeros_like(x)
    for gi, win in enumerate(POOL_WINDOWS):
        half = win // 2
        in_group = (lane // HEAD_DIM) == gi
        win_sum = jnp.where(in_group, down(past[gi], 1) + futr[gi], win_sum)
        n = (jnp.minimum(t + half, seqlen) - jnp.maximum(t - half, 0)).astype(F32)
        cnt = jnp.where(in_group, n, cnt)
    d = win_sum / cnt - x
    return _dot(d.astype(BF16), w_blk) * scale


def _out_kernel(yfc_ref, yfd_ref, ybc_ref, ybd_ref, g_ref, bon_ref, attc_ref, attd_ref, poolc_ref, poold_ref,
                hc_ref, hd_ref, mod_ref, w_ref, lng_ref, lnb_ref, gng_ref, gnb_ref, ones_ref, wrh_ref, wrl_ref,
                br_ref, h1_o, u2_o, lg_o, *, tm, bounds, alpha, n_ctx_tiles):
    grp = _group_of_row(pl.program_id(0) * tm, bounds)
    mod = mod_ref[pl.ds(grp, 1), :]
    g1 = mod[:, 2 * D_MODEL:3 * D_MODEL]
    sh2 = mod[:, 3 * D_MODEL:4 * D_MODEL]
    sc2 = mod[:, 4 * D_MODEL:5 * D_MODEL]
    ones = ones_ref[...]
    is_ctx = pl.program_id(0) < n_ctx_tiles
    half = tm // 2
    halves = [pl.ds(s * half, half) for s in range(2)]
    pick = lambda c_ref, d_ref, rows: jnp.where(is_ctx, c_ref[rows, :], d_ref[rows, :])
    y = [pick(yfc_ref, yfd_ref, rows) + pick(ybc_ref, ybd_ref, rows) for rows in halves]
    mu = [_segsum(v, ones) * (1.0 / HEAD_DIM) for v in y]
    yc = [v - m_ for v, m_ in zip(y, mu)]
    var = [_segsum(v * v, ones) * (1.0 / HEAD_DIM) for v in yc]
    m = []
    for rows, v, vr in zip(halves, yc, var):
        yn = v * lax.rsqrt(vr + GN_EPS) * gng_ref[...] + gnb_ref[...]
        a_out = (yn + bon_ref[rows, :]) * g_ref[rows, :]
        m.append(_dot(a_out.astype(BF16), w_ref[0:A_WIDTH, :])
                 + _dot(pick(attc_ref, attd_ref, rows), w_ref[A_WIDTH:A_WIDTH + B_WIDTH, :])
                 + _dot(pick(poolc_ref, poold_ref, rows), w_ref[A_WIDTH + B_WIDTH:D_MODEL, :]))
    for rows, m_ in zip(halves, m):
        h1 = _layer_norm(alpha * pick(hc_ref, hd_ref, rows) + g1 * m_, lng_ref[...], lnb_ref[...])
        h1_o[rows, :] = h1
        u2 = h1 * (1.0 + sc2) + sh2
        u_hi = u2.astype(BF16)
        u_lo = (u2 - u_hi.astype(F32)).astype(BF16)
        u2_o[rows, :] = u_hi
        lg_o[rows, :] = (_dot(u_hi, wrh_ref[...]) + _dot(u_lo, wrh_ref[...]) + _dot(u_hi, wrl_ref[...])) + br_ref[...]


def _out_projection(yf, yb, g, bonus, att, pool, h, mods, w_out, layer, lw, ones, *, tm, bounds, alpha):
    t = h[0].shape[0] + h[1].shape[0]
    nct = h[0].shape[0] // tm
    full = lambda shape: pl.BlockSpec(shape, lambda i: (0,) * len(shape))
    rows = lambda w: pl.BlockSpec((tm, w), lambda i: (i, 0))
    rows_c, rows_d = _pair_specs(tm, nct)
    lay = _layer_spec(layer)
    return pl.pallas_call(
        functools.partial(_out_kernel, tm=tm, bounds=bounds, alpha=alpha, n_ctx_tiles=nct),
        grid=(t // tm,),
        in_specs=[rows_c(A_WIDTH), rows_d(A_WIDTH), rows_c(A_WIDTH), rows_d(A_WIDTH), rows(A_WIDTH), rows(A_WIDTH),
                  rows_c(B_WIDTH), rows_d(B_WIDTH), rows_c(C_WIDTH), rows_d(C_WIDTH), rows_c(D_MODEL),
                  rows_d(D_MODEL), lay(8, 6 * D_MODEL), lay(D_MODEL, D_MODEL), lay(1, D_MODEL),
                  lay(1, D_MODEL), lay(1, A_WIDTH), lay(1, A_WIDTH), full((2 * HEAD_DIM, 2 * HEAD_DIM)),
                  lay(D_MODEL, ROUTER_LANES), lay(D_MODEL, ROUTER_LANES), lay(1, ROUTER_LANES)],
        out_specs=[rows(D_MODEL), rows(D_MODEL), rows(ROUTER_LANES)],
        out_shape=[jax.ShapeDtypeStruct((t, D_MODEL), F32), jax.ShapeDtypeStruct((t, D_MODEL), BF16),
                   jax.ShapeDtypeStruct((t, ROUTER_LANES), F32)],
        compiler_params=_cparams(("arbitrary",)),
        name="out_projection",
    )(yf[0], yf[1], yb[0], yb[1], g, bonus, att[0], att[1], pool[0], pool[1], h[0], h[1], mods, w_out,
      lw['ln1_g'], lw['ln1_b'], lw['gn_g'], lw['gn_b'], ones, lw['w_router_hi'], lw['w_router_lo'], lw['b_router'])


def _router_gates(logits):
    lane = lax.broadcasted_iota(I32, logits.shape, 1)
    lane_f = lane.astype(F32)
    neg = -jnp.inf
    big = 1e9

    def first_lane(mask):
        return jnp.min(jnp.where(mask, lane_f, big), axis=-1, keepdims=True).astype(I32)

    is_grp = lane < N_GROUPS
    gmax = jnp.max(jnp.where(is_grp, logits, neg), axis=-1, keepdims=True)
    sel = first_lane(is_grp & (logits == gmax))
    gsum = jnp.sum(jnp.where(is_grp, jnp.exp(logits - gmax), 0.0), axis=-1, keepdims=True)
    gsel = 1.0 / gsum
    first_exp = N_GROUPS + sel * EXP_PER_GROUP
    in_sel = (lane >= first_exp) & (lane < first_exp + EXP_PER_GROUP)
    v1 = jnp.max(jnp.where(in_sel, logits, neg), axis=-1, keepdims=True)
    i1 = first_lane(in_sel & (logits == v1))
    rest = in_sel & (lane != i1)
    v2 = jnp.max(jnp.where(rest, logits, neg), axis=-1, keepdims=True)
    i2 = first_lane(rest & (logits == v2))
    e2 = jnp.exp(v2 - v1)
    t1 = (1.0 / (1.0 + e2)) * gsel
    t2 = (e2 / (1.0 + e2)) * gsel
    return jnp.where(lane == i1, t1, 0.0) + jnp.where(lane == i2, t2, 0.0), sel


def _weight_stream(jobs, stage_ref, sem):
    nslot = stage_ref.shape[0]
    copy = lambda i: pltpu.make_async_copy(jobs[i][0], stage_ref.at[i % nslot], sem.at[i % nslot])

    def prime():
        for i in range(min(nslot, len(jobs))):
            copy(i).start()

    def drain():
        for i, (_, dst_ref, e) in enumerate(jobs):
            copy(i).wait()
            dst_ref[e] = stage_ref[i % nslot].astype(BF16)
            if i + nslot < len(jobs):
                copy(i + nslot).start()

    return prime, drain


def _moe_kernel(x_ref, lg_ref, w1_hbm, w3_hbm, w2_hbm, h1_ref, mod_ref, lng_ref, lnb_ref, oc_ref, od_ref,
                xs_ref, gs_ref, acc_ref, w1_ref, w3_ref, w2_ref, stage_in_ref, stage_out_ref, sem_in, sem_out,
                *, tm, sub, bounds, alpha, n_ctx_tiles, layer):
    @pl.when(pl.program_id(0) == 0)
    def _():
        jobs_in = [(w.at[layer, e], dst, e) for w, dst in ((w1_hbm, w1_ref), (w3_hbm, w3_ref))
                   for e in range(N_EXPERTS)]
        jobs_out = [(w2_hbm.at[layer, e], w2_ref, e) for e in range(N_EXPERTS)]
        prime_in, drain_in = _weight_stream(jobs_in, stage_in_ref, sem_in)
        prime_out, drain_out = _weight_stream(jobs_out, stage_out_ref, sem_out)
        prime_in()
        prime_out()
        drain_in()
        drain_out()

    gates, sel = _router_gates(lg_ref[...])
    lane = lax.broadcasted_iota(I32, gates.shape, 1)
    onehot = jnp.where(lane == sel, 1.0, 0.0)
    row_i = lax.broadcasted_iota(I32, (tm, tm), 0)
    col_i = lax.broadcasted_iota(I32, (tm, tm), 1)
    earlier = jnp.where(col_i < row_i, 1.0, 0.0).astype(BF16)
    rank = _dot(earlier, onehot.astype(BF16))
    lane1 = lax.broadcasted_iota(I32, (1, ROUTER_LANES), 1)
    counts = jnp.sum(onehot, axis=0, keepdims=True)
    cnt = [jnp.sum(jnp.where(lane1 == g, counts, 0.0)).astype(I32) for g in range(N_GROUPS)]
    off = [jnp.zeros((), I32)]
    for g in range(N_GROUPS - 1):
        off.append(off[g] + cnt[g])
    offs = jnp.zeros((1, ROUTER_LANES), F32)
    for g in range(N_GROUPS):
        offs = jnp.where(lane1 == g, off[g].astype(F32), offs)
    pos = jnp.sum(onehot * (rank + offs), axis=1, keepdims=True).astype(I32)
    pos_row = jnp.broadcast_to(pos, (tm, ROUTER_LANES)).T[0:1, :]
    perm = jnp.where(row_i == pos_row, 1.0, 0.0).astype(BF16)
    perm_t = jnp.where(col_i == pos, 1.0, 0.0).astype(BF16)
    xs_ref[...] = _dot(perm, x_ref[...]).astype(BF16)
    gs_ref[...] = _dot(perm, gates.astype(BF16))
    acc_ref[...] = jnp.zeros_like(acc_ref)

    lane_s = lax.broadcasted_iota(I32, (sub, ROUTER_LANES), 1)
    for s in range(tm // sub):
        rows = pl.ds(s * sub, sub)
        for g in range(N_GROUPS):
            @pl.when((off[g] < (s + 1) * sub) & (off[g] + cnt[g] > s * sub))
            def _():
                xb = xs_ref[rows, :]
                gsb = gs_ref[rows, :]
                hids = []
                for j in range(EXP_PER_GROUP):
                    e = g * EXP_PER_GROUP + j
                    ge = jnp.sum(jnp.where(lane_s == N_GROUPS + e, gsb, 0.0), axis=-1, keepdims=True)
                    h1 = _dot(xb, w1_ref[e])
                    h3 = _dot(xb, w3_ref[e])
                    hids.append(((h1 * _sigmoid(h1)) * h3 * ge).astype(BF16))
                hid = jnp.concatenate(hids, axis=1)
                w2g = w2_ref[g * EXP_PER_GROUP:(g + 1) * EXP_PER_GROUP].reshape(EXP_PER_GROUP * D_EXPERT, D_MODEL)
                acc_ref[rows, :] += _dot(hid, w2g)

    f = _dot(perm_t, acc_ref[...].astype(BF16))
    grp = _group_of_row(pl.program_id(0) * tm, bounds)
    g2 = mod_ref[pl.ds(grp, 1), 5 * D_MODEL:6 * D_MODEL]
    out = _layer_norm(alpha * h1_ref[...] + g2 * f, lng_ref[...], lnb_ref[...])
    is_ctx = pl.program_id(0) < n_ctx_tiles

    @pl.when(is_ctx)
    def _():
        oc_ref[...] = out

    @pl.when(jnp.logical_not(is_ctx))
    def _():
        od_ref[...] = out


def _moe(u2, logits, h1, mods, experts, layer, lw, *, tm, sub, bounds, alpha, n_ctx):
    t = h1.shape[0]
    nct = n_ctx // tm
    rows_c, rows_d = _pair_specs(tm, nct)
    lay = _layer_spec(layer)
    hbm = pl.BlockSpec(memory_space=pl.ANY)
    rows = lambda w: pl.BlockSpec((tm, w), lambda i: (i, 0))
    return pl.pallas_call(
        functools.partial(_moe_kernel, tm=tm, sub=sub, bounds=bounds, alpha=alpha, n_ctx_tiles=nct, layer=layer),
        grid=(t // tm,),
        in_specs=[rows(D_MODEL), rows(ROUTER_LANES), hbm, hbm, hbm,
                  rows(D_MODEL), lay(8, 6 * D_MODEL), lay(1, D_MODEL), lay(1, D_MODEL)],
        out_specs=[rows_c(D_MODEL), rows_d(D_MODEL)],
        out_shape=[jax.ShapeDtypeStruct((n_ctx, D_MODEL), F32), jax.ShapeDtypeStruct((t - n_ctx, D_MODEL), F32)],
        scratch_shapes=[pltpu.VMEM((tm, D_MODEL), BF16), pltpu.VMEM((tm, ROUTER_LANES), F32),
                        pltpu.VMEM((tm, D_MODEL), F32),
                        pltpu.VMEM((N_EXPERTS, D_MODEL, D_EXPERT), BF16),
                        pltpu.VMEM((N_EXPERTS, D_MODEL, D_EXPERT), BF16),
                        pltpu.VMEM((N_EXPERTS, D_EXPERT, D_MODEL), BF16),
                        pltpu.VMEM((WEIGHT_STAGE_SLOTS, D_MODEL, D_EXPERT), F32),
                        pltpu.VMEM((WEIGHT_STAGE_SLOTS, D_EXPERT, D_MODEL), F32),
                        pltpu.SemaphoreType.DMA((WEIGHT_STAGE_SLOTS,)),
                        pltpu.SemaphoreType.DMA((WEIGHT_STAGE_SLOTS,))],
        compiler_params=_cparams(("arbitrary",)),
        name="moe",
    )(u2, logits, experts[0], experts[1], experts[2], h1, mods, lw['ln2_g'], lw['ln2_b'])


def _rope_tables(dec_seq):
    n = HEAD_DIM // 4
    pos = jnp.arange(dec_seq)
    row = (pos // GRID_W).astype(F32)
    col = (pos % GRID_W).astype(F32)
    inv = 1.0 / (ROPE_THETA ** (jnp.arange(n, dtype=F32) / n))
    ra, ca = row[:, None] * inv, col[:, None] * inv
    cos64 = jnp.concatenate([jnp.cos(ra), jnp.cos(ra), jnp.cos(ca), jnp.cos(ca)], axis=-1)
    sin64 = jnp.concatenate([-jnp.sin(ra), jnp.sin(ra), -jnp.sin(ca), jnp.sin(ca)], axis=-1)
    return jnp.concatenate([cos64, cos64], axis=-1), jnp.concatenate([sin64, sin64], axis=-1)


def _block_diag2(w):
    z = jnp.zeros_like(w[:, 0])
    return jnp.concatenate([jnp.concatenate([w[:, 0], z], axis=2), jnp.concatenate([z, w[:, 1]], axis=2)], axis=1)


def _derived_weights(p):
    d = D_MODEL
    depth = p['w_in'].shape[0]
    row = lambda a, w: a.reshape(depth, 1, w)
    ng = len(POOL_WINDOWS)
    pool_blk = (p['pool_w'][:, :, :, None, :] * jnp.eye(ng, dtype=F32)[None, :, None, :, None]).reshape(
        depth, C_WIDTH, C_WIDTH)
    lane_pad = ROUTER_LANES - N_GROUPS - N_EXPERTS
    w_router = jnp.pad(jnp.concatenate([p['rt_grp_w'], p['rt_exp_w']], axis=-1), ((0, 0), (0, 0), (0, lane_pad)))
    b_router = jnp.pad(jnp.concatenate([p['rt_grp_b'], p['rt_exp_b']], axis=-1), ((0, 0), (0, lane_pad)))
    b_router = b_router.reshape(depth, 1, ROUTER_LANES)
    return {
        'w2blk': _block_diag2(p['rw_w2']).astype(BF16),
        'w0': row(p['rw_w0'], 2 * A_WIDTH),
        'a2blk': _block_diag2(p['rw_a2']).astype(BF16),
        'a0': row(p['rw_a0'], 2 * A_WIDTH),
        'g2': p['rw_g2'].astype(BF16),
        'kkw': row(p['rw_kk'], A_WIDTH),
        'ka': jnp.tile(row(p['rw_ka'], A_WIDTH), (1, 1, 2)),
        'rk': row(p['rw_rk'], A_WIDTH),
        'gn_g': row(p['rw_gn_g'], A_WIDTH),
        'gn_b': row(p['rw_gn_b'], A_WIDTH),
        'qn': jnp.tile(row(p['q_norm'], HEAD_DIM), (1, 1, B_Q_HEADS)),
        'knw': jnp.tile(row(p['k_norm'], HEAD_DIM), (1, 1, B_KV_HEADS)),
        'pool_blk': pool_blk.astype(BF16),
        'pool_scale': row(p['pool_scale'], C_WIDTH),
        'ln1_g': row(p['ln1_g'], d), 'ln1_b': row(p['ln1_b'], d),
        'ln2_g': row(p['ln2_g'], d), 'ln2_b': row(p['ln2_b'], d),
        'w_router_hi': w_router.astype(BF16),
        'w_router_lo': (w_router - w_router.astype(BF16).astype(F32)).astype(BF16),
        'b_router': b_router,
    }


def _forward(x_prompt, x_sample, state_rwkv, cache_k, cache_v, c, c_ctx, p):
    batch, seq, d = x_prompt.shape
    dec_batch, dec_seq, _ = x_sample.shape
    depth = p['w_in'].shape[0]
    past = cache_k.shape[2]
    n_ctx = batch * seq
    n_tok = n_ctx + dec_batch * dec_seq
    alpha = (2 * depth) ** 0.25
    bounds = tuple(n_ctx + j * dec_seq for j in range(dec_batch))
    tm = math.gcd(n_ctx, dec_seq, 512)
    tm_moe = math.gcd(n_ctx, dec_seq, 512)
    tq_ctx = min(256, seq)
    tq_dec = min(256, dec_seq)
    sb_ctx = 2

    cond8 = jnp.zeros((8, d), F32).at[0].set(c_ctx).at[1:1 + dec_batch].set(c)
    mods = _modulation(cond8, p['w_mod'], p['b_mod'])
    cos, sin = _rope_tables(dec_seq)
    w_in16, w_out16 = p['w_in'].astype(BF16), p['w_out'].astype(BF16)
    experts = (p['moe_w1'], p['moe_w3'], p['moe_w2'])
    lw = _derived_weights(p)
    head_id = jnp.arange(2 * HEAD_DIM) // HEAD_DIM
    ones = (head_id[:, None] == head_id[None, :]).astype(BF16)

    h = (x_prompt.reshape(n_ctx, d), x_sample.reshape(dec_batch * dec_seq, d))
    ck4 = cache_k.reshape(dec_batch, depth, past, KV_WIDTH)
    cv4 = cache_v.reshape(dec_batch, depth, past, KV_WIDTH)
    new_s, new_k, new_v = [], [], []
    for l in range(depth):
        (r, v, kk, logw, kd, bvec, g, bonus, q, k_att, v_att, pp) = _in_projection(
            h, mods, w_in16, l, lw, cos, sin, ones, tm=tm, bounds=bounds)
        scan_ops = (r, v, kk, logw, kd, bvec)
        yf_c, yb_c, s_fin = _rwkv_scan(scan_ops, None, row0=0, nseq=batch, seqlen=seq, sb=sb_ctx,
                                       nb=seq // CHUNK, want_state=True)
        yf_d, yb_d = _rwkv_scan(scan_ops, state_rwkv[:, l], row0=n_ctx, nseq=dec_batch, seqlen=dec_seq,
                                sb=dec_batch, nb=4, want_state=False)
        yf = (yf_c.reshape(n_ctx, A_WIDTH), yf_d.reshape(n_tok - n_ctx, A_WIDTH))
        yb = (yb_c.reshape(n_ctx, A_WIDTH), yb_d.reshape(n_tok - n_ctx, A_WIDTH))
        att_c, pool_c = _attention(q, k_att, v_att, None, pp, lw['pool_blk'], lw['pool_scale'], l,
                                   row0=0, nseq=batch, seqlen=seq, tq=tq_ctx)
        att_d, pool_d = _attention(q, k_att, v_att, (ck4, cv4), pp, lw['pool_blk'], lw['pool_scale'], l,
                                   row0=n_ctx, nseq=dec_batch, seqlen=dec_seq, tq=tq_dec)
        att, pool = (att_c, att_d), (pool_c, pool_d)
        h1, u2, logits = _out_projection(yf, yb, g, bonus, att, pool, h, mods, w_out16, l, lw, ones,
                                         tm=tm, bounds=bounds, alpha=alpha)
        h = _moe(u2, logits, h1, mods, experts, l, lw, tm=tm_moe, sub=128, bounds=bounds, alpha=alpha,
                 n_ctx=n_ctx)
        new_s.append(s_fin)
        new_k.append(k_att[:n_ctx].reshape(batch, seq, B_KV_HEADS, HEAD_DIM))
        new_v.append(v_att[:n_ctx].reshape(batch, seq, B_KV_HEADS, HEAD_DIM))
    return (h[0].reshape(batch, seq, d), h[1].reshape(dec_batch, dec_seq, d),
            jnp.stack(new_s, axis=1), jnp.stack(new_k, axis=1), jnp.stack(new_v, axis=1))


def kernel(x_prompt, x_sample, state_rwkv, cache_k, cache_v, c, c_ctx, w_in, w_out, w_mod, b_mod, ln1_g, ln1_b, ln2_g, ln2_b, rw_w0, rw_w2, rw_a0, rw_a2, rw_g2, rw_kk, rw_ka, rw_rk, rw_gn_g, rw_gn_b, q_norm, k_norm, pool_w, pool_scale, rt_grp_w, rt_grp_b, rt_exp_w, rt_exp_b, moe_w1, moe_w3, moe_w2):
    p = dict(w_in=w_in, w_out=w_out, w_mod=w_mod, b_mod=b_mod, ln1_g=ln1_g, ln1_b=ln1_b, ln2_g=ln2_g,
             ln2_b=ln2_b, rw_w0=rw_w0, rw_w2=rw_w2, rw_a0=rw_a0, rw_a2=rw_a2, rw_g2=rw_g2, rw_kk=rw_kk,
             rw_ka=rw_ka, rw_rk=rw_rk, rw_gn_g=rw_gn_g, rw_gn_b=rw_gn_b, q_norm=q_norm, k_norm=k_norm,
             pool_w=pool_w, pool_scale=pool_scale, rt_grp_w=rt_grp_w, rt_grp_b=rt_grp_b, rt_exp_w=rt_exp_w,
             rt_exp_b=rt_exp_b, moe_w1=moe_w1, moe_w3=moe_w3, moe_w2=moe_w2)
    return _forward(x_prompt, x_sample, state_rwkv, cache_k, cache_v, c, c_ctx, p)
```

```python
import functools
import math

import jax
import jax.numpy as jnp
from jax import lax
from jax.experimental import pallas as pl
from jax.experimental.pallas import tpu as pltpu

F32 = jnp.float32
BF16 = jnp.bfloat16
I32 = jnp.int32

D_MODEL = 1024
GRID_W = 64
HEAD_DIM = 64
A_HEADS = 6
A_WIDTH = A_HEADS * HEAD_DIM
LORA_W = 64
LORA_A = 64
LORA_G = 128
N_DIR = 2
DECAY_SCALE = 0.606531
B_Q_HEADS = 6
B_KV_HEADS = 2
B_WIDTH = B_Q_HEADS * HEAD_DIM
KV_WIDTH = B_KV_HEADS * HEAD_DIM
ROPE_THETA = 10000.0
POOL_WINDOWS = (2, 4, 8, 16)
C_WIDTH = 256
IN_COLS = 2432
N_GROUPS = 4
EXP_PER_GROUP = 4
N_EXPERTS = 16
D_EXPERT = 256
LN_EPS = 1e-5
GN_EPS = 64e-5
QK_EPS = 1e-6

_O_R, _O_K, _O_V = 0, 384, 768
_O_LW, _O_LA, _O_LG = 1152, 1280, 1408
_O_Q, _O_KB, _O_VB, _O_P = 1536, 1920, 2048, 2176

CHUNK = 64
ROUTER_LANES = 128
WEIGHT_STAGE_SLOTS = 4
VMEM_LIMIT = 56 * 1024 * 1024


def _cparams(sem):
    return pltpu.CompilerParams(dimension_semantics=sem, vmem_limit_bytes=VMEM_LIMIT)


def _sigmoid(x):
    return 1.0 / (1.0 + jnp.exp(-x))


def _dot(a, b):
    return jnp.dot(a, b, preferred_element_type=F32)


def _dot_nt(a, b):
    return lax.dot_general(a, b, (((1,), (1,)), ((), ())), preferred_element_type=F32)


def _segsum(x, ones2):
    x16 = x.astype(BF16)
    slabs = [_dot(x16[:, j:j + 128], ones2) for j in range(0, x.shape[1], 128)]
    return slabs[0] if len(slabs) == 1 else jnp.concatenate(slabs, axis=1)


def _rope128(x, cos, sin):
    lane = lax.broadcasted_iota(I32, x.shape, 1)
    first = (lane % 32) < 16
    up = pltpu.roll(x, 128 - 16, 1)
    dn = pltpu.roll(x, 16, 1)
    return x * cos + jnp.where(first, up, dn) * sin


def _layer_norm(z, g, b):
    mu = jnp.mean(z, axis=-1, keepdims=True)
    zc = z - mu
    var = jnp.mean(zc * zc, axis=-1, keepdims=True)
    return zc * lax.rsqrt(var + LN_EPS) * g + b


def _mod_kernel(c_ref, w_ref, b_ref, o_ref):
    x = c_ref[...]
    x = x * _sigmoid(x)
    w = w_ref[...]
    x_hi = x.astype(BF16)
    x_lo = (x - x_hi.astype(F32)).astype(BF16)
    w_hi = w.astype(BF16)
    w_lo = (w - w_hi.astype(F32)).astype(BF16)
    o_ref[...] = (_dot(x_hi, w_hi) + _dot(x_lo, w_hi) + _dot(x_hi, w_lo)) + b_ref[...]


def _modulation(cond8, w_mod, b_mod):
    depth = w_mod.shape[0]
    tn = 3072
    return pl.pallas_call(
        _mod_kernel,
        grid=(depth, 6 * D_MODEL // tn),
        in_specs=[
            pl.BlockSpec((8, D_MODEL), lambda l, j: (0, 0)),
            pl.BlockSpec((None, D_MODEL, tn), lambda l, j: (l, 0, j)),
            pl.BlockSpec((None, 1, tn), lambda l, j: (l, 0, j)),
        ],
        out_specs=pl.BlockSpec((None, 8, tn), lambda l, j: (l, 0, j)),
        out_shape=jax.ShapeDtypeStruct((depth, 8, 6 * D_MODEL), F32),
        compiler_params=_cparams(("arbitrary", "arbitrary")),
        name="modulation",
    )(cond8, w_mod, b_mod.reshape(depth, 1, 6 * D_MODEL))


def _group_of_row(row0, bounds):
    g = jnp.zeros((), I32)
    for b in bounds:
        g = g + jnp.where(row0 >= b, 1, 0).astype(I32)
    return g


def _in_kernel(hc_ref, hd_ref, mod_ref, w_ref, w2_ref, w0_ref, a2_ref, a0_ref, g2_ref, kkw_ref, ka_ref,
               rk_ref, qn_ref, knw_ref, cos_ref, sin_ref, ones_ref,
               r_o, v_o, kk_o, lw_o, kd_o, b_o, g_o, bon_o, q_o, k_o, vb_o, p_o, *, tm, bounds, n_ctx_tiles):
    grp = _group_of_row(pl.program_id(0) * tm, bounds)
    mod = mod_ref[pl.ds(grp, 1), :]
    sh1 = mod[:, 0:D_MODEL]
    sc1 = mod[:, D_MODEL:2 * D_MODEL]
    is_ctx = pl.program_id(0) < n_ctx_tiles
    h = jnp.where(is_ctx, hc_ref[...], hd_ref[...])
    u = h * (1.0 + sc1) + sh1
    ones = ones_ref[...]
    scale = 1.0 / math.sqrt(HEAD_DIM)
    proj = _dot(u.astype(BF16), w_ref[...])

    r = proj[:, _O_R:_O_R + A_WIDTH]
    k = proj[:, _O_K:_O_K + A_WIDTH]
    v = proj[:, _O_V:_O_V + A_WIDTH]
    lo_w = proj[:, _O_LW:_O_LW + 128]
    lo_a = proj[:, _O_LA:_O_LA + 128]
    lo_g = proj[:, _O_LG:_O_LG + 128]

    w_pre = w0_ref[...] + _dot(jnp.tanh(lo_w).astype(BF16), w2_ref[...])
    logw = -DECAY_SCALE * _sigmoid(w_pre)
    a = _sigmoid(a0_ref[...] + _dot(lo_a.astype(BF16), a2_ref[...]))
    g = _dot(_sigmoid(lo_g).astype(BF16), g2_ref[...])

    kk = k * kkw_ref[...]
    kk = kk * lax.rsqrt(jnp.maximum(_segsum(kk * kk, ones), 1e-24))
    k2 = jnp.concatenate([k, k], axis=1)
    kd = k2 * (1.0 + (a - 1.0) * ka_ref[...])
    bvec = jnp.concatenate([kk, kk], axis=1) * a
    kd_sum = kd[:, 0:A_WIDTH] + kd[:, A_WIDTH:2 * A_WIDTH]
    bonus = _segsum(r * kd_sum * rk_ref[...], ones) * v

    r_o[...] = r.astype(BF16)
    v_o[...] = v.astype(BF16)
    kk_o[...] = kk.astype(BF16)
    lw_o[...] = logw
    kd_o[...] = kd.astype(BF16)
    b_o[...] = bvec.astype(BF16)
    g_o[...] = g.astype(BF16)
    bon_o[...] = bonus.astype(BF16)

    cos = cos_ref[...]
    sin = sin_ref[...]
    rope = lambda x: jnp.where(is_ctx, x, _rope128(x, cos, sin))
    q = proj[:, _O_Q:_O_Q + B_WIDTH]
    qn = q * lax.rsqrt(_segsum(q * q, ones) * (1.0 / HEAD_DIM) + QK_EPS) * qn_ref[...]
    q_o[...] = (jnp.concatenate([rope(qn[:, j:j + 128]) for j in range(0, B_WIDTH, 128)], axis=1)
                * scale).astype(BF16)
    kb = proj[:, _O_KB:_O_KB + KV_WIDTH]
    kn = kb * lax.rsqrt(_segsum(kb * kb, ones) * (1.0 / HEAD_DIM) + QK_EPS) * knw_ref[...]
    k_o[...] = rope(kn)
    vb_o[...] = proj[:, _O_VB:_O_VB + KV_WIDTH]
    p_o[...] = proj[:, _O_P:_O_P + C_WIDTH]


def _pair_specs(tm, nct):
    rows_c = lambda w: pl.BlockSpec((tm, w), lambda i: (jnp.minimum(i, nct - 1), 0))
    rows_d = lambda w: pl.BlockSpec((tm, w), lambda i: (jnp.maximum(i - nct, 0), 0))
    return rows_c, rows_d


def _layer_spec(layer):
    return lambda *shape: pl.BlockSpec((None,) + shape, lambda *_: (layer,) + (0,) * len(shape))


def _in_projection(h, mods, w_in, layer, lw, cos, sin, ones, *, tm, bounds):
    nct = h[0].shape[0] // tm
    t = h[0].shape[0] + h[1].shape[0]
    full = lambda shape: pl.BlockSpec(shape, lambda i: (0,) * len(shape))
    rows = lambda w: pl.BlockSpec((tm, w), lambda i: (i, 0))
    rows_c, rows_d = _pair_specs(tm, nct)
    lay = _layer_spec(layer)
    tiles_per_req = cos.shape[0] // tm
    table = pl.BlockSpec((tm, 128), lambda i: (jnp.maximum(i - nct, 0) % tiles_per_req, 0))
    widths = (A_WIDTH, A_WIDTH, A_WIDTH, 2 * A_WIDTH, 2 * A_WIDTH, 2 * A_WIDTH, A_WIDTH, A_WIDTH,
              B_WIDTH, KV_WIDTH, KV_WIDTH, C_WIDTH)
    dtypes = (BF16, BF16, BF16, F32, BF16, BF16, BF16, BF16, BF16, F32, F32, F32)
    return pl.pallas_call(
        functools.partial(_in_kernel, tm=tm, bounds=bounds, n_ctx_tiles=nct),
        grid=(t // tm,),
        in_specs=[
            rows_c(D_MODEL), rows_d(D_MODEL), lay(8, 6 * D_MODEL), lay(D_MODEL, IN_COLS),
            lay(128, 2 * A_WIDTH), lay(1, 2 * A_WIDTH), lay(128, 2 * A_WIDTH), lay(1, 2 * A_WIDTH),
            lay(LORA_G, A_WIDTH), lay(1, A_WIDTH), lay(1, 2 * A_WIDTH), lay(1, A_WIDTH),
            lay(1, B_WIDTH), lay(1, KV_WIDTH), table, table, full((2 * HEAD_DIM, 2 * HEAD_DIM)),
        ],
        out_specs=[rows(w) for w in widths],
        out_shape=[jax.ShapeDtypeStruct((t, w), dt) for w, dt in zip(widths, dtypes)],
        compiler_params=_cparams(("arbitrary",)),
        name="in_projection",
    )(h[0], h[1], mods, w_in, lw['w2blk'], lw['w0'], lw['a2blk'], lw['a0'], lw['g2'], lw['kkw'], lw['ka'],
      lw['rk'], lw['qn'], lw['knw'], cos, sin, ones)


def _split2(x):
    hi = x.astype(BF16)
    lo = (x - hi.astype(F32)).astype(BF16)
    return hi, lo


def _scan_kernel(*refs, sb, nb, zero_init, want_state):
    fwd = refs[0:6]
    bwd = refs[6:12]
    pos = 12
    if not zero_init:
        s0_ref = refs[pos]
        pos += 1
    y_outs = (refs[pos], refs[pos + 1])
    pos += 2
    if want_state:
        sfin_o = refs[pos]
        pos += 1
    h_ref = refs[pos]

    step = pl.program_id(1)
    nstep = pl.num_programs(1)
    npair = A_HEADS // 2
    pw = 2 * HEAD_DIM
    pairs = [(q, d, hp) for q in range(sb) for d in range(N_DIR) for hp in range(npair)]

    @pl.when(step == 0)
    def _():
        h_ref[...] = jnp.zeros_like(h_ref)
        if not zero_init:
            for (q, d, hp) in pairs:
                for e in range(2):
                    lo = e * HEAD_DIM
                    h_ref[q, d, hp, lo:lo + HEAD_DIM, lo:lo + HEAD_DIM] = s0_ref[q, d, 2 * hp + e].T

    def iota(shape, dim):
        return lax.broadcasted_iota(I32, shape, dim)

    first64 = iota((CHUNK, pw), 1) < HEAD_DIM
    eye_p = ((iota((CHUNK, pw), 1) % HEAD_DIM) == iota((CHUNK, pw), 0)).astype(F32)
    eye2 = (iota((pw, pw), 0) == iota((pw, pw), 1)).astype(F32)
    same_head = (iota((pw, pw), 0) < HEAD_DIM) == (iota((pw, pw), 1) < HEAD_DIM)
    ti = iota((CHUNK, CHUNK), 0)
    si = iota((CHUNK, CHUNK), 1)
    incl16 = ((si <= ti).astype(BF16), (si >= ti).astype(BF16))
    t4 = iota((2 * CHUNK, 2 * pw), 0)
    s4 = iota((2 * CHUNK, 2 * pw), 1) % CHUNK
    incl_off = jnp.where(t4 < CHUNK, 0, 1)
    t4 = t4 % CHUNK
    mask4 = (s4 < t4 + incl_off, s4 > t4 - incl_off)

    def bdiag(x):
        return jnp.concatenate([jnp.where(first64, x, 0.0), jnp.where(first64, 0.0, x)], axis=0)

    lhs, rhs1, bkt, gcol, v_p, amat = {}, {}, {}, {}, {}, {}
    blocks = [(q, d, j) for q in range(sb) for d in range(N_DIR) for j in range(nb)]
    csum = {}
    for (q, d, j) in blocks:
        hi, lo = _split2((fwd, bwd)[d][3][q, pl.ds(j * CHUNK, CHUNK), :])
        csum[(q, d, j)] = _dot(incl16[d], hi) + _dot(incl16[d], lo)
    for (q, d, j) in blocks:
        r_ref, v_ref, kk_ref, lw_ref, kd_ref, b_ref = (fwd, bwd)[d]
        rows = pl.ds(j * CHUNK, CHUNK)
        lw = lw_ref[q, rows, :]
        cs = csum[(q, d, j)]
        gam = jnp.exp(cs)
        gam_prev = jnp.exp(cs - lw)
        gam_inv = jnp.exp(-cs)
        gam_end = jnp.exp(cs[CHUNK - 1:CHUNK, :] if d == 0 else cs[0:1, :])
        a_hat = -kk_ref[q, rows, :] * gam_prev
        r_hat = r_ref[q, rows, :] * gam
        b_hat = b_ref[q, rows, :] * gam_inv
        k_hat = kd_ref[q, rows, :] * gam_inv
        v_all = v_ref[q, rows, :]
        for hp in range(npair):
            key = (q, d, hp, j)
            sl = slice(hp * pw, (hp + 1) * pw)
            lhs[key] = jnp.concatenate([a_hat[:, sl], r_hat[:, sl]], axis=0).astype(BF16)
            bh, kh, ge = b_hat[:, sl], k_hat[:, sl], gam_end[:, sl]
            rhs1[key] = jnp.concatenate([bdiag(bh), bdiag(kh)], axis=0).astype(BF16)
            gcol[key] = jnp.sum(eye2 * ge, axis=1, keepdims=True)
            bkt[key] = (jnp.concatenate([bh, kh], axis=0) * ge).T.astype(BF16)
            v_p[key] = v_all[:, sl]
    keys = list(lhs.keys())
    for key in keys:
        amat[key] = _dot_nt(lhs[key], rhs1[key])
    npow, tcur, a_rb, lhs2, vbd = {}, {}, {}, {}, {}
    for key in keys:
        am = jnp.where(mask4[key[1]], amat[key], 0.0)
        npow[key] = am[0:CHUNK, 0:pw]
        tcur[key] = eye_p + npow[key]
        a_rb[key] = am[CHUNK:2 * CHUNK, 0:pw].astype(BF16)
        vbd[key] = bdiag(v_p[key])
        lhs2[key] = jnp.concatenate([lhs[key], am[:, pw:2 * pw].astype(BF16)], axis=1)
    nsq = int(math.log2(CHUNK)) - 1
    for key in keys:
        npow[key] = _dot(npow[key].astype(BF16), bdiag(npow[key]).astype(BF16))
    for _ in range(nsq - 1):
        for key in keys:
            rhs = jnp.concatenate([bdiag(npow[key]), bdiag(tcur[key])], axis=1).astype(BF16)
            res = _dot(npow[key].astype(BF16), rhs)
            npow[key] = res[:, 0:pw]
            tcur[key] = tcur[key] + res[:, pw:2 * pw]
    t_inv = {}
    for key in keys:
        t_inv[key] = (tcur[key] + _dot(npow[key].astype(BF16), bdiag(tcur[key]).astype(BF16))).astype(BF16)

    hcur = {pr: h_ref[pr[0], pr[1], pr[2]] for pr in pairs}
    for jj in range(nb):
        z, u = {}, {}
        kof = lambda pr: (pr[0], pr[1], pr[2], jj if pr[1] == 0 else nb - 1 - jj)
        for pr in pairs:
            key = kof(pr)
            hv = jnp.concatenate([hcur[pr].astype(BF16), vbd[key]], axis=0)
            z[pr] = _dot(lhs2[key], hv)
        for pr in pairs:
            u[pr] = _dot(t_inv[kof(pr)], bdiag(z[pr][0:CHUNK]).astype(BF16))
        for pr in pairs:
            key = kof(pr)
            q, d, hp, j = key
            y = z[pr][CHUNK:2 * CHUNK] + _dot(a_rb[key], bdiag(u[pr]).astype(BF16))
            y_outs[d][q, pl.ds(j * CHUNK, CHUNK), hp * pw:(hp + 1) * pw] = y.astype(BF16)
            uv = jnp.concatenate([u[pr].astype(BF16), v_p[key]], axis=0)
            hcur[pr] = hcur[pr] * gcol[key] + jnp.where(same_head, _dot(bkt[key], uv), 0.0)
    for pr in pairs:
        h_ref[pr[0], pr[1], pr[2]] = hcur[pr]

    if want_state:
        @pl.when(step == nstep - 1)
        def _():
            for (q, d, hp) in pairs:
                for e in range(2):
                    lo = e * HEAD_DIM
                    sfin_o[q, d, 2 * hp + e] = hcur[(q, d, hp)][lo:lo + HEAD_DIM, lo:lo + HEAD_DIM].T


def _rwkv_scan(ops, s0, layer, *, row0, nseq, seqlen, sb, nb, want_state):
    g = nb * CHUNK
    nstep = seqlen // g
    zero_init = s0 is None
    ops3 = [a.reshape(a.shape[0] // seqlen, seqlen, a.shape[1]) for a in ops]
    base = row0 // (seqlen * sb)
    fidx = lambda col: (lambda s, c: (base + s, c, col))
    bidx = lambda col: (lambda s, c: (base + s, nstep - 1 - c, col))
    blk = lambda im: pl.BlockSpec((sb, g, A_WIDTH), im)
    in_specs = [blk(fidx(0))] * 6 + [blk(bidx(0))] * 3 + [blk(bidx(1))] * 3
    args = ops3 + ops3
    state_spec = pl.BlockSpec((sb, N_DIR, A_HEADS, HEAD_DIM, HEAD_DIM), lambda s, c: (s, 0, 0, 0, 0))
    if not zero_init:
        in_specs.append(pl.BlockSpec((sb, None, N_DIR, A_HEADS, HEAD_DIM, HEAD_DIM),
                                     lambda s, c: (s, layer, 0, 0, 0, 0)))
        args.append(s0)
    out_specs = [blk(lambda s, c: (s, c, 0)), blk(lambda s, c: (s, nstep - 1 - c, 0))]
    out_shape = [jax.ShapeDtypeStruct((nseq, seqlen, A_WIDTH), BF16)] * 2
    if want_state:
        out_specs.append(state_spec)
        out_shape.append(jax.ShapeDtypeStruct((nseq, N_DIR, A_HEADS, HEAD_DIM, HEAD_DIM), F32))
    return pl.pallas_call(
        functools.partial(_scan_kernel, sb=sb, nb=nb, zero_init=zero_init, want_state=want_state),
        grid=(nseq // sb, nstep),
        in_specs=in_specs,
        out_specs=out_specs,
        out_shape=out_shape,
        scratch_shapes=[pltpu.VMEM((sb, N_DIR, A_HEADS // 2, 2 * HEAD_DIM, 2 * HEAD_DIM), F32)],
        compiler_params=_cparams(("arbitrary", "arbitrary")),
        name="rwkv_scan_ctx" if zero_init else "rwkv_scan_dec",
    )(*args)


def _attn_kernel(*refs, with_cache):
    grp = B_Q_HEADS // B_KV_HEADS
    if with_cache:
        q_ref, k_ref, v_ref, ck_ref, cv_ref, p_ref, pw_ref, psc_ref, o_ref, pool_o = refs
        kall = jnp.concatenate([k_ref[...], ck_ref[...]], axis=0)
        vall = jnp.concatenate([v_ref[...], cv_ref[...]], axis=0)
    else:
        q_ref, k_ref, v_ref, p_ref, pw_ref, psc_ref, o_ref, pool_o = refs
        kall = k_ref[...]
        vall = v_ref[...]

    @pl.when(pl.program_id(1) == 0)
    def _():
        pool_o[...] = _pool_mix(p_ref[...], pw_ref[...], psc_ref[...]).astype(BF16)

    q = q_ref[...]
    k16 =[kall[:, hk * HEAD_DIM:(hk + 1) * HEAD_DIM].astype(BF16) for hk in range(B_KV_HEADS)]
    v16 = [vall[:, hk * HEAD_DIM:(hk + 1) * HEAD_DIM].astype(BF16) for hk in range(B_KV_HEADS)]

    def scores(j):
        return _dot_nt(q[:, j * HEAD_DIM:(j + 1) * HEAD_DIM], k16[j // grp])

    outs = []
    s_next = scores(0)
    for j in range(B_Q_HEADS):
        s = s_next
        if j + 1 < B_Q_HEADS:
            s_next = scores(j + 1)
        m = jnp.max(s, axis=-1, keepdims=True)
        p = jnp.exp(s - m)
        den = jnp.sum(p, axis=-1, keepdims=True)
        outs.append(_dot(p.astype(BF16), v16[j // grp]) / den)
    o_ref[...] = jnp.concatenate(outs, axis=1).astype(BF16)


def _attention(q, k_tok, v_tok, cache, p_tok, pool_w, pool_scale, layer, *, row0, nseq, seqlen, tq):
    nq = seqlen // tq
    base = row0 // tq
    kbase = row0 // seqlen
    kspec = pl.BlockSpec((seqlen, KV_WIDTH), lambda s, i: (kbase + s, 0))
    in_specs = [pl.BlockSpec((tq, B_WIDTH), lambda s, i: (base + s * nq + i, 0)), kspec, kspec]
    args = [q, k_tok, v_tok]
    if cache is not None:
        ck, cv = cache
        cspec = pl.BlockSpec((None, None, ck.shape[2], KV_WIDTH), lambda s, i: (s, layer, 0, 0))
        in_specs += [cspec, cspec]
        args += [ck, cv]
    lay = _layer_spec(layer)
    in_specs += [pl.BlockSpec((seqlen, C_WIDTH), lambda s, i: (kbase + s, 0)), lay(C_WIDTH, C_WIDTH), lay(1, C_WIDTH)]
    args += [p_tok, pool_w, pool_scale]
    return pl.pallas_call(
        functools.partial(_attn_kernel, with_cache=cache is not None),
        grid=(nseq, nq),
        in_specs=in_specs,
        out_specs=[pl.BlockSpec((tq, B_WIDTH), lambda s, i: (s * nq + i, 0)),
                   pl.BlockSpec((seqlen, C_WIDTH), lambda s, i: (s, 0))],
        out_shape=[jax.ShapeDtypeStruct((nseq * seqlen, B_WIDTH), BF16),
                   jax.ShapeDtypeStruct((nseq * seqlen, C_WIDTH), BF16)],
        compiler_params=_cparams(("arbitrary", "arbitrary")),
        name="attention_ctx" if cache is None else "attention_dec",
    )(*args)


def _pool_mix(x, w_blk, scale):
    seqlen = x.shape[0]
    t = lax.broadcasted_iota(I32, x.shape, 0)
    lane = lax.broadcasted_iota(I32, x.shape, 1)

    def down(a, k):
        return jnp.where(t >= k, pltpu.roll(a, k, 0), 0.0)

    def up(a, k):
        return jnp.where(t < seqlen - k, pltpu.roll(a, seqlen - k, 0), 0.0)

    past = [x]
    futr = [x]
    for j in range(len(POOL_WINDOWS) - 1):
        past.append(past[j] + down(past[j], 2 ** j))
        futr.append(futr[j] + up(futr[j], 2 ** j))
    win_sum = jnp.zeros_like(x)
    cnt = jnp.zeros_like(x)
    for gi, win in enumerate(POOL_WINDOWS):
        half = win // 2
        in_group = (lane // HEAD_DIM) == gi
        win_sum = jnp.where(in_group, down(past[gi], 1) + futr[gi], win_sum)
        n = (jnp.minimum(t + half, seqlen) - jnp.maximum(t - half, 0)).astype(F32)
        cnt = jnp.where(in_group, n, cnt)
    d = win_sum / cnt - x
    return _dot(d.astype(BF16), w_blk) * scale


def _out_kernel(yfc_ref, yfd_ref, ybc_ref, ybd_ref, g_ref, bon_ref, attc_ref, attd_ref, poolc_ref, poold_ref,
                hc_ref, hd_ref, mod_ref, w_ref, lng_ref, lnb_ref, gng_ref, gnb_ref, ones_ref, wrh_ref, wrl_ref,
                br_ref, h1_o, u2_o, lg_o, *, tm, bounds, alpha, n_ctx_tiles):
    grp = _group_of_row(pl.program_id(0) * tm, bounds)
    mod = mod_ref[pl.ds(grp, 1), :]
    g1 = mod[:, 2 * D_MODEL:3 * D_MODEL]
    sh2 = mod[:, 3 * D_MODEL:4 * D_MODEL]
    sc2 = mod[:, 4 * D_MODEL:5 * D_MODEL]
    ones = ones_ref[...]
    is_ctx = pl.program_id(0) < n_ctx_tiles
    half = tm // 2
    halves = [pl.ds(s * half, half) for s in range(2)]
    pick = lambda c_ref, d_ref, rows: jnp.where(is_ctx, c_ref[rows, :], d_ref[rows, :])
    y = [pick(yfc_ref, yfd_ref, rows).astype(F32) + pick(ybc_ref, ybd_ref, rows).astype(F32) for rows in halves]
    mu = [_segsum(v, ones) * (1.0 / HEAD_DIM) for v in y]
    yc = [v - m_ for v, m_ in zip(y, mu)]
    var = [_segsum(v * v, ones) * (1.0 / HEAD_DIM) for v in yc]
    m = []
    for rows, v, vr in zip(halves, yc, var):
        yn = v * lax.rsqrt(vr + GN_EPS) * gng_ref[...] + gnb_ref[...]
        a_out = (yn + bon_ref[rows, :]) * g_ref[rows, :]
        m.append(_dot(a_out.astype(BF16), w_ref[0:A_WIDTH, :])
                 + _dot(pick(attc_ref, attd_ref, rows), w_ref[A_WIDTH:A_WIDTH + B_WIDTH, :])
                 + _dot(pick(poolc_ref, poold_ref, rows), w_ref[A_WIDTH + B_WIDTH:D_MODEL, :]))
    for rows, m_ in zip(halves, m):
        h1 = _layer_norm(alpha * pick(hc_ref, hd_ref, rows) + g1 * m_, lng_ref[...], lnb_ref[...])
        h1_o[rows, :] = h1
        u2 = h1 * (1.0 + sc2) + sh2
        u_hi = u2.astype(BF16)
        u_lo = (u2 - u_hi.astype(F32)).astype(BF16)
        u2_o[rows, :] = u_hi
        lg_o[rows, :] = (_dot(u_hi, wrh_ref[...]) + _dot(u_lo, wrh_ref[...]) + _dot(u_hi, wrl_ref[...])) + br_ref[...]


def _out_projection(yf, yb, g, bonus, att, pool, h, mods, w_out, layer, lw, ones, *, tm, bounds, alpha):
    t = h[0].shape[0] + h[1].shape[0]
    nct = h[0].shape[0] // tm
    full = lambda shape: pl.BlockSpec(shape, lambda i: (0,) * len(shape))
    rows = lambda w: pl.BlockSpec((tm, w), lambda i: (i, 0))
    rows_c, rows_d = _pair_specs(tm, nct)
    lay = _layer_spec(layer)
    return pl.pallas_call(
        functools.partial(_out_kernel, tm=tm, bounds=bounds, alpha=alpha, n_ctx_tiles=nct),
        grid=(t // tm,),
        in_specs=[rows_c(A_WIDTH), rows_d(A_WIDTH), rows_c(A_WIDTH), rows_d(A_WIDTH), rows(A_WIDTH), rows(A_WIDTH),
                  rows_c(B_WIDTH), rows_d(B_WIDTH), rows_c(C_WIDTH), rows_d(C_WIDTH), rows_c(D_MODEL),
                  rows_d(D_MODEL), lay(8, 6 * D_MODEL), lay(D_MODEL, D_MODEL), lay(1, D_MODEL),
                  lay(1, D_MODEL), lay(1, A_WIDTH), lay(1, A_WIDTH), full((2 * HEAD_DIM, 2 * HEAD_DIM)),
                  lay(D_MODEL, ROUTER_LANES), lay(D_MODEL, ROUTER_LANES), lay(1, ROUTER_LANES)],
        out_specs=[rows(D_MODEL), rows(D_MODEL), rows(ROUTER_LANES)],
        out_shape=[jax.ShapeDtypeStruct((t, D_MODEL), F32), jax.ShapeDtypeStruct((t, D_MODEL), BF16),
                   jax.ShapeDtypeStruct((t, ROUTER_LANES), F32)],
        compiler_params=_cparams(("arbitrary",)),
        name="out_projection",
    )(yf[0], yf[1], yb[0], yb[1], g, bonus, att[0], att[1], pool[0], pool[1], h[0], h[1], mods, w_out,
      lw['ln1_g'], lw['ln1_b'], lw['gn_g'], lw['gn_b'], ones, lw['w_router_hi'], lw['w_router_lo'], lw['b_router'])


def _router_gates(logits):
    lane = lax.broadcasted_iota(I32, logits.shape, 1)
    lane_f = lane.astype(F32)
    neg = -jnp.inf
    big = 1e9

    def first_lane(mask):
        return jnp.min(jnp.where(mask, lane_f, big), axis=-1, keepdims=True).astype(I32)

    is_grp = lane < N_GROUPS
    gmax = jnp.max(jnp.where(is_grp, logits, neg), axis=-1, keepdims=True)
    sel = first_lane(is_grp & (logits == gmax))
    gsum = jnp.sum(jnp.where(is_grp, jnp.exp(logits - gmax), 0.0), axis=-1, keepdims=True)
    gsel = 1.0 / gsum
    first_exp = N_GROUPS + sel * EXP_PER_GROUP
    in_sel = (lane >= first_exp) & (lane < first_exp + EXP_PER_GROUP)
    v1 = jnp.max(jnp.where(in_sel, logits, neg), axis=-1, keepdims=True)
    i1 = first_lane(in_sel & (logits == v1))
    rest = in_sel & (lane != i1)
    v2 = jnp.max(jnp.where(rest, logits, neg), axis=-1, keepdims=True)
    i2 = first_lane(rest & (logits == v2))
    e2 = jnp.exp(v2 - v1)
    t1 = (1.0 / (1.0 + e2)) * gsel
    t2 = (e2 / (1.0 + e2)) * gsel
    return jnp.where(lane == i1, t1, 0.0) + jnp.where(lane == i2, t2, 0.0), sel


def _weight_stream(jobs, stage_ref, sem):
    nslot = stage_ref.shape[0]
    copy = lambda i: pltpu.make_async_copy(jobs[i][0], stage_ref.at[i % nslot], sem.at[i % nslot])

    def prime():
        for i in range(min(nslot, len(jobs))):
            copy(i).start()

    def drain():
        for i, (_, dst_ref, e) in enumerate(jobs):
            copy(i).wait()
            dst_ref[e] = stage_ref[i % nslot].astype(BF16)
            if i + nslot < len(jobs):
                copy(i + nslot).start()

    return prime, drain


def _moe_kernel(x_ref, lg_ref, w1_hbm, w3_hbm, w2_hbm, h1_ref, mod_ref, lng_ref, lnb_ref, oc_ref, od_ref,
                xs_ref, gs_ref, acc_ref, w1_ref, w3_ref, w2_ref, stage_in_ref, stage_out_ref, sem_in, sem_out,
                *, tm, sub, bounds, alpha, n_ctx_tiles, layer):
    @pl.when(pl.program_id(0) == 0)
    def _():
        jobs_in = [(w.at[layer, e], dst, e) for w, dst in ((w1_hbm, w1_ref), (w3_hbm, w3_ref))
                   for e in range(N_EXPERTS)]
        jobs_out = [(w2_hbm.at[layer, e], w2_ref, e) for e in range(N_EXPERTS)]
        prime_in, drain_in = _weight_stream(jobs_in, stage_in_ref, sem_in)
        prime_out, drain_out = _weight_stream(jobs_out, stage_out_ref, sem_out)
        prime_in()
        prime_out()
        drain_in()
        drain_out()

    gates, sel = _router_gates(lg_ref[...])
    lane = lax.broadcasted_iota(I32, gates.shape, 1)
    onehot = jnp.where(lane == sel, 1.0, 0.0)
    row_i = lax.broadcasted_iota(I32, (tm, tm), 0)
    col_i = lax.broadcasted_iota(I32, (tm, tm), 1)
    earlier = jnp.where(col_i < row_i, 1.0, 0.0).astype(BF16)
    rank = _dot(earlier, onehot.astype(BF16))
    lane1 = lax.broadcasted_iota(I32, (1, ROUTER_LANES), 1)
    counts = jnp.sum(onehot, axis=0, keepdims=True)
    cnt = [jnp.sum(jnp.where(lane1 == g, counts, 0.0)).astype(I32) for g in range(N_GROUPS)]
    off = [jnp.zeros((), I32)]
    for g in range(N_GROUPS - 1):
        off.append(off[g] + cnt[g])
    offs = jnp.zeros((1, ROUTER_LANES), F32)
    for g in range(N_GROUPS):
        offs = jnp.where(lane1 == g, off[g].astype(F32), offs)
    pos = jnp.sum(onehot * (rank + offs), axis=1, keepdims=True).astype(I32)
    pos_row = jnp.broadcast_to(pos, (tm, ROUTER_LANES)).T[0:1, :]
    perm = jnp.where(row_i == pos_row, 1.0, 0.0).astype(BF16)
    perm_t = jnp.where(col_i == pos, 1.0, 0.0).astype(BF16)
    xs_ref[...] = _dot(perm, x_ref[...]).astype(BF16)
    gs_ref[...] = _dot(perm, gates.astype(BF16))
    acc_ref[...] = jnp.zeros_like(acc_ref)

    lane_s = lax.broadcasted_iota(I32, (sub, ROUTER_LANES), 1)
    for s in range(tm // sub):
        rows = pl.ds(s * sub, sub)
        for g in range(N_GROUPS):
            @pl.when((off[g] < (s + 1) * sub) & (off[g] + cnt[g] > s * sub))
            def _():
                xb = xs_ref[rows, :]
                gsb = gs_ref[rows, :]
                hids = []
                for j in range(EXP_PER_GROUP):
                    e = g * EXP_PER_GROUP + j
                    ge = jnp.sum(jnp.where(lane_s == N_GROUPS + e, gsb, 0.0), axis=-1, keepdims=True)
                    h1 = _dot(xb, w1_ref[e])
                    h3 = _dot(xb, w3_ref[e])
                    hids.append(((h1 * _sigmoid(h1)) * h3 * ge).astype(BF16))
                hid = jnp.concatenate(hids, axis=1)
                w2g = w2_ref[g * EXP_PER_GROUP:(g + 1) * EXP_PER_GROUP].reshape(EXP_PER_GROUP * D_EXPERT, D_MODEL)
                acc_ref[rows, :] += _dot(hid, w2g)

    f = _dot(perm_t, acc_ref[...].astype(BF16))
    grp = _group_of_row(pl.program_id(0) * tm, bounds)
    g2 = mod_ref[pl.ds(grp, 1), 5 * D_MODEL:6 * D_MODEL]
    out = _layer_norm(alpha * h1_ref[...] + g2 * f, lng_ref[...], lnb_ref[...])
    is_ctx = pl.program_id(0) < n_ctx_tiles

    @pl.when(is_ctx)
    def _():
        oc_ref[...] = out

    @pl.when(jnp.logical_not(is_ctx))
    def _():
        od_ref[...] = out


def _moe(u2, logits, h1, mods, experts, layer, lw, *, tm, sub, bounds, alpha, n_ctx):
    t = h1.shape[0]
    nct = n_ctx // tm
    rows_c, rows_d = _pair_specs(tm, nct)
    lay = _layer_spec(layer)
    hbm = pl.BlockSpec(memory_space=pl.ANY)
    rows = lambda w: pl.BlockSpec((tm, w), lambda i: (i, 0))
    return pl.pallas_call(
        functools.partial(_moe_kernel, tm=tm, sub=sub, bounds=bounds, alpha=alpha, n_ctx_tiles=nct, layer=layer),
        grid=(t // tm,),
        in_specs=[rows(D_MODEL), rows(ROUTER_LANES), hbm, hbm, hbm,
                  rows(D_MODEL), lay(8, 6 * D_MODEL), lay(1, D_MODEL), lay(1, D_MODEL)],
        out_specs=[rows_c(D_MODEL), rows_d(D_MODEL)],
        out_shape=[jax.ShapeDtypeStruct((n_ctx, D_MODEL), F32), jax.ShapeDtypeStruct((t - n_ctx, D_MODEL), F32)],
        scratch_shapes=[pltpu.VMEM((tm, D_MODEL), BF16), pltpu.VMEM((tm, ROUTER_LANES), F32),
                        pltpu.VMEM((tm, D_MODEL), F32),
                        pltpu.VMEM((N_EXPERTS, D_MODEL, D_EXPERT), BF16),
                        pltpu.VMEM((N_EXPERTS, D_MODEL, D_EXPERT), BF16),
                        pltpu.VMEM((N_EXPERTS, D_EXPERT, D_MODEL), BF16),
                        pltpu.VMEM((WEIGHT_STAGE_SLOTS, D_MODEL, D_EXPERT), F32),
                        pltpu.VMEM((WEIGHT_STAGE_SLOTS, D_EXPERT, D_MODEL), F32),
                        pltpu.SemaphoreType.DMA((WEIGHT_STAGE_SLOTS,)),
                        pltpu.SemaphoreType.DMA((WEIGHT_STAGE_SLOTS,))],
        compiler_params=_cparams(("arbitrary",)),
        name="moe",
    )(u2, logits, experts[0], experts[1], experts[2], h1, mods, lw['ln2_g'], lw['ln2_b'])


def _rope_tables(dec_seq):
    n = HEAD_DIM // 4
    pos = jnp.arange(dec_seq)
    row = (pos // GRID_W).astype(F32)
    col = (pos % GRID_W).astype(F32)
    inv = 1.0 / (ROPE_THETA ** (jnp.arange(n, dtype=F32) / n))
    ra, ca = row[:, None] * inv, col[:, None] * inv
    cos64 = jnp.concatenate([jnp.cos(ra), jnp.cos(ra), jnp.cos(ca), jnp.cos(ca)], axis=-1)
    sin64 = jnp.concatenate([-jnp.sin(ra), jnp.sin(ra), -jnp.sin(ca), jnp.sin(ca)], axis=-1)
    return jnp.concatenate([cos64, cos64], axis=-1), jnp.concatenate([sin64, sin64], axis=-1)


def _block_diag2(w):
    z = jnp.zeros_like(w[:, 0])
    return jnp.concatenate([jnp.concatenate([w[:, 0], z], axis=2), jnp.concatenate([z, w[:, 1]], axis=2)], axis=1)


def _derived_weights(p):
    d = D_MODEL
    depth = p['w_in'].shape[0]
    row = lambda a, w: a.reshape(depth, 1, w)
    ng = len(POOL_WINDOWS)
    pool_blk = (p['pool_w'][:, :, :, None, :] * jnp.eye(ng, dtype=F32)[None, :, None, :, None]).reshape(
        depth, C_WIDTH, C_WIDTH)
    lane_pad = ROUTER_LANES - N_GROUPS - N_EXPERTS
    w_router = jnp.pad(jnp.concatenate([p['rt_grp_w'], p['rt_exp_w']], axis=-1), ((0, 0), (0, 0), (0, lane_pad)))
    b_router = jnp.pad(jnp.concatenate([p['rt_grp_b'], p['rt_exp_b']], axis=-1), ((0, 0), (0, lane_pad)))
    b_router = b_router.reshape(depth, 1, ROUTER_LANES)
    return {
        'w2blk': _block_diag2(p['rw_w2']).astype(BF16),
        'w0': row(p['rw_w0'], 2 * A_WIDTH),
        'a2blk': _block_diag2(p['rw_a2']).astype(BF16),
        'a0': row(p['rw_a0'], 2 * A_WIDTH),
        'g2': p['rw_g2'].astype(BF16),
        'kkw': row(p['rw_kk'], A_WIDTH),
        'ka': jnp.tile(row(p['rw_ka'], A_WIDTH), (1, 1, 2)),
        'rk': row(p['rw_rk'], A_WIDTH),
        'gn_g': row(p['rw_gn_g'], A_WIDTH),
        'gn_b': row(p['rw_gn_b'], A_WIDTH),
        'qn': jnp.tile(row(p['q_norm'], HEAD_DIM), (1, 1, B_Q_HEADS)),
        'knw': jnp.tile(row(p['k_norm'], HEAD_DIM), (1, 1, B_KV_HEADS)),
        'pool_blk': pool_blk.astype(BF16),
        'pool_scale': row(p['pool_scale'], C_WIDTH),
        'ln1_g': row(p['ln1_g'], d), 'ln1_b': row(p['ln1_b'], d),
        'ln2_g': row(p['ln2_g'], d), 'ln2_b': row(p['ln2_b'], d),
        'w_router_hi': w_router.astype(BF16),
        'w_router_lo': (w_router - w_router.astype(BF16).astype(F32)).astype(BF16),
        'b_router': b_router,
    }


def _forward(x_prompt, x_sample, state_rwkv, cache_k, cache_v, c, c_ctx, p):
    batch, seq, d = x_prompt.shape
    dec_batch, dec_seq, _ = x_sample.shape
    depth = p['w_in'].shape[0]
    past = cache_k.shape[2]
    n_ctx = batch * seq
    n_tok = n_ctx + dec_batch * dec_seq
    alpha = (2 * depth) ** 0.25
    bounds = tuple(n_ctx + j * dec_seq for j in range(dec_batch))
    tm = math.gcd(n_ctx, dec_seq, 512)
    tm_moe = math.gcd(n_ctx, dec_seq, 512)
    tq_ctx = min(256, seq)
    tq_dec = min(256, dec_seq)
    sb_ctx = 2

    cond8 = jnp.zeros((8, d), F32).at[0].set(c_ctx).at[1:1 + dec_batch].set(c)
    mods = _modulation(cond8, p['w_mod'], p['b_mod'])
    cos, sin = _rope_tables(dec_seq)
    w_in16, w_out16 = p['w_in'].astype(BF16), p['w_out'].astype(BF16)
    experts = (p['moe_w1'], p['moe_w3'], p['moe_w2'])
    lw = _derived_weights(p)
    head_id = jnp.arange(2 * HEAD_DIM) // HEAD_DIM
    ones = (head_id[:, None] == head_id[None, :]).astype(BF16)

    h = (x_prompt.reshape(n_ctx, d), x_sample.reshape(dec_batch * dec_seq, d))
    ck4 = cache_k.reshape(dec_batch, depth, past, KV_WIDTH)
    cv4 = cache_v.reshape(dec_batch, depth, past, KV_WIDTH)
    new_s, new_k, new_v = [], [], []
    for l in range(depth):
        (r, v, kk, logw, kd, bvec, g, bonus, q, k_att, v_att, pp) = _in_projection(
            h, mods, w_in16, l, lw, cos, sin, ones, tm=tm, bounds=bounds)
        scan_ops = (r, v, kk, logw, kd, bvec)
        yf_c, yb_c, s_fin = _rwkv_scan(scan_ops, None, l, row0=0, nseq=batch, seqlen=seq, sb=sb_ctx,
                                       nb=seq // CHUNK, want_state=True)
        yf_d, yb_d = _rwkv_scan(scan_ops, state_rwkv, l, row0=n_ctx, nseq=dec_batch, seqlen=dec_seq,
                                sb=dec_batch, nb=4, want_state=False)
        yf = (yf_c.reshape(n_ctx, A_WIDTH), yf_d.reshape(n_tok - n_ctx, A_WIDTH))
        yb = (yb_c.reshape(n_ctx, A_WIDTH), yb_d.reshape(n_tok - n_ctx, A_WIDTH))
        att_c, pool_c = _attention(q, k_att, v_att, None, pp, lw['pool_blk'], lw['pool_scale'], l,
                                   row0=0, nseq=batch, seqlen=seq, tq=tq_ctx)
        att_d, pool_d = _attention(q, k_att, v_att, (ck4, cv4), pp, lw['pool_blk'], lw['pool_scale'], l,
                                   row0=n_ctx, nseq=dec_batch, seqlen=dec_seq, tq=tq_dec)
        att, pool = (att_c, att_d), (pool_c, pool_d)
        h1, u2, logits = _out_projection(yf, yb, g, bonus, att, pool, h, mods, w_out16, l, lw, ones,
                                         tm=tm, bounds=bounds, alpha=alpha)
        h = _moe(u2, logits, h1, mods, experts, l, lw, tm=tm_moe, sub=128, bounds=bounds, alpha=alpha,
                 n_ctx=n_ctx)
        new_s.append(s_fin)
        new_k.append(k_att[:n_ctx].reshape(batch, seq, B_KV_HEADS, HEAD_DIM))
        new_v.append(v_att[:n_ctx].reshape(batch, seq, B_KV_HEADS, HEAD_DIM))
    return (h[0].reshape(batch, seq, d), h[1].reshape(dec_batch, dec_seq, d),
            jnp.stack(new_s, axis=1), jnp.stack(new_k, axis=1), jnp.stack(new_v, axis=1))


def kernel(x_prompt, x_sample, state_rwkv, cache_k, cache_v, c, c_ctx, w_in, w_out, w_mod, b_mod, ln1_g, ln1_b, ln2_g, ln2_b, rw_w0, rw_w2, rw_a0, rw_a2, rw_g2, rw_kk, rw_ka, rw_rk, rw_gn_g, rw_gn_b, q_norm, k_norm, pool_w, pool_scale, rt_grp_w, rt_grp_b, rt_exp_w, rt_exp_b, moe_w1, moe_w3, moe_w2):
    p = dict(w_in=w_in, w_out=w_out, w_mod=w_mod, b_mod=b_mod, ln1_g=ln1_g, ln1_b=ln1_b, ln2_g=ln2_g,
             ln2_b=ln2_b, rw_w0=rw_w0, rw_w2=rw_w2, rw_a0=rw_a0, rw_a2=rw_a2, rw_g2=rw_g2, rw_kk=rw_kk,
             rw_ka=rw_ka, rw_rk=rw_rk, rw_gn_g=rw_gn_g, rw_gn_b=rw_gn_b, q_norm=q_norm, k_norm=k_norm,
             pool_w=pool_w, pool_scale=pool_scale, rt_grp_w=rt_grp_w, rt_grp_b=rt_grp_b, rt_exp_w=rt_exp_w,
             rt_exp_b=rt_exp_b, moe_w1=moe_w1, moe_w3=moe_w3, moe_w2=moe_w2)
    return _forward(x_prompt, x_sample, state_rwkv, cache_k, cache_v, c, c_ctx, p)
```

```python
import functools
import math

import jax
import jax.numpy as jnp
from jax import lax
from jax.experimental import pallas as pl
from jax.experimental.pallas import tpu as pltpu

F32 = jnp.float32
BF16 = jnp.bfloat16
I32 = jnp.int32

D_MODEL = 1024
GRID_W = 64
HEAD_DIM = 64
A_HEADS = 6
A_WIDTH = A_HEADS * HEAD_DIM
LORA_W = 64
LORA_A = 64
LORA_G = 128
N_DIR = 2
DECAY_SCALE = 0.606531
B_Q_HEADS = 6
B_KV_HEADS = 2
B_WIDTH = B_Q_HEADS * HEAD_DIM
KV_WIDTH = B_KV_HEADS * HEAD_DIM
ROPE_THETA = 10000.0
POOL_WINDOWS = (2, 4, 8, 16)
C_WIDTH = 256
IN_COLS = 2432
N_GROUPS = 4
EXP_PER_GROUP = 4
N_EXPERTS = 16
D_EXPERT = 256
LN_EPS = 1e-5
GN_EPS = 64e-5
QK_EPS = 1e-6

_O_R, _O_K, _O_V = 0, 384, 768
_O_LW, _O_LA, _O_LG = 1152, 1280, 1408
_O_Q, _O_KB, _O_VB, _O_P = 1536, 1920, 2048, 2176

CHUNK = 64
ROUTER_LANES = 128
WEIGHT_STAGE_SLOTS = 4
ATTN_UNIT_ROWS = 256
VMEM_LIMIT = 56 * 1024 * 1024


def _cparams(sem):
    return pltpu.CompilerParams(dimension_semantics=sem, vmem_limit_bytes=VMEM_LIMIT)


def _sigmoid(x):
    return 1.0 / (1.0 + jnp.exp(-x))


def _dot(a, b):
    return jnp.dot(a, b, preferred_element_type=F32)


def _dot_nt(a, b):
    return lax.dot_general(a, b, (((1,), (1,)), ((), ())), preferred_element_type=F32)


def _segsum(x, ones2):
    x16 = x.astype(BF16)
    slabs = [_dot(x16[:, j:j + 128], ones2) for j in range(0, x.shape[1], 128)]
    return slabs[0] if len(slabs) == 1 else jnp.concatenate(slabs, axis=1)


def _rope128(x, cos, sin):
    lane = lax.broadcasted_iota(I32, x.shape, 1)
    first = (lane % 32) < 16
    up = pltpu.roll(x, 128 - 16, 1)
    dn = pltpu.roll(x, 16, 1)
    return x * cos + jnp.where(first, up, dn) * sin


def _layer_norm(z, g, b):
    mu = jnp.mean(z, axis=-1, keepdims=True)
    zc = z - mu
    var = jnp.mean(zc * zc, axis=-1, keepdims=True)
    return zc * lax.rsqrt(var + LN_EPS) * g + b


def _mod_kernel(c_ref, w_ref, b_ref, o_ref):
    x = c_ref[...]
    x = x * _sigmoid(x)
    w = w_ref[...]
    x_hi = x.astype(BF16)
    x_lo = (x - x_hi.astype(F32)).astype(BF16)
    w_hi = w.astype(BF16)
    w_lo = (w - w_hi.astype(F32)).astype(BF16)
    o_ref[...] = (_dot(x_hi, w_hi) + _dot(x_lo, w_hi) + _dot(x_hi, w_lo)) + b_ref[...]


def _modulation(cond8, w_mod, b_mod):
    depth = w_mod.shape[0]
    tn = 1536
    return pl.pallas_call(
        _mod_kernel,
        grid=(depth, 6 * D_MODEL // tn),
        in_specs=[
            pl.BlockSpec((8, D_MODEL), lambda l, j: (0, 0)),
            pl.BlockSpec((None, D_MODEL, tn), lambda l, j: (l, 0, j)),
            pl.BlockSpec((None, 1, tn), lambda l, j: (l, 0, j)),
        ],
        out_specs=pl.BlockSpec((None, 8, tn), lambda l, j: (l, 0, j)),
        out_shape=jax.ShapeDtypeStruct((depth, 8, 6 * D_MODEL), F32),
        compiler_params=_cparams(("arbitrary", "arbitrary")),
        name="modulation",
    )(cond8, w_mod, b_mod.reshape(depth, 1, 6 * D_MODEL))


def _group_of_row(row0, bounds):
    g = jnp.zeros((), I32)
    for b in bounds:
        g = g + jnp.where(row0 >= b, 1, 0).astype(I32)
    return g


def _in_kernel(hc_ref, hd_ref, mod_ref, w_ref, w2_ref, w0_ref, a2_ref, a0_ref, g2_ref, kkw_ref, ka_ref,
               rk_ref, qn_ref, knw_ref, cos_ref, sin_ref, ones_ref,
               r_o, v_o, kk_o, lw_o, kd_o, b_o, g_o, bon_o, q_o, k_o, vb_o, p_o, *, tm, bounds, n_ctx_tiles):
    grp = _group_of_row(pl.program_id(0) * tm, bounds)
    mod = mod_ref[pl.ds(grp, 1), :]
    sh1 = mod[:, 0:D_MODEL]
    sc1 = mod[:, D_MODEL:2 * D_MODEL]
    is_ctx = pl.program_id(0) < n_ctx_tiles
    h = jnp.where(is_ctx, hc_ref[...], hd_ref[...])
    u = h * (1.0 + sc1) + sh1
    ones = ones_ref[...]
    scale = 1.0 / math.sqrt(HEAD_DIM)
    proj = _dot(u.astype(BF16), w_ref[...])

    r = proj[:, _O_R:_O_R + A_WIDTH]
    k = proj[:, _O_K:_O_K + A_WIDTH]
    v = proj[:, _O_V:_O_V + A_WIDTH]
    lo_w = proj[:, _O_LW:_O_LW + 128]
    lo_a = proj[:, _O_LA:_O_LA + 128]
    lo_g = proj[:, _O_LG:_O_LG + 128]

    w_pre = w0_ref[...] + _dot(jnp.tanh(lo_w).astype(BF16), w2_ref[...])
    logw = -DECAY_SCALE * _sigmoid(w_pre)
    a = _sigmoid(a0_ref[...] + _dot(lo_a.astype(BF16), a2_ref[...]))
    g = _dot(_sigmoid(lo_g).astype(BF16), g2_ref[...])

    kk = k * kkw_ref[...]
    kk = kk * lax.rsqrt(jnp.maximum(_segsum(kk * kk, ones), 1e-24))
    k2 = jnp.concatenate([k, k], axis=1)
    kd = k2 * (1.0 + (a - 1.0) * ka_ref[...])
    bvec = jnp.concatenate([kk, kk], axis=1) * a
    kd_sum = kd[:, 0:A_WIDTH] + kd[:, A_WIDTH:2 * A_WIDTH]
    bonus = _segsum(r * kd_sum * rk_ref[...], ones) * v

    r_o[...] = r.astype(BF16)
    v_o[...] = v.astype(BF16)
    kk_o[...] = kk.astype(BF16)
    lw_o[...] = logw
    kd_o[...] = kd.astype(BF16)
    b_o[...] = bvec.astype(BF16)
    g_o[...] = g.astype(BF16)
    bon_o[...] = bonus.astype(BF16)

    cos = cos_ref[...]
    sin = sin_ref[...]
    rope = lambda x: jnp.where(is_ctx, x, _rope128(x, cos, sin))
    q = proj[:, _O_Q:_O_Q + B_WIDTH]
    qn = q * lax.rsqrt(_segsum(q * q, ones) * (1.0 / HEAD_DIM) + QK_EPS) * qn_ref[...]
    q_o[...] = (jnp.concatenate([rope(qn[:, j:j + 128]) for j in range(0, B_WIDTH, 128)], axis=1)
                * scale).astype(BF16)
    kb = proj[:, _O_KB:_O_KB + KV_WIDTH]
    kn = kb * lax.rsqrt(_segsum(kb * kb, ones) * (1.0 / HEAD_DIM) + QK_EPS) * knw_ref[...]
    k_o[...] = rope(kn)
    vb_o[...] = proj[:, _O_VB:_O_VB + KV_WIDTH]
    p_o[...] = proj[:, _O_P:_O_P + C_WIDTH]


def _pair_specs(tm, nct):
    rows_c = lambda w: pl.BlockSpec((tm, w), lambda i: (jnp.minimum(i, nct - 1), 0))
    rows_d = lambda w: pl.BlockSpec((tm, w), lambda i: (jnp.maximum(i - nct, 0), 0))
    return rows_c, rows_d


def _layer_spec(layer):
    return lambda *shape: pl.BlockSpec((None,) + shape, lambda *_: (layer,) + (0,) * len(shape))


def _in_projection(h, mods, w_in, layer, lw, cos, sin, ones, *, tm, bounds):
    nct = h[0].shape[0] // tm
    t = h[0].shape[0] + h[1].shape[0]
    full = lambda shape: pl.BlockSpec(shape, lambda i: (0,) * len(shape))
    rows = lambda w: pl.BlockSpec((tm, w), lambda i: (i, 0))
    rows_c, rows_d = _pair_specs(tm, nct)
    lay = _layer_spec(layer)
    tiles_per_req = cos.shape[0] // tm
    table = pl.BlockSpec((tm, 128), lambda i: (jnp.maximum(i - nct, 0) % tiles_per_req, 0))
    widths = (A_WIDTH, A_WIDTH, A_WIDTH, 2 * A_WIDTH, 2 * A_WIDTH, 2 * A_WIDTH, A_WIDTH, A_WIDTH,
              B_WIDTH, KV_WIDTH, KV_WIDTH, C_WIDTH)
    dtypes = (BF16, BF16, BF16, F32, BF16, BF16, BF16, BF16, BF16, F32, F32, F32)
    return pl.pallas_call(
        functools.partial(_in_kernel, tm=tm, bounds=bounds, n_ctx_tiles=nct),
        grid=(t // tm,),
        in_specs=[
            rows_c(D_MODEL), rows_d(D_MODEL), lay(8, 6 * D_MODEL), lay(D_MODEL, IN_COLS),
            lay(128, 2 * A_WIDTH), lay(1, 2 * A_WIDTH), lay(128, 2 * A_WIDTH), lay(1, 2 * A_WIDTH),
            lay(LORA_G, A_WIDTH), lay(1, A_WIDTH), lay(1, 2 * A_WIDTH), lay(1, A_WIDTH),
            lay(1, B_WIDTH), lay(1, KV_WIDTH), table, table, full((2 * HEAD_DIM, 2 * HEAD_DIM)),
        ],
        out_specs=[rows(w) for w in widths],
        out_shape=[jax.ShapeDtypeStruct((t, w), dt) for w, dt in zip(widths, dtypes)],
        compiler_params=_cparams(("arbitrary",)),
        name="in_projection",
    )(h[0], h[1], mods, w_in, lw['w2blk'], lw['w0'], lw['a2blk'], lw['a0'], lw['g2'], lw['kkw'], lw['ka'],
      lw['rk'], lw['qn'], lw['knw'], cos, sin, ones)


def _split2(x):
    hi = x.astype(BF16)
    lo = (x - hi.astype(F32)).astype(BF16)
    return hi, lo


def _scan_kernel(*refs, sb, nb, zero_init, want_state):
    fwd = refs[0:6]
    bwd = refs[6:12]
    pos = 12
    if not zero_init:
        s0_ref = refs[pos]
        pos += 1
    y_outs = (refs[pos], refs[pos + 1])
    pos += 2
    if want_state:
        sfin_o = refs[pos]
        pos += 1
    h_ref = refs[pos]

    step = pl.program_id(1)
    nstep = pl.num_programs(1)
    npair = A_HEADS // 2
    pw = 2 * HEAD_DIM
    pairs = [(q, d, hp) for q in range(sb) for d in range(N_DIR) for hp in range(npair)]

    @pl.when(step == 0)
    def _():
        h_ref[...] = jnp.zeros_like(h_ref)
        if not zero_init:
            for (q, d, hp) in pairs:
                for e in range(2):
                    lo = e * HEAD_DIM
                    h_ref[q, d, hp, lo:lo + HEAD_DIM, lo:lo + HEAD_DIM] = s0_ref[q, d, 2 * hp + e].T

    def iota(shape, dim):
        return lax.broadcasted_iota(I32, shape, dim)

    first64 = iota((CHUNK, pw), 1) < HEAD_DIM
    eye_p = ((iota((CHUNK, pw), 1) % HEAD_DIM) == iota((CHUNK, pw), 0)).astype(F32)
    eye2 = (iota((pw, pw), 0) == iota((pw, pw), 1)).astype(F32)
    same_head = (iota((pw, pw), 0) < HEAD_DIM) == (iota((pw, pw), 1) < HEAD_DIM)
    ti = iota((CHUNK, CHUNK), 0)
    si = iota((CHUNK, CHUNK), 1)
    incl16 = ((si <= ti).astype(BF16), (si >= ti).astype(BF16))
    t4 = iota((2 * CHUNK, 2 * pw), 0)
    s4 = iota((2 * CHUNK, 2 * pw), 1) % CHUNK
    incl_off = jnp.where(t4 < CHUNK, 0, 1)
    t4 = t4 % CHUNK
    mask4 = (s4 < t4 + incl_off, s4 > t4 - incl_off)

    def bdiag(x):
        return jnp.concatenate([jnp.where(first64, x, 0.0), jnp.where(first64, 0.0, x)], axis=0)

    lhs, rhs1, bkt, gcol, v_p, amat = {}, {}, {}, {}, {}, {}
    blocks = [(q, d, j) for q in range(sb) for d in range(N_DIR) for j in range(nb)]
    csum = {}
    for (q, d, j) in blocks:
        hi, lo = _split2((fwd, bwd)[d][3][q, pl.ds(j * CHUNK, CHUNK), :])
        csum[(q, d, j)] = _dot(incl16[d], hi) + _dot(incl16[d], lo)
    for (q, d, j) in blocks:
        r_ref, v_ref, kk_ref, lw_ref, kd_ref, b_ref = (fwd, bwd)[d]
        rows = pl.ds(j * CHUNK, CHUNK)
        lw = lw_ref[q, rows, :]
        cs = csum[(q, d, j)]
        gam = jnp.exp(cs)
        gam_prev = jnp.exp(cs - lw)
        gam_inv = jnp.exp(-cs)
        gam_end = jnp.exp(cs[CHUNK - 1:CHUNK, :] if d == 0 else cs[0:1, :])
        a_hat = -kk_ref[q, rows, :] * gam_prev
        r_hat = r_ref[q, rows, :] * gam
        b_hat = b_ref[q, rows, :] * gam_inv
        k_hat = kd_ref[q, rows, :] * gam_inv
        v_all = v_ref[q, rows, :]
        for hp in range(npair):
            key = (q, d, hp, j)
            sl = slice(hp * pw, (hp + 1) * pw)
            lhs[key] = jnp.concatenate([a_hat[:, sl], r_hat[:, sl]], axis=0).astype(BF16)
            bh, kh, ge = b_hat[:, sl], k_hat[:, sl], gam_end[:, sl]
            rhs1[key] = jnp.concatenate([bdiag(bh), bdiag(kh)], axis=0).astype(BF16)
            gcol[key] = jnp.sum(eye2 * ge, axis=1, keepdims=True)
            bkt[key] = (jnp.concatenate([bh, kh], axis=0) * ge).T.astype(BF16)
            v_p[key] = v_all[:, sl]
    keys = list(lhs.keys())
    for key in keys:
        amat[key] = _dot_nt(lhs[key], rhs1[key])
    npow, tcur, a_rb, lhs2, vbd = {}, {}, {}, {}, {}
    for key in keys:
        am = jnp.where(mask4[key[1]], amat[key], 0.0)
        npow[key] = am[0:CHUNK, 0:pw]
        tcur[key] = eye_p + npow[key]
        a_rb[key] = am[CHUNK:2 * CHUNK, 0:pw].astype(BF16)
        vbd[key] = bdiag(v_p[key])
        lhs2[key] = jnp.concatenate([lhs[key], am[:, pw:2 * pw].astype(BF16)], axis=1)
    nsq = int(math.log2(CHUNK)) - 1
    for key in keys:
        npow[key] = _dot(npow[key].astype(BF16), bdiag(npow[key]).astype(BF16))
    for _ in range(nsq - 1):
        for key in keys:
            rhs = jnp.concatenate([bdiag(npow[key]), bdiag(tcur[key])], axis=1).astype(BF16)
            res = _dot(npow[key].astype(BF16), rhs)
            npow[key] = res[:, 0:pw]
            tcur[key] = tcur[key] + res[:, pw:2 * pw]
    t_inv = {}
    for key in keys:
        t_inv[key] = (tcur[key] + _dot(npow[key].astype(BF16), bdiag(tcur[key]).astype(BF16))).astype(BF16)

    hcur = {pr: h_ref[pr[0], pr[1], pr[2]] for pr in pairs}
    for jj in range(nb):
        z, u = {}, {}
        kof = lambda pr: (pr[0], pr[1], pr[2], jj if pr[1] == 0 else nb - 1 - jj)
        for pr in pairs:
            key = kof(pr)
            hv = jnp.concatenate([hcur[pr].astype(BF16), vbd[key]], axis=0)
            z[pr] = _dot(lhs2[key], hv)
        for pr in pairs:
            u[pr] = _dot(t_inv[kof(pr)], bdiag(z[pr][0:CHUNK]).astype(BF16))
        for pr in pairs:
            key = kof(pr)
            q, d, hp, j = key
            y = z[pr][CHUNK:2 * CHUNK] + _dot(a_rb[key], bdiag(u[pr]).astype(BF16))
            y_outs[d][q, pl.ds(j * CHUNK, CHUNK), hp * pw:(hp + 1) * pw] = y.astype(BF16)
            uv = jnp.concatenate([u[pr].astype(BF16), v_p[key]], axis=0)
            hcur[pr] = hcur[pr] * gcol[key] + jnp.where(same_head, _dot(bkt[key], uv), 0.0)
    for pr in pairs:
        h_ref[pr[0], pr[1], pr[2]] = hcur[pr]

    if want_state:
        @pl.when(step == nstep - 1)
        def _():
            for (q, d, hp) in pairs:
                for e in range(2):
                    lo = e * HEAD_DIM
                    sfin_o[q, d, 2 * hp + e] = hcur[(q, d, hp)][lo:lo + HEAD_DIM, lo:lo + HEAD_DIM].T


def _rwkv_scan(ops, s0, layer, *, row0, nseq, seqlen, sb, nb, want_state):
    g = nb * CHUNK
    nstep = seqlen // g
    zero_init = s0 is None
    ops3 = [a.reshape(a.shape[0] // seqlen, seqlen, a.shape[1]) for a in ops]
    base = row0 // (seqlen * sb)
    fidx = lambda col: (lambda s, c: (base + s, c, col))
    bidx = lambda col: (lambda s, c: (base + s, nstep - 1 - c, col))
    blk = lambda im: pl.BlockSpec((sb, g, A_WIDTH), im)
    in_specs = [blk(fidx(0))] * 6 + [blk(bidx(0))] * 3 + [blk(bidx(1))] * 3
    args = ops3 + ops3
    state_spec = pl.BlockSpec((sb, N_DIR, A_HEADS, HEAD_DIM, HEAD_DIM), lambda s, c: (s, 0, 0, 0, 0))
    if not zero_init:
        in_specs.append(pl.BlockSpec((sb, None, N_DIR, A_HEADS, HEAD_DIM, HEAD_DIM),
                                     lambda s, c: (s, layer, 0, 0, 0, 0)))
        args.append(s0)
    out_specs = [blk(lambda s, c: (s, c, 0)), blk(lambda s, c: (s, nstep - 1 - c, 0))]
    out_shape = [jax.ShapeDtypeStruct((nseq, seqlen, A_WIDTH), BF16)] * 2
    if want_state:
        out_specs.append(state_spec)
        out_shape.append(jax.ShapeDtypeStruct((nseq, N_DIR, A_HEADS, HEAD_DIM, HEAD_DIM), F32))
    return pl.pallas_call(
        functools.partial(_scan_kernel, sb=sb, nb=nb, zero_init=zero_init, want_state=want_state),
        grid=(nseq // sb, nstep),
        in_specs=in_specs,
        out_specs=out_specs,
        out_shape=out_shape,
        scratch_shapes=[pltpu.VMEM((sb, N_DIR, A_HEADS // 2, 2 * HEAD_DIM, 2 * HEAD_DIM), F32)],
        compiler_params=_cparams(("arbitrary", "arbitrary")),
        name="rwkv_scan_ctx" if zero_init else "rwkv_scan_dec",
    )(*args)


def _attn_kernel(*refs, with_cache):
    grp = B_Q_HEADS // B_KV_HEADS
    if with_cache:
        q_ref, k_ref, v_ref, ck_ref, cv_ref, p_ref, pw_ref, psc_ref, o_ref, pool_o = refs
        kall = jnp.concatenate([k_ref[...], ck_ref[...]], axis=0)
        vall = jnp.concatenate([v_ref[...], cv_ref[...]], axis=0)
    else:
        q_ref, k_ref, v_ref, p_ref, pw_ref, psc_ref, o_ref, pool_o = refs
        kall = k_ref[...]
        vall = v_ref[...]

    @pl.when(pl.program_id(1) == 0)
    def _():
        pool_o[...] = _pool_mix(p_ref[...], pw_ref[...], psc_ref[...]).astype(BF16)

    q = q_ref[...]
    k16 = [kall[:, hk * HEAD_DIM:(hk + 1) * HEAD_DIM].astype(BF16) for hk in range(B_KV_HEADS)]
    v16 = [vall[:, hk * HEAD_DIM:(hk + 1) * HEAD_DIM].astype(BF16) for hk in range(B_KV_HEADS)]
    rows = ATTN_UNIT_ROWS if q.shape[0] % ATTN_UNIT_ROWS == 0 else q.shape[0]
    units = [(r0, j) for r0 in range(0, q.shape[0], rows) for j in range(B_Q_HEADS)]

    def scores(unit):
        r0, j = unit
        return _dot_nt(q[r0:r0 + rows, j * HEAD_DIM:(j + 1) * HEAD_DIM], k16[j // grp])

    outs = {}
    s_next = scores(units[0])
    for i, unit in enumerate(units):
        s = s_next
        if i + 1 < len(units):
            s_next = scores(units[i + 1])
        m = jnp.max(s, axis=-1, keepdims=True)
        p = jnp.exp(s - m)
        den = jnp.sum(p, axis=-1, keepdims=True)
        outs[unit] = _dot(p.astype(BF16), v16[unit[1] // grp]) / den
    for r0 in range(0, q.shape[0], rows):
        o_ref[r0:r0 + rows, :] = jnp.concatenate([outs[(r0, j)] for j in range(B_Q_HEADS)], axis=1).astype(BF16)


def _attention(q, k_tok, v_tok, cache, p_tok, pool_w, pool_scale, layer, *, row0, nseq, seqlen, tq):
    nq = seqlen // tq
    base = row0 // tq
    kbase = row0 // seqlen
    kspec = pl.BlockSpec((seqlen, KV_WIDTH), lambda s, i: (kbase + s, 0))
    in_specs = [pl.BlockSpec((tq, B_WIDTH), lambda s, i: (base + s * nq + i, 0)), kspec, kspec]
    args = [q, k_tok, v_tok]
    if cache is not None:
        ck, cv = cache
        cspec = pl.BlockSpec((None, None, ck.shape[2], KV_WIDTH), lambda s, i: (s, layer, 0, 0))
        in_specs += [cspec, cspec]
        args += [ck, cv]
    lay = _layer_spec(layer)
    in_specs += [pl.BlockSpec((seqlen, C_WIDTH), lambda s, i: (kbase + s, 0)), lay(C_WIDTH, C_WIDTH), lay(1, C_WIDTH)]
    args += [p_tok, pool_w, pool_scale]
    return pl.pallas_call(
        functools.partial(_attn_kernel, with_cache=cache is not None),
        grid=(nseq, nq),
        in_specs=in_specs,
        out_specs=[pl.BlockSpec((tq, B_WIDTH), lambda s, i: (s * nq + i, 0)),
                   pl.BlockSpec((seqlen, C_WIDTH), lambda s, i: (s, 0))],
        out_shape=[jax.ShapeDtypeStruct((nseq * seqlen, B_WIDTH), BF16),
                   jax.ShapeDtypeStruct((nseq * seqlen, C_WIDTH), BF16)],
        compiler_params=_cparams(("arbitrary", "arbitrary")),
        name="attention_ctx" if cache is None else "attention_dec",
    )(*args)


def _pool_mix(x, w_blk, scale):
    seqlen = x.shape[0]
    t = lax.broadcasted_iota(I32, x.shape, 0)
    lane = lax.broadcasted_iota(I32, x.shape, 1)

    def down(a, k):
        return jnp.where(t >= k, pltpu.roll(a, k, 0), 0.0)

    def up(a, k):
        return jnp.where(t < seqlen - k, pltpu.roll(a, seqlen - k, 0), 0.0)

    past = [x]
    futr = [x]
    for j in range(len(POOL_WINDOWS) - 1):
        past.append(past[j] + down(past[j], 2 ** j))
        futr.append(futr[j] + up(futr[j], 2 ** j))
    win_sum = jnp.zeros_like(x)
    cnt = jnp.zeros_like(x)
    for gi, win in enumerate(POOL_WINDOWS):
        half = win // 2
        in_group = (lane // HEAD_DIM) == gi
        win_sum = jnp.where(in_group, down(past[gi], 1) + futr[gi], win_sum)
        n = (jnp.minimum(t + half, seqlen) - jnp.maximum(t - half, 0)).astype(F32)
        cnt = jnp.where(in_group, n, cnt)
    d = win_sum / cnt - x
    return _dot(d.astype(BF16), w_blk) * scale


def _out_kernel(yfc_ref, yfd_ref, ybc_ref, ybd_ref, g_ref, bon_ref, attc_ref, attd_ref, poolc_ref, poold_ref,
                hc_ref, hd_ref, mod_ref, w_ref, lng_ref, lnb_ref, gng_ref, gnb_ref, ones_ref, wrh_ref, wrl_ref,
                br_ref, h1_o, u2_o, lg_o, *, tm, bounds, alpha, n_ctx_tiles):
    grp = _group_of_row(pl.program_id(0) * tm, bounds)
    mod = mod_ref[pl.ds(grp, 1), :]
    g1 = mod[:, 2 * D_MODEL:3 * D_MODEL]
    sh2 = mod[:, 3 * D_MODEL:4 * D_MODEL]
    sc2 = mod[:, 4 * D_MODEL:5 * D_MODEL]
    ones = ones_ref[...]
    is_ctx = pl.program_id(0) < n_ctx_tiles
    half = tm // 2
    halves = [pl.ds(s * half, half) for s in range(2)]
    pick = lambda c_ref, d_ref, rows: jnp.where(is_ctx, c_ref[rows, :], d_ref[rows, :])
    y = [pick(yfc_ref, yfd_ref, rows).astype(F32) + pick(ybc_ref, ybd_ref, rows).astype(F32) for rows in halves]
    mu = [_segsum(v, ones) * (1.0 / HEAD_DIM) for v in y]
    yc = [v - m_ for v, m_ in zip(y, mu)]
    var = [_segsum(v * v, ones) * (1.0 / HEAD_DIM) for v in yc]
    m = []
    for rows, v, vr in zip(halves, yc, var):
        yn = v * lax.rsqrt(vr + GN_EPS) * gng_ref[...] + gnb_ref[...]
        a_out = (yn + bon_ref[rows, :]) * g_ref[rows, :]
        m.append(_dot(a_out.astype(BF16), w_ref[0:A_WIDTH, :])
                 + _dot(pick(attc_ref, attd_ref, rows), w_ref[A_WIDTH:A_WIDTH + B_WIDTH, :])
                 + _dot(pick(poolc_ref, poold_ref, rows), w_ref[A_WIDTH + B_WIDTH:D_MODEL, :]))
    for rows, m_ in zip(halves, m):
        h1 = _layer_norm(alpha * pick(hc_ref, hd_ref, rows) + g1 * m_, lng_ref[...], lnb_ref[...])
        h1_o[rows, :] = h1
        u2 = h1 * (1.0 + sc2) + sh2
        u_hi = u2.astype(BF16)
        u_lo = (u2 - u_hi.astype(F32)).astype(BF16)
        u2_o[rows, :] = u_hi
        lg_o[rows, :] = (_dot(u_hi, wrh_ref[...]) + _dot(u_lo, wrh_ref[...]) + _dot(u_hi, wrl_ref[...])) + br_ref[...]


def _out_projection(yf, yb, g, bonus, att, pool, h, mods, w_out, layer, lw, ones, *, tm, bounds, alpha):
    t = h[0].shape[0] + h[1].shape[0]
    nct = h[0].shape[0] // tm
    full = lambda shape: pl.BlockSpec(shape, lambda i: (0,) * len(shape))
    rows = lambda w: pl.BlockSpec((tm, w), lambda i: (i, 0))
    rows_c, rows_d = _pair_specs(tm, nct)
    lay = _layer_spec(layer)
    return pl.pallas_call(
        functools.partial(_out_kernel, tm=tm, bounds=bounds, alpha=alpha, n_ctx_tiles=nct),
        grid=(t // tm,),
        in_specs=[rows_c(A_WIDTH), rows_d(A_WIDTH), rows_c(A_WIDTH), rows_d(A_WIDTH), rows(A_WIDTH), rows(A_WIDTH),
                  rows_c(B_WIDTH), rows_d(B_WIDTH), rows_c(C_WIDTH), rows_d(C_WIDTH), rows_c(D_MODEL),
                  rows_d(D_MODEL), lay(8, 6 * D_MODEL), lay(D_MODEL, D_MODEL), lay(1, D_MODEL),
                  lay(1, D_MODEL), lay(1, A_WIDTH), lay(1, A_WIDTH), full((2 * HEAD_DIM, 2 * HEAD_DIM)),
                  lay(D_MODEL, ROUTER_LANES), lay(D_MODEL, ROUTER_LANES), lay(1, ROUTER_LANES)],
        out_specs=[rows(D_MODEL), rows(D_MODEL), rows(ROUTER_LANES)],
        out_shape=[jax.ShapeDtypeStruct((t, D_MODEL), F32), jax.ShapeDtypeStruct((t, D_MODEL), BF16),
                   jax.ShapeDtypeStruct((t, ROUTER_LANES), F32)],
        compiler_params=_cparams(("arbitrary",)),
        name="out_projection",
    )(yf[0], yf[1], yb[0], yb[1], g, bonus, att[0], att[1], pool[0], pool[1], h[0], h[1], mods, w_out,
      lw['ln1_g'], lw['ln1_b'], lw['gn_g'], lw['gn_b'], ones, lw['w_router_hi'], lw['w_router_lo'], lw['b_router'])


def _router_gates(logits):
    lane = lax.broadcasted_iota(I32, logits.shape, 1)
    lane_f = lane.astype(F32)
    neg = -jnp.inf
    big = 1e9

    def first_lane(mask):
        return jnp.min(jnp.where(mask, lane_f, big), axis=-1, keepdims=True).astype(I32)

    is_grp = lane < N_GROUPS
    gmax = jnp.max(jnp.where(is_grp, logits, neg), axis=-1, keepdims=True)
    sel = first_lane(is_grp & (logits == gmax))
    gsum = jnp.sum(jnp.where(is_grp, jnp.exp(logits - gmax), 0.0), axis=-1, keepdims=True)
    gsel = 1.0 / gsum
    first_exp = N_GROUPS + sel * EXP_PER_GROUP
    in_sel = (lane >= first_exp) & (lane < first_exp + EXP_PER_GROUP)
    v1 = jnp.max(jnp.where(in_sel, logits, neg), axis=-1, keepdims=True)
    i1 = first_lane(in_sel & (logits == v1))
    rest = in_sel & (lane != i1)
    v2 = jnp.max(jnp.where(rest, logits, neg), axis=-1, keepdims=True)
    i2 = first_lane(rest & (logits == v2))
    e2 = jnp.exp(v2 - v1)
    t1 = (1.0 / (1.0 + e2)) * gsel
    t2 = (e2 / (1.0 + e2)) * gsel
    return jnp.where(lane == i1, t1, 0.0) + jnp.where(lane == i2, t2, 0.0), sel


def _weight_stream(jobs, stage_ref, sem):
    nslot = stage_ref.shape[0]
    copy = lambda i: pltpu.make_async_copy(jobs[i][0], stage_ref.at[i % nslot], sem.at[i % nslot])

    def prime():
        for i in range(min(nslot, len(jobs))):
            copy(i).start()

    def drain():
        for i, (_, dst_ref, e) in enumerate(jobs):
            copy(i).wait()
            dst_ref[e] = stage_ref[i % nslot].astype(BF16)
            if i + nslot < len(jobs):
                copy(i + nslot).start()

    return prime, drain


def _moe_kernel(x_ref, lg_ref, w1_hbm, w3_hbm, w2_hbm, h1_ref, mod_ref, lng_ref, lnb_ref, oc_ref, od_ref,
                xs_ref, gs_ref, acc_ref, w1_ref, w3_ref, w2_ref, stage_in_ref, stage_out_ref, sem_in, sem_out,
                *, tm, sub, bounds, alpha, n_ctx_tiles, layer):
    @pl.when(pl.program_id(0) == 0)
    def _():
        jobs_in = [(w.at[layer, e], dst, e) for w, dst in ((w1_hbm, w1_ref), (w3_hbm, w3_ref))
                   for e in range(N_EXPERTS)]
        jobs_out = [(w2_hbm.at[layer, e], w2_ref, e) for e in range(N_EXPERTS)]
        prime_in, drain_in = _weight_stream(jobs_in, stage_in_ref, sem_in)
        prime_out, drain_out = _weight_stream(jobs_out, stage_out_ref, sem_out)
        prime_in()
        prime_out()
        drain_in()
        drain_out()

    gates, sel = _router_gates(lg_ref[...])
    lane = lax.broadcasted_iota(I32, gates.shape, 1)
    onehot = jnp.where(lane == sel, 1.0, 0.0)
    row_i = lax.broadcasted_iota(I32, (tm, tm), 0)
    col_i = lax.broadcasted_iota(I32, (tm, tm), 1)
    earlier = jnp.where(col_i < row_i, 1.0, 0.0).astype(BF16)
    rank = _dot(earlier, onehot.astype(BF16))
    lane1 = lax.broadcasted_iota(I32, (1, ROUTER_LANES), 1)
    counts = jnp.sum(onehot, axis=0, keepdims=True)
    cnt = [jnp.sum(jnp.where(lane1 == g, counts, 0.0)).astype(I32) for g in range(N_GROUPS)]
    off = [jnp.zeros((), I32)]
    for g in range(N_GROUPS - 1):
        off.append(off[g] + cnt[g])
    offs = jnp.zeros((1, ROUTER_LANES), F32)
    for g in range(N_GROUPS):
        offs = jnp.where(lane1 == g, off[g].astype(F32), offs)
    pos = jnp.sum(onehot * (rank + offs), axis=1, keepdims=True).astype(I32)
    pos_row = jnp.broadcast_to(pos, (tm, ROUTER_LANES)).T[0:1, :]
    perm = jnp.where(row_i == pos_row, 1.0, 0.0).astype(BF16)
    perm_t = jnp.where(col_i == pos, 1.0, 0.0).astype(BF16)
    xs_ref[...] = _dot(perm, x_ref[...]).astype(BF16)
    gs_ref[...] = _dot(perm, gates.astype(BF16))
    acc_ref[...] = jnp.zeros_like(acc_ref)

    lane_s = lax.broadcasted_iota(I32, (sub, ROUTER_LANES), 1)
    for s in range(tm // sub):
        rows = pl.ds(s * sub, sub)
        for g in range(N_GROUPS):
            @pl.when((off[g] < (s + 1) * sub) & (off[g] + cnt[g] > s * sub))
            def _():
                xb = xs_ref[rows, :]
                gsb = gs_ref[rows, :]
                hids = []
                for j in range(EXP_PER_GROUP):
                    e = g * EXP_PER_GROUP + j
                    ge = jnp.sum(jnp.where(lane_s == N_GROUPS + e, gsb, 0.0), axis=-1, keepdims=True)
                    h1 = _dot(xb, w1_ref[e])
                    h3 = _dot(xb, w3_ref[e])
                    hids.append(((h1 * _sigmoid(h1)) * h3 * ge).astype(BF16))
                hid = jnp.concatenate(hids, axis=1)
                w2g = w2_ref[g * EXP_PER_GROUP:(g + 1) * EXP_PER_GROUP].reshape(EXP_PER_GROUP * D_EXPERT, D_MODEL)
                acc_ref[rows, :] += _dot(hid, w2g)

    f = _dot(perm_t, acc_ref[...].astype(BF16))
    grp = _group_of_row(pl.program_id(0) * tm, bounds)
    g2 = mod_ref[pl.ds(grp, 1), 5 * D_MODEL:6 * D_MODEL]
    out = _layer_norm(alpha * h1_ref[...] + g2 * f, lng_ref[...], lnb_ref[...])
    is_ctx = pl.program_id(0) < n_ctx_tiles

    @pl.when(is_ctx)
    def _():
        oc_ref[...] = out

    @pl.when(jnp.logical_not(is_ctx))
    def _():
        od_ref[...] = out


def _moe(u2, logits, h1, mods, experts, layer, lw, *, tm, sub, bounds, alpha, n_ctx):
    t = h1.shape[0]
    nct = n_ctx // tm
    rows_c, rows_d = _pair_specs(tm, nct)
    lay = _layer_spec(layer)
    hbm = pl.BlockSpec(memory_space=pl.ANY)
    rows = lambda w: pl.BlockSpec((tm, w), lambda i: (i, 0))
    return pl.pallas_call(
        functools.partial(_moe_kernel, tm=tm, sub=sub, bounds=bounds, alpha=alpha, n_ctx_tiles=nct, layer=layer),
        grid=(t // tm,),
        in_specs=[rows(D_MODEL), rows(ROUTER_LANES), hbm, hbm, hbm,
                  rows(D_MODEL), lay(8, 6 * D_MODEL), lay(1, D_MODEL), lay(1, D_MODEL)],
        out_specs=[rows_c(D_MODEL), rows_d(D_MODEL)],
        out_shape=[jax.ShapeDtypeStruct((n_ctx, D_MODEL), F32), jax.ShapeDtypeStruct((t - n_ctx, D_MODEL), F32)],
        scratch_shapes=[pltpu.VMEM((tm, D_MODEL), BF16), pltpu.VMEM((tm, ROUTER_LANES), F32),
                        pltpu.VMEM((tm, D_MODEL), F32),
                        pltpu.VMEM((N_EXPERTS, D_MODEL, D_EXPERT), BF16),
                        pltpu.VMEM((N_EXPERTS, D_MODEL, D_EXPERT), BF16),
                        pltpu.VMEM((N_EXPERTS, D_EXPERT, D_MODEL), BF16),
                        pltpu.VMEM((WEIGHT_STAGE_SLOTS, D_MODEL, D_EXPERT), F32),
                        pltpu.VMEM((WEIGHT_STAGE_SLOTS, D_EXPERT, D_MODEL), F32),
                        pltpu.SemaphoreType.DMA((WEIGHT_STAGE_SLOTS,)),
                        pltpu.SemaphoreType.DMA((WEIGHT_STAGE_SLOTS,))],
        compiler_params=_cparams(("arbitrary",)),
        name="moe",
    )(u2, logits, experts[0], experts[1], experts[2], h1, mods, lw['ln2_g'], lw['ln2_b'])


def _rope_tables(dec_seq):
    n = HEAD_DIM // 4
    pos = jnp.arange(dec_seq)
    row = (pos // GRID_W).astype(F32)
    col = (pos % GRID_W).astype(F32)
    inv = 1.0 / (ROPE_THETA ** (jnp.arange(n, dtype=F32) / n))
    ra, ca = row[:, None] * inv, col[:, None] * inv
    cos64 = jnp.concatenate([jnp.cos(ra), jnp.cos(ra), jnp.cos(ca), jnp.cos(ca)], axis=-1)
    sin64 = jnp.concatenate([-jnp.sin(ra), jnp.sin(ra), -jnp.sin(ca), jnp.sin(ca)], axis=-1)
    return jnp.concatenate([cos64, cos64], axis=-1), jnp.concatenate([sin64, sin64], axis=-1)


def _block_diag2(w):
    z = jnp.zeros_like(w[:, 0])
    return jnp.concatenate([jnp.concatenate([w[:, 0], z], axis=2), jnp.concatenate([z, w[:, 1]], axis=2)], axis=1)


def _derived_weights(p):
    d = D_MODEL
    depth = p['w_in'].shape[0]
    row = lambda a, w: a.reshape(depth, 1, w)
    ng = len(POOL_WINDOWS)
    pool_blk = (p['pool_w'][:, :, :, None, :] * jnp.eye(ng, dtype=F32)[None, :, None, :, None]).reshape(
        depth, C_WIDTH, C_WIDTH)
    lane_pad = ROUTER_LANES - N_GROUPS - N_EXPERTS
    w_router = jnp.pad(jnp.concatenate([p['rt_grp_w'], p['rt_exp_w']], axis=-1), ((0, 0), (0, 0), (0, lane_pad)))
    b_router = jnp.pad(jnp.concatenate([p['rt_grp_b'], p['rt_exp_b']], axis=-1), ((0, 0), (0, lane_pad)))
    b_router = b_router.reshape(depth, 1, ROUTER_LANES)
    return {
        'w2blk': _block_diag2(p['rw_w2']).astype(BF16),
        'w0': row(p['rw_w0'], 2 * A_WIDTH),
        'a2blk': _block_diag2(p['rw_a2']).astype(BF16),
        'a0': row(p['rw_a0'], 2 * A_WIDTH),
        'g2': p['rw_g2'].astype(BF16),
        'kkw': row(p['rw_kk'], A_WIDTH),
        'ka': jnp.tile(row(p['rw_ka'], A_WIDTH), (1, 1, 2)),
        'rk': row(p['rw_rk'], A_WIDTH),
        'gn_g': row(p['rw_gn_g'], A_WIDTH),
        'gn_b': row(p['rw_gn_b'], A_WIDTH),
        'qn': jnp.tile(row(p['q_norm'], HEAD_DIM), (1, 1, B_Q_HEADS)),
        'knw': jnp.tile(row(p['k_norm'], HEAD_DIM), (1, 1, B_KV_HEADS)),
        'pool_blk': pool_blk.astype(BF16),
        'pool_scale': row(p['pool_scale'], C_WIDTH),
        'ln1_g': row(p['ln1_g'], d), 'ln1_b': row(p['ln1_b'], d),
        'ln2_g': row(p['ln2_g'], d), 'ln2_b': row(p['ln2_b'], d),
        'w_router_hi': w_router.astype(BF16),
        'w_router_lo': (w_router - w_router.astype(BF16).astype(F32)).astype(BF16),
        'b_router': b_router,
    }


def _forward(x_prompt, x_sample, state_rwkv, cache_k, cache_v, c, c_ctx, p):
    batch, seq, d = x_prompt.shape
    dec_batch, dec_seq, _ = x_sample.shape
    depth = p['w_in'].shape[0]
    past = cache_k.shape[2]
    n_ctx = batch * seq
    n_tok = n_ctx + dec_batch * dec_seq
    alpha = (2 * depth) ** 0.25
    bounds = tuple(n_ctx + j * dec_seq for j in range(dec_batch))
    tm = math.gcd(n_ctx, dec_seq, 512)
    tm_moe = math.gcd(n_ctx, dec_seq, 512)
    tq_ctx = min(256, seq)
    tq_dec = min(512, dec_seq)
    sb_ctx = 2

    cond8 = jnp.zeros((8, d), F32).at[0].set(c_ctx).at[1:1 + dec_batch].set(c)
    mods = _modulation(cond8, p['w_mod'], p['b_mod'])
    cos, sin = _rope_tables(dec_seq)
    w_in16, w_out16 = p['w_in'].astype(BF16), p['w_out'].astype(BF16)
    experts = (p['moe_w1'], p['moe_w3'], p['moe_w2'])
    lw = _derived_weights(p)
    head_id = jnp.arange(2 * HEAD_DIM) // HEAD_DIM
    ones = (head_id[:, None] == head_id[None, :]).astype(BF16)

    h = (x_prompt.reshape(n_ctx, d), x_sample.reshape(dec_batch * dec_seq, d))
    ck4 = cache_k.reshape(dec_batch, depth, past, KV_WIDTH)
    cv4 = cache_v.reshape(dec_batch, depth, past, KV_WIDTH)
    new_s, new_k, new_v = [], [], []
    for l in range(depth):
        (r, v, kk, logw, kd, bvec, g, bonus, q, k_att, v_att, pp) = _in_projection(
            h, mods, w_in16, l, lw, cos, sin, ones, tm=tm, bounds=bounds)
        scan_ops = (r, v, kk, logw, kd, bvec)
        yf_c, yb_c, s_fin = _rwkv_scan(scan_ops, None, l, row0=0, nseq=batch, seqlen=seq, sb=sb_ctx,
                                       nb=seq // CHUNK, want_state=True)
        yf_d, yb_d = _rwkv_scan(scan_ops, state_rwkv, l, row0=n_ctx, nseq=dec_batch, seqlen=dec_seq,
                                sb=dec_batch, nb=4, want_state=False)
        yf = (yf_c.reshape(n_ctx, A_WIDTH), yf_d.reshape(n_tok - n_ctx, A_WIDTH))
        yb = (yb_c.reshape(n_ctx, A_WIDTH), yb_d.reshape(n_tok - n_ctx, A_WIDTH))
        att_c, pool_c = _attention(q, k_att, v_att, None, pp, lw['pool_blk'], lw['pool_scale'], l,
                                   row0=0, nseq=batch, seqlen=seq, tq=tq_ctx)
        att_d, pool_d = _attention(q, k_att, v_att, (ck4, cv4), pp, lw['pool_blk'], lw['pool_scale'], l,
                                   row0=n_ctx, nseq=dec_batch, seqlen=dec_seq, tq=tq_dec)
        att, pool = (att_c, att_d), (pool_c, pool_d)
        h1, u2, logits = _out_projection(yf, yb, g, bonus, att, pool, h, mods, w_out16, l, lw, ones,
                                         tm=tm, bounds=bounds, alpha=alpha)
        h = _moe(u2, logits, h1, mods, experts, l, lw, tm=tm_moe, sub=128, bounds=bounds, alpha=alpha,
                 n_ctx=n_ctx)
        new_s.append(s_fin)
        new_k.append(k_att[:n_ctx].reshape(batch, seq, B_KV_HEADS, HEAD_DIM))
        new_v.append(v_att[:n_ctx].reshape(batch, seq, B_KV_HEADS, HEAD_DIM))
    return (h[0].reshape(batch, seq, d), h[1].reshape(dec_batch, dec_seq, d),
            jnp.stack(new_s, axis=1), jnp.stack(new_k, axis=1), jnp.stack(new_v, axis=1))


def kernel(x_prompt, x_sample, state_rwkv, cache_k, cache_v, c, c_ctx, w_in, w_out, w_mod, b_mod, ln1_g, ln1_b, ln2_g, ln2_b, rw_w0, rw_w2, rw_a0, rw_a2, rw_g2, rw_kk, rw_ka, rw_rk, rw_gn_g, rw_gn_b, q_norm, k_norm, pool_w, pool_scale, rt_grp_w, rt_grp_b, rt_exp_w, rt_exp_b, moe_w1, moe_w3, moe_w2):
    p = dict(w_in=w_in, w_out=w_out, w_mod=w_mod, b_mod=b_mod, ln1_g=ln1_g, ln1_b=ln1_b, ln2_g=ln2_g,
             ln2_b=ln2_b, rw_w0=rw_w0, rw_w2=rw_w2, rw_a0=rw_a0, rw_a2=rw_a2, rw_g2=rw_g2, rw_kk=rw_kk,
             rw_ka=rw_ka, rw_rk=rw_rk, rw_gn_g=rw_gn_g, rw_gn_b=rw_gn_b, q_norm=q_norm, k_norm=k_norm,
             pool_w=pool_w, pool_scale=pool_scale, rt_grp_w=rt_grp_w, rt_grp_b=rt_grp_b, rt_exp_w=rt_exp_w,
             rt_exp_b=rt_exp_b, moe_w1=moe_w1, moe_w3=moe_w3, moe_w2=moe_w2)
    return _forward(x_prompt, x_sample, state_rwkv, cache_k, cache_v, c, c_ctx, p)
```

```python
import functools
import math

import jax
import jax.numpy as jnp
from jax import lax
from jax.experimental import pallas as pl
from jax.experimental.pallas import tpu as pltpu

F32 = jnp.float32
BF16 = jnp.bfloat16
I32 = jnp.int32

D_MODEL = 1024
GRID_W = 64
HEAD_DIM = 64
A_HEADS = 6
A_WIDTH = A_HEADS * HEAD_DIM
LORA_W = 64
LORA_A = 64
LORA_G = 128
N_DIR = 2
DECAY_SCALE = 0.606531
B_Q_HEADS = 6
B_KV_HEADS = 2
B_WIDTH = B_Q_HEADS * HEAD_DIM
KV_WIDTH = B_KV_HEADS * HEAD_DIM
ROPE_THETA = 10000.0
POOL_WINDOWS = (2, 4, 8, 16)
C_WIDTH = 256
IN_COLS = 2432
N_GROUPS = 4
EXP_PER_GROUP = 4
N_EXPERTS = 16
D_EXPERT = 256
LN_EPS = 1e-5
GN_EPS = 64e-5
QK_EPS = 1e-6

_O_R, _O_K, _O_V = 0, 384, 768
_O_LW, _O_LA, _O_LG = 1152, 1280, 1408
_O_Q, _O_KB, _O_VB, _O_P = 1536, 1920, 2048, 2176

CHUNK = 64
ROUTER_LANES = 128
WEIGHT_STAGE_SLOTS = 4
ATTN_UNIT_ROWS = 256
VMEM_LIMIT = 56 * 1024 * 1024


def _cparams(sem):
    return pltpu.CompilerParams(dimension_semantics=sem, vmem_limit_bytes=VMEM_LIMIT)


def _sigmoid(x):
    return 1.0 / (1.0 + jnp.exp(-x))


def _dot(a, b):
    return jnp.dot(a, b, preferred_element_type=F32)


def _dot_nt(a, b):
    return lax.dot_general(a, b, (((1,), (1,)), ((), ())), preferred_element_type=F32)


def _segsum(x, ones2):
    x16 = x.astype(BF16)
    slabs = [_dot(x16[:, j:j + 128], ones2) for j in range(0, x.shape[1], 128)]
    return slabs[0] if len(slabs) == 1 else jnp.concatenate(slabs, axis=1)


def _rope128(x, cos, sin):
    lane = lax.broadcasted_iota(I32, x.shape, 1)
    first = (lane % 32) < 16
    up = pltpu.roll(x, 128 - 16, 1)
    dn = pltpu.roll(x, 16, 1)
    return x * cos + jnp.where(first, up, dn) * sin


def _layer_norm(z, g, b):
    mu = jnp.mean(z, axis=-1, keepdims=True)
    zc = z - mu
    var = jnp.mean(zc * zc, axis=-1, keepdims=True)
    return zc * lax.rsqrt(var + LN_EPS) * g + b


def _mod_kernel(c_ref, w_ref, b_ref, o_ref):
    x = c_ref[...]
    x = x * _sigmoid(x)
    w = w_ref[...]
    x_hi = x.astype(BF16)
    x_lo = (x - x_hi.astype(F32)).astype(BF16)
    w_hi = w.astype(BF16)
    w_lo = (w - w_hi.astype(F32)).astype(BF16)
    o_ref[...] = (_dot(x_hi, w_hi) + _dot(x_lo, w_hi) + _dot(x_hi, w_lo)) + b_ref[...]


def _modulation(cond8, w_mod, b_mod):
    depth = w_mod.shape[0]
    tn = 1536
    return pl.pallas_call(
        _mod_kernel,
        grid=(depth, 6 * D_MODEL // tn),
        in_specs=[
            pl.BlockSpec((8, D_MODEL), lambda l, j: (0, 0)),
            pl.BlockSpec((None, D_MODEL, tn), lambda l, j: (l, 0, j)),
            pl.BlockSpec((None, 1, tn), lambda l, j: (l, 0, j)),
        ],
        out_specs=pl.BlockSpec((None, 8, tn), lambda l, j: (l, 0, j)),
        out_shape=jax.ShapeDtypeStruct((depth, 8, 6 * D_MODEL), F32),
        compiler_params=_cparams(("arbitrary", "arbitrary")),
        name="modulation",
    )(cond8, w_mod, b_mod.reshape(depth, 1, 6 * D_MODEL))


def _group_of_row(row0, bounds):
    g = jnp.zeros((), I32)
    for b in bounds:
        g = g + jnp.where(row0 >= b, 1, 0).astype(I32)
    return g


def _in_kernel(hc_ref, hd_ref, mod_ref, w_ref, w2_ref, w0_ref, a2_ref, a0_ref, g2_ref, kkw_ref, ka_ref,
               rk_ref, qn_ref, knw_ref, cos_ref, sin_ref, ones_ref,
               r_o, v_o, kk_o, lw_o, kd_o, b_o, g_o, bon_o, q_o, k_o, vb_o, p_o, *, tm, bounds, n_ctx_tiles):
    grp = _group_of_row(pl.program_id(0) * tm, bounds)
    mod = mod_ref[pl.ds(grp, 1), :]
    sh1 = mod[:, 0:D_MODEL]
    sc1 = mod[:, D_MODEL:2 * D_MODEL]
    is_ctx = pl.program_id(0) < n_ctx_tiles
    h = jnp.where(is_ctx, hc_ref[...], hd_ref[...])
    u = h * (1.0 + sc1) + sh1
    ones = ones_ref[...]
    scale = 1.0 / math.sqrt(HEAD_DIM)
    proj = _dot(u.astype(BF16), w_ref[...])

    r = proj[:, _O_R:_O_R + A_WIDTH]
    k = proj[:, _O_K:_O_K + A_WIDTH]
    v = proj[:, _O_V:_O_V + A_WIDTH]
    lo_w = proj[:, _O_LW:_O_LW + 128]
    lo_a = proj[:, _O_LA:_O_LA + 128]
    lo_g = proj[:, _O_LG:_O_LG + 128]

    w_pre = w0_ref[...] + _dot(jnp.tanh(lo_w).astype(BF16), w2_ref[...])
    logw = -DECAY_SCALE * _sigmoid(w_pre)
    a = _sigmoid(a0_ref[...] + _dot(lo_a.astype(BF16), a2_ref[...]))
    g = _dot(_sigmoid(lo_g).astype(BF16), g2_ref[...])

    kk = k * kkw_ref[...]
    kk = kk * lax.rsqrt(jnp.maximum(_segsum(kk * kk, ones), 1e-24))
    k2 = jnp.concatenate([k, k], axis=1)
    kd = k2 * (1.0 + (a - 1.0) * ka_ref[...])
    bvec = jnp.concatenate([kk, kk], axis=1) * a
    kd_sum = kd[:, 0:A_WIDTH] + kd[:, A_WIDTH:2 * A_WIDTH]
    bonus = _segsum(r * kd_sum * rk_ref[...], ones) * v

    r_o[...] = r.astype(BF16)
    v_o[...] = v.astype(BF16)
    kk_o[...] = kk.astype(BF16)
    lw_o[...] = logw
    kd_o[...] = kd.astype(BF16)
    b_o[...] = bvec.astype(BF16)
    g_o[...] = g.astype(BF16)
    bon_o[...] = bonus.astype(BF16)

    cos = cos_ref[...]
    sin = sin_ref[...]
    rope = lambda x: jnp.where(is_ctx, x, _rope128(x, cos, sin))
    q = proj[:, _O_Q:_O_Q + B_WIDTH]
    qn = q * lax.rsqrt(_segsum(q * q, ones) * (1.0 / HEAD_DIM) + QK_EPS) * qn_ref[...]
    q_o[...] = (jnp.concatenate([rope(qn[:, j:j + 128]) for j in range(0, B_WIDTH, 128)], axis=1)
                * scale).astype(BF16)
    kb = proj[:, _O_KB:_O_KB + KV_WIDTH]
    kn = kb * lax.rsqrt(_segsum(kb * kb, ones) * (1.0 / HEAD_DIM) + QK_EPS) * knw_ref[...]
    k_o[...] = rope(kn)
    vb_o[...] = proj[:, _O_VB:_O_VB + KV_WIDTH]
    p_o[...] = proj[:, _O_P:_O_P + C_WIDTH]


def _pair_specs(tm, nct):
    rows_c = lambda w: pl.BlockSpec((tm, w), lambda i: (jnp.minimum(i, nct - 1), 0))
    rows_d = lambda w: pl.BlockSpec((tm, w), lambda i: (jnp.maximum(i - nct, 0), 0))
    return rows_c, rows_d


def _layer_spec(layer):
    return lambda *shape: pl.BlockSpec((None,) + shape, lambda *_: (layer,) + (0,) * len(shape))


def _in_projection(h, mods, w_in, layer, lw, cos, sin, ones, *, tm, bounds):
    nct = h[0].shape[0] // tm
    t = h[0].shape[0] + h[1].shape[0]
    full = lambda shape: pl.BlockSpec(shape, lambda i: (0,) * len(shape))
    rows = lambda w: pl.BlockSpec((tm, w), lambda i: (i, 0))
    rows_c, rows_d = _pair_specs(tm, nct)
    lay = _layer_spec(layer)
    tiles_per_req = cos.shape[0] // tm
    table = pl.BlockSpec((tm, 128), lambda i: (jnp.maximum(i - nct, 0) % tiles_per_req, 0))
    widths = (A_WIDTH, A_WIDTH, A_WIDTH, 2 * A_WIDTH, 2 * A_WIDTH, 2 * A_WIDTH, A_WIDTH, A_WIDTH,
              B_WIDTH, KV_WIDTH, KV_WIDTH, C_WIDTH)
    dtypes = (BF16, BF16, BF16, F32, BF16, BF16, BF16, BF16, BF16, F32, F32, F32)
    return pl.pallas_call(
        functools.partial(_in_kernel, tm=tm, bounds=bounds, n_ctx_tiles=nct),
        grid=(t // tm,),
        in_specs=[
            rows_c(D_MODEL), rows_d(D_MODEL), lay(8, 6 * D_MODEL), lay(D_MODEL, IN_COLS),
            lay(128, 2 * A_WIDTH), lay(1, 2 * A_WIDTH), lay(128, 2 * A_WIDTH), lay(1, 2 * A_WIDTH),
            lay(LORA_G, A_WIDTH), lay(1, A_WIDTH), lay(1, 2 * A_WIDTH), lay(1, A_WIDTH),
            lay(1, B_WIDTH), lay(1, KV_WIDTH), table, table, full((2 * HEAD_DIM, 2 * HEAD_DIM)),
        ],
        out_specs=[rows(w) for w in widths],
        out_shape=[jax.ShapeDtypeStruct((t, w), dt) for w, dt in zip(widths, dtypes)],
        compiler_params=_cparams(("arbitrary",)),
        name="in_projection",
    )(h[0], h[1], mods, w_in, lw['w2blk'], lw['w0'], lw['a2blk'], lw['a0'], lw['g2'], lw['kkw'], lw['ka'],
      lw['rk'], lw['qn'], lw['knw'], cos, sin, ones)


def _split2(x):
    hi = x.astype(BF16)
    lo = (x - hi.astype(F32)).astype(BF16)
    return hi, lo


def _scan_kernel(*refs, sb, nb, zero_init, want_state):
    fwd = refs[0:6]
    bwd = refs[6:12]
    pos = 12
    if not zero_init:
        s0_ref = refs[pos]
        pos += 1
    y_outs = (refs[pos], refs[pos + 1])
    pos += 2
    if want_state:
        sfin_o = refs[pos]
        pos += 1
    h_ref = refs[pos]

    step = pl.program_id(1)
    nstep = pl.num_programs(1)
    npair = A_HEADS // 2
    pw = 2 * HEAD_DIM
    pairs = [(q, d, hp) for q in range(sb) for d in range(N_DIR) for hp in range(npair)]

    @pl.when(step == 0)
    def _():
        h_ref[...] = jnp.zeros_like(h_ref)
        if not zero_init:
            for (q, d, hp) in pairs:
                for e in range(2):
                    lo = e * HEAD_DIM
                    h_ref[q, d, hp, lo:lo + HEAD_DIM, lo:lo + HEAD_DIM] = s0_ref[q, d, 2 * hp + e].T

    def iota(shape, dim):
        return lax.broadcasted_iota(I32, shape, dim)

    first64 = iota((CHUNK, pw), 1) < HEAD_DIM
    eye_p = ((iota((CHUNK, pw), 1) % HEAD_DIM) == iota((CHUNK, pw), 0)).astype(F32)
    eye2 = (iota((pw, pw), 0) == iota((pw, pw), 1)).astype(F32)
    same_head = (iota((pw, pw), 0) < HEAD_DIM) == (iota((pw, pw), 1) < HEAD_DIM)
    ti = iota((CHUNK, CHUNK), 0)
    si = iota((CHUNK, CHUNK), 1)
    incl16 = ((si <= ti).astype(BF16), (si >= ti).astype(BF16))
    t4 = iota((2 * CHUNK, 2 * pw), 0)
    s4 = iota((2 * CHUNK, 2 * pw), 1) % CHUNK
    incl_off = jnp.where(t4 < CHUNK, 0, 1)
    t4 = t4 % CHUNK
    mask4 = (s4 < t4 + incl_off, s4 > t4 - incl_off)

    def bdiag(x):
        return jnp.concatenate([jnp.where(first64, x, 0.0), jnp.where(first64, 0.0, x)], axis=0)

    lhs, rhs1, bkt, gcol, v_p, amat = {}, {}, {}, {}, {}, {}
    blocks = [(q, d, j) for q in range(sb) for d in range(N_DIR) for j in range(nb)]
    csum = {}
    for (q, d, j) in blocks:
        hi, lo = _split2((fwd, bwd)[d][3][q, pl.ds(j * CHUNK, CHUNK), :])
        csum[(q, d, j)] = _dot(incl16[d], hi) + _dot(incl16[d], lo)
    for (q, d, j) in blocks:
        r_ref, v_ref, kk_ref, lw_ref, kd_ref, b_ref = (fwd, bwd)[d]
        rows = pl.ds(j * CHUNK, CHUNK)
        lw = lw_ref[q, rows, :]
        cs = csum[(q, d, j)]
        gam = jnp.exp(cs)
        gam_prev = jnp.exp(cs - lw)
        gam_inv = jnp.exp(-cs)
        gam_end = jnp.exp(cs[CHUNK - 1:CHUNK, :] if d == 0 else cs[0:1, :])
        a_hat = -kk_ref[q, rows, :] * gam_prev
        r_hat = r_ref[q, rows, :] * gam
        b_hat = b_ref[q, rows, :] * gam_inv
        k_hat = kd_ref[q, rows, :] * gam_inv
        v_all = v_ref[q, rows, :]
        for hp in range(npair):
            key = (q, d, hp, j)
            sl = slice(hp * pw, (hp + 1) * pw)
            lhs[key] = jnp.concatenate([a_hat[:, sl], r_hat[:, sl]], axis=0).astype(BF16)
            bh, kh, ge = b_hat[:, sl], k_hat[:, sl], gam_end[:, sl]
            rhs1[key] = jnp.concatenate([bdiag(bh), bdiag(kh)], axis=0).astype(BF16)
            gcol[key] = jnp.sum(eye2 * ge, axis=1, keepdims=True)
            bkt[key] = (jnp.concatenate([bh, kh], axis=0) * ge).T.astype(BF16)
            v_p[key] = v_all[:, sl]
    keys = list(lhs.keys())
    for key in keys:
        amat[key] = _dot_nt(lhs[key], rhs1[key])
    npow, tcur, a_rb, lhs2, vbd = {}, {}, {}, {}, {}
    for key in keys:
        am = jnp.where(mask4[key[1]], amat[key], 0.0)
        npow[key] = am[0:CHUNK, 0:pw]
        tcur[key] = eye_p + npow[key]
        a_rb[key] = am[CHUNK:2 * CHUNK, 0:pw].astype(BF16)
        vbd[key] = bdiag(v_p[key])
        lhs2[key] = jnp.concatenate([lhs[key], am[:, pw:2 * pw].astype(BF16)], axis=1)
    nsq = int(math.log2(CHUNK)) - 1
    for key in keys:
        npow[key] = _dot(npow[key].astype(BF16), bdiag(npow[key]).astype(BF16))
    for _ in range(nsq - 1):
        for key in keys:
            rhs = jnp.concatenate([bdiag(npow[key]), bdiag(tcur[key])], axis=1).astype(BF16)
            res = _dot(npow[key].astype(BF16), rhs)
            npow[key] = res[:, 0:pw]
            tcur[key] = tcur[key] + res[:, pw:2 * pw]
    t_inv = {}
    for key in keys:
        t_inv[key] = (tcur[key] + _dot(npow[key].astype(BF16), bdiag(tcur[key]).astype(BF16))).astype(BF16)

    hcur = {pr: h_ref[pr[0], pr[1], pr[2]] for pr in pairs}
    for jj in range(nb):
        z, u = {}, {}
        kof = lambda pr: (pr[0], pr[1], pr[2], jj if pr[1] == 0 else nb - 1 - jj)
        for pr in pairs:
            key = kof(pr)
            hv = jnp.concatenate([hcur[pr].astype(BF16), vbd[key]], axis=0)
            z[pr] = _dot(lhs2[key], hv)
        for pr in pairs:
            u[pr] = _dot(t_inv[kof(pr)], bdiag(z[pr][0:CHUNK]).astype(BF16))
        for pr in pairs:
            key = kof(pr)
            q, d, hp, j = key
            y = z[pr][CHUNK:2 * CHUNK] + _dot(a_rb[key], bdiag(u[pr]).astype(BF16))
            y_outs[d][q, pl.ds(j * CHUNK, CHUNK), hp * pw:(hp + 1) * pw] = y.astype(BF16)
            uv = jnp.concatenate([u[pr].astype(BF16), v_p[key]], axis=0)
            hcur[pr] = hcur[pr] * gcol[key] + jnp.where(same_head, _dot(bkt[key], uv), 0.0)
    for pr in pairs:
        h_ref[pr[0], pr[1], pr[2]] = hcur[pr]

    if want_state:
        @pl.when(step == nstep - 1)
        def _():
            for (q, d, hp) in pairs:
                for e in range(2):
                    lo = e * HEAD_DIM
                    sfin_o[q, d, 2 * hp + e] = hcur[(q, d, hp)][lo:lo + HEAD_DIM, lo:lo + HEAD_DIM].T


def _rwkv_scan(ops, s0, layer, *, row0, nseq, seqlen, sb, nb, want_state):
    g = nb * CHUNK
    nstep = seqlen // g
    zero_init = s0 is None
    ops3 = [a.reshape(a.shape[0] // seqlen, seqlen, a.shape[1]) for a in ops]
    base = row0 // (seqlen * sb)
    fidx = lambda col: (lambda s, c: (base + s, c, col))
    bidx = lambda col: (lambda s, c: (base + s, nstep - 1 - c, col))
    blk = lambda im: pl.BlockSpec((sb, g, A_WIDTH), im)
    in_specs = [blk(fidx(0))] * 6 + [blk(bidx(0))] * 3 + [blk(bidx(1))] * 3
    args = ops3 + ops3
    state_spec = pl.BlockSpec((sb, N_DIR, A_HEADS, HEAD_DIM, HEAD_DIM), lambda s, c: (s, 0, 0, 0, 0))
    if not zero_init:
        in_specs.append(pl.BlockSpec((sb, None, N_DIR, A_HEADS, HEAD_DIM, HEAD_DIM),
                                     lambda s, c: (s, layer, 0, 0, 0, 0)))
        args.append(s0)
    out_specs = [blk(lambda s, c: (s, c, 0)), blk(lambda s, c: (s, nstep - 1 - c, 0))]
    out_shape = [jax.ShapeDtypeStruct((nseq, seqlen, A_WIDTH), BF16)] * 2
    if want_state:
        out_specs.append(state_spec)
        out_shape.append(jax.ShapeDtypeStruct((nseq, N_DIR, A_HEADS, HEAD_DIM, HEAD_DIM), F32))
    return pl.pallas_call(
        functools.partial(_scan_kernel, sb=sb, nb=nb, zero_init=zero_init, want_state=want_state),
        grid=(nseq // sb, nstep),
        in_specs=in_specs,
        out_specs=out_specs,
        out_shape=out_shape,
        scratch_shapes=[pltpu.VMEM((sb, N_DIR, A_HEADS // 2, 2 * HEAD_DIM, 2 * HEAD_DIM), F32)],
        compiler_params=_cparams(("arbitrary", "arbitrary")),
        name="rwkv_scan_ctx" if zero_init else "rwkv_scan_dec",
    )(*args)


def _attn_kernel(*refs, with_cache):
    grp = B_Q_HEADS // B_KV_HEADS
    if with_cache:
        q_ref, k_ref, v_ref, ck_ref, cv_ref, p_ref, pw_ref, psc_ref, o_ref, pool_o = refs
        kall = jnp.concatenate([k_ref[...], ck_ref[...]], axis=0)
        vall = jnp.concatenate([v_ref[...], cv_ref[...]], axis=0)
    else:
        q_ref, k_ref, v_ref, p_ref, pw_ref, psc_ref, o_ref, pool_o = refs
        kall = k_ref[...]
        vall = v_ref[...]

    @pl.when(pl.program_id(1) == 0)
    def _():
        pool_o[...] = _pool_mix(p_ref[...], pw_ref[...], psc_ref[...]).astype(BF16)

    q = q_ref[...]
    k16 = [kall[:, hk * HEAD_DIM:(hk + 1) * HEAD_DIM].astype(BF16) for hk in range(B_KV_HEADS)]
    v16 = [vall[:, hk * HEAD_DIM:(hk + 1) * HEAD_DIM].astype(BF16) for hk in range(B_KV_HEADS)]
    rows = ATTN_UNIT_ROWS if q.shape[0] % ATTN_UNIT_ROWS == 0 else q.shape[0]
    units = [(r0, j) for r0 in range(0, q.shape[0], rows) for j in range(B_Q_HEADS)]

    def scores(unit):
        r0, j = unit
        return _dot_nt(q[r0:r0 + rows, j * HEAD_DIM:(j + 1) * HEAD_DIM], k16[j // grp])

    outs = {}
    s_next = scores(units[0])
    for i, unit in enumerate(units):
        s = s_next
        if i + 1 < len(units):
            s_next = scores(units[i + 1])
        m = jnp.max(s, axis=-1, keepdims=True)
        p = jnp.exp(s - m)
        den = jnp.sum(p, axis=-1, keepdims=True)
        outs[unit] = _dot(p.astype(BF16), v16[unit[1] // grp]) / den
    for r0 in range(0, q.shape[0], rows):
        o_ref[r0:r0 + rows, :] = jnp.concatenate([outs[(r0, j)] for j in range(B_Q_HEADS)], axis=1).astype(BF16)


def _attention(q, k_tok, v_tok, cache, p_tok, pool_w, pool_scale, layer, *, row0, nseq, seqlen, tq):
    nq = seqlen // tq
    base = row0 // tq
    kbase = row0 // seqlen
    kspec = pl.BlockSpec((seqlen, KV_WIDTH), lambda s, i: (kbase + s, 0))
    in_specs = [pl.BlockSpec((tq, B_WIDTH), lambda s, i: (base + s * nq + i, 0)), kspec, kspec]
    args = [q, k_tok, v_tok]
    if cache is not None:
        ck, cv = cache
        cspec = pl.BlockSpec((None, None, ck.shape[2], KV_WIDTH), lambda s, i: (s, layer, 0, 0))
        in_specs += [cspec, cspec]
        args += [ck, cv]
    lay = _layer_spec(layer)
    in_specs += [pl.BlockSpec((seqlen, C_WIDTH), lambda s, i: (kbase + s, 0)), lay(C_WIDTH, C_WIDTH), lay(1, C_WIDTH)]
    args += [p_tok, pool_w, pool_scale]
    return pl.pallas_call(
        functools.partial(_attn_kernel, with_cache=cache is not None),
        grid=(nseq, nq),
        in_specs=in_specs,
        out_specs=[pl.BlockSpec((tq, B_WIDTH), lambda s, i: (s * nq + i, 0)),
                   pl.BlockSpec((seqlen, C_WIDTH), lambda s, i: (s, 0))],
        out_shape=[jax.ShapeDtypeStruct((nseq * seqlen, B_WIDTH), BF16),
                   jax.ShapeDtypeStruct((nseq * seqlen, C_WIDTH), BF16)],
        compiler_params=_cparams(("arbitrary", "arbitrary")),
        name="attention_ctx" if cache is None else "attention_dec",
    )(*args)


def _pool_mix(x, w_blk, scale):
    seqlen = x.shape[0]
    t = lax.broadcasted_iota(I32, x.shape, 0)
    lane = lax.broadcasted_iota(I32, x.shape, 1)

    def down(a, k):
        return jnp.where(t >= k, pltpu.roll(a, k, 0), 0.0)

    def up(a, k):
        return jnp.where(t < seqlen - k, pltpu.roll(a, seqlen - k, 0), 0.0)

    past = [x]
    futr = [x]
    for j in range(len(POOL_WINDOWS) - 1):
        past.append(past[j] + down(past[j], 2 ** j))
        futr.append(futr[j] + up(futr[j], 2 ** j))
    win_sum = jnp.zeros_like(x)
    cnt = jnp.zeros_like(x)
    for gi, win in enumerate(POOL_WINDOWS):
        half = win // 2
        in_group = (lane // HEAD_DIM) == gi
        win_sum = jnp.where(in_group, down(past[gi], 1) + futr[gi], win_sum)
        n = (jnp.minimum(t + half, seqlen) - jnp.maximum(t - half, 0)).astype(F32)
        cnt = jnp.where(in_group, n, cnt)
    d = win_sum / cnt - x
    return _dot(d.astype(BF16), w_blk) * scale


def _out_kernel(yfc_ref, yfd_ref, ybc_ref, ybd_ref, g_ref, bon_ref, attc_ref, attd_ref, poolc_ref, poold_ref,
                hc_ref, hd_ref, mod_ref, w_ref, lng_ref, lnb_ref, gng_ref, gnb_ref, ones_ref, wrh_ref, wrl_ref,
                br_ref, h1_o, u2_o, lg_o, *, tm, bounds, alpha, n_ctx_tiles):
    grp = _group_of_row(pl.program_id(0) * tm, bounds)
    mod = mod_ref[pl.ds(grp, 1), :]
    g1 = mod[:, 2 * D_MODEL:3 * D_MODEL]
    sh2 = mod[:, 3 * D_MODEL:4 * D_MODEL]
    sc2 = mod[:, 4 * D_MODEL:5 * D_MODEL]
    ones = ones_ref[...]
    is_ctx = pl.program_id(0) < n_ctx_tiles
    half = tm // 2
    halves = [pl.ds(s * half, half) for s in range(2)]
    pick = lambda c_ref, d_ref, rows: jnp.where(is_ctx, c_ref[rows, :], d_ref[rows, :])
    y = [pick(yfc_ref, yfd_ref, rows).astype(F32) + pick(ybc_ref, ybd_ref, rows).astype(F32) for rows in halves]
    mu = [_segsum(v, ones) * (1.0 / HEAD_DIM) for v in y]
    yc = [v - m_ for v, m_ in zip(y, mu)]
    var = [_segsum(v * v, ones) * (1.0 / HEAD_DIM) for v in yc]
    m = []
    for rows, v, vr in zip(halves, yc, var):
        yn = v * lax.rsqrt(vr + GN_EPS) * gng_ref[...] + gnb_ref[...]
        a_out = (yn + bon_ref[rows, :]) * g_ref[rows, :]
        m.append(_dot(a_out.astype(BF16), w_ref[0:A_WIDTH, :])
                 + _dot(pick(attc_ref, attd_ref, rows), w_ref[A_WIDTH:A_WIDTH + B_WIDTH, :])
                 + _dot(pick(poolc_ref, poold_ref, rows), w_ref[A_WIDTH + B_WIDTH:D_MODEL, :]))
    for rows, m_ in zip(halves, m):
        h1 = _layer_norm(alpha * pick(hc_ref, hd_ref, rows) + g1 * m_, lng_ref[...], lnb_ref[...])
        h1_o[rows, :] = h1
        u2 = h1 * (1.0 + sc2) + sh2
        u_hi = u2.astype(BF16)
        u_lo = (u2 - u_hi.astype(F32)).astype(BF16)
        u2_o[rows, :] = u_hi
        lg_o[rows, :] = (_dot(u_hi, wrh_ref[...]) + _dot(u_lo, wrh_ref[...]) + _dot(u_hi, wrl_ref[...])) + br_ref[...]


def _out_projection(yf, yb, g, bonus, att, pool, h, mods, w_out, layer, lw, ones, *, tm, bounds, alpha):
    t = h[0].shape[0] + h[1].shape[0]
    nct = h[0].shape[0] // tm
    full = lambda shape: pl.BlockSpec(shape, lambda i: (0,) * len(shape))
    rows = lambda w: pl.BlockSpec((tm, w), lambda i: (i, 0))
    rows_c, rows_d = _pair_specs(tm, nct)
    lay = _layer_spec(layer)
    return pl.pallas_call(
        functools.partial(_out_kernel, tm=tm, bounds=bounds, alpha=alpha, n_ctx_tiles=nct),
        grid=(t // tm,),
        in_specs=[rows_c(A_WIDTH), rows_d(A_WIDTH), rows_c(A_WIDTH), rows_d(A_WIDTH), rows(A_WIDTH), rows(A_WIDTH),
                  rows_c(B_WIDTH), rows_d(B_WIDTH), rows_c(C_WIDTH), rows_d(C_WIDTH), rows_c(D_MODEL),
                  rows_d(D_MODEL), lay(8, 6 * D_MODEL), lay(D_MODEL, D_MODEL), lay(1, D_MODEL),
                  lay(1, D_MODEL), lay(1, A_WIDTH), lay(1, A_WIDTH), full((2 * HEAD_DIM, 2 * HEAD_DIM)),
                  lay(D_MODEL, ROUTER_LANES), lay(D_MODEL, ROUTER_LANES), lay(1, ROUTER_LANES)],
        out_specs=[rows(D_MODEL), rows(D_MODEL), rows(ROUTER_LANES)],
        out_shape=[jax.ShapeDtypeStruct((t, D_MODEL), F32), jax.ShapeDtypeStruct((t, D_MODEL), BF16),
                   jax.ShapeDtypeStruct((t, ROUTER_LANES), F32)],
        compiler_params=_cparams(("arbitrary",)),
        name="out_projection",
    )(yf[0], yf[1], yb[0], yb[1], g, bonus, att[0], att[1], pool[0], pool[1], h[0], h[1], mods, w_out,
      lw['ln1_g'], lw['ln1_b'], lw['gn_g'], lw['gn_b'], ones, lw['w_router_hi'], lw['w_router_lo'], lw['b_router'])


def _router_gates(logits):
    lane = lax.broadcasted_iota(I32, logits.shape, 1)
    lane_f = lane.astype(F32)
    neg = -jnp.inf
    big = 1e9

    def first_lane(mask):
        return jnp.min(jnp.where(mask, lane_f, big), axis=-1, keepdims=True).astype(I32)

    is_grp = lane < N_GROUPS
    gmax = jnp.max(jnp.where(is_grp, logits, neg), axis=-1, keepdims=True)
    sel = first_lane(is_grp & (logits == gmax))
    gsum = jnp.sum(jnp.where(is_grp, jnp.exp(logits - gmax), 0.0), axis=-1, keepdims=True)
    gsel = 1.0 / gsum
    first_exp = N_GROUPS + sel * EXP_PER_GROUP
    in_sel = (lane >= first_exp) & (lane < first_exp + EXP_PER_GROUP)
    v1 = jnp.max(jnp.where(in_sel, logits, neg), axis=-1, keepdims=True)
    i1 = first_lane(in_sel & (logits == v1))
    rest = in_sel & (lane != i1)
    v2 = jnp.max(jnp.where(rest, logits, neg), axis=-1, keepdims=True)
    i2 = first_lane(rest & (logits == v2))
    e2 = jnp.exp(v2 - v1)
    t1 = (1.0 / (1.0 + e2)) * gsel
    t2 = (e2 / (1.0 + e2)) * gsel
    return jnp.where(lane == i1, t1, 0.0) + jnp.where(lane == i2, t2, 0.0), sel


def _weight_stream(jobs, stage_ref, sem):
    nslot = stage_ref.shape[0]
    copy = lambda i: pltpu.make_async_copy(jobs[i][0], stage_ref.at[i % nslot], sem.at[i % nslot])

    def prime():
        for i in range(min(nslot, len(jobs))):
            copy(i).start()

    def drain():
        for i, (_, dst_ref, e) in enumerate(jobs):
            copy(i).wait()
            dst_ref[e] = stage_ref[i % nslot].astype(BF16)
            if i + nslot < len(jobs):
                copy(i + nslot).start()

    return prime, drain


def _moe_kernel(x_ref, lg_ref, w1_hbm, w3_hbm, w2_hbm, h1_ref, mod_ref, lng_ref, lnb_ref, oc_ref, od_ref,
                xs_ref, gs_ref, acc_ref, w1_ref, w3_ref, w2_ref, stage_in_ref, stage_out_ref, sem_in, sem_out,
                *, tm, sub, bounds, alpha, n_ctx_tiles, layer):
    jobs_in = [(w.at[layer, e], dst, e) for w, dst in ((w1_hbm, w1_ref), (w3_hbm, w3_ref))
               for e in range(N_EXPERTS)]
    jobs_out = [(w2_hbm.at[layer, e], w2_ref, e) for e in range(N_EXPERTS)]
    prime_in, drain_in = _weight_stream(jobs_in, stage_in_ref, sem_in)
    prime_out, drain_out = _weight_stream(jobs_out, stage_out_ref, sem_out)

    @pl.when(pl.program_id(0) == 0)
    def _():
        prime_in()
        prime_out()

    gates, sel = _router_gates(lg_ref[...])
    lane = lax.broadcasted_iota(I32, gates.shape, 1)
    onehot = jnp.where(lane == sel, 1.0, 0.0)
    row_i = lax.broadcasted_iota(I32, (tm, tm), 0)
    col_i = lax.broadcasted_iota(I32, (tm, tm), 1)
    earlier = jnp.where(col_i < row_i, 1.0, 0.0).astype(BF16)
    rank = _dot(earlier, onehot.astype(BF16))
    lane1 = lax.broadcasted_iota(I32, (1, ROUTER_LANES), 1)
    counts = jnp.sum(onehot, axis=0, keepdims=True)
    cnt = [jnp.sum(jnp.where(lane1 == g, counts, 0.0)).astype(I32) for g in range(N_GROUPS)]
    off = [jnp.zeros((), I32)]
    for g in range(N_GROUPS - 1):
        off.append(off[g] + cnt[g])
    offs = jnp.zeros((1, ROUTER_LANES), F32)
    for g in range(N_GROUPS):
        offs = jnp.where(lane1 == g, off[g].astype(F32), offs)
    pos = jnp.sum(onehot * (rank + offs), axis=1, keepdims=True).astype(I32)
    pos_row = jnp.broadcast_to(pos, (tm, ROUTER_LANES)).T[0:1, :]
    perm = jnp.where(row_i == pos_row, 1.0, 0.0).astype(BF16)
    perm_t = jnp.where(col_i == pos, 1.0, 0.0).astype(BF16)
    xs_ref[...] = _dot(perm, x_ref[...]).astype(BF16)
    gs_ref[...] = _dot(perm, gates.astype(BF16))
    acc_ref[...] = jnp.zeros_like(acc_ref)

    @pl.when(pl.program_id(0) == 0)
    def _():
        drain_in()
        drain_out()

    lane_s = lax.broadcasted_iota(I32, (sub, ROUTER_LANES), 1)
    for s in range(tm // sub):
        rows = pl.ds(s * sub, sub)
        for g in range(N_GROUPS):
            @pl.when((off[g] < (s + 1) * sub) & (off[g] + cnt[g] > s * sub))
            def _():
                xb = xs_ref[rows, :]
                gsb = gs_ref[rows, :]
                hids = []
                for j in range(EXP_PER_GROUP):
                    e = g * EXP_PER_GROUP + j
                    ge = jnp.sum(jnp.where(lane_s == N_GROUPS + e, gsb, 0.0), axis=-1, keepdims=True)
                    h1 = _dot(xb, w1_ref[e])
                    h3 = _dot(xb, w3_ref[e])
                    hids.append(((h1 * _sigmoid(h1)) * h3 * ge).astype(BF16))
                hid = jnp.concatenate(hids, axis=1)
                w2g = w2_ref[g * EXP_PER_GROUP:(g + 1) * EXP_PER_GROUP].reshape(EXP_PER_GROUP * D_EXPERT, D_MODEL)
                acc_ref[rows, :] += _dot(hid, w2g)

    f = _dot(perm_t, acc_ref[...].astype(BF16))
    grp = _group_of_row(pl.program_id(0) * tm, bounds)
    g2 = mod_ref[pl.ds(grp, 1), 5 * D_MODEL:6 * D_MODEL]
    out = _layer_norm(alpha * h1_ref[...] + g2 * f, lng_ref[...], lnb_ref[...])
    is_ctx = pl.program_id(0) < n_ctx_tiles

    @pl.when(is_ctx)
    def _():
        oc_ref[...] = out

    @pl.when(jnp.logical_not(is_ctx))
    def _():
        od_ref[...] = out


def _moe(u2, logits, h1, mods, experts, layer, lw, *, tm, sub, bounds, alpha, n_ctx):
    t = h1.shape[0]
    nct = n_ctx // tm
    rows_c, rows_d = _pair_specs(tm, nct)
    lay = _layer_spec(layer)
    hbm = pl.BlockSpec(memory_space=pl.ANY)
    rows = lambda w: pl.BlockSpec((tm, w), lambda i: (i, 0))
    return pl.pallas_call(
        functools.partial(_moe_kernel, tm=tm, sub=sub, bounds=bounds, alpha=alpha, n_ctx_tiles=nct, layer=layer),
        grid=(t // tm,),
        in_specs=[rows(D_MODEL), rows(ROUTER_LANES), hbm, hbm, hbm,
                  rows(D_MODEL), lay(8, 6 * D_MODEL), lay(1, D_MODEL), lay(1, D_MODEL)],
        out_specs=[rows_c(D_MODEL), rows_d(D_MODEL)],
        out_shape=[jax.ShapeDtypeStruct((n_ctx, D_MODEL), F32), jax.ShapeDtypeStruct((t - n_ctx, D_MODEL), F32)],
        scratch_shapes=[pltpu.VMEM((tm, D_MODEL), BF16), pltpu.VMEM((tm, ROUTER_LANES), F32),
                        pltpu.VMEM((tm, D_MODEL), F32),
                        pltpu.VMEM((N_EXPERTS, D_MODEL, D_EXPERT), BF16),
                        pltpu.VMEM((N_EXPERTS, D_MODEL, D_EXPERT), BF16),
                        pltpu.VMEM((N_EXPERTS, D_EXPERT, D_MODEL), BF16),
                        pltpu.VMEM((WEIGHT_STAGE_SLOTS, D_MODEL, D_EXPERT), F32),
                        pltpu.VMEM((WEIGHT_STAGE_SLOTS, D_EXPERT, D_MODEL), F32),
                        pltpu.SemaphoreType.DMA((WEIGHT_STAGE_SLOTS,)),
                        pltpu.SemaphoreType.DMA((WEIGHT_STAGE_SLOTS,))],
        compiler_params=_cparams(("arbitrary",)),
        name="moe",
    )(u2, logits, experts[0], experts[1], experts[2], h1, mods, lw['ln2_g'], lw['ln2_b'])


def _rope_tables(dec_seq):
    n = HEAD_DIM // 4
    pos = jnp.arange(dec_seq)
    row = (pos // GRID_W).astype(F32)
    col = (pos % GRID_W).astype(F32)
    inv = 1.0 / (ROPE_THETA ** (jnp.arange(n, dtype=F32) / n))
    ra, ca = row[:, None] * inv, col[:, None] * inv
    cos64 = jnp.concatenate([jnp.cos(ra), jnp.cos(ra), jnp.cos(ca), jnp.cos(ca)], axis=-1)
    sin64 = jnp.concatenate([-jnp.sin(ra), jnp.sin(ra), -jnp.sin(ca), jnp.sin(ca)], axis=-1)
    return jnp.concatenate([cos64, cos64], axis=-1), jnp.concatenate([sin64, sin64], axis=-1)


def _block_diag2(w):
    z = jnp.zeros_like(w[:, 0])
    return jnp.concatenate([jnp.concatenate([w[:, 0], z], axis=2), jnp.concatenate([z, w[:, 1]], axis=2)], axis=1)


def _derived_weights(p):
    d = D_MODEL
    depth = p['w_in'].shape[0]
    row = lambda a, w: a.reshape(depth, 1, w)
    ng = len(POOL_WINDOWS)
    pool_blk = (p['pool_w'][:, :, :, None, :] * jnp.eye(ng, dtype=F32)[None, :, None, :, None]).reshape(
        depth, C_WIDTH, C_WIDTH)
    lane_pad = ROUTER_LANES - N_GROUPS - N_EXPERTS
    w_router = jnp.pad(jnp.concatenate([p['rt_grp_w'], p['rt_exp_w']], axis=-1), ((0, 0), (0, 0), (0, lane_pad)))
    b_router = jnp.pad(jnp.concatenate([p['rt_grp_b'], p['rt_exp_b']], axis=-1), ((0, 0), (0, lane_pad)))
    b_router = b_router.reshape(depth, 1, ROUTER_LANES)
    return {
        'w2blk': _block_diag2(p['rw_w2']).astype(BF16),
        'w0': row(p['rw_w0'], 2 * A_WIDTH),
        'a2blk': _block_diag2(p['rw_a2']).astype(BF16),
        'a0': row(p['rw_a0'], 2 * A_WIDTH),
        'g2': p['rw_g2'].astype(BF16),
        'kkw': row(p['rw_kk'], A_WIDTH),
        'ka': jnp.tile(row(p['rw_ka'], A_WIDTH), (1, 1, 2)),
        'rk': row(p['rw_rk'], A_WIDTH),
        'gn_g': row(p['rw_gn_g'], A_WIDTH),
        'gn_b': row(p['rw_gn_b'], A_WIDTH),
        'qn': jnp.tile(row(p['q_norm'], HEAD_DIM), (1, 1, B_Q_HEADS)),
        'knw': jnp.tile(row(p['k_norm'], HEAD_DIM), (1, 1, B_KV_HEADS)),
        'pool_blk': pool_blk.astype(BF16),
        'pool_scale': row(p['pool_scale'], C_WIDTH),
        'ln1_g': row(p['ln1_g'], d), 'ln1_b': row(p['ln1_b'], d),
        'ln2_g': row(p['ln2_g'], d), 'ln2_b': row(p['ln2_b'], d),
        'w_router_hi': w_router.astype(BF16),
        'w_router_lo': (w_router - w_router.astype(BF16).astype(F32)).astype(BF16),
        'b_router': b_router,
    }


def _forward(x_prompt, x_sample, state_rwkv, cache_k, cache_v, c, c_ctx, p):
    batch, seq, d = x_prompt.shape
    dec_batch, dec_seq, _ = x_sample.shape
    depth = p['w_in'].shape[0]
    past = cache_k.shape[2]
    n_ctx = batch * seq
    n_tok = n_ctx + dec_batch * dec_seq
    alpha = (2 * depth) ** 0.25
    bounds = tuple(n_ctx + j * dec_seq for j in range(dec_batch))
    tm = math.gcd(n_ctx, dec_seq, 512)
    tm_moe = math.gcd(n_ctx, dec_seq, 512)
    tq_ctx = min(256, seq)
    tq_dec = min(512, dec_seq)
    sb_ctx = 2

    cond8 = jnp.zeros((8, d), F32).at[0].set(c_ctx).at[1:1 + dec_batch].set(c)
    mods = _modulation(cond8, p['w_mod'], p['b_mod'])
    cos, sin = _rope_tables(dec_seq)
    w_in16, w_out16 = p['w_in'].astype(BF16), p['w_out'].astype(BF16)
    experts = (p['moe_w1'], p['moe_w3'], p['moe_w2'])
    lw = _derived_weights(p)
    head_id = jnp.arange(2 * HEAD_DIM) // HEAD_DIM
    ones = (head_id[:, None] == head_id[None, :]).astype(BF16)

    h = (x_prompt.reshape(n_ctx, d), x_sample.reshape(dec_batch * dec_seq, d))
    ck4 = cache_k.reshape(dec_batch, depth, past, KV_WIDTH)
    cv4 = cache_v.reshape(dec_batch, depth, past, KV_WIDTH)
    new_s, new_k, new_v = [], [], []
    for l in range(depth):
        (r, v, kk, logw, kd, bvec, g, bonus, q, k_att, v_att, pp) = _in_projection(
            h, mods, w_in16, l, lw, cos, sin, ones, tm=tm, bounds=bounds)
        scan_ops = (r, v, kk, logw, kd, bvec)
        yf_c, yb_c, s_fin = _rwkv_scan(scan_ops, None, l, row0=0, nseq=batch, seqlen=seq, sb=sb_ctx,
                                       nb=seq // CHUNK, want_state=True)
        yf_d, yb_d = _rwkv_scan(scan_ops, state_rwkv, l, row0=n_ctx, nseq=dec_batch, seqlen=dec_seq,
                                sb=dec_batch, nb=4, want_state=False)
        yf = (yf_c.reshape(n_ctx, A_WIDTH), yf_d.reshape(n_tok - n_ctx, A_WIDTH))
        yb = (yb_c.reshape(n_ctx, A_WIDTH), yb_d.reshape(n_tok - n_ctx, A_WIDTH))
        att_c, pool_c = _attention(q, k_att, v_att, None, pp, lw['pool_blk'], lw['pool_scale'], l,
                                   row0=0, nseq=batch, seqlen=seq, tq=tq_ctx)
        att_d, pool_d = _attention(q, k_att, v_att, (ck4, cv4), pp, lw['pool_blk'], lw['pool_scale'], l,
                                   row0=n_ctx, nseq=dec_batch, seqlen=dec_seq, tq=tq_dec)
        att, pool = (att_c, att_d), (pool_c, pool_d)
        h1, u2, logits = _out_projection(yf, yb, g, bonus, att, pool, h, mods, w_out16, l, lw, ones,
                                         tm=tm, bounds=bounds, alpha=alpha)
        h = _moe(u2, logits, h1, mods, experts, l, lw, tm=tm_moe, sub=128, bounds=bounds, alpha=alpha,
                 n_ctx=n_ctx)
        new_s.append(s_fin)
        new_k.append(k_att[:n_ctx].reshape(batch, seq, B_KV_HEADS, HEAD_DIM))
        new_v.append(v_att[:n_ctx].reshape(batch, seq, B_KV_HEADS, HEAD_DIM))
    return (h[0].reshape(batch, seq, d), h[1].reshape(dec_batch, dec_seq, d),
            jnp.stack(new_s, axis=1), jnp.stack(new_k, axis=1), jnp.stack(new_v, axis=1))


def kernel(x_prompt, x_sample, state_rwkv, cache_k, cache_v, c, c_ctx, w_in, w_out, w_mod, b_mod, ln1_g, ln1_b, ln2_g, ln2_b, rw_w0, rw_w2, rw_a0, rw_a2, rw_g2, rw_kk, rw_ka, rw_rk, rw_gn_g, rw_gn_b, q_norm, k_norm, pool_w, pool_scale, rt_grp_w, rt_grp_b, rt_exp_w, rt_exp_b, moe_w1, moe_w3, moe_w2):
    p = dict(w_in=w_in, w_out=w_out, w_mod=w_mod, b_mod=b_mod, ln1_g=ln1_g, ln1_b=ln1_b, ln2_g=ln2_g,
             ln2_b=ln2_b, rw_w0=rw_w0, rw_w2=rw_w2, rw_a0=rw_a0, rw_a2=rw_a2, rw_g2=rw_g2, rw_kk=rw_kk,
             rw_ka=rw_ka, rw_rk=rw_rk, rw_gn_g=rw_gn_g, rw_gn_b=rw_gn_b, q_norm=q_norm, k_norm=k_norm,
             pool_w=pool_w, pool_scale=pool_scale, rt_grp_w=rt_grp_w, rt_grp_b=rt_grp_b, rt_exp_w=rt_exp_w,
             rt_exp_b=rt_exp_b, moe_w1=moe_w1, moe_w3=moe_w3, moe_w2=moe_w2)
    return _forward(x_prompt, x_sample, state_rwkv, cache_k, cache_v, c, c_ctx, p)
```
